```python
import math
import jax
import jax.numpy as jnp
from jax import lax
import numpy as np

D_MODEL = 2048
BATCH = 2
SEQ = 8192
DEPTH = 2

PLE_DIM = 256
D_FF = 5632
A_HEADS = 8
A_QK_DIM = 64
A_V_DIM = 128
B_HEADS = 4
B_QK_DIM = 128
B_V_DIM = 256
CONV_WIDTH = 4
C_HEADS = 4
C_QK_DIM = 256
C_V_DIM = 512
C_GATE_RANK = 16
C_GATE_TEMP = 16.0
CHUNK = 64
Q_BLOCK = 128
N_BUCKETS = 32
MAX_DISTANCE = 128
LN_EPS = 1e-5
NORM_EPS = 1e-6
DEEPNORM_ALPHA = (2 * DEPTH) ** 0.25
DEEPNORM_BETA = (8 * DEPTH) ** -0.25
N_EVEN = (DEPTH + 1) // 2
N_ODD = DEPTH // 2

A_WIDTH = A_HEADS * A_V_DIM
B_WIDTH = B_HEADS * B_V_DIM
MIX_WIDTH = A_WIDTH + B_WIDTH
AB_SIZES = (A_HEADS * 2 * A_QK_DIM, A_HEADS * 2 * A_QK_DIM, A_WIDTH, 2 * B_HEADS * B_QK_DIM, B_WIDTH, B_WIDTH, B_HEADS, B_HEADS)
AB_PROJ = sum(AB_SIZES)
C_WIDTH = C_HEADS * C_V_DIM
C_SIZES = (C_HEADS * C_QK_DIM, C_HEADS * C_QK_DIM, C_WIDTH, C_WIDTH, C_GATE_RANK)
C_PROJ = sum(C_SIZES)

kernel_name = 'hybrid_diffattn_mlstm_gla_macaron'


def split_cols(t, sizes):
    points = np.cumsum(np.array(sizes))[:-1].tolist()
    return jnp.split(t, points, axis=-1)


def layer_norm(x, g, b):
    xf = x.astype(jnp.float32)
    mu = jnp.mean(xf, -1, keepdims=True)
    var = jnp.mean(jnp.square(xf - mu), -1, keepdims=True)
    return ((xf - mu) * lax.rsqrt(var + LN_EPS)).astype(x.dtype) * g + b


def head_rms_norm(h, g):
    hf = h.astype(jnp.float32)
    hn = hf * lax.rsqrt(jnp.mean(hf * hf, -1, keepdims=True) + NORM_EPS)
    bsz, s = h.shape[:2]
    return hn.astype(h.dtype).reshape(bsz, s, -1) * g


def swiglu(x, w_in, w_out):
    gate, up = jnp.split(x @ w_in, 2, axis=-1)
    return (jax.nn.silu(gate) * up) @ w_out


def causal_dwconv(x, w, b):
    s = x.shape[1]
    xp = jnp.pad(x, ((0, 0), (CONV_WIDTH - 1, 0), (0, 0)))
    return sum(w[j] * xp[:, j:j + s] for j in range(CONV_WIDTH)) + b


def t5_bucket(rel):
    n = jnp.maximum(rel, 0)
    max_exact = N_BUCKETS // 2
    nf = jnp.maximum(n, 1).astype(jnp.float32)
    large = max_exact + (jnp.log(nf / max_exact) / math.log(MAX_DISTANCE / max_exact) * (N_BUCKETS - max_exact)).astype(jnp.int32)
    large = jnp.minimum(large, N_BUCKETS - 1)
    return jnp.where(n < max_exact, n, large)


def to_chunks(t):
    bsz, s, h = t.shape[:3]
    t = t.reshape((bsz, s // CHUNK, CHUNK, h) + t.shape[3:])
    return jnp.moveaxis(t, (1, 3), (0, 2))


def from_chunks(t):
    nc, bsz, h, l = t.shape[:4]
    return jnp.moveaxis(t, (0, 2), (1, 3)).reshape((bsz, nc * l, h) + t.shape[4:])


def diff_attention(q, k, v, lam, rel_bias):
    bsz, s, h = q.shape[:3]
    nb = s // Q_BLOCK
    qb = jnp.moveaxis(q.reshape(bsz, nb, Q_BLOCK, h, 2, A_QK_DIM), 1, 0)
    k_pos = jnp.arange(s)
    scale = A_QK_DIM ** -0.5

    def block(args):
        q_blk, blk = args
        q_pos = blk * Q_BLOCK + jnp.arange(Q_BLOCK)
        rel = q_pos[:, None] - k_pos[None, :]
        bias = jnp.transpose(rel_bias[t5_bucket(rel)], (2, 0, 1)).astype(jnp.float32)
        logits = jnp.einsum('bqhcd,bkhcd->bhcqk', q_blk, k).astype(jnp.float32) * scale + bias[None, :, None]
        logits = jnp.where((rel >= 0)[None, None, None], logits, -jnp.inf)
        probs = jax.nn.softmax(logits, axis=-1)
        attn = probs[:, :, 0] - lam * probs[:, :, 1]
        return jnp.einsum('bhqk,bkhd->bqhd', attn.astype(v.dtype), v)

    out = lax.map(block, (qb, jnp.arange(nb)))
    return jnp.moveaxis(out, 0, 1).reshape(bsz, s, h, A_V_DIM)


def mlstm_chunkwise(q, k, v, i_pre, f_pre):
    bsz, _, heads, dk = q.shape
    dv = v.shape[-1]
    f32 = jnp.float32
    qc = to_chunks(q.astype(f32) * dk ** -0.5)
    kc = to_chunks(k.astype(f32))
    vc = to_chunks(v.astype(f32))
    ic = to_chunks(i_pre.astype(f32))
    fc = to_chunks(jax.nn.log_sigmoid(f_pre.astype(f32)))
    tril = jnp.tril(jnp.ones((CHUNK, CHUNK), bool))

    def step(carry, xs):
        c_mat, n_vec, m_prev = carry
        q_, k_, v_, i_, lf = xs
        b = jnp.cumsum(lf, axis=-1)
        a = b + m_prev[..., None]
        d = jnp.where(tril, b[..., :, None] - b[..., None, :] + i_[..., None, :], -jnp.inf)
        m_t = jnp.maximum(a, jnp.max(d, -1))
        w_inter = jnp.exp(a - m_t)
        sw = jnp.einsum('bhtd,bhsd->bhts', q_, k_) * jnp.exp(d - m_t[..., None])
        num = w_inter[..., None] * jnp.einsum('bhtd,bhde->bhte', q_, c_mat) + jnp.einsum('bhts,bhse->bhte', sw, v_)
        den = w_inter * jnp.einsum('bhtd,bhd->bht', q_, n_vec) + jnp.sum(sw, -1)
        h = num / jnp.maximum(jnp.abs(den), jnp.exp(-m_t))[..., None]
        b_last = b[..., -1]
        g = b_last[..., None] - b + i_
        m_new = jnp.maximum(b_last + m_prev, jnp.max(g, -1))
        decay = jnp.exp(b_last + m_prev - m_new)
        wk = jnp.exp(g - m_new[..., None])
        c_new = decay[..., None, None] * c_mat + jnp.einsum('bhsd,bhse->bhde', k_ * wk[..., None], v_)
        n_new = decay[..., None] * n_vec + jnp.einsum('bhsd,bhs->bhd', k_, wk)
        return (c_new, n_new, m_new), h

    init = (jnp.zeros((bsz, heads, dk, dv), f32), jnp.zeros((bsz, heads, dk), f32), jnp.zeros((bsz, heads), f32))
    _, h = lax.scan(step, init, (qc, kc, vc, ic, fc))
    return from_chunks(h).astype(v.dtype)


def gla_chunkwise(q, k, v, log_a):
    bsz, _, heads, dk = q.shape
    dv = v.shape[-1]
    f32 = jnp.float32
    qc = to_chunks(q.astype(f32) * dk ** -0.5)
    kc = to_chunks(k.astype(f32))
    vc = to_chunks(v.astype(f32))
    gc = to_chunks(log_a.astype(f32))
    tril = jnp.tril(jnp.ones((CHUNK, CHUNK), bool))

    def step(s_mat, xs):
        q_, k_, v_, g_ = xs
        b = jnp.cumsum(g_, axis=2)
        q_dec = q_ * jnp.exp(b)
        k_dec = k_ * jnp.exp(-b)
        att = jnp.where(tril, jnp.einsum('bhtd,bhsd->bhts', q_dec, k_dec), 0.0)
        o = jnp.einsum('bhts,bhse->bhte', att, v_) + jnp.einsum('bhtd,bhde->bhte', q_dec, s_mat)
        b_last = b[:, :, -1]
        s_new = jnp.exp(b_last)[..., None] * s_mat + jnp.einsum('bhsd,bhse->bhde', k_ * jnp.exp(b_last[:, :, None] - b), v_)
        return s_new, o

    _, o = lax.scan(step, jnp.zeros((bsz, heads, dk, dv), f32), (qc, kc, vc, gc))
    return from_chunks(o).astype(v.dtype)


def mixer_ab(x, w_in, w_out, rel_bias, lq1, lk1, lq2, lk2, lam_init, diff_g, conv_w, conv_b, b_i, b_f, mlstm_g):
    bsz, s, _ = x.shape
    qa, ka, va, qkb, vb, ob, ib, fb = split_cols(x @ w_in, AB_SIZES)
    lam = (jnp.exp(jnp.sum(lq1 * lk1)) - jnp.exp(jnp.sum(lq2 * lk2)) + lam_init).astype(jnp.float32)
    ya = diff_attention(qa.reshape(bsz, s, A_HEADS, 2, A_QK_DIM), ka.reshape(bsz, s, A_HEADS, 2, A_QK_DIM),
                        va.reshape(bsz, s, A_HEADS, A_V_DIM), lam, rel_bias)
    ya = head_rms_norm(ya, diff_g) * (1.0 - lam_init)
    qkb = jax.nn.silu(causal_dwconv(qkb, conv_w, conv_b))
    qb, kb = jnp.split(qkb, 2, axis=-1)
    yb = mlstm_chunkwise(qb.reshape(bsz, s, B_HEADS, B_QK_DIM), kb.reshape(bsz, s, B_HEADS, B_QK_DIM),
                         vb.reshape(bsz, s, B_HEADS, B_V_DIM), ib + b_i, fb + b_f)
    yb = head_rms_norm(yb, mlstm_g) * jax.nn.sigmoid(ob)
    return jnp.concatenate([ya, yb], axis=-1) @ w_out


def mixer_c(x, w_in, w_a2, b_a, gla_g, w_out):
    bsz, s, _ = x.shape
    q, k, v, r, a1 = split_cols(x @ w_in, C_SIZES)
    log_a = jax.nn.log_sigmoid((a1 @ w_a2 + b_a).astype(jnp.float32)) / C_GATE_TEMP
    y = gla_chunkwise(q.reshape(bsz, s, C_HEADS, C_QK_DIM), k.reshape(bsz, s, C_HEADS, C_QK_DIM),
                      v.reshape(bsz, s, C_HEADS, C_V_DIM), log_a.reshape(bsz, s, C_HEADS, C_QK_DIM))
    y = head_rms_norm(y, gla_g) * jax.nn.silu(r)
    return y @ w_out


def setup_inputs(seed: int = 0) -> dict:
    key = jax.random.key(seed)
    ks = jax.random.split(key, 32)
    f32 = jnp.float32

    def nrm(k, shape, scale):
        return jax.random.normal(k, shape, f32) * scale

    return {
        'x': nrm(ks[0], (BATCH, SEQ, D_MODEL), 1.0),
        'p': nrm(ks[1], (DEPTH, BATCH, SEQ, PLE_DIM), 1.0),
        'ln_g': 1.0 + nrm(ks[2], (DEPTH, 3, D_MODEL), 0.02),
        'ln_b': nrm(ks[3], (DEPTH, 3, D_MODEL), 0.02),
        'w_ffn_in': nrm(ks[4], (DEPTH, 2, D_MODEL, 2 * D_FF), D_MODEL ** -0.5),
        'w_ffn_out': nrm(ks[5], (DEPTH, 2, D_FF, D_MODEL), D_FF ** -0.5 * DEEPNORM_BETA),
        'w_in_ab': nrm(ks[6], (N_EVEN, D_MODEL, AB_PROJ), D_MODEL ** -0.5),
        'w_out_ab': nrm(ks[7], (N_EVEN, MIX_WIDTH, D_MODEL), MIX_WIDTH ** -0.5 * DEEPNORM_BETA),
        'rel_bias': nrm(ks[8], (N_BUCKETS, A_HEADS), 0.5),
        'lambda_q1': nrm(ks[9], (N_EVEN, A_QK_DIM), 0.1),
        'lambda_k1': nrm(ks[10], (N_EVEN, A_QK_DIM), 0.1),
        'lambda_q2': nrm(ks[11], (N_EVEN, A_QK_DIM), 0.1),
        'lambda_k2': nrm(ks[12], (N_EVEN, A_QK_DIM), 0.1),
        'diff_norm': 1.0 + nrm(ks[13], (N_EVEN, A_WIDTH), 0.02),
        'conv_w': nrm(ks[14], (N_EVEN, CONV_WIDTH, 2 * B_HEADS * B_QK_DIM), 0.5),
        'conv_b': nrm(ks[15], (N_EVEN, 2 * B_HEADS * B_QK_DIM), 0.02),
        'b_igate': nrm(ks[16], (N_EVEN, B_HEADS), 0.1),
        'b_fgate': jnp.linspace(3.0, 6.0, B_HEADS, dtype=f32)[None] + nrm(ks[17], (N_EVEN, B_HEADS), 0.1),
        'mlstm_norm': 1.0 + nrm(ks[18], (N_EVEN, B_WIDTH), 0.02),
        'w_in_c': nrm(ks[19], (N_ODD, D_MODEL, C_PROJ), D_MODEL ** -0.5),
        'w_alpha2': nrm(ks[20], (N_ODD, C_GATE_RANK, C_HEADS * C_QK_DIM), C_GATE_RANK ** -0.5),
        'b_alpha': nrm(ks[21], (N_ODD, C_HEADS * C_QK_DIM), 0.1),
        'gla_norm': 1.0 + nrm(ks[22], (N_ODD, C_WIDTH), 0.02),
        'w_out_c': nrm(ks[23], (N_ODD, C_WIDTH, D_MODEL), C_WIDTH ** -0.5 * DEEPNORM_BETA),
        'w_ple_proj': nrm(ks[24], (DEPTH, PLE_DIM, D_MODEL), PLE_DIM ** -0.5),
        'w_ple_gate': nrm(ks[25], (DEPTH, D_MODEL, D_MODEL), D_MODEL ** -0.5),
    }


def reference(x, p, ln_g, ln_b, w_ffn_in, w_ffn_out, w_in_ab, w_out_ab, rel_bias, lambda_q1, lambda_k1,
              lambda_q2, lambda_k2, diff_norm, conv_w, conv_b, b_igate, b_fgate, mlstm_norm, w_in_c,
              w_alpha2, b_alpha, gla_norm, w_out_c, w_ple_proj, w_ple_gate):
    for i in range(DEPTH):
        x = layer_norm(DEEPNORM_ALPHA * x + 0.5 * swiglu(x, w_ffn_in[i, 0], w_ffn_out[i, 0]), ln_g[i, 0], ln_b[i, 0])
        if i % 2 == 0:
            e = i // 2
            lam_init = 0.8 - 0.6 * math.exp(-0.3 * i)
            y = mixer_ab(x, w_in_ab[e], w_out_ab[e], rel_bias, lambda_q1[e], lambda_k1[e], lambda_q2[e],
                         lambda_k2[e], lam_init, diff_norm[e], conv_w[e], conv_b[e], b_igate[e], b_fgate[e],
                         mlstm_norm[e])
        else:
            o = i // 2
            y = mixer_c(x, w_in_c[o], w_alpha2[o], b_alpha[o], gla_norm[o], w_out_c[o])
        x = layer_norm(DEEPNORM_ALPHA * x + y, ln_g[i, 1], ln_b[i, 1])
        x = layer_norm(DEEPNORM_ALPHA * x + 0.5 * swiglu(x, w_ffn_in[i, 1], w_ffn_out[i, 1]), ln_g[i, 2], ln_b[i, 2])
        x = x + (p[i] @ w_ple_proj[i]) * jax.nn.sigmoid(x @ w_ple_gate[i])
    return x
```

```python
import functools
import math

import jax
import jax.numpy as jnp
from jax import lax
from jax.experimental import pallas as pl
from jax.experimental.pallas import tpu as pltpu

F32 = jnp.float32
BF16 = jnp.bfloat16
HIGHEST = lax.Precision.HIGHEST

A_HEADS = 8
A_QK_DIM = 64
A_V_DIM = 128
B_HEADS = 4
B_QK_DIM = 128
B_V_DIM = 256
CONV_WIDTH = 4
C_HEADS = 4
C_QK_DIM = 256
C_V_DIM = 512
C_GATE_TEMP = 16.0
N_BUCKETS = 32
MAX_DISTANCE = 128
LN_EPS = 1e-5
NORM_EPS = 1e-6

V7X_VMEM_LIMIT_BYTES = 56 * 1024 * 1024
LANES = 128
SUBLANES = 8

NT_DIMS = (((1,), (1,)), ((), ()))
TN_DIMS = (((0,), (0,)), ((), ()))


def _params(*sem):
    return pltpu.CompilerParams(dimension_semantics=sem, vmem_limit_bytes=V7X_VMEM_LIMIT_BYTES)


def _layer_norm(y, g, b):
    mu = jnp.mean(y, -1, keepdims=True)
    yc = y - mu
    var = jnp.mean(yc * yc, -1, keepdims=True)
    return yc * lax.rsqrt(var + LN_EPS) * g + b


def _sigmoid(x):
    return 1.0 / (1.0 + jnp.exp(-x))


def _log_sigmoid(x):
    return jnp.minimum(x, 0.0) - jnp.log(1.0 + jnp.exp(-jnp.abs(x)))


def _ffn_ln_kernel(x_ref, wg_ref, wu_ref, wo_ref, g_ref, b_ref, o_ref, xb_ref, acc_ref, *, alpha):
    j = pl.program_id(1)

    @pl.when(j == 0)
    def _():
        xb_ref[...] = x_ref[...].astype(BF16)
        acc_ref[...] = jnp.zeros_like(acc_ref)

    xb = xb_ref[...]
    gate = jnp.dot(xb, wg_ref[...], preferred_element_type=F32)
    up = jnp.dot(xb, wu_ref[...], preferred_element_type=F32)
    h = (gate * _sigmoid(gate) * up).astype(BF16)
    acc_ref[...] += jnp.dot(h, wo_ref[...], preferred_element_type=F32)

    @pl.when(j == pl.num_programs(1) - 1)
    def _():
        y = alpha * x_ref[...] + 0.5 * acc_ref[...]
        o_ref[...] = _layer_norm(y, g_ref[...], b_ref[...])


def _ffn_ln(x, w_in, w_out, g, b, *, alpha, tm, tf):
    t, d = x.shape
    f = w_out.shape[0]
    nf = f // tf
    assert t % tm == 0 and f % tf == 0 and w_in.shape == (d, 2 * f)
    return pl.pallas_call(
        functools.partial(_ffn_ln_kernel, alpha=alpha),
        grid=(t // tm, nf),
        in_specs=[
            pl.BlockSpec((tm, d), lambda i, j: (i, 0)),
            pl.BlockSpec((d, tf), lambda i, j: (0, j)),
            pl.BlockSpec((d, tf), lambda i, j: (0, nf + j)),
            pl.BlockSpec((tf, d), lambda i, j: (j, 0)),
            pl.BlockSpec((1, d), lambda i, j: (0, 0)),
            pl.BlockSpec((1, d), lambda i, j: (0, 0)),
        ],
        out_specs=pl.BlockSpec((tm, d), lambda i, j: (i, 0)),
        out_shape=jax.ShapeDtypeStruct((t, d), F32),
        scratch_shapes=[pltpu.VMEM((tm, d), BF16), pltpu.VMEM((tm, d), F32)],
        compiler_params=_params("parallel", "arbitrary"),
        name="ffn_ln",
    )(x, w_in, w_in, w_out, g, b)


def _in_proj_kernel(x_ref, w_ref, wgc_ref, wgr_ref, y_ref, gc_ref, gr_ref, xb_ref):
    j = pl.program_id(1)

    @pl.when(j == 0)
    def _():
        xb = x_ref[...].astype(BF16)
        xb_ref[...] = xb
        gc_ref[...] = jnp.dot(xb, wgc_ref[...], preferred_element_type=F32)
        gr_ref[...] = lax.dot_general(wgr_ref[...], xb, NT_DIMS, preferred_element_type=F32)

    y_ref[...] = jnp.dot(xb_ref[...], w_ref[...], preferred_element_type=F32).astype(y_ref.dtype)


def _in_proj(x, w_main, w_gate_cols, w_gate_rows, *, tm, tn):
    t, d = x.shape
    n = w_main.shape[1]
    assert t % tm == 0 and n % tn == 0
    return pl.pallas_call(
        _in_proj_kernel,
        grid=(t // tm, n // tn),
        in_specs=[
            pl.BlockSpec((tm, d), lambda i, j: (i, 0)),
            pl.BlockSpec((d, tn), lambda i, j: (0, j)),
            pl.BlockSpec((d, LANES), lambda i, j: (0, 0)),
            pl.BlockSpec((SUBLANES, d), lambda i, j: (0, 0)),
        ],
        out_specs=[
            pl.BlockSpec((tm, tn), lambda i, j: (i, j)),
            pl.BlockSpec((tm, LANES), lambda i, j: (i, 0)),
            pl.BlockSpec((SUBLANES, tm), lambda i, j: (0, i)),
        ],
        out_shape=[
            jax.ShapeDtypeStruct((t, n), BF16),
            jax.ShapeDtypeStruct((t, LANES), F32),
            jax.ShapeDtypeStruct((SUBLANES, t), F32),
        ],
        scratch_shapes=[pltpu.VMEM((tm, d), BF16)],
        compiler_params=_params("parallel", "arbitrary"),
        name="in_proj",
    )(x, w_main, w_gate_cols, w_gate_rows)


def _diff_attn_kernel(q_ref, k_ref, v_ref, bdiag_ref, bnear_ref, cfar_ref, lam_ref, g_ref, o_ref,
                      m_ref, l_ref, acc_ref, *, t, lam_init):
    qi = pl.program_id(2)
    dk = A_QK_DIM

    q = q_ref[...].astype(F32) * (dk ** -0.5)
    lane = lax.broadcasted_iota(jnp.int32, q.shape, 1)
    qs = jnp.concatenate([jnp.where(lane < dk, q, 0.0), jnp.where(lane >= dk, q, 0.0)], axis=0).astype(BF16)

    m_ref[...] = jnp.full_like(m_ref, -jnp.inf)
    l_ref[...] = jnp.zeros_like(l_ref)
    acc_ref[...] = jnp.zeros_like(acc_ref)

    def scores(kj):
        off = pl.multiple_of(kj * t, t)
        kt = k_ref[pl.ds(off, t), :]
        vt = v_ref[pl.ds(off, t), :]
        return lax.dot_general(qs, kt, NT_DIMS, preferred_element_type=F32), vt

    def update(s, vt, shift):
        m_old = m_ref[...]
        m_new = jnp.maximum(m_old, jnp.max(s, -1, keepdims=True) + shift)
        a = jnp.exp(m_old - m_new)
        p = jnp.exp(s - (m_new - shift))
        l_ref[...] = a * l_ref[...] + jnp.sum(p, -1, keepdims=True)
        acc_ref[...] = a * acc_ref[...] + jnp.dot(p.astype(BF16), vt, preferred_element_type=F32)
        m_ref[...] = m_new

    c_far = cfar_ref[...][:, 0:1]

    def far_body(kj, carry):
        s, vt = scores(kj)
        update(s, vt, c_far)
        return carry

    lax.fori_loop(0, jnp.maximum(qi - 1, 0), far_body, 0)

    @pl.when(qi >= 1)
    def _():
        s, vt = scores(qi - 1)
        update(s + bnear_ref[...], vt, 0.0)

    s, vt = scores(qi)
    update(s + bdiag_ref[...], vt, 0.0)

    lv = lam_ref[...]
    lam = (jnp.exp(jnp.sum(lv[0:1] * lv[1:2], -1, keepdims=True))
           - jnp.exp(jnp.sum(lv[2:3] * lv[3:4], -1, keepdims=True)) + lam_init)
    o = acc_ref[...] / l_ref[...]
    out = o[:t] - lam * o[t:]
    hn = out * lax.rsqrt(jnp.mean(out * out, -1, keepdims=True) + NORM_EPS)
    o_ref[...] = (hn * g_ref[...] * (1.0 - lam_init)).astype(o_ref.dtype)


def _t5_bias_by_distance(rel_bias, n):
    r = jnp.arange(n, dtype=jnp.int32)
    max_exact = N_BUCKETS // 2
    rf = jnp.maximum(r, 1).astype(F32)
    large = max_exact + (jnp.log(rf / max_exact) / math.log(MAX_DISTANCE / max_exact)
                         * (N_BUCKETS - max_exact)).astype(jnp.int32)
    large = jnp.minimum(large, N_BUCKETS - 1)
    bucket = jnp.where(r < max_exact, r, large)
    return jnp.transpose(rel_bias[bucket]).astype(F32)


def _diff_attention(y3, rel_bias, lam_vecs, diff_g, *, lam_init, t):
    bsz, s, _ = y3.shape
    hd = 2 * A_QK_DIM
    assert hd == A_V_DIM == LANES and s % t == 0 and t >= MAX_DISTANCE
    nq = s // t
    table = _t5_bias_by_distance(rel_bias, 2 * t)
    i = jnp.arange(t)[:, None]
    j = jnp.arange(t)[None, :]
    diag = jnp.where((i >= j)[None], table[:, jnp.maximum(i - j, 0)], -1e30)
    near = table[:, t + i - j]
    bdiag = jnp.concatenate([diag, diag], axis=1)
    bnear = jnp.concatenate([near, near], axis=1)
    cfar = jnp.broadcast_to(rel_bias[N_BUCKETS - 1][:, None, None], (A_HEADS, 1, LANES)).astype(F32)
    return pl.pallas_call(
        functools.partial(_diff_attn_kernel, t=t, lam_init=lam_init),
        grid=(bsz, A_HEADS, nq),
        in_specs=[
            pl.BlockSpec((None, t, hd), lambda b, h, q: (b, q, h)),
            pl.BlockSpec((None, s, hd), lambda b, h, q: (b, 0, A_HEADS + h)),
            pl.BlockSpec((None, s, A_V_DIM), lambda b, h, q: (b, 0, 2 * A_HEADS + h)),
            pl.BlockSpec((None, 2 * t, t), lambda b, h, q: (h, 0, 0)),
            pl.BlockSpec((None, 2 * t, t), lambda b, h, q: (h, 0, 0)),
            pl.BlockSpec((None, 1, LANES), lambda b, h, q: (h, 0, 0)),
            pl.BlockSpec((4, A_QK_DIM), lambda b, h, q: (0, 0)),
            pl.BlockSpec((1, A_V_DIM), lambda b, h, q: (0, h)),
        ],
        out_specs=pl.BlockSpec((None, t, A_V_DIM), lambda b, h, q: (b, q, h)),
        out_shape=jax.ShapeDtypeStruct((bsz, s, A_HEADS * A_V_DIM), BF16),
        scratch_shapes=[pltpu.VMEM((2 * t, 1), F32), pltpu.VMEM((2 * t, 1), F32),
                        pltpu.VMEM((2 * t, A_V_DIM), F32)],
        compiler_params=_params("parallel", "parallel", "arbitrary"),
        name="diff_attn",
    )(y3, y3, y3, bdiag, bnear, cfar, lam_vecs, diff_g)


def _mlstm_kernel(qk_ref, v_ref, ob_ref, gc_ref, gr_ref, cw_ref, cb_ref, bc_ref, br_ref, g_ref, y_ref,
                  xp_ref, c_ref, n_ref, m_ref, *, chunk):
    heads, dk, dv = B_HEADS, B_QK_DIM, B_V_DIM
    pad = SUBLANES
    c = pl.program_id(1)

    @pl.when(c == 0)
    def _():
        xp_ref[0:pad, :] = jnp.zeros((pad, xp_ref.shape[1]), F32)
        c_ref[...] = jnp.zeros_like(c_ref)
        n_ref[...] = jnp.zeros_like(n_ref)
        m_ref[...] = jnp.zeros_like(m_ref)

    x = qk_ref[...].astype(F32)
    xp_ref[pad:pad + chunk, :] = x
    cw = cw_ref[...]
    conv = cb_ref[...]
    for j in range(CONV_WIDTH):
        conv = conv + cw[j:j + 1, :] * xp_ref[pl.ds(pad - (CONV_WIDTH - 1) + j, chunk), :]
    xp_ref[0:pad, :] = x[chunk - pad:chunk, :]
    qk = conv * _sigmoid(conv)

    gc = gc_ref[...] + bc_ref[...]
    gr = gr_ref[...] + br_ref[...]
    row = lax.broadcasted_iota(jnp.int32, (chunk, chunk), 0)
    col = lax.broadcasted_iota(jnp.int32, (chunk, chunk), 1)
    causal = row >= col
    tril = causal.astype(F32)
    triu = (row <= col).astype(F32)
    b_c = jnp.dot(tril, _log_sigmoid(gc), precision=HIGHEST, preferred_element_type=F32)
    b_r = jnp.dot(_log_sigmoid(gr), triu, precision=HIGHEST, preferred_element_type=F32)

    ob = ob_ref[...].astype(F32)
    g = g_ref[...]
    for h in range(heads):
        q = qk[:, h * dk:(h + 1) * dk] * (dk ** -0.5)
        k = qk[:, (heads + h) * dk:(heads + h + 1) * dk]
        v = v_ref[:, h * dv:(h + 1) * dv]
        b_col = b_c[:, heads + h:heads + h + 1]
        i_col = gc[:, h:h + 1]
        b_row = b_r[heads + h:heads + h + 1, :]
        i_row = gr[h:h + 1, :]
        m_prev = m_ref[h][0:1, 0:1]

        d = jnp.where(causal, b_col - b_row + i_row, -jnp.inf)
        a_col = b_col + m_prev
        m_t = jnp.maximum(a_col, jnp.max(d, -1, keepdims=True))
        w_inter = jnp.exp(a_col - m_t)
        qb = q.astype(BF16)
        sw = lax.dot_general(qb, k.astype(BF16), NT_DIMS, preferred_element_type=F32) * jnp.exp(d - m_t)
        c_mat = c_ref[h]
        num = (w_inter * jnp.dot(qb, c_mat.astype(BF16), preferred_element_type=F32)
               + jnp.dot(sw.astype(BF16), v, preferred_element_type=F32))
        den = w_inter * jnp.sum(q * n_ref[h], -1, keepdims=True) + jnp.sum(sw, -1, keepdims=True)
        hh = num / jnp.maximum(jnp.abs(den), jnp.exp(-m_t))

        b_last = b_row[:, chunk - 1:chunk]
        g_col = b_last - b_col + i_col
        m_new = jnp.maximum(b_last + m_prev, jnp.max(g_col, 0, keepdims=True))
        decay = jnp.exp(b_last + m_prev - m_new)
        kw = k * jnp.exp(g_col - m_new)
        c_ref[h] = decay * c_mat + lax.dot_general(kw.astype(BF16), v, TN_DIMS, preferred_element_type=F32)
        n_ref[h] = decay * n_ref[h] + jnp.sum(kw, 0, keepdims=True)
        m_ref[h] = jnp.broadcast_to(m_new, m_ref.shape[1:])

        hn = hh * lax.rsqrt(jnp.mean(hh * hh, -1, keepdims=True) + NORM_EPS)
        sl = slice(h * dv, (h + 1) * dv)
        y_ref[:, sl] = (hn * g[:, sl] * _sigmoid(ob[:, sl])).astype(y_ref.dtype)


def _mlstm(y3, gcol3, grow3, conv_w, conv_b, gate_bias_cols, gate_bias_rows, mlstm_g, *, chunk):
    bsz, s, _ = y3.shape
    heads, dk, dv = B_HEADS, B_QK_DIM, B_V_DIM
    w = heads * dv
    assert 2 * heads * dk == w and s % chunk == 0 and chunk % LANES == 0
    nc = s // chunk
    return pl.pallas_call(
        functools.partial(_mlstm_kernel, chunk=chunk),
        grid=(bsz, nc),
        in_specs=[
            pl.BlockSpec((None, chunk, w), lambda b, c: (b, c, 3)),
            pl.BlockSpec((None, chunk, w), lambda b, c: (b, c, 4)),
            pl.BlockSpec((None, chunk, w), lambda b, c: (b, c, 5)),
            pl.BlockSpec((None, chunk, LANES), lambda b, c: (b, c, 0)),
            pl.BlockSpec((SUBLANES, chunk), lambda b, c: (0, b * nc + c)),
            pl.BlockSpec((CONV_WIDTH, w), lambda b, c: (0, 0)),
            pl.BlockSpec((1, w), lambda b, c: (0, 0)),
            pl.BlockSpec((1, LANES), lambda b, c: (0, 0)),
            pl.BlockSpec((SUBLANES, 1), lambda b, c: (0, 0)),
            pl.BlockSpec((1, w), lambda b, c: (0, 0)),
        ],
        out_specs=pl.BlockSpec((None, chunk, w), lambda b, c: (b, c, 0)),
        out_shape=jax.ShapeDtypeStruct((bsz, s, w), BF16),
        scratch_shapes=[
            pltpu.VMEM((SUBLANES + chunk, w), F32),
            pltpu.VMEM((heads, dk, dv), F32),
            pltpu.VMEM((heads, 1, dk), F32),
            pltpu.VMEM((heads, SUBLANES, LANES), F32),
        ],
        compiler_params=_params("parallel", "arbitrary"),
        name="mlstm",
    )(y3, y3, y3, gcol3, grow3, conv_w, conv_b, gate_bias_cols, gate_bias_rows, mlstm_g)


def _gla_kernel(q_ref, k_ref, v_ref, r_ref, a1_ref, wa_ref, ba_ref, g_ref, y_ref, st_ref, *, chunk):
    heads, dk, dv = C_HEADS, C_QK_DIM, C_V_DIM
    c = pl.program_id(1)

    @pl.when(c == 0)
    def _():
        st_ref[...] = jnp.zeros_like(st_ref)

    z = jnp.dot(a1_ref[...], wa_ref[...], precision=HIGHEST, preferred_element_type=F32) + ba_ref[...]
    log_a = _log_sigmoid(z) * (1.0 / C_GATE_TEMP)
    row = lax.broadcasted_iota(jnp.int32, (chunk, chunk), 0)
    col = lax.broadcasted_iota(jnp.int32, (chunk, chunk), 1)
    causal = row >= col
    b_all = jnp.dot(causal.astype(F32), log_a, precision=HIGHEST, preferred_element_type=F32)

    r = r_ref[...].astype(F32)
    g = g_ref[...]
    for h in range(heads):
        ks = slice(h * dk, (h + 1) * dk)
        vs = slice(h * dv, (h + 1) * dv)
        b = b_all[:, ks]
        q = q_ref[:, ks].astype(F32) * (dk ** -0.5)
        k = k_ref[:, ks].astype(F32)
        v = v_ref[:, vs]
        q_dec = (q * jnp.exp(b)).astype(BF16)
        k_dec = (k * jnp.exp(-b)).astype(BF16)
        att = jnp.where(causal, lax.dot_general(q_dec, k_dec, NT_DIMS, preferred_element_type=F32), 0.0)
        st = st_ref[h]
        o = (jnp.dot(att.astype(BF16), v, preferred_element_type=F32)
             + lax.dot_general(q_dec, st.astype(BF16), NT_DIMS, preferred_element_type=F32))
        b_last = b[chunk - 1:chunk, :]
        k_upd = (k * jnp.exp(b_last - b)).astype(BF16)
        st_ref[h] = jnp.exp(b_last) * st + lax.dot_general(v, k_upd, TN_DIMS, preferred_element_type=F32)

        on = o * lax.rsqrt(jnp.mean(o * o, -1, keepdims=True) + NORM_EPS)
        rr = r[:, vs]
        y_ref[:, vs] = (on * g[:, vs] * (rr * _sigmoid(rr))).astype(y_ref.dtype)


def _gla(y3, a13, w_a2, b_a, gla_g, *, chunk):
    bsz, s, _ = y3.shape
    heads, dk, dv = C_HEADS, C_QK_DIM, C_V_DIM
    wk, wv = heads * dk, heads * dv
    assert wv == 2 * wk and s % chunk == 0
    nc = s // chunk
    return pl.pallas_call(
        functools.partial(_gla_kernel, chunk=chunk),
        grid=(bsz, nc),
        in_specs=[
            pl.BlockSpec((None, chunk, wk), lambda b, c: (b, c, 0)),
            pl.BlockSpec((None, chunk, wk), lambda b, c: (b, c, 1)),
            pl.BlockSpec((None, chunk, wv), lambda b, c: (b, c, 1)),
            pl.BlockSpec((None, chunk, wv), lambda b, c: (b, c, 2)),
            pl.BlockSpec((None, chunk, LANES), lambda b, c: (b, c, 0)),
            pl.BlockSpec((LANES, wk), lambda b, c: (0, 0)),
            pl.BlockSpec((1, wk), lambda b, c: (0, 0)),
            pl.BlockSpec((1, wv), lambda b, c: (0, 0)),
        ],
        out_specs=pl.BlockSpec((None, chunk, wv), lambda b, c: (b, c, 0)),
        out_shape=jax.ShapeDtypeStruct((bsz, s, wv), BF16),
        scratch_shapes=[pltpu.VMEM((heads, dv, dk), F32)],
        compiler_params=_params("parallel", "arbitrary"),
        name="gla",
    )(y3, y3, y3, y3, a13, w_a2, b_a, gla_g)


def _out_ln_kernel(*refs, n_in, alpha):
    y_refs, w_refs = refs[:n_in], refs[n_in:2 * n_in]
    x_ref, g_ref, b_ref, o_ref = refs[2 * n_in:]
    acc = jnp.dot(y_refs[0][...], w_refs[0][...], preferred_element_type=F32)
    for y_ref, w_ref in zip(y_refs[1:], w_refs[1:]):
        acc = acc + jnp.dot(y_ref[...], w_ref[...], preferred_element_type=F32)
    o_ref[...] = _layer_norm(alpha * x_ref[...] + acc, g_ref[...], b_ref[...])


def _out_ln(ys, ws, x, g, b, *, alpha, tm):
    t, d = x.shape
    n_in = len(ys)
    in_specs = ([pl.BlockSpec((tm, y.shape[1]), lambda i: (i, 0)) for y in ys]
                + [pl.BlockSpec(w.shape, lambda i: (0, 0)) for w in ws]
                + [pl.BlockSpec((tm, d), lambda i: (i, 0)),
                   pl.BlockSpec((1, d), lambda i: (0, 0)),
                   pl.BlockSpec((1, d), lambda i: (0, 0))])
    return pl.pallas_call(
        functools.partial(_out_ln_kernel, n_in=n_in, alpha=alpha),
        grid=(t // tm,),
        in_specs=in_specs,
        out_specs=pl.BlockSpec((tm, d), lambda i: (i, 0)),
        out_shape=jax.ShapeDtypeStruct((t, d), F32),
        compiler_params=_params("parallel"),
        name="out_ln",
    )(*ys, *ws, x, g, b)


def _ple_kernel(x_ref, p_ref, wp_ref, wg_ref, o_ref):
    x = x_ref[...]
    e = jnp.dot(p_ref[...].astype(BF16), wp_ref[...], preferred_element_type=F32)
    gate = jnp.dot(x.astype(BF16), wg_ref[...], preferred_element_type=F32)
    o_ref[...] = x + e * _sigmoid(gate)


def _ple(x, p, w_proj, w_gate, *, tm):
    t, d = x.shape
    pd = p.shape[1]
    return pl.pallas_call(
        _ple_kernel,
        grid=(t // tm,),
        in_specs=[
            pl.BlockSpec((tm, d), lambda i: (i, 0)),
            pl.BlockSpec((tm, pd), lambda i: (i, 0)),
            pl.BlockSpec((pd, d), lambda i: (0, 0)),
            pl.BlockSpec((d, d), lambda i: (0, 0)),
        ],
        out_specs=pl.BlockSpec((tm, d), lambda i: (i, 0)),
        out_shape=jax.ShapeDtypeStruct((t, d), F32),
        compiler_params=_params("parallel"),
        name="ple",
    )(x, p, w_proj, w_gate)


def _tiles(t, s):
    return dict(
        ffn_tm=min(512, t), ffn_tf=512,
        proj_tm=min(1024, t), proj_tn=1024,
        row_tm=min(512, t),
        attn_t=min(256, s),
        mlstm_chunk=min(128, s),
        gla_chunk=min(64, s),
    )


def _pad_cols(w, n):
    return jnp.pad(w, ((0, 0), (0, n - w.shape[1])))


def kernel(x, p, ln_g, ln_b, w_ffn_in, w_ffn_out, w_in_ab, w_out_ab, rel_bias, lambda_q1, lambda_k1,
           lambda_q2, lambda_k2, diff_norm, conv_w, conv_b, b_igate, b_fgate, mlstm_norm, w_in_c,
           w_alpha2, b_alpha, gla_norm, w_out_c, w_ple_proj, w_ple_gate):
    bsz, s, d = x.shape
    depth = p.shape[0]
    t = bsz * s
    tl = _tiles(t, s)
    alpha = (2 * depth) ** 0.25
    a_w = A_HEADS * A_V_DIM
    ab_main = 3 * a_w + 3 * B_HEADS * B_V_DIM
    c_main = 2 * C_HEADS * C_QK_DIM + 2 * C_HEADS * C_V_DIM

    xf = x.reshape(t, d)
    for i in range(depth):
        row = lambda a: a.reshape(1, -1)
        xf = _ffn_ln(xf, w_ffn_in[i, 0].astype(BF16), w_ffn_out[i, 0].astype(BF16), row(ln_g[i, 0]),
                     row(ln_b[i, 0]), alpha=alpha, tm=tl["ffn_tm"], tf=tl["ffn_tf"])
        if i % 2 == 0:
            e = i // 2
            lam_init = 0.8 - 0.6 * math.exp(-0.3 * i)
            w = w_in_ab[e]
            w_gates = w[:, ab_main:]
            y, gcol, grow = _in_proj(xf, w[:, :ab_main].astype(BF16), _pad_cols(w_gates, LANES).astype(BF16),
                                     jnp.transpose(w_gates).astype(BF16), tm=tl["proj_tm"], tn=tl["proj_tn"])
            y3 = y.reshape(bsz, s, ab_main)
            lam_vecs = jnp.stack([lambda_q1[e], lambda_k1[e], lambda_q2[e], lambda_k2[e]])
            ya = _diff_attention(y3, rel_bias, lam_vecs, row(diff_norm[e]), lam_init=lam_init, t=tl["attn_t"])
            gate_bias = jnp.concatenate([b_igate[e], b_fgate[e]])
            yb = _mlstm(y3, gcol.reshape(bsz, s, LANES), grow,
                        conv_w[e], row(conv_b[e]), _pad_cols(row(gate_bias), LANES), gate_bias.reshape(-1, 1),
                        row(mlstm_norm[e]), chunk=tl["mlstm_chunk"])
            wo = w_out_ab[e].astype(BF16)
            ys, ws = [ya.reshape(t, a_w), yb.reshape(t, -1)], [wo[:a_w], wo[a_w:]]
        else:
            o = i // 2
            w = w_in_c[o]
            w_a1 = w[:, c_main:]
            y, a1, _ = _in_proj(xf, w[:, :c_main].astype(BF16), _pad_cols(w_a1, LANES).astype(BF16),
                                jnp.transpose(w_a1[:, :SUBLANES]).astype(BF16), tm=tl["proj_tm"], tn=tl["proj_tn"])
            w_a2 = jnp.pad(w_alpha2[o], ((0, LANES - w_alpha2.shape[1]), (0, 0)))
            yc = _gla(y.reshape(bsz, s, c_main), a1.reshape(bsz, s, LANES), w_a2, row(b_alpha[o]),
                      row(gla_norm[o]), chunk=tl["gla_chunk"])
            ys, ws = [yc.reshape(t, -1)], [w_out_c[o].astype(BF16)]
        xf = _out_ln(ys, ws, xf, row(ln_g[i, 1]), row(ln_b[i, 1]), alpha=alpha, tm=tl["row_tm"])
        xf = _ffn_ln(xf, w_ffn_in[i, 1].astype(BF16), w_ffn_out[i, 1].astype(BF16), row(ln_g[i, 2]),
                     row(ln_b[i, 2]), alpha=alpha, tm=tl["ffn_tm"], tf=tl["ffn_tf"])
        xf = _ple(xf, p[i].reshape(t, -1), w_ple_proj[i].astype(BF16), w_ple_gate[i].astype(BF16), tm=tl["row_tm"])
    return xf.reshape(bsz, s, d)
```

```python
import functools
import math

import jax
import jax.numpy as jnp
from jax import lax
from jax.experimental import pallas as pl
from jax.experimental.pallas import tpu as pltpu

F32 = jnp.float32
BF16 = jnp.bfloat16
HIGHEST = lax.Precision.HIGHEST

A_HEADS = 8
A_QK_DIM = 64
A_V_DIM = 128
B_HEADS = 4
B_QK_DIM = 128
B_V_DIM = 256
CONV_WIDTH = 4
C_HEADS = 4
C_QK_DIM = 256
C_V_DIM = 512
C_GATE_TEMP = 16.0
N_BUCKETS = 32
MAX_DISTANCE = 128
LN_EPS = 1e-5
NORM_EPS = 1e-6

V7X_VMEM_LIMIT_BYTES = 56 * 1024 * 1024
LANES = 128
SUBLANES = 8
ATTN_ONES_ROWS = 16

NT_DIMS = (((1,), (1,)), ((), ()))
TN_DIMS = (((0,), (0,)), ((), ()))


def _params(*sem):
    return pltpu.CompilerParams(dimension_semantics=sem, vmem_limit_bytes=V7X_VMEM_LIMIT_BYTES)


def _layer_norm(y, g, b):
    mu = jnp.mean(y, -1, keepdims=True)
    yc = y - mu
    var = jnp.mean(yc * yc, -1, keepdims=True)
    return yc * lax.rsqrt(var + LN_EPS) * g + b


def _sigmoid(x):
    return 1.0 / (1.0 + jnp.exp(-x))


def _log_sigmoid(x):
    return jnp.minimum(x, 0.0) - jnp.log(1.0 + jnp.exp(-jnp.abs(x)))


def _ffn_ln_kernel(x_ref, wg_ref, wu_ref, wo_ref, g_ref, b_ref, o_ref, xb_ref, acc_ref, *, alpha):
    j = pl.program_id(1)

    @pl.when(j == 0)
    def _():
        xb_ref[...] = x_ref[...].astype(BF16)
        acc_ref[...] = jnp.zeros_like(acc_ref)

    xb = xb_ref[...]
    gate = jnp.dot(xb, wg_ref[...], preferred_element_type=F32)
    up = jnp.dot(xb, wu_ref[...], preferred_element_type=F32)
    h = (gate * _sigmoid(gate) * up).astype(BF16)
    acc_ref[...] += jnp.dot(h, wo_ref[...], preferred_element_type=F32)

    @pl.when(j == pl.num_programs(1) - 1)
    def _():
        y = alpha * x_ref[...] + 0.5 * acc_ref[...]
        o_ref[...] = _layer_norm(y, g_ref[...], b_ref[...])


def _ffn_ln(x, w_in, w_out, g, b, *, alpha, tm, tf):
    t, d = x.shape
    f = w_out.shape[0]
    nf = f // tf
    assert t % tm == 0 and f % tf == 0 and w_in.shape == (d, 2 * f)
    return pl.pallas_call(
        functools.partial(_ffn_ln_kernel, alpha=alpha),
        grid=(t // tm, nf),
        in_specs=[
            pl.BlockSpec((tm, d), lambda i, j: (i, 0)),
            pl.BlockSpec((d, tf), lambda i, j: (0, j)),
            pl.BlockSpec((d, tf), lambda i, j: (0, nf + j)),
            pl.BlockSpec((tf, d), lambda i, j: (j, 0)),
            pl.BlockSpec((1, d), lambda i, j: (0, 0)),
            pl.BlockSpec((1, d), lambda i, j: (0, 0)),
        ],
        out_specs=pl.BlockSpec((tm, d), lambda i, j: (i, 0)),
        out_shape=jax.ShapeDtypeStruct((t, d), F32),
        scratch_shapes=[pltpu.VMEM((tm, d), BF16), pltpu.VMEM((tm, d), F32)],
        compiler_params=_params("parallel", "arbitrary"),
        name="ffn_ln",
    )(x, w_in, w_in, w_out, g, b)


def _in_proj_kernel(x_ref, w_ref, wgc_ref, wgr_ref, y_ref, gc_ref, gr_ref, xb_ref):
    j = pl.program_id(1)

    @pl.when(j == 0)
    def _():
        xb = x_ref[...].astype(BF16)
        xb_ref[...] = xb
        gc_ref[...] = jnp.dot(xb, wgc_ref[...], preferred_element_type=F32)
        gr_ref[...] = lax.dot_general(wgr_ref[...], xb, NT_DIMS, preferred_element_type=F32)

    y_ref[...] = jnp.dot(xb_ref[...], w_ref[...], preferred_element_type=F32).astype(y_ref.dtype)


def _in_proj(x, w_main, w_gate_cols, w_gate_rows, *, tm, tn):
    t, d = x.shape
    n = w_main.shape[1]
    assert t % tm == 0 and n % tn == 0
    return pl.pallas_call(
        _in_proj_kernel,
        grid=(t // tm, n // tn),
        in_specs=[
            pl.BlockSpec((tm, d), lambda i, j: (i, 0)),
            pl.BlockSpec((d, tn), lambda i, j: (0, j)),
            pl.BlockSpec((d, LANES), lambda i, j: (0, 0)),
            pl.BlockSpec((SUBLANES, d), lambda i, j: (0, 0)),
        ],
        out_specs=[
            pl.BlockSpec((tm, tn), lambda i, j: (i, j)),
            pl.BlockSpec((tm, LANES), lambda i, j: (i, 0)),
            pl.BlockSpec((SUBLANES, tm), lambda i, j: (0, i)),
        ],
        out_shape=[
            jax.ShapeDtypeStruct((t, n), BF16),
            jax.ShapeDtypeStruct((t, LANES), F32),
            jax.ShapeDtypeStruct((SUBLANES, t), F32),
        ],
        scratch_shapes=[pltpu.VMEM((tm, d), BF16)],
        compiler_params=_params("parallel", "arbitrary"),
        name="in_proj",
    )(x, w_main, w_gate_cols, w_gate_rows)


def _diff_attn_kernel(q_ref, k_ref, v_ref, bdiag_ref, bnear_ref, cfar_ref, lam_ref, g_ref, o_ref,
                      vt_ref, m_ref, acc_ref, *, t, lam_init):
    qi = pl.program_id(2)
    dk, dv = A_QK_DIM, A_V_DIM
    nk = k_ref.shape[0] // t

    @pl.when(qi == 0)
    def _():
        for c in range(nk):
            vt_ref[c, 0:dv, :] = jnp.transpose(v_ref[c * t:(c + 1) * t, :].astype(F32)).astype(BF16)
            vt_ref[c, dv:, :] = jnp.ones((vt_ref.shape[1] - dv, t), BF16)

    q_t = jnp.transpose(q_ref[...].astype(F32)) * (dk ** -0.5)
    feat = lax.broadcasted_iota(jnp.int32, q_t.shape, 0)
    qs_t = jnp.concatenate([jnp.where(feat < dk, q_t, 0.0), jnp.where(feat >= dk, q_t, 0.0)],
                           axis=1).astype(BF16)

    m_ref[...] = jnp.full_like(m_ref, -jnp.inf)
    acc_ref[...] = jnp.zeros_like(acc_ref)

    def tile(kj, bias, shift):
        off = pl.multiple_of(kj * t, t)
        s = jnp.dot(k_ref[pl.ds(off, t), :], qs_t, preferred_element_type=F32)
        if bias is not None:
            b = bias[...]
            s = s + jnp.concatenate([b, b], axis=1)
        m_old = m_ref[...]
        m_new = jnp.maximum(m_old, jnp.max(s, 0, keepdims=True) + shift)
        p = jnp.exp(s - (m_new - shift)).astype(BF16)
        acc_ref[...] = (jnp.exp(m_old - m_new) * acc_ref[...]
                        + jnp.dot(vt_ref[kj], p, preferred_element_type=F32))
        m_ref[...] = m_new

    c_far = cfar_ref[...][:, 0:1]

    def far_body(kj, carry):
        tile(kj, None, c_far)
        return carry

    lax.fori_loop(0, jnp.maximum(qi - 1, 0), far_body, 0)

    @pl.when(qi >= 1)
    def _():
        tile(qi - 1, bnear_ref, 0.0)

    tile(qi, bdiag_ref, 0.0)

    lv = lam_ref[...]
    lam = (jnp.exp(jnp.sum(lv[0:1] * lv[1:2], -1, keepdims=True))
           - jnp.exp(jnp.sum(lv[2:3] * lv[3:4], -1, keepdims=True)) + lam_init)
    acc = acc_ref[...]
    o_t = acc[0:dv] * (1.0 / acc[dv:dv + 1])
    out = jnp.transpose(o_t[:, :t] - lam * o_t[:, t:])
    hn = out * lax.rsqrt(jnp.mean(out * out, -1, keepdims=True) + NORM_EPS)
    o_ref[...] = (hn * g_ref[...] * (1.0 - lam_init)).astype(o_ref.dtype)


def _t5_bias_by_distance(rel_bias, n):
    r = jnp.arange(n, dtype=jnp.int32)
    max_exact = N_BUCKETS // 2
    rf = jnp.maximum(r, 1).astype(F32)
    large = max_exact + (jnp.log(rf / max_exact) / math.log(MAX_DISTANCE / max_exact)
                         * (N_BUCKETS - max_exact)).astype(jnp.int32)
    large = jnp.minimum(large, N_BUCKETS - 1)
    bucket = jnp.where(r < max_exact, r, large)
    return jnp.transpose(rel_bias[bucket]).astype(F32)


def _diff_attention(y3, rel_bias, lam_vecs, diff_g, *, lam_init, t):
    bsz, s, _ = y3.shape
    hd = 2 * A_QK_DIM
    dv = A_V_DIM
    assert hd == dv == LANES and s % t == 0 and t >= MAX_DISTANCE
    nq = s // t
    table = _t5_bias_by_distance(rel_bias, 2 * t + 1)
    skew = jnp.broadcast_to(table[:, None, :], (A_HEADS, t, 2 * t + 1)).reshape(A_HEADS, -1)
    skew = skew[:, :t * 2 * t].reshape(A_HEADS, t, 2 * t)
    j = jnp.arange(t)[:, None]
    i = jnp.arange(t)[None, :]
    bdiag = jnp.where((i >= j)[None], skew[:, :, :t], -1e30)
    bnear = skew[:, :, t:]
    cfar = jnp.broadcast_to(rel_bias[N_BUCKETS - 1][:, None, None], (A_HEADS, 1, LANES)).astype(F32)
    return pl.pallas_call(
        functools.partial(_diff_attn_kernel, t=t, lam_init=lam_init),
        grid=(bsz, A_HEADS, nq),
        in_specs=[
            pl.BlockSpec((None, t, hd), lambda b, h, q: (b, q, h)),
            pl.BlockSpec((None, s, hd), lambda b, h, q: (b, 0, A_HEADS + h)),
            pl.BlockSpec((None, s, dv), lambda b, h, q: (b, 0, 2 * A_HEADS + h)),
            pl.BlockSpec((None, t, t), lambda b, h, q: (h, 0, 0)),
            pl.BlockSpec((None, t, t), lambda b, h, q: (h, 0, 0)),
            pl.BlockSpec((None, 1, LANES), lambda b, h, q: (h, 0, 0)),
            pl.BlockSpec((4, A_QK_DIM), lambda b, h, q: (0, 0)),
            pl.BlockSpec((1, dv), lambda b, h, q: (0, h)),
        ],
        out_specs=pl.BlockSpec((None, t, dv), lambda b, h, q: (b, q, h)),
        out_shape=jax.ShapeDtypeStruct((bsz, s, A_HEADS * dv), BF16),
        scratch_shapes=[pltpu.VMEM((nq, dv + ATTN_ONES_ROWS, t), BF16),
                        pltpu.VMEM((1, 2 * t), F32),
                        pltpu.VMEM((dv + ATTN_ONES_ROWS, 2 * t), F32)],
        compiler_params=_params("parallel", "parallel", "arbitrary"),
        name="diff_attn",
    )(y3, y3, y3, bdiag, bnear, cfar, lam_vecs, diff_g)


def _mlstm_kernel(qk_ref, v_ref, ob_ref, gc_ref, gr_ref, cw_ref, cb_ref, bc_ref, br_ref, g_ref, y_ref,
                  xp_ref, c_ref, n_ref, m_ref, *, chunk):
    heads, dk, dv = B_HEADS, B_QK_DIM, B_V_DIM
    pad = SUBLANES
    c = pl.program_id(1)

    @pl.when(c == 0)
    def _():
        xp_ref[0:pad, :] = jnp.zeros((pad, xp_ref.shape[1]), F32)
        c_ref[...] = jnp.zeros_like(c_ref)
        n_ref[...] = jnp.zeros_like(n_ref)
        m_ref[...] = jnp.zeros_like(m_ref)

    x = qk_ref[...].astype(F32)
    xp_ref[pad:pad + chunk, :] = x
    cw = cw_ref[...]
    conv = cb_ref[...]
    for j in range(CONV_WIDTH):
        conv = conv + cw[j:j + 1, :] * xp_ref[pl.ds(pad - (CONV_WIDTH - 1) + j, chunk), :]
    xp_ref[0:pad, :] = x[chunk - pad:chunk, :]
    qk = conv * _sigmoid(conv)

    gc = gc_ref[...] + bc_ref[...]
    gr = gr_ref[...] + br_ref[...]
    row = lax.broadcasted_iota(jnp.int32, (chunk, chunk), 0)
    col = lax.broadcasted_iota(jnp.int32, (chunk, chunk), 1)
    causal = row >= col
    tril = causal.astype(F32)
    triu = (row <= col).astype(F32)
    b_c = jnp.dot(tril, _log_sigmoid(gc), precision=HIGHEST, preferred_element_type=F32)
    b_r = jnp.dot(_log_sigmoid(gr), triu, precision=HIGHEST, preferred_element_type=F32)

    ob = ob_ref[...].astype(F32)
    g = g_ref[...]
    for h in range(heads):
        q = qk[:, h * dk:(h + 1) * dk] * (dk ** -0.5)
        k = qk[:, (heads + h) * dk:(heads + h + 1) * dk]
        v = v_ref[:, h * dv:(h + 1) * dv]
        b_col = b_c[:, heads + h:heads + h + 1]
        i_col = gc[:, h:h + 1]
        b_row = b_r[heads + h:heads + h + 1, :]
        i_row = gr[h:h + 1, :]
        m_prev = m_ref[h][0:1, 0:1]

        d = jnp.where(causal, b_col - b_row + i_row, -jnp.inf)
        a_col = b_col + m_prev
        m_t = jnp.maximum(a_col, jnp.max(d, -1, keepdims=True))
        w_inter = jnp.exp(a_col - m_t)
        qb = q.astype(BF16)
        sw = lax.dot_general(qb, k.astype(BF16), NT_DIMS, preferred_element_type=F32) * jnp.exp(d - m_t)
        c_mat = c_ref[h]
        num = (w_inter * jnp.dot(qb, c_mat.astype(BF16), preferred_element_type=F32)
               + jnp.dot(sw.astype(BF16), v, preferred_element_type=F32))
        den = w_inter * jnp.sum(q * n_ref[h], -1, keepdims=True) + jnp.sum(sw, -1, keepdims=True)
        hh = num / jnp.maximum(jnp.abs(den), jnp.exp(-m_t))

        b_last = b_row[:, chunk - 1:chunk]
        g_col = b_last - b_col + i_col
        m_new = jnp.maximum(b_last + m_prev, jnp.max(g_col, 0, keepdims=True))
        decay = jnp.exp(b_last + m_prev - m_new)
        kw = k * jnp.exp(g_col - m_new)
        c_ref[h] = decay * c_mat + lax.dot_general(kw.astype(BF16), v, TN_DIMS, preferred_element_type=F32)
        n_ref[h] = decay * n_ref[h] + jnp.sum(kw, 0, keepdims=True)
        m_ref[h] = jnp.broadcast_to(m_new, m_ref.shape[1:])

        hn = hh * lax.rsqrt(jnp.mean(hh * hh, -1, keepdims=True) + NORM_EPS)
        sl = slice(h * dv, (h + 1) * dv)
        y_ref[:, sl] = (hn * g[:, sl] * _sigmoid(ob[:, sl])).astype(y_ref.dtype)


def _mlstm(y3, gcol3, grow3, conv_w, conv_b, gate_bias_cols, gate_bias_rows, mlstm_g, *, chunk):
    bsz, s, _ = y3.shape
    heads, dk, dv = B_HEADS, B_QK_DIM, B_V_DIM
    w = heads * dv
    assert 2 * heads * dk == w and s % chunk == 0 and chunk % LANES == 0
    nc = s // chunk
    return pl.pallas_call(
        functools.partial(_mlstm_kernel, chunk=chunk),
        grid=(bsz, nc),
        in_specs=[
            pl.BlockSpec((None, chunk, w), lambda b, c: (b, c, 3)),
            pl.BlockSpec((None, chunk, w), lambda b, c: (b, c, 4)),
            pl.BlockSpec((None, chunk, w), lambda b, c: (b, c, 5)),
            pl.BlockSpec((None, chunk, LANES), lambda b, c: (b, c, 0)),
            pl.BlockSpec((SUBLANES, chunk), lambda b, c: (0, b * nc + c)),
            pl.BlockSpec((CONV_WIDTH, w), lambda b, c: (0, 0)),
            pl.BlockSpec((1, w), lambda b, c: (0, 0)),
            pl.BlockSpec((1, LANES), lambda b, c: (0, 0)),
            pl.BlockSpec((SUBLANES, 1), lambda b, c: (0, 0)),
            pl.BlockSpec((1, w), lambda b, c: (0, 0)),
        ],
        out_specs=pl.BlockSpec((None, chunk, w), lambda b, c: (b, c, 0)),
        out_shape=jax.ShapeDtypeStruct((bsz, s, w), BF16),
        scratch_shapes=[
            pltpu.VMEM((SUBLANES + chunk, w), F32),
            pltpu.VMEM((heads, dk, dv), F32),
            pltpu.VMEM((heads, 1, dk), F32),
            pltpu.VMEM((heads, SUBLANES, LANES), F32),
        ],
        compiler_params=_params("parallel", "arbitrary"),
        name="mlstm",
    )(y3, y3, y3, gcol3, grow3, conv_w, conv_b, gate_bias_cols, gate_bias_rows, mlstm_g)


def _gla_kernel(q_ref, k_ref, v_ref, r_ref, a1_ref, wa_ref, ba_ref, g_ref, y_ref, st_ref, *, chunk):
    heads, dk, dv = C_HEADS, C_QK_DIM, C_V_DIM
    c = pl.program_id(1)

    @pl.when(c == 0)
    def _():
        st_ref[...] = jnp.zeros_like(st_ref)

    z = jnp.dot(a1_ref[...], wa_ref[...], precision=HIGHEST, preferred_element_type=F32) + ba_ref[...]
    log_a = _log_sigmoid(z) * (1.0 / C_GATE_TEMP)
    row = lax.broadcasted_iota(jnp.int32, (chunk, chunk), 0)
    col = lax.broadcasted_iota(jnp.int32, (chunk, chunk), 1)
    causal = row >= col
    b_all = jnp.dot(causal.astype(F32), log_a, precision=HIGHEST, preferred_element_type=F32)

    r = r_ref[...].astype(F32)
    g = g_ref[...]
    for h in range(heads):
        ks = slice(h * dk, (h + 1) * dk)
        vs = slice(h * dv, (h + 1) * dv)
        b = b_all[:, ks]
        q = q_ref[:, ks].astype(F32) * (dk ** -0.5)
        k = k_ref[:, ks].astype(F32)
        v = v_ref[:, vs]
        q_dec = (q * jnp.exp(b)).astype(BF16)
        k_dec = (k * jnp.exp(-b)).astype(BF16)
        att = jnp.where(causal, lax.dot_general(q_dec, k_dec, NT_DIMS, preferred_element_type=F32), 0.0)
        st = st_ref[h]
        o = (jnp.dot(att.astype(BF16), v, preferred_element_type=F32)
             + lax.dot_general(q_dec, st.astype(BF16), NT_DIMS, preferred_element_type=F32))
        b_last = b[chunk - 1:chunk, :]
        k_upd = (k * jnp.exp(b_last - b)).astype(BF16)
        st_ref[h] = jnp.exp(b_last) * st + lax.dot_general(v, k_upd, TN_DIMS, preferred_element_type=F32)

        on = o * lax.rsqrt(jnp.mean(o * o, -1, keepdims=True) + NORM_EPS)
        rr = r[:, vs]
        y_ref[:, vs] = (on * g[:, vs] * (rr * _sigmoid(rr))).astype(y_ref.dtype)


def _gla(y3, a13, w_a2, b_a, gla_g, *, chunk):
    bsz, s, _ = y3.shape
    heads, dk, dv = C_HEADS, C_QK_DIM, C_V_DIM
    wk, wv = heads * dk, heads * dv
    assert wv == 2 * wk and s % chunk == 0
    nc = s // chunk
    return pl.pallas_call(
        functools.partial(_gla_kernel, chunk=chunk),
        grid=(bsz, nc),
        in_specs=[
            pl.BlockSpec((None, chunk, wk), lambda b, c: (b, c, 0)),
            pl.BlockSpec((None, chunk, wk), lambda b, c: (b, c, 1)),
            pl.BlockSpec((None, chunk, wv), lambda b, c: (b, c, 1)),
            pl.BlockSpec((None, chunk, wv), lambda b, c: (b, c, 2)),
            pl.BlockSpec((None, chunk, LANES), lambda b, c: (b, c, 0)),
            pl.BlockSpec((LANES, wk), lambda b, c: (0, 0)),
            pl.BlockSpec((1, wk), lambda b, c: (0, 0)),
            pl.BlockSpec((1, wv), lambda b, c: (0, 0)),
        ],
        out_specs=pl.BlockSpec((None, chunk, wv), lambda b, c: (b, c, 0)),
        out_shape=jax.ShapeDtypeStruct((bsz, s, wv), BF16),
        scratch_shapes=[pltpu.VMEM((heads, dv, dk), F32)],
        compiler_params=_params("parallel", "arbitrary"),
        name="gla",
    )(y3, y3, y3, y3, a13, w_a2, b_a, gla_g)


def _out_ln_kernel(*refs, n_in, alpha):
    y_refs, w_refs = refs[:n_in], refs[n_in:2 * n_in]
    x_ref, g_ref, b_ref, o_ref = refs[2 * n_in:]
    acc = jnp.dot(y_refs[0][...], w_refs[0][...], preferred_element_type=F32)
    for y_ref, w_ref in zip(y_refs[1:], w_refs[1:]):
        acc = acc + jnp.dot(y_ref[...], w_ref[...], preferred_element_type=F32)
    o_ref[...] = _layer_norm(alpha * x_ref[...] + acc, g_ref[...], b_ref[...])


def _out_ln(ys, ws, x, g, b, *, alpha, tm):
    t, d = x.shape
    n_in = len(ys)
    in_specs = ([pl.BlockSpec((tm, y.shape[1]), lambda i: (i, 0)) for y in ys]
                + [pl.BlockSpec(w.shape, lambda i: (0, 0)) for w in ws]
                + [pl.BlockSpec((tm, d), lambda i: (i, 0)),
                   pl.BlockSpec((1, d), lambda i: (0, 0)),
                   pl.BlockSpec((1, d), lambda i: (0, 0))])
    return pl.pallas_call(
        functools.partial(_out_ln_kernel, n_in=n_in, alpha=alpha),
        grid=(t // tm,),
        in_specs=in_specs,
        out_specs=pl.BlockSpec((tm, d), lambda i: (i, 0)),
        out_shape=jax.ShapeDtypeStruct((t, d), F32),
        compiler_params=_params("parallel"),
        name="out_ln",
    )(*ys, *ws, x, g, b)


def _ple_kernel(x_ref, p_ref, wp_ref, wg_ref, o_ref):
    x = x_ref[...]
    e = jnp.dot(p_ref[...].astype(BF16), wp_ref[...], preferred_element_type=F32)
    gate = jnp.dot(x.astype(BF16), wg_ref[...], preferred_element_type=F32)
    o_ref[...] = x + e * _sigmoid(gate)


def _ple(x, p, w_proj, w_gate, *, tm):
    t, d = x.shape
    pd = p.shape[1]
    return pl.pallas_call(
        _ple_kernel,
        grid=(t // tm,),
        in_specs=[
            pl.BlockSpec((tm, d), lambda i: (i, 0)),
            pl.BlockSpec((tm, pd), lambda i: (i, 0)),
            pl.BlockSpec((pd, d), lambda i: (0, 0)),
            pl.BlockSpec((d, d), lambda i: (0, 0)),
        ],
        out_specs=pl.BlockSpec((tm, d), lambda i: (i, 0)),
        out_shape=jax.ShapeDtypeStruct((t, d), F32),
        compiler_params=_params("parallel"),
        name="ple",
    )(x, p, w_proj, w_gate)


def _tiles(t, s):
    return dict(
        ffn_tm=min(512, t), ffn_tf=512,
        proj_tm=min(1024, t), proj_tn=1024,
        row_tm=min(512, t),
        attn_t=min(512, s),
        mlstm_chunk=min(128, s),
        gla_chunk=min(64, s),
    )


def _pad_cols(w, n):
    return jnp.pad(w, ((0, 0), (0, n - w.shape[1])))


def kernel(x, p, ln_g, ln_b, w_ffn_in, w_ffn_out, w_in_ab, w_out_ab, rel_bias, lambda_q1, lambda_k1,
           lambda_q2, lambda_k2, diff_norm, conv_w, conv_b, b_igate, b_fgate, mlstm_norm, w_in_c,
           w_alpha2, b_alpha, gla_norm, w_out_c, w_ple_proj, w_ple_gate):
    bsz, s, d = x.shape
    depth = p.shape[0]
    t = bsz * s
    tl = _tiles(t, s)
    alpha = (2 * depth) ** 0.25
    a_w = A_HEADS * A_V_DIM
    ab_main = 3 * a_w + 3 * B_HEADS * B_V_DIM
    c_main = 2 * C_HEADS * C_QK_DIM + 2 * C_HEADS * C_V_DIM

    xf = x.reshape(t, d)
    for i in range(depth):
        row = lambda a: a.reshape(1, -1)
        xf = _ffn_ln(xf, w_ffn_in[i, 0].astype(BF16), w_ffn_out[i, 0].astype(BF16), row(ln_g[i, 0]),
                     row(ln_b[i, 0]), alpha=alpha, tm=tl["ffn_tm"], tf=tl["ffn_tf"])
        if i % 2 == 0:
            e = i // 2
            lam_init = 0.8 - 0.6 * math.exp(-0.3 * i)
            w = w_in_ab[e]
            w_gates = w[:, ab_main:]
            y, gcol, grow = _in_proj(xf, w[:, :ab_main].astype(BF16), _pad_cols(w_gates, LANES).astype(BF16),
                                     jnp.transpose(w_gates).astype(BF16), tm=tl["proj_tm"], tn=tl["proj_tn"])
            y3 = y.reshape(bsz, s, ab_main)
            lam_vecs = jnp.stack([lambda_q1[e], lambda_k1[e], lambda_q2[e], lambda_k2[e]])
            ya = _diff_attention(y3, rel_bias, lam_vecs, row(diff_norm[e]), lam_init=lam_init, t=tl["attn_t"])
            gate_bias = jnp.concatenate([b_igate[e], b_fgate[e]])
            yb = _mlstm(y3, gcol.reshape(bsz, s, LANES), grow,
                        conv_w[e], row(conv_b[e]), _pad_cols(row(gate_bias), LANES), gate_bias.reshape(-1, 1),
                        row(mlstm_norm[e]), chunk=tl["mlstm_chunk"])
            wo = w_out_ab[e].astype(BF16)
            ys, ws = [ya.reshape(t, a_w), yb.reshape(t, -1)], [wo[:a_w], wo[a_w:]]
        else:
            o = i // 2
            w = w_in_c[o]
            w_a1 = w[:, c_main:]
            y, a1, _ = _in_proj(xf, w[:, :c_main].astype(BF16), _pad_cols(w_a1, LANES).astype(BF16),
                                jnp.transpose(w_a1[:, :SUBLANES]).astype(BF16), tm=tl["proj_tm"], tn=tl["proj_tn"])
            w_a2 = jnp.pad(w_alpha2[o], ((0, LANES - w_alpha2.shape[1]), (0, 0)))
            yc = _gla(y.reshape(bsz, s, c_main), a1.reshape(bsz, s, LANES), w_a2, row(b_alpha[o]),
                      row(gla_norm[o]), chunk=tl["gla_chunk"])
            ys, ws = [yc.reshape(t, -1)], [w_out_c[o].astype(BF16)]
        xf = _out_ln(ys, ws, xf, row(ln_g[i, 1]), row(ln_b[i, 1]), alpha=alpha, tm=tl["row_tm"])
        xf = _ffn_ln(xf, w_ffn_in[i, 1].astype(BF16), w_ffn_out[i, 1].astype(BF16), row(ln_g[i, 2]),
                     row(ln_b[i, 2]), alpha=alpha, tm=tl["ffn_tm"], tf=tl["ffn_tf"])
        xf = _ple(xf, p[i].reshape(t, -1), w_ple_proj[i].astype(BF16), w_ple_gate[i].astype(BF16), tm=tl["row_tm"])
    return xf.reshape(bsz, s, d)
```

```python
import functools
import math

import jax
import jax.numpy as jnp
from jax import lax
from jax.experimental import pallas as pl
from jax.experimental.pallas import tpu as pltpu

F32 = jnp.float32
BF16 = jnp.bfloat16
HIGHEST = lax.Precision.HIGHEST

A_HEADS = 8
A_QK_DIM = 64
A_V_DIM = 128
B_HEADS = 4
B_QK_DIM = 128
B_V_DIM = 256
CONV_WIDTH = 4
C_HEADS = 4
C_QK_DIM = 256
C_V_DIM = 512
C_GATE_TEMP = 16.0
N_BUCKETS = 32
MAX_DISTANCE = 128
LN_EPS = 1e-5
NORM_EPS = 1e-6

V7X_VMEM_LIMIT_BYTES = 56 * 1024 * 1024
LANES = 128
SUBLANES = 8
ATTN_ONES_ROWS = 16

NT_DIMS = (((1,), (1,)), ((), ()))
TN_DIMS = (((0,), (0,)), ((), ()))


def _params(*sem):
    return pltpu.CompilerParams(dimension_semantics=sem, vmem_limit_bytes=V7X_VMEM_LIMIT_BYTES)


def _layer_norm(y, g, b):
    mu = jnp.mean(y, -1, keepdims=True)
    yc = y - mu
    var = jnp.mean(yc * yc, -1, keepdims=True)
    return yc * lax.rsqrt(var + LN_EPS) * g + b


def _sigmoid(x):
    return 1.0 / (1.0 + jnp.exp(-x))


def _log_sigmoid(x):
    return jnp.minimum(x, 0.0) - jnp.log(1.0 + jnp.exp(-jnp.abs(x)))


def _ffn_ln_kernel(x_ref, wg_ref, wu_ref, wo_ref, g_ref, b_ref, o_ref, xb_ref, acc_ref, *, alpha):
    j = pl.program_id(1)

    @pl.when(j == 0)
    def _():
        xb_ref[...] = x_ref[...].astype(BF16)
        acc_ref[...] = jnp.zeros_like(acc_ref)

    xb = xb_ref[...]
    gate = jnp.dot(xb, wg_ref[...], preferred_element_type=F32)
    up = jnp.dot(xb, wu_ref[...], preferred_element_type=F32)
    h = (gate * _sigmoid(gate) * up).astype(BF16)
    acc_ref[...] += jnp.dot(h, wo_ref[...], preferred_element_type=F32)

    @pl.when(j == pl.num_programs(1) - 1)
    def _():
        y = alpha * x_ref[...] + 0.5 * acc_ref[...]
        o_ref[...] = _layer_norm(y, g_ref[...], b_ref[...])


def _ffn_ln(x, w_in, w_out, g, b, *, alpha, tm, tf):
    t, d = x.shape
    f = w_out.shape[0]
    nf = f // tf
    assert t % tm == 0 and f % tf == 0 and w_in.shape == (d, 2 * f)
    return pl.pallas_call(
        functools.partial(_ffn_ln_kernel, alpha=alpha),
        grid=(t // tm, nf),
        in_specs=[
            pl.BlockSpec((tm, d), lambda i, j: (i, 0)),
            pl.BlockSpec((d, tf), lambda i, j: (0, j)),
            pl.BlockSpec((d, tf), lambda i, j: (0, nf + j)),
            pl.BlockSpec((tf, d), lambda i, j: (j, 0)),
            pl.BlockSpec((1, d), lambda i, j: (0, 0)),
            pl.BlockSpec((1, d), lambda i, j: (0, 0)),
        ],
        out_specs=pl.BlockSpec((tm, d), lambda i, j: (i, 0)),
        out_shape=jax.ShapeDtypeStruct((t, d), F32),
        scratch_shapes=[pltpu.VMEM((tm, d), BF16), pltpu.VMEM((tm, d), F32)],
        compiler_params=_params("parallel", "arbitrary"),
        name="ffn_ln",
    )(x, w_in, w_in, w_out, g, b)


def _in_proj_kernel(x_ref, w_ref, wgc_ref, wgr_ref, y_ref, gc_ref, gr_ref, xb_ref):
    j = pl.program_id(1)

    @pl.when(j == 0)
    def _():
        xb = x_ref[...].astype(BF16)
        xb_ref[...] = xb
        gc_ref[...] = jnp.dot(xb, wgc_ref[...], preferred_element_type=F32)
        gr_ref[...] = lax.dot_general(wgr_ref[...], xb, NT_DIMS, preferred_element_type=F32)

    y_ref[...] = jnp.dot(xb_ref[...], w_ref[...], preferred_element_type=F32).astype(y_ref.dtype)


def _in_proj(x, w_main, w_gate_cols, w_gate_rows, *, tm, tn):
    t, d = x.shape
    n = w_main.shape[1]
    assert t % tm == 0 and n % tn == 0
    return pl.pallas_call(
        _in_proj_kernel,
        grid=(t // tm, n // tn),
        in_specs=[
            pl.BlockSpec((tm, d), lambda i, j: (i, 0)),
            pl.BlockSpec((d, tn), lambda i, j: (0, j)),
            pl.BlockSpec((d, LANES), lambda i, j: (0, 0)),
            pl.BlockSpec((SUBLANES, d), lambda i, j: (0, 0)),
        ],
        out_specs=[
            pl.BlockSpec((tm, tn), lambda i, j: (i, j)),
            pl.BlockSpec((tm, LANES), lambda i, j: (i, 0)),
            pl.BlockSpec((SUBLANES, tm), lambda i, j: (0, i)),
        ],
        out_shape=[
            jax.ShapeDtypeStruct((t, n), BF16),
            jax.ShapeDtypeStruct((t, LANES), F32),
            jax.ShapeDtypeStruct((SUBLANES, t), F32),
        ],
        scratch_shapes=[pltpu.VMEM((tm, d), BF16)],
        compiler_params=_params("parallel", "arbitrary"),
        name="in_proj",
    )(x, w_main, w_gate_cols, w_gate_rows)


def _diff_attn_kernel(q_ref, k_ref, v_ref, tab_ref, lam_ref, g_ref, o_ref,
                      vt_ref, bdiag_ref, bnear_ref, m_ref, acc_ref, *, t, lam_init):
    qi = pl.program_id(2)
    dk, dv = A_QK_DIM, A_V_DIM
    nk = k_ref.shape[0] // t

    @pl.when(qi == 0)
    def _():
        for c in range(nk):
            vt_ref[c, 0:dv, :] = jnp.transpose(v_ref[c * t:(c + 1) * t, :].astype(F32)).astype(BF16)
            vt_ref[c, dv:, :] = jnp.ones((vt_ref.shape[1] - dv, t), BF16)
        skew = pltpu.roll(jnp.broadcast_to(tab_ref[...], (t, 2 * t)), 0, 1, stride=1, stride_axis=0)
        key = lax.broadcasted_iota(jnp.int32, (t, t), 0)
        qry = lax.broadcasted_iota(jnp.int32, (t, t), 1)
        bdiag_ref[...] = jnp.where(qry >= key, skew[:, :t], -1e30)
        bnear_ref[...] = skew[:, t:]

    q_t = jnp.transpose(q_ref[...].astype(F32)) * (dk ** -0.5)
    feat = lax.broadcasted_iota(jnp.int32, q_t.shape, 0)
    qs_t = jnp.concatenate([jnp.where(feat < dk, q_t, 0.0), jnp.where(feat >= dk, q_t, 0.0)],
                           axis=1).astype(BF16)

    m_ref[...] = jnp.full_like(m_ref, -jnp.inf)
    acc_ref[...] = jnp.zeros_like(acc_ref)

    def tile(kj, bias, shift):
        off = pl.multiple_of(kj * t, t)
        s = jnp.dot(k_ref[pl.ds(off, t), :], qs_t, preferred_element_type=F32)
        if bias is not None:
            b = bias[...]
            s = s + jnp.concatenate([b, b], axis=1)
        m_old = m_ref[...]
        m_new = jnp.maximum(m_old, jnp.max(s, 0, keepdims=True) + shift)
        p = jnp.exp(s - (m_new - shift)).astype(BF16)
        acc_ref[...] = (jnp.exp(m_old - m_new) * acc_ref[...]
                        + jnp.dot(vt_ref[kj], p, preferred_element_type=F32))
        m_ref[...] = m_new

    c_far = tab_ref[:, 2 * t - 1:2 * t]

    def far_body(kj, carry):
        tile(kj, None, c_far)
        return carry

    lax.fori_loop(0, jnp.maximum(qi - 1, 0), far_body, 0)

    @pl.when(qi >= 1)
    def _():
        tile(qi - 1, bnear_ref, 0.0)

    tile(qi, bdiag_ref, 0.0)

    lv = lam_ref[...]
    lam = (jnp.exp(jnp.sum(lv[0:1] * lv[1:2], -1, keepdims=True))
           - jnp.exp(jnp.sum(lv[2:3] * lv[3:4], -1, keepdims=True)) + lam_init)
    acc = acc_ref[...]
    o_t = acc[0:dv] * (1.0 / acc[dv:dv + 1])
    out = jnp.transpose(o_t[:, :t] - lam * o_t[:, t:])
    hn = out * lax.rsqrt(jnp.mean(out * out, -1, keepdims=True) + NORM_EPS)
    o_ref[...] = (hn * g_ref[...] * (1.0 - lam_init)).astype(o_ref.dtype)


def _t5_bias_by_distance(rel_bias, n):
    r = jnp.arange(n, dtype=jnp.int32)
    max_exact = N_BUCKETS // 2
    rf = jnp.maximum(r, 1).astype(F32)
    large = max_exact + (jnp.log(rf / max_exact) / math.log(MAX_DISTANCE / max_exact)
                         * (N_BUCKETS - max_exact)).astype(jnp.int32)
    large = jnp.minimum(large, N_BUCKETS - 1)
    bucket = jnp.where(r < max_exact, r, large)
    return jnp.transpose(rel_bias[bucket]).astype(F32)


def _diff_attention(y3, rel_bias, lam_vecs, diff_g, *, lam_init, t):
    bsz, s, _ = y3.shape
    hd = 2 * A_QK_DIM
    dv = A_V_DIM
    assert hd == dv == LANES and s % t == 0 and t >= MAX_DISTANCE
    nq = s // t
    table = _t5_bias_by_distance(rel_bias, 2 * t).reshape(A_HEADS, 1, 2 * t)
    return pl.pallas_call(
        functools.partial(_diff_attn_kernel, t=t, lam_init=lam_init),
        grid=(bsz, A_HEADS, nq),
        in_specs=[
            pl.BlockSpec((None, t, hd), lambda b, h, q: (b, q, h)),
            pl.BlockSpec((None, s, hd), lambda b, h, q: (b, 0, A_HEADS + h)),
            pl.BlockSpec((None, s, dv), lambda b, h, q: (b, 0, 2 * A_HEADS + h)),
            pl.BlockSpec((None, 1, 2 * t), lambda b, h, q: (h, 0, 0)),
            pl.BlockSpec((4, A_QK_DIM), lambda b, h, q: (0, 0)),
            pl.BlockSpec((1, dv), lambda b, h, q: (0, h)),
        ],
        out_specs=pl.BlockSpec((None, t, dv), lambda b, h, q: (b, q, h)),
        out_shape=jax.ShapeDtypeStruct((bsz, s, A_HEADS * dv), BF16),
        scratch_shapes=[pltpu.VMEM((nq, dv + ATTN_ONES_ROWS, t), BF16),
                        pltpu.VMEM((t, t), F32),
                        pltpu.VMEM((t, t), F32),
                        pltpu.VMEM((1, 2 * t), F32),
                        pltpu.VMEM((dv + ATTN_ONES_ROWS, 2 * t), F32)],
        compiler_params=_params("parallel", "parallel", "arbitrary"),
        name="diff_attn",
    )(y3, y3, y3, table, lam_vecs, diff_g)


def _mlstm_kernel(qk_ref, v_ref, ob_ref, gc_ref, gr_ref, cw_ref, cb_ref, bc_ref, br_ref, g_ref, y_ref,
                  xp_ref, c_ref, n_ref, m_ref, *, chunk):
    heads, dk, dv = B_HEADS, B_QK_DIM, B_V_DIM
    pad = SUBLANES
    c = pl.program_id(1)

    @pl.when(c == 0)
    def _():
        xp_ref[0:pad, :] = jnp.zeros((pad, xp_ref.shape[1]), F32)
        c_ref[...] = jnp.zeros_like(c_ref)
        n_ref[...] = jnp.zeros_like(n_ref)
        m_ref[...] = jnp.zeros_like(m_ref)

    x = qk_ref[...].astype(F32)
    xp_ref[pad:pad + chunk, :] = x
    cw = cw_ref[...]
    conv = cb_ref[...]
    for j in range(CONV_WIDTH):
        conv = conv + cw[j:j + 1, :] * xp_ref[pl.ds(pad - (CONV_WIDTH - 1) + j, chunk), :]
    xp_ref[0:pad, :] = x[chunk - pad:chunk, :]
    qk = conv * _sigmoid(conv)

    gc = gc_ref[...] + bc_ref[...]
    gr = gr_ref[...] + br_ref[...]
    row = lax.broadcasted_iota(jnp.int32, (chunk, chunk), 0)
    col = lax.broadcasted_iota(jnp.int32, (chunk, chunk), 1)
    causal = row >= col
    tril = causal.astype(F32)
    triu = (row <= col).astype(F32)
    b_c = jnp.dot(tril, _log_sigmoid(gc), precision=HIGHEST, preferred_element_type=F32)
    b_r = jnp.dot(_log_sigmoid(gr), triu, precision=HIGHEST, preferred_element_type=F32)

    ob = ob_ref[...].astype(F32)
    g = g_ref[...]
    for h in range(heads):
        q = qk[:, h * dk:(h + 1) * dk] * (dk ** -0.5)
        k = qk[:, (heads + h) * dk:(heads + h + 1) * dk]
        v = v_ref[:, h * dv:(h + 1) * dv]
        b_col = b_c[:, heads + h:heads + h + 1]
        i_col = gc[:, h:h + 1]
        b_row = b_r[heads + h:heads + h + 1, :]
        i_row = gr[h:h + 1, :]
        m_prev = m_ref[h][0:1, 0:1]

        d = jnp.where(causal, b_col - b_row + i_row, -jnp.inf)
        a_col = b_col + m_prev
        m_t = jnp.maximum(a_col, jnp.max(d, -1, keepdims=True))
        w_inter = jnp.exp(a_col - m_t)
        qb = q.astype(BF16)
        sw = lax.dot_general(qb, k.astype(BF16), NT_DIMS, preferred_element_type=F32) * jnp.exp(d - m_t)
        c_mat = c_ref[h]
        num = (w_inter * jnp.dot(qb, c_mat.astype(BF16), preferred_element_type=F32)
               + jnp.dot(sw.astype(BF16), v, preferred_element_type=F32))
        den = w_inter * jnp.sum(q * n_ref[h], -1, keepdims=True) + jnp.sum(sw, -1, keepdims=True)
        hh = num / jnp.maximum(jnp.abs(den), jnp.exp(-m_t))

        b_last = b_row[:, chunk - 1:chunk]
        g_col = b_last - b_col + i_col
        m_new = jnp.maximum(b_last + m_prev, jnp.max(g_col, 0, keepdims=True))
        decay = jnp.exp(b_last + m_prev - m_new)
        kw = k * jnp.exp(g_col - m_new)
        c_ref[h] = decay * c_mat + lax.dot_general(kw.astype(BF16), v, TN_DIMS, preferred_element_type=F32)
        n_ref[h] = decay * n_ref[h] + jnp.sum(kw, 0, keepdims=True)
        m_ref[h] = jnp.broadcast_to(m_new, m_ref.shape[1:])

        hn = hh * lax.rsqrt(jnp.mean(hh * hh, -1, keepdims=True) + NORM_EPS)
        sl = slice(h * dv, (h + 1) * dv)
        y_ref[:, sl] = (hn * g[:, sl] * _sigmoid(ob[:, sl])).astype(y_ref.dtype)


def _mlstm(y3, gcol3, grow3, conv_w, conv_b, gate_bias_cols, gate_bias_rows, mlstm_g, *, chunk):
    bsz, s, _ = y3.shape
    heads, dk, dv = B_HEADS, B_QK_DIM, B_V_DIM
    w = heads * dv
    assert 2 * heads * dk == w and s % chunk == 0 and chunk % LANES == 0
    nc = s // chunk
    return pl.pallas_call(
        functools.partial(_mlstm_kernel, chunk=chunk),
        grid=(bsz, nc),
        in_specs=[
            pl.BlockSpec((None, chunk, w), lambda b, c: (b, c, 3)),
            pl.BlockSpec((None, chunk, w), lambda b, c: (b, c, 4)),
            pl.BlockSpec((None, chunk, w), lambda b, c: (b, c, 5)),
            pl.BlockSpec((None, chunk, LANES), lambda b, c: (b, c, 0)),
            pl.BlockSpec((SUBLANES, chunk), lambda b, c: (0, b * nc + c)),
            pl.BlockSpec((CONV_WIDTH, w), lambda b, c: (0, 0)),
            pl.BlockSpec((1, w), lambda b, c: (0, 0)),
            pl.BlockSpec((1, LANES), lambda b, c: (0, 0)),
            pl.BlockSpec((SUBLANES, 1), lambda b, c: (0, 0)),
            pl.BlockSpec((1, w), lambda b, c: (0, 0)),
        ],
        out_specs=pl.BlockSpec((None, chunk, w), lambda b, c: (b, c, 0)),
        out_shape=jax.ShapeDtypeStruct((bsz, s, w), BF16),
        scratch_shapes=[
            pltpu.VMEM((SUBLANES + chunk, w), F32),
            pltpu.VMEM((heads, dk, dv), F32),
            pltpu.VMEM((heads, 1, dk), F32),
            pltpu.VMEM((heads, SUBLANES, LANES), F32),
        ],
        compiler_params=_params("parallel", "arbitrary"),
        name="mlstm",
    )(y3, y3, y3, gcol3, grow3, conv_w, conv_b, gate_bias_cols, gate_bias_rows, mlstm_g)


def _gla_kernel(q_ref, k_ref, v_ref, r_ref, a1_ref, wa_ref, ba_ref, g_ref, y_ref, st_ref, *, chunk):
    heads, dk, dv = C_HEADS, C_QK_DIM, C_V_DIM
    c = pl.program_id(1)

    @pl.when(c == 0)
    def _():
        st_ref[...] = jnp.zeros_like(st_ref)

    z = jnp.dot(a1_ref[...], wa_ref[...], precision=HIGHEST, preferred_element_type=F32) + ba_ref[...]
    log_a = _log_sigmoid(z) * (1.0 / C_GATE_TEMP)
    row = lax.broadcasted_iota(jnp.int32, (chunk, chunk), 0)
    col = lax.broadcasted_iota(jnp.int32, (chunk, chunk), 1)
    causal = row >= col
    b_all = jnp.dot(causal.astype(F32), log_a, precision=HIGHEST, preferred_element_type=F32)

    r = r_ref[...].astype(F32)
    g = g_ref[...]
    for h in range(heads):
        ks = slice(h * dk, (h + 1) * dk)
        vs = slice(h * dv, (h + 1) * dv)
        b = b_all[:, ks]
        q = q_ref[:, ks].astype(F32) * (dk ** -0.5)
        k = k_ref[:, ks].astype(F32)
        v = v_ref[:, vs]
        q_dec = (q * jnp.exp(b)).astype(BF16)
        k_dec = (k * jnp.exp(-b)).astype(BF16)
        att = jnp.where(causal, lax.dot_general(q_dec, k_dec, NT_DIMS, preferred_element_type=F32), 0.0)
        st = st_ref[h]
        o = (jnp.dot(att.astype(BF16), v, preferred_element_type=F32)
             + lax.dot_general(q_dec, st.astype(BF16), NT_DIMS, preferred_element_type=F32))
        b_last = b[chunk - 1:chunk, :]
        k_upd = (k * jnp.exp(b_last - b)).astype(BF16)
        st_ref[h] = jnp.exp(b_last) * st + lax.dot_general(v, k_upd, TN_DIMS, preferred_element_type=F32)

        on = o * lax.rsqrt(jnp.mean(o * o, -1, keepdims=True) + NORM_EPS)
        rr = r[:, vs]
        y_ref[:, vs] = (on * g[:, vs] * (rr * _sigmoid(rr))).astype(y_ref.dtype)


def _gla(y3, a13, w_a2, b_a, gla_g, *, chunk):
    bsz, s, _ = y3.shape
    heads, dk, dv = C_HEADS, C_QK_DIM, C_V_DIM
    wk, wv = heads * dk, heads * dv
    assert wv == 2 * wk and s % chunk == 0
    nc = s // chunk
    return pl.pallas_call(
        functools.partial(_gla_kernel, chunk=chunk),
        grid=(bsz, nc),
        in_specs=[
            pl.BlockSpec((None, chunk, wk), lambda b, c: (b, c, 0)),
            pl.BlockSpec((None, chunk, wk), lambda b, c: (b, c, 1)),
            pl.BlockSpec((None, chunk, wv), lambda b, c: (b, c, 1)),
            pl.BlockSpec((None, chunk, wv), lambda b, c: (b, c, 2)),
            pl.BlockSpec((None, chunk, LANES), lambda b, c: (b, c, 0)),
            pl.BlockSpec((LANES, wk), lambda b, c: (0, 0)),
            pl.BlockSpec((1, wk), lambda b, c: (0, 0)),
            pl.BlockSpec((1, wv), lambda b, c: (0, 0)),
        ],
        out_specs=pl.BlockSpec((None, chunk, wv), lambda b, c: (b, c, 0)),
        out_shape=jax.ShapeDtypeStruct((bsz, s, wv), BF16),
        scratch_shapes=[pltpu.VMEM((heads, dv, dk), F32)],
        compiler_params=_params("parallel", "arbitrary"),
        name="gla",
    )(y3, y3, y3, y3, a13, w_a2, b_a, gla_g)


def _out_ln_kernel(*refs, n_in, alpha):
    y_refs, w_refs = refs[:n_in], refs[n_in:2 * n_in]
    x_ref, g_ref, b_ref, o_ref = refs[2 * n_in:]
    acc = jnp.dot(y_refs[0][...], w_refs[0][...], preferred_element_type=F32)
    for y_ref, w_ref in zip(y_refs[1:], w_refs[1:]):
        acc = acc + jnp.dot(y_ref[...], w_ref[...], preferred_element_type=F32)
    o_ref[...] = _layer_norm(alpha * x_ref[...] + acc, g_ref[...], b_ref[...])


def _out_ln(ys, ws, x, g, b, *, alpha, tm):
    t, d = x.shape
    n_in = len(ys)
    in_specs = ([pl.BlockSpec((tm, y.shape[1]), lambda i: (i, 0)) for y in ys]
                + [pl.BlockSpec(w.shape, lambda i: (0, 0)) for w in ws]
                + [pl.BlockSpec((tm, d), lambda i: (i, 0)),
                   pl.BlockSpec((1, d), lambda i: (0, 0)),
                   pl.BlockSpec((1, d), lambda i: (0, 0))])
    return pl.pallas_call(
        functools.partial(_out_ln_kernel, n_in=n_in, alpha=alpha),
        grid=(t // tm,),
        in_specs=in_specs,
        out_specs=pl.BlockSpec((tm, d), lambda i: (i, 0)),
        out_shape=jax.ShapeDtypeStruct((t, d), F32),
        compiler_params=_params("parallel"),
        name="out_ln",
    )(*ys, *ws, x, g, b)


def _ple_kernel(x_ref, p_ref, wp_ref, wg_ref, o_ref):
    x = x_ref[...]
    e = jnp.dot(p_ref[...].astype(BF16), wp_ref[...], preferred_element_type=F32)
    gate = jnp.dot(x.astype(BF16), wg_ref[...], preferred_element_type=F32)
    o_ref[...] = x + e * _sigmoid(gate)


def _ple(x, p, w_proj, w_gate, *, tm):
    t, d = x.shape
    pd = p.shape[1]
    return pl.pallas_call(
        _ple_kernel,
        grid=(t // tm,),
        in_specs=[
            pl.BlockSpec((tm, d), lambda i: (i, 0)),
            pl.BlockSpec((tm, pd), lambda i: (i, 0)),
            pl.BlockSpec((pd, d), lambda i: (0, 0)),
            pl.BlockSpec((d, d), lambda i: (0, 0)),
        ],
        out_specs=pl.BlockSpec((tm, d), lambda i: (i, 0)),
        out_shape=jax.ShapeDtypeStruct((t, d), F32),
        compiler_params=_params("parallel"),
        name="ple",
    )(x, p, w_proj, w_gate)


def _tiles(t, s):
    return dict(
        ffn_tm=min(512, t), ffn_tf=512,
        proj_tm=min(1024, t), proj_tn=1024,
        row_tm=min(512, t),
        attn_t=min(512, s),
        mlstm_chunk=min(128, s),
        gla_chunk=min(64, s),
    )


def _pad_cols(w, n):
    return jnp.pad(w, ((0, 0), (0, n - w.shape[1])))


def kernel(x, p, ln_g, ln_b, w_ffn_in, w_ffn_out, w_in_ab, w_out_ab, rel_bias, lambda_q1, lambda_k1,
           lambda_q2, lambda_k2, diff_norm, conv_w, conv_b, b_igate, b_fgate, mlstm_norm, w_in_c,
           w_alpha2, b_alpha, gla_norm, w_out_c, w_ple_proj, w_ple_gate):
    bsz, s, d = x.shape
    depth = p.shape[0]
    t = bsz * s
    tl = _tiles(t, s)
    alpha = (2 * depth) ** 0.25
    a_w = A_HEADS * A_V_DIM
    ab_main = 3 * a_w + 3 * B_HEADS * B_V_DIM
    c_main = 2 * C_HEADS * C_QK_DIM + 2 * C_HEADS * C_V_DIM

    xf = x.reshape(t, d)
    for i in range(depth):
        row = lambda a: a.reshape(1, -1)
        xf = _ffn_ln(xf, w_ffn_in[i, 0].astype(BF16), w_ffn_out[i, 0].astype(BF16), row(ln_g[i, 0]),
                     row(ln_b[i, 0]), alpha=alpha, tm=tl["ffn_tm"], tf=tl["ffn_tf"])
        if i % 2 == 0:
            e = i // 2
            lam_init = 0.8 - 0.6 * math.exp(-0.3 * i)
            w = w_in_ab[e]
            w_gates = w[:, ab_main:]
            y, gcol, grow = _in_proj(xf, w[:, :ab_main].astype(BF16), _pad_cols(w_gates, LANES).astype(BF16),
                                     jnp.transpose(w_gates).astype(BF16), tm=tl["proj_tm"], tn=tl["proj_tn"])
            y3 = y.reshape(bsz, s, ab_main)
            lam_vecs = jnp.stack([lambda_q1[e], lambda_k1[e], lambda_q2[e], lambda_k2[e]])
            ya = _diff_attention(y3, rel_bias, lam_vecs, row(diff_norm[e]), lam_init=lam_init, t=tl["attn_t"])
            gate_bias = jnp.concatenate([b_igate[e], b_fgate[e]])
            yb = _mlstm(y3, gcol.reshape(bsz, s, LANES), grow,
                        conv_w[e], row(conv_b[e]), _pad_cols(row(gate_bias), LANES), gate_bias.reshape(-1, 1),
                        row(mlstm_norm[e]), chunk=tl["mlstm_chunk"])
            wo = w_out_ab[e].astype(BF16)
            ys, ws = [ya.reshape(t, a_w), yb.reshape(t, -1)], [wo[:a_w], wo[a_w:]]
        else:
            o = i // 2
            w = w_in_c[o]
            w_a1 = w[:, c_main:]
            y, a1, _ = _in_proj(xf, w[:, :c_main].astype(BF16), _pad_cols(w_a1, LANES).astype(BF16),
                                jnp.transpose(w_a1[:, :SUBLANES]).astype(BF16), tm=tl["proj_tm"], tn=tl["proj_tn"])
            w_a2 = jnp.pad(w_alpha2[o], ((0, LANES - w_alpha2.shape[1]), (0, 0)))
            yc = _gla(y.reshape(bsz, s, c_main), a1.reshape(bsz, s, LANES), w_a2, row(b_alpha[o]),
                      row(gla_norm[o]), chunk=tl["gla_chunk"])
            ys, ws = [yc.reshape(t, -1)], [w_out_c[o].astype(BF16)]
        xf = _out_ln(ys, ws, xf, row(ln_g[i, 1]), row(ln_b[i, 1]), alpha=alpha, tm=tl["row_tm"])
        xf = _ffn_ln(xf, w_ffn_in[i, 1].astype(BF16), w_ffn_out[i, 1].astype(BF16), row(ln_g[i, 2]),
                     row(ln_b[i, 2]), alpha=alpha, tm=tl["ffn_tm"], tf=tl["ffn_tf"])
        xf = _ple(xf, p[i].reshape(t, -1), w_ple_proj[i].astype(BF16), w_ple_gate[i].astype(BF16), tm=tl["row_tm"])
    return xf.reshape(bsz, s, d)
```

```python
import functools
import math

import jax
import jax.numpy as jnp
from jax import lax
from jax.experimental import pallas as pl
from jax.experimental.pallas import tpu as pltpu

F32 = jnp.float32
BF16 = jnp.bfloat16
HIGHEST = lax.Precision.HIGHEST

A_HEADS = 8
A_QK_DIM = 64
A_V_DIM = 128
B_HEADS = 4
B_QK_DIM = 128
B_V_DIM = 256
CONV_WIDTH = 4
C_HEADS = 4
C_QK_DIM = 256
C_V_DIM = 512
C_GATE_TEMP = 16.0
N_BUCKETS = 32
MAX_DISTANCE = 128
LN_EPS = 1e-5
NORM_EPS = 1e-6

V7X_VMEM_LIMIT_BYTES = 56 * 1024 * 1024
LANES = 128
SUBLANES = 8
ATTN_ONES_ROWS = 16

NT_DIMS = (((1,), (1,)), ((), ()))
TN_DIMS = (((0,), (0,)), ((), ()))


def _params(*sem):
    return pltpu.CompilerParams(dimension_semantics=sem, vmem_limit_bytes=V7X_VMEM_LIMIT_BYTES)


def _layer_norm(y, g, b):
    mu = jnp.mean(y, -1, keepdims=True)
    yc = y - mu
    var = jnp.mean(yc * yc, -1, keepdims=True)
    return yc * lax.rsqrt(var + LN_EPS) * g + b


def _sigmoid(x):
    return 1.0 / (1.0 + jnp.exp(-x))


def _log_sigmoid(x):
    return jnp.minimum(x, 0.0) - jnp.log(1.0 + jnp.exp(-jnp.abs(x)))


def _ffn_ln_kernel(x_ref, wg_ref, wu_ref, wo_ref, g_ref, b_ref, o_ref, xb_ref, acc_ref, *, alpha):
    j = pl.program_id(1)

    @pl.when(j == 0)
    def _():
        xb_ref[...] = x_ref[...].astype(BF16)
        acc_ref[...] = jnp.zeros_like(acc_ref)

    xb = xb_ref[...]
    gate = jnp.dot(xb, wg_ref[...], preferred_element_type=F32)
    up = jnp.dot(xb, wu_ref[...], preferred_element_type=F32)
    h = (gate * _sigmoid(gate) * up).astype(BF16)
    acc_ref[...] += jnp.dot(h, wo_ref[...], preferred_element_type=F32)

    @pl.when(j == pl.num_programs(1) - 1)
    def _():
        y = alpha * x_ref[...] + 0.5 * acc_ref[...]
        o_ref[...] = _layer_norm(y, g_ref[...], b_ref[...])


def _ffn_ln(x, w_in, w_out, g, b, *, alpha, tm, tf):
    t, d = x.shape
    f = w_out.shape[0]
    nf = f // tf
    assert t % tm == 0 and f % tf == 0 and w_in.shape == (d, 2 * f)
    return pl.pallas_call(
        functools.partial(_ffn_ln_kernel, alpha=alpha),
        grid=(t // tm, nf),
        in_specs=[
            pl.BlockSpec((tm, d), lambda i, j: (i, 0)),
            pl.BlockSpec((d, tf), lambda i, j: (0, j)),
            pl.BlockSpec((d, tf), lambda i, j: (0, nf + j)),
            pl.BlockSpec((tf, d), lambda i, j: (j, 0)),
            pl.BlockSpec((1, d), lambda i, j: (0, 0)),
            pl.BlockSpec((1, d), lambda i, j: (0, 0)),
        ],
        out_specs=pl.BlockSpec((tm, d), lambda i, j: (i, 0)),
        out_shape=jax.ShapeDtypeStruct((t, d), F32),
        scratch_shapes=[pltpu.VMEM((tm, d), BF16), pltpu.VMEM((tm, d), F32)],
        compiler_params=_params("parallel", "arbitrary"),
        name="ffn_ln",
    )(x, w_in, w_in, w_out, g, b)


def _in_proj_kernel(x_ref, w_ref, wgc_ref, wgr_ref, y_ref, gc_ref, gr_ref, xb_ref):
    j = pl.program_id(1)

    @pl.when(j == 0)
    def _():
        xb = x_ref[...].astype(BF16)
        xb_ref[...] = xb
        gc_ref[...] = jnp.dot(xb, wgc_ref[...], preferred_element_type=F32)
        gr_ref[...] = lax.dot_general(wgr_ref[...], xb, NT_DIMS, preferred_element_type=F32)

    y_ref[...] = jnp.dot(xb_ref[...], w_ref[...], preferred_element_type=F32).astype(y_ref.dtype)


def _in_proj(x, w_main, w_gate_cols, w_gate_rows, *, tm, tn):
    t, d = x.shape
    n = w_main.shape[1]
    assert t % tm == 0 and n % tn == 0
    return pl.pallas_call(
        _in_proj_kernel,
        grid=(t // tm, n // tn),
        in_specs=[
            pl.BlockSpec((tm, d), lambda i, j: (i, 0)),
            pl.BlockSpec((d, tn), lambda i, j: (0, j)),
            pl.BlockSpec((d, LANES), lambda i, j: (0, 0)),
            pl.BlockSpec((SUBLANES, d), lambda i, j: (0, 0)),
        ],
        out_specs=[
            pl.BlockSpec((tm, tn), lambda i, j: (i, j)),
            pl.BlockSpec((tm, LANES), lambda i, j: (i, 0)),
            pl.BlockSpec((SUBLANES, tm), lambda i, j: (0, i)),
        ],
        out_shape=[
            jax.ShapeDtypeStruct((t, n), BF16),
            jax.ShapeDtypeStruct((t, LANES), F32),
            jax.ShapeDtypeStruct((SUBLANES, t), F32),
        ],
        scratch_shapes=[pltpu.VMEM((tm, d), BF16)],
        compiler_params=_params("parallel", "arbitrary"),
        name="in_proj",
    )(x, w_main, w_gate_cols, w_gate_rows)


def _diff_attn_kernel(q_ref, k_ref, v_ref, tab_ref, lam_ref, g_ref, o_ref,
                      vt_ref, bdiag_ref, bnear_ref, m_ref, acc_ref,
                      s0_ref, mx0_ref, sh0_ref, s1_ref, mx1_ref, sh1_ref, *, t, lam_init):
    qi = pl.program_id(2)
    dk, dv = A_QK_DIM, A_V_DIM
    nk = k_ref.shape[0] // t
    bufs = ((s0_ref, mx0_ref, sh0_ref), (s1_ref, mx1_ref, sh1_ref))

    @pl.when(qi == 0)
    def _():
        for c in range(nk):
            vt_ref[c, 0:dv, :] = jnp.transpose(v_ref[c * t:(c + 1) * t, :].astype(F32)).astype(BF16)
            vt_ref[c, dv:, :] = jnp.ones((vt_ref.shape[1] - dv, t), BF16)
        skew = pltpu.roll(jnp.broadcast_to(tab_ref[...], (t, 2 * t)), 0, 1, stride=1, stride_axis=0)
        key = lax.broadcasted_iota(jnp.int32, (t, t), 0)
        qry = lax.broadcasted_iota(jnp.int32, (t, t), 1)
        bdiag_ref[...] = jnp.where(qry >= key, skew[:, :t], -1e30)
        bnear_ref[...] = skew[:, t:]

    q_t = jnp.transpose(q_ref[...].astype(F32)) * (dk ** -0.5)
    feat = lax.broadcasted_iota(jnp.int32, q_t.shape, 0)
    qs_t = jnp.concatenate([jnp.where(feat < dk, q_t, 0.0), jnp.where(feat >= dk, q_t, 0.0)],
                           axis=1).astype(BF16)

    m_ref[...] = jnp.full_like(m_ref, -jnp.inf)
    acc_ref[...] = jnp.zeros_like(acc_ref)

    def score(kj, buf, bias_ref, shift):
        s_ref, mx_ref, sh_ref = bufs[buf]
        off = pl.multiple_of(kj * t, t)
        s = jnp.dot(k_ref[pl.ds(off, t), :], qs_t, preferred_element_type=F32)
        if bias_ref is not None:
            b = bias_ref[...]
            s = s + jnp.concatenate([b, b], axis=1)
        s_ref[...] = s
        mx_ref[...] = jnp.max(s, 0, keepdims=True) + shift
        sh_ref[...] = jnp.zeros_like(sh_ref) + shift

    def absorb(kj, buf):
        s_ref, mx_ref, sh_ref = bufs[buf]
        m_old = m_ref[...]
        m_new = jnp.maximum(m_old, mx_ref[...])
        p = jnp.exp(s_ref[...] - (m_new - sh_ref[...])).astype(BF16)
        acc_ref[...] = (jnp.exp(m_old - m_new) * acc_ref[...]
                        + jnp.dot(vt_ref[kj], p, preferred_element_type=F32))
        m_ref[...] = m_new

    c_far = tab_ref[:, 2 * t - 1:2 * t]
    score(qi, 0, bdiag_ref, 0.0)

    @pl.when(qi >= 1)
    def _():
        score(qi - 1, 1, bnear_ref, 0.0)
        absorb(qi, 0)

    def far_pair(p, carry):
        kj = qi - 2 - 2 * p
        score(kj, 0, None, c_far)
        absorb(kj + 1, 1)
        score(kj - 1, 1, None, c_far)
        absorb(kj, 0)
        return carry

    lax.fori_loop(0, jnp.maximum(qi - 1, 0) // 2, far_pair, 0)

    @pl.when((qi >= 2) & (qi % 2 == 0))
    def _():
        score(0, 0, None, c_far)
        absorb(1, 1)

    @pl.when(qi % 2 == 0)
    def _():
        absorb(0, 0)

    @pl.when(qi % 2 == 1)
    def _():
        absorb(0, 1)

    lv = lam_ref[...]
    lam = (jnp.exp(jnp.sum(lv[0:1] * lv[1:2], -1, keepdims=True))
           - jnp.exp(jnp.sum(lv[2:3] * lv[3:4], -1, keepdims=True)) + lam_init)
    acc = acc_ref[...]
    o_t = acc[0:dv] * (1.0 / acc[dv:dv + 1])
    out = jnp.transpose(o_t[:, :t] - lam * o_t[:, t:])
    hn = out * lax.rsqrt(jnp.mean(out * out, -1, keepdims=True) + NORM_EPS)
    o_ref[...] = (hn * g_ref[...] * (1.0 - lam_init)).astype(o_ref.dtype)


def _t5_bias_by_distance(rel_bias, n):
    r = jnp.arange(n, dtype=jnp.int32)
    max_exact = N_BUCKETS // 2
    rf = jnp.maximum(r, 1).astype(F32)
    large = max_exact + (jnp.log(rf / max_exact) / math.log(MAX_DISTANCE / max_exact)
                         * (N_BUCKETS - max_exact)).astype(jnp.int32)
    large = jnp.minimum(large, N_BUCKETS - 1)
    bucket = jnp.where(r < max_exact, r, large)
    return jnp.transpose(rel_bias[bucket]).astype(F32)


def _diff_attention(y3, rel_bias, lam_vecs, diff_g, *, lam_init, t):
    bsz, s, _ = y3.shape
    hd = 2 * A_QK_DIM
    dv = A_V_DIM
    assert hd == dv == LANES and s % t == 0 and t >= MAX_DISTANCE
    nq = s // t
    table = _t5_bias_by_distance(rel_bias, 2 * t).reshape(A_HEADS, 1, 2 * t)
    return pl.pallas_call(
        functools.partial(_diff_attn_kernel, t=t, lam_init=lam_init),
        grid=(bsz, A_HEADS, nq),
        in_specs=[
            pl.BlockSpec((None, t, hd), lambda b, h, q: (b, q, h)),
            pl.BlockSpec((None, s, hd), lambda b, h, q: (b, 0, A_HEADS + h)),
            pl.BlockSpec((None, s, dv), lambda b, h, q: (b, 0, 2 * A_HEADS + h)),
            pl.BlockSpec((None, 1, 2 * t), lambda b, h, q: (h, 0, 0)),
            pl.BlockSpec((4, A_QK_DIM), lambda b, h, q: (0, 0)),
            pl.BlockSpec((1, dv), lambda b, h, q: (0, h)),
        ],
        out_specs=pl.BlockSpec((None, t, dv), lambda b, h, q: (b, q, h)),
        out_shape=jax.ShapeDtypeStruct((bsz, s, A_HEADS * dv), BF16),
        scratch_shapes=[pltpu.VMEM((nq, dv + ATTN_ONES_ROWS, t), BF16),
                        pltpu.VMEM((t, t), F32),
                        pltpu.VMEM((t, t), F32),
                        pltpu.VMEM((1, 2 * t), F32),
                        pltpu.VMEM((dv + ATTN_ONES_ROWS, 2 * t), F32)]
                       + 2 * [pltpu.VMEM((t, 2 * t), F32), pltpu.VMEM((1, 2 * t), F32),
                              pltpu.VMEM((1, 2 * t), F32)],
        compiler_params=_params("parallel", "parallel", "arbitrary"),
        name="diff_attn",
    )(y3, y3, y3, table, lam_vecs, diff_g)


def _mlstm_kernel(qk_ref, v_ref, ob_ref, gc_ref, gr_ref, cw_ref, cb_ref, bc_ref, br_ref, g_ref, y_ref,
                  xp_ref, c_ref, n_ref, m_ref, *, chunk):
    heads, dk, dv = B_HEADS, B_QK_DIM, B_V_DIM
    pad = SUBLANES
    c = pl.program_id(1)

    @pl.when(c == 0)
    def _():
        xp_ref[0:pad, :] = jnp.zeros((pad, xp_ref.shape[1]), F32)
        c_ref[...] = jnp.zeros_like(c_ref)
        n_ref[...] = jnp.zeros_like(n_ref)
        m_ref[...] = jnp.zeros_like(m_ref)

    x = qk_ref[...].astype(F32)
    xp_ref[pad:pad + chunk, :] = x
    cw = cw_ref[...]
    conv = cb_ref[...]
    for j in range(CONV_WIDTH):
        conv = conv + cw[j:j + 1, :] * xp_ref[pl.ds(pad - (CONV_WIDTH - 1) + j, chunk), :]
    xp_ref[0:pad, :] = x[chunk - pad:chunk, :]
    qk = conv * _sigmoid(conv)

    gc = gc_ref[...] + bc_ref[...]
    gr = gr_ref[...] + br_ref[...]
    row = lax.broadcasted_iota(jnp.int32, (chunk, chunk), 0)
    col = lax.broadcasted_iota(jnp.int32, (chunk, chunk), 1)
    causal = row >= col
    tril = causal.astype(F32)
    triu = (row <= col).astype(F32)
    b_c = jnp.dot(tril, _log_sigmoid(gc), precision=HIGHEST, preferred_element_type=F32)
    b_r = jnp.dot(_log_sigmoid(gr), triu, precision=HIGHEST, preferred_element_type=F32)

    ob = ob_ref[...].astype(F32)
    g = g_ref[...]
    for h in range(heads):
        q = qk[:, h * dk:(h + 1) * dk] * (dk ** -0.5)
        k = qk[:, (heads + h) * dk:(heads + h + 1) * dk]
        v = v_ref[:, h * dv:(h + 1) * dv]
        b_col = b_c[:, heads + h:heads + h + 1]
        i_col = gc[:, h:h + 1]
        b_row = b_r[heads + h:heads + h + 1, :]
        i_row = gr[h:h + 1, :]
        m_prev = m_ref[h][0:1, 0:1]

        d = jnp.where(causal, b_col - b_row + i_row, -jnp.inf)
        a_col = b_col + m_prev
        m_t = jnp.maximum(a_col, jnp.max(d, -1, keepdims=True))
        w_inter = jnp.exp(a_col - m_t)
        qb = q.astype(BF16)
        sw = lax.dot_general(qb, k.astype(BF16), NT_DIMS, preferred_element_type=F32) * jnp.exp(d - m_t)
        c_mat = c_ref[h]
        num = (w_inter * jnp.dot(qb, c_mat.astype(BF16), preferred_element_type=F32)
               + jnp.dot(sw.astype(BF16), v, preferred_element_type=F32))
        den = w_inter * jnp.sum(q * n_ref[h], -1, keepdims=True) + jnp.sum(sw, -1, keepdims=True)
        hh = num / jnp.maximum(jnp.abs(den), jnp.exp(-m_t))

        b_last = b_row[:, chunk - 1:chunk]
        g_col = b_last - b_col + i_col
        m_new = jnp.maximum(b_last + m_prev, jnp.max(g_col, 0, keepdims=True))
        decay = jnp.exp(b_last + m_prev - m_new)
        kw = k * jnp.exp(g_col - m_new)
        c_ref[h] = decay * c_mat + lax.dot_general(kw.astype(BF16), v, TN_DIMS, preferred_element_type=F32)
        n_ref[h] = decay * n_ref[h] + jnp.sum(kw, 0, keepdims=True)
        m_ref[h] = jnp.broadcast_to(m_new, m_ref.shape[1:])

        hn = hh * lax.rsqrt(jnp.mean(hh * hh, -1, keepdims=True) + NORM_EPS)
        sl = slice(h * dv, (h + 1) * dv)
        y_ref[:, sl] = (hn * g[:, sl] * _sigmoid(ob[:, sl])).astype(y_ref.dtype)


def _mlstm(y3, gcol3, grow3, conv_w, conv_b, gate_bias_cols, gate_bias_rows, mlstm_g, *, chunk):
    bsz, s, _ = y3.shape
    heads, dk, dv = B_HEADS, B_QK_DIM, B_V_DIM
    w = heads * dv
    assert 2 * heads * dk == w and s % chunk == 0 and chunk % LANES == 0
    nc = s // chunk
    return pl.pallas_call(
        functools.partial(_mlstm_kernel, chunk=chunk),
        grid=(bsz, nc),
        in_specs=[
            pl.BlockSpec((None, chunk, w), lambda b, c: (b, c, 3)),
            pl.BlockSpec((None, chunk, w), lambda b, c: (b, c, 4)),
            pl.BlockSpec((None, chunk, w), lambda b, c: (b, c, 5)),
            pl.BlockSpec((None, chunk, LANES), lambda b, c: (b, c, 0)),
            pl.BlockSpec((SUBLANES, chunk), lambda b, c: (0, b * nc + c)),
            pl.BlockSpec((CONV_WIDTH, w), lambda b, c: (0, 0)),
            pl.BlockSpec((1, w), lambda b, c: (0, 0)),
            pl.BlockSpec((1, LANES), lambda b, c: (0, 0)),
            pl.BlockSpec((SUBLANES, 1), lambda b, c: (0, 0)),
            pl.BlockSpec((1, w), lambda b, c: (0, 0)),
        ],
        out_specs=pl.BlockSpec((None, chunk, w), lambda b, c: (b, c, 0)),
        out_shape=jax.ShapeDtypeStruct((bsz, s, w), BF16),
        scratch_shapes=[
            pltpu.VMEM((SUBLANES + chunk, w), F32),
            pltpu.VMEM((heads, dk, dv), F32),
            pltpu.VMEM((heads, 1, dk), F32),
            pltpu.VMEM((heads, SUBLANES, LANES), F32),
        ],
        compiler_params=_params("parallel", "arbitrary"),
        name="mlstm",
    )(y3, y3, y3, gcol3, grow3, conv_w, conv_b, gate_bias_cols, gate_bias_rows, mlstm_g)


def _gla_kernel(q_ref, k_ref, v_ref, r_ref, a1_ref, wa_ref, ba_ref, g_ref, y_ref, st_ref, *, chunk):
    heads, dk, dv = C_HEADS, C_QK_DIM, C_V_DIM
    c = pl.program_id(1)

    @pl.when(c == 0)
    def _():
        st_ref[...] = jnp.zeros_like(st_ref)

    z = jnp.dot(a1_ref[...], wa_ref[...], precision=HIGHEST, preferred_element_type=F32) + ba_ref[...]
    log_a = _log_sigmoid(z) * (1.0 / C_GATE_TEMP)
    row = lax.broadcasted_iota(jnp.int32, (chunk, chunk), 0)
    col = lax.broadcasted_iota(jnp.int32, (chunk, chunk), 1)
    causal = row >= col
    b_all = jnp.dot(causal.astype(F32), log_a, precision=HIGHEST, preferred_element_type=F32)

    r = r_ref[...].astype(F32)
    g = g_ref[...]
    for h in range(heads):
        ks = slice(h * dk, (h + 1) * dk)
        vs = slice(h * dv, (h + 1) * dv)
        b = b_all[:, ks]
        q = q_ref[:, ks].astype(F32) * (dk ** -0.5)
        k = k_ref[:, ks].astype(F32)
        v = v_ref[:, vs]
        q_dec = (q * jnp.exp(b)).astype(BF16)
        k_dec = (k * jnp.exp(-b)).astype(BF16)
        att = jnp.where(causal, lax.dot_general(q_dec, k_dec, NT_DIMS, preferred_element_type=F32), 0.0)
        st = st_ref[h]
        o = (jnp.dot(att.astype(BF16), v, preferred_element_type=F32)
             + lax.dot_general(q_dec, st.astype(BF16), NT_DIMS, preferred_element_type=F32))
        b_last = b[chunk - 1:chunk, :]
        k_upd = (k * jnp.exp(b_last - b)).astype(BF16)
        st_ref[h] = jnp.exp(b_last) * st + lax.dot_general(v, k_upd, TN_DIMS, preferred_element_type=F32)

        on = o * lax.rsqrt(jnp.mean(o * o, -1, keepdims=True) + NORM_EPS)
        rr = r[:, vs]
        y_ref[:, vs] = (on * g[:, vs] * (rr * _sigmoid(rr))).astype(y_ref.dtype)


def _gla(y3, a13, w_a2, b_a, gla_g, *, chunk):
    bsz, s, _ = y3.shape
    heads, dk, dv = C_HEADS, C_QK_DIM, C_V_DIM
    wk, wv = heads * dk, heads * dv
    assert wv == 2 * wk and s % chunk == 0
    nc = s // chunk
    return pl.pallas_call(
        functools.partial(_gla_kernel, chunk=chunk),
        grid=(bsz, nc),
        in_specs=[
            pl.BlockSpec((None, chunk, wk), lambda b, c: (b, c, 0)),
            pl.BlockSpec((None, chunk, wk), lambda b, c: (b, c, 1)),
            pl.BlockSpec((None, chunk, wv), lambda b, c: (b, c, 1)),
            pl.BlockSpec((None, chunk, wv), lambda b, c: (b, c, 2)),
            pl.BlockSpec((None, chunk, LANES), lambda b, c: (b, c, 0)),
            pl.BlockSpec((LANES, wk), lambda b, c: (0, 0)),
            pl.BlockSpec((1, wk), lambda b, c: (0, 0)),
            pl.BlockSpec((1, wv), lambda b, c: (0, 0)),
        ],
        out_specs=pl.BlockSpec((None, chunk, wv), lambda b, c: (b, c, 0)),
        out_shape=jax.ShapeDtypeStruct((bsz, s, wv), BF16),
        scratch_shapes=[pltpu.VMEM((heads, dv, dk), F32)],
        compiler_params=_params("parallel", "arbitrary"),
        name="gla",
    )(y3, y3, y3, y3, a13, w_a2, b_a, gla_g)


def _out_ln_kernel(*refs, n_in, alpha):
    y_refs, w_refs = refs[:n_in], refs[n_in:2 * n_in]
    x_ref, g_ref, b_ref, o_ref = refs[2 * n_in:]
    acc = jnp.dot(y_refs[0][...], w_refs[0][...], preferred_element_type=F32)
    for y_ref, w_ref in zip(y_refs[1:], w_refs[1:]):
        acc = acc + jnp.dot(y_ref[...], w_ref[...], preferred_element_type=F32)
    o_ref[...] = _layer_norm(alpha * x_ref[...] + acc, g_ref[...], b_ref[...])


def _out_ln(ys, ws, x, g, b, *, alpha, tm):
    t, d = x.shape
    n_in = len(ys)
    in_specs = ([pl.BlockSpec((tm, y.shape[1]), lambda i: (i, 0)) for y in ys]
                + [pl.BlockSpec(w.shape, lambda i: (0, 0)) for w in ws]
                + [pl.BlockSpec((tm, d), lambda i: (i, 0)),
                   pl.BlockSpec((1, d), lambda i: (0, 0)),
                   pl.BlockSpec((1, d), lambda i: (0, 0))])
    return pl.pallas_call(
        functools.partial(_out_ln_kernel, n_in=n_in, alpha=alpha),
        grid=(t // tm,),
        in_specs=in_specs,
        out_specs=pl.BlockSpec((tm, d), lambda i: (i, 0)),
        out_shape=jax.ShapeDtypeStruct((t, d), F32),
        compiler_params=_params("parallel"),
        name="out_ln",
    )(*ys, *ws, x, g, b)


def _ple_kernel(x_ref, p_ref, wp_ref, wg_ref, o_ref):
    x = x_ref[...]
    e = jnp.dot(p_ref[...].astype(BF16), wp_ref[...], preferred_element_type=F32)
    gate = jnp.dot(x.astype(BF16), wg_ref[...], preferred_element_type=F32)
    o_ref[...] = x + e * _sigmoid(gate)


def _ple(x, p, w_proj, w_gate, *, tm):
    t, d = x.shape
    pd = p.shape[1]
    return pl.pallas_call(
        _ple_kernel,
        grid=(t // tm,),
        in_specs=[
            pl.BlockSpec((tm, d), lambda i: (i, 0)),
            pl.BlockSpec((tm, pd), lambda i: (i, 0)),
            pl.BlockSpec((pd, d), lambda i: (0, 0)),
            pl.BlockSpec((d, d), lambda i: (0, 0)),
        ],
        out_specs=pl.BlockSpec((tm, d), lambda i: (i, 0)),
        out_shape=jax.ShapeDtypeStruct((t, d), F32),
        compiler_params=_params("parallel"),
        name="ple",
    )(x, p, w_proj, w_gate)


def _tiles(t, s):
    return dict(
        ffn_tm=min(512, t), ffn_tf=512,
        proj_tm=min(1024, t), proj_tn=1024,
        row_tm=min(512, t),
        attn_t=min(512, s),
        mlstm_chunk=min(128, s),
        gla_chunk=min(64, s),
    )


def _pad_cols(w, n):
    return jnp.pad(w, ((0, 0), (0, n - w.shape[1])))


def kernel(x, p, ln_g, ln_b, w_ffn_in, w_ffn_out, w_in_ab, w_out_ab, rel_bias, lambda_q1, lambda_k1,
           lambda_q2, lambda_k2, diff_norm, conv_w, conv_b, b_igate, b_fgate, mlstm_norm, w_in_c,
           w_alpha2, b_alpha, gla_norm, w_out_c, w_ple_proj, w_ple_gate):
    bsz, s, d = x.shape
    depth = p.shape[0]
    t = bsz * s
    tl = _tiles(t, s)
    alpha = (2 * depth) ** 0.25
    a_w = A_HEADS * A_V_DIM
    ab_main = 3 * a_w + 3 * B_HEADS * B_V_DIM
    c_main = 2 * C_HEADS * C_QK_DIM + 2 * C_HEADS * C_V_DIM

    xf = x.reshape(t, d)
    for i in range(depth):
        row = lambda a: a.reshape(1, -1)
        xf = _ffn_ln(xf, w_ffn_in[i, 0].astype(BF16), w_ffn_out[i, 0].astype(BF16), row(ln_g[i, 0]),
                     row(ln_b[i, 0]), alpha=alpha, tm=tl["ffn_tm"], tf=tl["ffn_tf"])
        if i % 2 == 0:
            e = i // 2
            lam_init = 0.8 - 0.6 * math.exp(-0.3 * i)
            w = w_in_ab[e]
            w_gates = w[:, ab_main:]
            y, gcol, grow = _in_proj(xf, w[:, :ab_main].astype(BF16), _pad_cols(w_gates, LANES).astype(BF16),
                                     jnp.transpose(w_gates).astype(BF16), tm=tl["proj_tm"], tn=tl["proj_tn"])
            y3 = y.reshape(bsz, s, ab_main)
            lam_vecs = jnp.stack([lambda_q1[e], lambda_k1[e], lambda_q2[e], lambda_k2[e]])
            ya = _diff_attention(y3, rel_bias, lam_vecs, row(diff_norm[e]), lam_init=lam_init, t=tl["attn_t"])
            gate_bias = jnp.concatenate([b_igate[e], b_fgate[e]])
            yb = _mlstm(y3, gcol.reshape(bsz, s, LANES), grow,
                        conv_w[e], row(conv_b[e]), _pad_cols(row(gate_bias), LANES), gate_bias.reshape(-1, 1),
                        row(mlstm_norm[e]), chunk=tl["mlstm_chunk"])
            wo = w_out_ab[e].astype(BF16)
            ys, ws = [ya.reshape(t, a_w), yb.reshape(t, -1)], [wo[:a_w], wo[a_w:]]
        else:
            o = i // 2
            w = w_in_c[o]
            w_a1 = w[:, c_main:]
            y, a1, _ = _in_proj(xf, w[:, :c_main].astype(BF16), _pad_cols(w_a1, LANES).astype(BF16),
                                jnp.transpose(w_a1[:, :SUBLANES]).astype(BF16), tm=tl["proj_tm"], tn=tl["proj_tn"])
            w_a2 = jnp.pad(w_alpha2[o], ((0, LANES - w_alpha2.shape[1]), (0, 0)))
            yc = _gla(y.reshape(bsz, s, c_main), a1.reshape(bsz, s, LANES), w_a2, row(b_alpha[o]),
                      row(gla_norm[o]), chunk=tl["gla_chunk"])
            ys, ws = [yc.reshape(t, -1)], [w_out_c[o].astype(BF16)]
        xf = _out_ln(ys, ws, xf, row(ln_g[i, 1]), row(ln_b[i, 1]), alpha=alpha, tm=tl["row_tm"])
        xf = _ffn_ln(xf, w_ffn_in[i, 1].astype(BF16), w_ffn_out[i, 1].astype(BF16), row(ln_g[i, 2]),
                     row(ln_b[i, 2]), alpha=alpha, tm=tl["ffn_tm"], tf=tl["ffn_tf"])
        xf = _ple(xf, p[i].reshape(t, -1), w_ple_proj[i].astype(BF16), w_ple_gate[i].astype(BF16), tm=tl["row_tm"])
    return xf.reshape(bsz, s, d)
```

```python
import functools
import math

import jax
import jax.numpy as jnp
from jax import lax
from jax.experimental import pallas as pl
from jax.experimental.pallas import tpu as pltpu

F32 = jnp.float32
BF16 = jnp.bfloat16
HIGHEST = lax.Precision.HIGHEST

A_HEADS = 8
A_QK_DIM = 64
A_V_DIM = 128
B_HEADS = 4
B_QK_DIM = 128
B_V_DIM = 256
CONV_WIDTH = 4
C_HEADS = 4
C_QK_DIM = 256
C_V_DIM = 512
C_GATE_TEMP = 16.0
N_BUCKETS = 32
MAX_DISTANCE = 128
LN_EPS = 1e-5
NORM_EPS = 1e-6

V7X_VMEM_LIMIT_BYTES = 56 * 1024 * 1024
LANES = 128
SUBLANES = 8
ATTN_ONES_ROWS = 16

NT_DIMS = (((1,), (1,)), ((), ()))
TN_DIMS = (((0,), (0,)), ((), ()))


def _params(*sem):
    return pltpu.CompilerParams(dimension_semantics=sem, vmem_limit_bytes=V7X_VMEM_LIMIT_BYTES)


def _layer_norm(y, g, b):
    mu = jnp.mean(y, -1, keepdims=True)
    yc = y - mu
    var = jnp.mean(yc * yc, -1, keepdims=True)
    return yc * lax.rsqrt(var + LN_EPS) * g + b


def _sigmoid(x):
    return 1.0 / (1.0 + jnp.exp(-x))


def _log_sigmoid(x):
    return jnp.minimum(x, 0.0) - jnp.log(1.0 + jnp.exp(-jnp.abs(x)))


def _ffn_ln_kernel(x_ref, wg_ref, wu_ref, wo_ref, g_ref, b_ref, o_ref, *, alpha):
    j = pl.program_id(1)

    @pl.when(j == 0)
    def _():
        o_ref[...] = jnp.zeros_like(o_ref)

    xb = x_ref[...].astype(BF16)
    gate = jnp.dot(xb, wg_ref[...], preferred_element_type=F32)
    up = jnp.dot(xb, wu_ref[...], preferred_element_type=F32)
    h = (gate * _sigmoid(gate) * up).astype(BF16)
    o_ref[...] += jnp.dot(h, wo_ref[...], preferred_element_type=F32)

    @pl.when(j == pl.num_programs(1) - 1)
    def _():
        y = alpha * x_ref[...] + 0.5 * o_ref[...]
        o_ref[...] = _layer_norm(y, g_ref[...], b_ref[...])


def _ffn_ln(x, w_in, w_out, ln_g, ln_b, layer, slot, ln_slot, *, alpha, tm, tf):
    t, d = x.shape
    f = w_out.shape[2]
    nf = f // tf
    assert t % tm == 0 and f % tf == 0 and w_in.shape[2:] == (d, 2 * f)
    ln_spec = pl.BlockSpec((None, None, 1, d), lambda i, j: (layer, ln_slot, 0, 0))
    return pl.pallas_call(
        functools.partial(_ffn_ln_kernel, alpha=alpha),
        grid=(t // tm, nf),
        in_specs=[
            pl.BlockSpec((tm, d), lambda i, j: (i, 0)),
            pl.BlockSpec((None, None, d, tf), lambda i, j: (layer, slot, 0, j)),
            pl.BlockSpec((None, None, d, tf), lambda i, j: (layer, slot, 0, nf + j)),
            pl.BlockSpec((None, None, tf, d), lambda i, j: (layer, slot, j, 0)),
            ln_spec,
            ln_spec,
        ],
        out_specs=pl.BlockSpec((tm, d), lambda i, j: (i, 0)),
        out_shape=jax.ShapeDtypeStruct((t, d), F32),
        compiler_params=_params("parallel", "arbitrary"),
        name="ffn_ln",
    )(x, w_in, w_in, w_out, ln_g, ln_b)


def _in_proj_kernel(x_ref, w_ref, wgc_ref, wgr_ref, y_ref, gc_ref, gr_ref, xb_ref):
    j = pl.program_id(1)

    @pl.when(j == 0)
    def _():
        xb = x_ref[...].astype(BF16)
        xb_ref[...] = xb
        gc_ref[...] = jnp.dot(xb, wgc_ref[...], preferred_element_type=F32)
        gr_ref[...] = lax.dot_general(wgr_ref[...], xb, NT_DIMS, preferred_element_type=F32)

    y_ref[...] = jnp.dot(xb_ref[...], w_ref[...], preferred_element_type=F32).astype(y_ref.dtype)


def _in_proj(x, w, n_main, *, tm, tn):
    t, d = x.shape
    assert t % tm == 0 and n_main % tn == 0
    w_gates = w[:, n_main:]
    w_gate_cols = _pad_cols(w_gates, LANES)
    w_gate_rows = jnp.transpose(w_gates[:, :SUBLANES])
    return pl.pallas_call(
        _in_proj_kernel,
        grid=(t // tm, n_main // tn),
        in_specs=[
            pl.BlockSpec((tm, d), lambda i, j: (i, 0)),
            pl.BlockSpec((d, tn), lambda i, j: (0, j)),
            pl.BlockSpec((d, LANES), lambda i, j: (0, 0)),
            pl.BlockSpec((SUBLANES, d), lambda i, j: (0, 0)),
        ],
        out_specs=[
            pl.BlockSpec((tm, tn), lambda i, j: (i, j)),
            pl.BlockSpec((tm, LANES), lambda i, j: (i, 0)),
            pl.BlockSpec((SUBLANES, tm), lambda i, j: (0, i)),
        ],
        out_shape=[
            jax.ShapeDtypeStruct((t, n_main), BF16),
            jax.ShapeDtypeStruct((t, LANES), F32),
            jax.ShapeDtypeStruct((SUBLANES, t), F32),
        ],
        scratch_shapes=[pltpu.VMEM((tm, d), BF16)],
        compiler_params=_params("parallel", "arbitrary"),
        name="in_proj",
    )(x, w, w_gate_cols, w_gate_rows)


def _diff_attn_kernel(q_ref, k_ref, v_ref, tab_ref, lam_ref, g_ref, o_ref,
                      vt_ref, bdiag_ref, bnear_ref, m_ref, acc_ref,
                      s0_ref, mx0_ref, sh0_ref, s1_ref, mx1_ref, sh1_ref, *, t, lam_init):
    qi = pl.program_id(2)
    dk, dv = A_QK_DIM, A_V_DIM
    nk = k_ref.shape[0] // t
    bufs = ((s0_ref, mx0_ref, sh0_ref), (s1_ref, mx1_ref, sh1_ref))

    @pl.when(qi == 0)
    def _():
        for c in range(nk):
            vt_ref[c, 0:dv, :] = jnp.transpose(v_ref[c * t:(c + 1) * t, :].astype(F32)).astype(BF16)
            vt_ref[c, dv:, :] = jnp.ones((vt_ref.shape[1] - dv, t), BF16)
        skew = pltpu.roll(jnp.broadcast_to(tab_ref[...], (t, 2 * t)), 0, 1, stride=1, stride_axis=0)
        key = lax.broadcasted_iota(jnp.int32, (t, t), 0)
        qry = lax.broadcasted_iota(jnp.int32, (t, t), 1)
        bdiag_ref[...] = jnp.where(qry >= key, skew[:, :t], -1e30)
        bnear_ref[...] = skew[:, t:]

    q_t = jnp.transpose(q_ref[...].astype(F32)) * (dk ** -0.5)
    feat = lax.broadcasted_iota(jnp.int32, q_t.shape, 0)
    qs_t = jnp.concatenate([jnp.where(feat < dk, q_t, 0.0), jnp.where(feat >= dk, q_t, 0.0)],
                           axis=1).astype(BF16)

    m_ref[...] = jnp.full_like(m_ref, -jnp.inf)
    acc_ref[...] = jnp.zeros_like(acc_ref)

    def score(kj, buf, bias_ref, shift):
        s_ref, mx_ref, sh_ref = bufs[buf]
        off = pl.multiple_of(kj * t, t)
        s = jnp.dot(k_ref[pl.ds(off, t), :], qs_t, preferred_element_type=F32)
        if bias_ref is not None:
            b = bias_ref[...]
            s = s + jnp.concatenate([b, b], axis=1)
        s_ref[...] = s
        mx_ref[...] = jnp.max(s, 0, keepdims=True) + shift
        sh_ref[...] = jnp.zeros_like(sh_ref) + shift

    def absorb(kj, buf):
        s_ref, mx_ref, sh_ref = bufs[buf]
        m_old = m_ref[...]
        m_new = jnp.maximum(m_old, mx_ref[...])
        p = jnp.exp(s_ref[...] - (m_new - sh_ref[...])).astype(BF16)
        acc_ref[...] = (jnp.exp(m_old - m_new) * acc_ref[...]
                        + jnp.dot(vt_ref[kj], p, preferred_element_type=F32))
        m_ref[...] = m_new

    c_far = tab_ref[:, 2 * t - 1:2 * t]
    score(qi, 0, bdiag_ref, 0.0)

    @pl.when(qi >= 1)
    def _():
        score(qi - 1, 1, bnear_ref, 0.0)
        absorb(qi, 0)

    def far_pair(p, carry):
        kj = qi - 2 - 2 * p
        score(kj, 0, None, c_far)
        absorb(kj + 1, 1)
        score(kj - 1, 1, None, c_far)
        absorb(kj, 0)
        return carry

    lax.fori_loop(0, jnp.maximum(qi - 1, 0) // 2, far_pair, 0)

    @pl.when((qi >= 2) & (qi % 2 == 0))
    def _():
        score(0, 0, None, c_far)
        absorb(1, 1)

    @pl.when(qi % 2 == 0)
    def _():
        absorb(0, 0)

    @pl.when(qi % 2 == 1)
    def _():
        absorb(0, 1)

    lv = lam_ref[...]
    lam = (jnp.exp(jnp.sum(lv[0:1] * lv[1:2], -1, keepdims=True))
           - jnp.exp(jnp.sum(lv[2:3] * lv[3:4], -1, keepdims=True)) + lam_init)
    acc = acc_ref[...]
    o_t = acc[0:dv] * (1.0 / acc[dv:dv + 1])
    out = jnp.transpose(o_t[:, :t] - lam * o_t[:, t:])
    hn = out * lax.rsqrt(jnp.mean(out * out, -1, keepdims=True) + NORM_EPS)
    o_ref[...] = (hn * g_ref[...] * (1.0 - lam_init)).astype(o_ref.dtype)


def _t5_bias_by_distance(rel_bias, n):
    r = jnp.arange(n, dtype=jnp.int32)
    max_exact = N_BUCKETS // 2
    rf = jnp.maximum(r, 1).astype(F32)
    large = max_exact + (jnp.log(rf / max_exact) / math.log(MAX_DISTANCE / max_exact)
                         * (N_BUCKETS - max_exact)).astype(jnp.int32)
    large = jnp.minimum(large, N_BUCKETS - 1)
    bucket = jnp.where(r < max_exact, r, large)
    return jnp.transpose(rel_bias[bucket]).astype(F32)


def _diff_attention(y3, rel_bias, lam_vecs, diff_g, *, lam_init, t):
    bsz, s, _ = y3.shape
    hd = 2 * A_QK_DIM
    dv = A_V_DIM
    assert hd == dv == LANES and s % t == 0 and t >= MAX_DISTANCE
    nq = s // t
    table = _t5_bias_by_distance(rel_bias, 2 * t).reshape(A_HEADS, 1, 2 * t)
    return pl.pallas_call(
        functools.partial(_diff_attn_kernel, t=t, lam_init=lam_init),
        grid=(bsz, A_HEADS, nq),
        in_specs=[
            pl.BlockSpec((None, t, hd), lambda b, h, q: (b, q, h)),
            pl.BlockSpec((None, s, hd), lambda b, h, q: (b, 0, A_HEADS + h)),
            pl.BlockSpec((None, s, dv), lambda b, h, q: (b, 0, 2 * A_HEADS + h)),
            pl.BlockSpec((None, 1, 2 * t), lambda b, h, q: (h, 0, 0)),
            pl.BlockSpec((4, A_QK_DIM), lambda b, h, q: (0, 0)),
            pl.BlockSpec((1, dv), lambda b, h, q: (0, h)),
        ],
        out_specs=pl.BlockSpec((None, t, dv), lambda b, h, q: (b, q, h)),
        out_shape=jax.ShapeDtypeStruct((bsz, s, A_HEADS * dv), BF16),
        scratch_shapes=[pltpu.VMEM((nq, dv + ATTN_ONES_ROWS, t), BF16),
                        pltpu.VMEM((t, t), F32),
                        pltpu.VMEM((t, t), F32),
                        pltpu.VMEM((1, 2 * t), F32),
                        pltpu.VMEM((dv + ATTN_ONES_ROWS, 2 * t), F32)]
                       + 2 * [pltpu.VMEM((t, 2 * t), F32), pltpu.VMEM((1, 2 * t), F32),
                              pltpu.VMEM((1, 2 * t), F32)],
        compiler_params=_params("parallel", "parallel", "arbitrary"),
        name="diff_attn",
    )(y3, y3, y3, table, lam_vecs, diff_g)


def _mlstm_kernel(qk_ref, v_ref, ob_ref, gc_ref, gr_ref, cw_ref, cb_ref, bc_ref, br_ref, g_ref, y_ref,
                  xp_ref, c_ref, n_ref, m_ref, *, chunk):
    heads, dk, dv = B_HEADS, B_QK_DIM, B_V_DIM
    pad = SUBLANES
    c = pl.program_id(1)

    @pl.when(c == 0)
    def _():
        xp_ref[0:pad, :] = jnp.zeros((pad, xp_ref.shape[1]), F32)
        c_ref[...] = jnp.zeros_like(c_ref)
        n_ref[...] = jnp.zeros_like(n_ref)
        m_ref[...] = jnp.zeros_like(m_ref)

    x = qk_ref[...].astype(F32)
    xp_ref[pad:pad + chunk, :] = x
    cw = cw_ref[...]
    conv = cb_ref[...]
    for j in range(CONV_WIDTH):
        conv = conv + cw[j:j + 1, :] * xp_ref[pl.ds(pad - (CONV_WIDTH - 1) + j, chunk), :]
    xp_ref[0:pad, :] = x[chunk - pad:chunk, :]
    qk = conv * _sigmoid(conv)

    gc = gc_ref[...] + bc_ref[...]
    gr = gr_ref[...] + br_ref[...]
    row = lax.broadcasted_iota(jnp.int32, (chunk, chunk), 0)
    col = lax.broadcasted_iota(jnp.int32, (chunk, chunk), 1)
    causal = row >= col
    tril = causal.astype(F32)
    triu = (row <= col).astype(F32)
    b_c = jnp.dot(tril, _log_sigmoid(gc), precision=HIGHEST, preferred_element_type=F32)
    b_r = jnp.dot(_log_sigmoid(gr), triu, precision=HIGHEST, preferred_element_type=F32)

    ob = ob_ref[...].astype(F32)
    g = g_ref[...]
    for h in range(heads):
        q = qk[:, h * dk:(h + 1) * dk] * (dk ** -0.5)
        k = qk[:, (heads + h) * dk:(heads + h + 1) * dk]
        v = v_ref[:, h * dv:(h + 1) * dv]
        b_col = b_c[:, heads + h:heads + h + 1]
        i_col = gc[:, h:h + 1]
        b_row = b_r[heads + h:heads + h + 1, :]
        i_row = gr[h:h + 1, :]
        m_prev = m_ref[h][0:1, 0:1]

        d = jnp.where(causal, b_col - b_row + i_row, -jnp.inf)
        a_col = b_col + m_prev
        m_t = jnp.maximum(a_col, jnp.max(d, -1, keepdims=True))
        w_inter = jnp.exp(a_col - m_t)
        qb = q.astype(BF16)
        sw = lax.dot_general(qb, k.astype(BF16), NT_DIMS, preferred_element_type=F32) * jnp.exp(d - m_t)
        c_mat = c_ref[h]
        num = (w_inter * jnp.dot(qb, c_mat.astype(BF16), preferred_element_type=F32)
               + jnp.dot(sw.astype(BF16), v, preferred_element_type=F32))
        den = w_inter * jnp.sum(q * n_ref[h], -1, keepdims=True) + jnp.sum(sw, -1, keepdims=True)
        hh = num / jnp.maximum(jnp.abs(den), jnp.exp(-m_t))

        b_last = b_row[:, chunk - 1:chunk]
        g_col = b_last - b_col + i_col
        m_new = jnp.maximum(b_last + m_prev, jnp.max(g_col, 0, keepdims=True))
        decay = jnp.exp(b_last + m_prev - m_new)
        kw = k * jnp.exp(g_col - m_new)
        c_ref[h] = decay * c_mat + lax.dot_general(kw.astype(BF16), v, TN_DIMS, preferred_element_type=F32)
        n_ref[h] = decay * n_ref[h] + jnp.sum(kw, 0, keepdims=True)
        m_ref[h] = jnp.broadcast_to(m_new, m_ref.shape[1:])

        hn = hh * lax.rsqrt(jnp.mean(hh * hh, -1, keepdims=True) + NORM_EPS)
        sl = slice(h * dv, (h + 1) * dv)
        y_ref[:, sl] = (hn * g[:, sl] * _sigmoid(ob[:, sl])).astype(y_ref.dtype)


def _mlstm(y3, gcol3, grow3, conv_w, conv_b, gate_bias_cols, gate_bias_rows, mlstm_g, *, chunk):
    bsz, s, _ = y3.shape
    heads, dk, dv = B_HEADS, B_QK_DIM, B_V_DIM
    w = heads * dv
    assert 2 * heads * dk == w and s % chunk == 0 and chunk % LANES == 0
    nc = s // chunk
    return pl.pallas_call(
        functools.partial(_mlstm_kernel, chunk=chunk),
        grid=(bsz, nc),
        in_specs=[
            pl.BlockSpec((None, chunk, w), lambda b, c: (b, c, 3)),
            pl.BlockSpec((None, chunk, w), lambda b, c: (b, c, 4)),
            pl.BlockSpec((None, chunk, w), lambda b, c: (b, c, 5)),
            pl.BlockSpec((None, chunk, LANES), lambda b, c: (b, c, 0)),
            pl.BlockSpec((SUBLANES, chunk), lambda b, c: (0, b * nc + c)),
            pl.BlockSpec((CONV_WIDTH, w), lambda b, c: (0, 0)),
            pl.BlockSpec((1, w), lambda b, c: (0, 0)),
            pl.BlockSpec((1, LANES), lambda b, c: (0, 0)),
            pl.BlockSpec((SUBLANES, 1), lambda b, c: (0, 0)),
            pl.BlockSpec((1, w), lambda b, c: (0, 0)),
        ],
        out_specs=pl.BlockSpec((None, chunk, w), lambda b, c: (b, c, 0)),
        out_shape=jax.ShapeDtypeStruct((bsz, s, w), BF16),
        scratch_shapes=[
            pltpu.VMEM((SUBLANES + chunk, w), F32),
            pltpu.VMEM((heads, dk, dv), F32),
            pltpu.VMEM((heads, 1, dk), F32),
            pltpu.VMEM((heads, SUBLANES, LANES), F32),
        ],
        compiler_params=_params("parallel", "arbitrary"),
        name="mlstm",
    )(y3, y3, y3, gcol3, grow3, conv_w, conv_b, gate_bias_cols, gate_bias_rows, mlstm_g)


def _gla_kernel(q_ref, k_ref, v_ref, r_ref, a1_ref, wa_ref, ba_ref, g_ref, y_ref, st_ref, *, chunk):
    heads, dk, dv = C_HEADS, C_QK_DIM, C_V_DIM
    c = pl.program_id(1)

    @pl.when(c == 0)
    def _():
        st_ref[...] = jnp.zeros_like(st_ref)

    z = jnp.dot(a1_ref[...], wa_ref[...], precision=HIGHEST, preferred_element_type=F32) + ba_ref[...]
    log_a = _log_sigmoid(z) * (1.0 / C_GATE_TEMP)
    row = lax.broadcasted_iota(jnp.int32, (chunk, chunk), 0)
    col = lax.broadcasted_iota(jnp.int32, (chunk, chunk), 1)
    causal = row >= col
    b_all = jnp.dot(causal.astype(F32), log_a, precision=HIGHEST, preferred_element_type=F32)

    r = r_ref[...].astype(F32)
    g = g_ref[...]
    for h in range(heads):
        ks = slice(h * dk, (h + 1) * dk)
        vs = slice(h * dv, (h + 1) * dv)
        b = b_all[:, ks]
        q = q_ref[:, ks].astype(F32) * (dk ** -0.5)
        k = k_ref[:, ks].astype(F32)
        v = v_ref[:, vs]
        q_dec = (q * jnp.exp(b)).astype(BF16)
        k_dec = (k * jnp.exp(-b)).astype(BF16)
        att = jnp.where(causal, lax.dot_general(q_dec, k_dec, NT_DIMS, preferred_element_type=F32), 0.0)
        st = st_ref[h]
        o = (jnp.dot(att.astype(BF16), v, preferred_element_type=F32)
             + lax.dot_general(q_dec, st.astype(BF16), NT_DIMS, preferred_element_type=F32))
        b_last = b[chunk - 1:chunk, :]
        k_upd = (k * jnp.exp(b_last - b)).astype(BF16)
        st_ref[h] = jnp.exp(b_last) * st + lax.dot_general(v, k_upd, TN_DIMS, preferred_element_type=F32)

        on = o * lax.rsqrt(jnp.mean(o * o, -1, keepdims=True) + NORM_EPS)
        rr = r[:, vs]
        y_ref[:, vs] = (on * g[:, vs] * (rr * _sigmoid(rr))).astype(y_ref.dtype)


def _gla(y3, a13, w_a2, b_a, gla_g, *, chunk):
    bsz, s, _ = y3.shape
    heads, dk, dv = C_HEADS, C_QK_DIM, C_V_DIM
    wk, wv = heads * dk, heads * dv
    assert wv == 2 * wk and s % chunk == 0
    nc = s // chunk
    return pl.pallas_call(
        functools.partial(_gla_kernel, chunk=chunk),
        grid=(bsz, nc),
        in_specs=[
            pl.BlockSpec((None, chunk, wk), lambda b, c: (b, c, 0)),
            pl.BlockSpec((None, chunk, wk), lambda b, c: (b, c, 1)),
            pl.BlockSpec((None, chunk, wv), lambda b, c: (b, c, 1)),
            pl.BlockSpec((None, chunk, wv), lambda b, c: (b, c, 2)),
            pl.BlockSpec((None, chunk, LANES), lambda b, c: (b, c, 0)),
            pl.BlockSpec((LANES, wk), lambda b, c: (0, 0)),
            pl.BlockSpec((1, wk), lambda b, c: (0, 0)),
            pl.BlockSpec((1, wv), lambda b, c: (0, 0)),
        ],
        out_specs=pl.BlockSpec((None, chunk, wv), lambda b, c: (b, c, 0)),
        out_shape=jax.ShapeDtypeStruct((bsz, s, wv), BF16),
        scratch_shapes=[pltpu.VMEM((heads, dv, dk), F32)],
        compiler_params=_params("parallel", "arbitrary"),
        name="gla",
    )(y3, y3, y3, y3, a13, w_a2, b_a, gla_g)


def _out_ln_kernel(*refs, n_in, alpha):
    y_refs, w_refs = refs[:n_in], refs[n_in:2 * n_in]
    x_ref, g_ref, b_ref, o_ref = refs[2 * n_in:]
    acc = jnp.dot(y_refs[0][...], w_refs[0][...], preferred_element_type=F32)
    for y_ref, w_ref in zip(y_refs[1:], w_refs[1:]):
        acc = acc + jnp.dot(y_ref[...], w_ref[...], preferred_element_type=F32)
    o_ref[...] = _layer_norm(alpha * x_ref[...] + acc, g_ref[...], b_ref[...])


def _out_ln(ys, w, x, ln_g, ln_b, layer, w_layer, ln_slot, *, alpha, tm):
    t, d = x.shape
    n_in = len(ys)
    wd = ys[0].shape[1]
    assert all(y.shape[1] == wd for y in ys) and w.shape[1:] == (n_in * wd, d)
    ln_spec = pl.BlockSpec((None, None, 1, d), lambda i: (layer, ln_slot, 0, 0))
    in_specs = ([pl.BlockSpec((tm, wd), lambda i: (i, 0)) for _ in ys]
                + [pl.BlockSpec((None, wd, d), lambda i, k=k: (w_layer, k, 0)) for k in range(n_in)]
                + [pl.BlockSpec((tm, d), lambda i: (i, 0)), ln_spec, ln_spec])
    return pl.pallas_call(
        functools.partial(_out_ln_kernel, n_in=n_in, alpha=alpha),
        grid=(t // tm,),
        in_specs=in_specs,
        out_specs=pl.BlockSpec((tm, d), lambda i: (i, 0)),
        out_shape=jax.ShapeDtypeStruct((t, d), F32),
        compiler_params=_params("parallel"),
        name="out_ln",
    )(*ys, *([w] * n_in), x, ln_g, ln_b)


def _ple_kernel(x_ref, p_ref, wp_ref, wg_ref, o_ref):
    x = x_ref[...]
    e = jnp.dot(p_ref[...].astype(BF16), wp_ref[...], preferred_element_type=F32)
    gate = jnp.dot(x.astype(BF16), wg_ref[...], preferred_element_type=F32)
    o_ref[...] = x + e * _sigmoid(gate)


def _ple(x, p, w_proj, w_gate, layer, *, tm):
    t, d = x.shape
    pd = p.shape[2]
    return pl.pallas_call(
        _ple_kernel,
        grid=(t // tm,),
        in_specs=[
            pl.BlockSpec((tm, d), lambda i: (i, 0)),
            pl.BlockSpec((None, tm, pd), lambda i: (layer, i, 0)),
            pl.BlockSpec((None, pd, d), lambda i: (layer, 0, 0)),
            pl.BlockSpec((None, d, d), lambda i: (layer, 0, 0)),
        ],
        out_specs=pl.BlockSpec((tm, d), lambda i: (i, 0)),
        out_shape=jax.ShapeDtypeStruct((t, d), F32),
        compiler_params=_params("parallel"),
        name="ple",
    )(x, p, w_proj, w_gate)


def _tiles(t, s):
    return dict(
        ffn_tm=min(512, t), ffn_tf=512,
        proj_tm=min(1024, t), proj_tn=1024,
        row_tm=min(512, t),
        attn_t=min(512, s),
        mlstm_chunk=min(128, s),
        gla_chunk=min(64, s),
    )


def _pad_cols(w, n):
    return jnp.pad(w, ((0, 0), (0, n - w.shape[1])))


def kernel(x, p, ln_g, ln_b, w_ffn_in, w_ffn_out, w_in_ab, w_out_ab, rel_bias, lambda_q1, lambda_k1,
           lambda_q2, lambda_k2, diff_norm, conv_w, conv_b, b_igate, b_fgate, mlstm_norm, w_in_c,
           w_alpha2, b_alpha, gla_norm, w_out_c, w_ple_proj, w_ple_gate):
    bsz, s, d = x.shape
    depth = p.shape[0]
    t = bsz * s
    tl = _tiles(t, s)
    alpha = (2 * depth) ** 0.25
    a_w = A_HEADS * A_V_DIM
    ab_main = 3 * a_w + 3 * B_HEADS * B_V_DIM
    c_main = 2 * C_HEADS * C_QK_DIM + 2 * C_HEADS * C_V_DIM
    row = lambda a: a.reshape(1, -1)

    wfi, wfo = w_ffn_in.astype(BF16), w_ffn_out.astype(BF16)
    w_ab, w_c = w_in_ab.astype(BF16), w_in_c.astype(BF16)
    wo_ab, wo_c = w_out_ab.astype(BF16), w_out_c.astype(BF16)
    wpp, wpg = w_ple_proj.astype(BF16), w_ple_gate.astype(BF16)
    g4, b4 = ln_g.reshape(depth, 3, 1, d), ln_b.reshape(depth, 3, 1, d)
    p3 = p.reshape(depth, t, -1)

    xf = x.reshape(t, d)
    for i in range(depth):
        xf = _ffn_ln(xf, wfi, wfo, g4, b4, i, 0, 0, alpha=alpha, tm=tl["ffn_tm"], tf=tl["ffn_tf"])
        if i % 2 == 0:
            e = i // 2
            lam_init = 0.8 - 0.6 * math.exp(-0.3 * i)
            y, gcol, grow = _in_proj(xf, w_ab[e], ab_main, tm=tl["proj_tm"], tn=tl["proj_tn"])
            y3 = y.reshape(bsz, s, ab_main)
            lam_vecs = jnp.stack([lambda_q1[e], lambda_k1[e], lambda_q2[e], lambda_k2[e]])
            ya = _diff_attention(y3, rel_bias, lam_vecs, row(diff_norm[e]), lam_init=lam_init, t=tl["attn_t"])
            gate_bias = jnp.concatenate([b_igate[e], b_fgate[e]])
            yb = _mlstm(y3, gcol.reshape(bsz, s, LANES), grow,
                        conv_w[e], row(conv_b[e]), _pad_cols(row(gate_bias), LANES), gate_bias.reshape(-1, 1),
                        row(mlstm_norm[e]), chunk=tl["mlstm_chunk"])
            ys, w_out, w_layer = [ya.reshape(t, a_w), yb.reshape(t, -1)], wo_ab, e
        else:
            o = i // 2
            y, a1, _ = _in_proj(xf, w_c[o], c_main, tm=tl["proj_tm"], tn=tl["proj_tn"])
            w_a2 = jnp.pad(w_alpha2[o], ((0, LANES - w_alpha2.shape[1]), (0, 0)))
            yc = _gla(y.reshape(bsz, s, c_main), a1.reshape(bsz, s, LANES), w_a2, row(b_alpha[o]),
                      row(gla_norm[o]), chunk=tl["gla_chunk"])
            ys, w_out, w_layer = [yc.reshape(t, -1)], wo_c, o
        xf = _out_ln(ys, w_out, xf, g4, b4, i, w_layer, 1, alpha=alpha, tm=tl["row_tm"])
        xf = _ffn_ln(xf, wfi, wfo, g4, b4, i, 1, 2, alpha=alpha, tm=tl["ffn_tm"], tf=tl["ffn_tf"])
        xf = _ple(xf, p3, wpp, wpg, i, tm=tl["row_tm"])
    return xf.reshape(bsz, s, d)
```

```python
import functools
import math

import jax
import jax.numpy as jnp
from jax import lax
from jax.experimental import pallas as pl
from jax.experimental.pallas import tpu as pltpu

F32 = jnp.float32
BF16 = jnp.bfloat16
HIGHEST = lax.Precision.HIGHEST

A_HEADS = 8
A_QK_DIM = 64
A_V_DIM = 128
B_HEADS = 4
B_QK_DIM = 128
B_V_DIM = 256
CONV_WIDTH = 4
C_HEADS = 4
C_QK_DIM = 256
C_V_DIM = 512
C_GATE_TEMP = 16.0
N_BUCKETS = 32
MAX_DISTANCE = 128
LN_EPS = 1e-5
NORM_EPS = 1e-6

V7X_VMEM_LIMIT_BYTES = 56 * 1024 * 1024
LANES = 128
SUBLANES = 8
ATTN_ONES_ROWS = 16
ROW_SPLIT = 4

NT_DIMS = (((1,), (1,)), ((), ()))
TN_DIMS = (((0,), (0,)), ((), ()))


def _params(*sem):
    return pltpu.CompilerParams(dimension_semantics=sem, vmem_limit_bytes=V7X_VMEM_LIMIT_BYTES)


def _layer_norm(y, g, b, eps=LN_EPS):
    mu = jnp.mean(y, -1, keepdims=True)
    yc = y - mu
    var = jnp.mean(yc * yc, -1, keepdims=True)
    return yc * lax.rsqrt(var + eps) * g + b


def _sigmoid(x):
    return 1.0 / (1.0 + jnp.exp(-x))


def _log_sigmoid(x):
    return jnp.minimum(x, 0.0) - jnp.log(1.0 + jnp.exp(-jnp.abs(x)))


def _ffn_ln_kernel(x_ref, wg0_ref, wu0_ref, wo0_ref, wg1_ref, wu1_ref, wo1_ref, g_ref, b_ref, o_ref, xb_ref,
                   *, alpha, n_chunks):
    j = pl.program_id(1)
    last = pl.num_programs(1) - 1

    @pl.when(j == 0)
    def _():
        x = x_ref[...]
        xb_ref[...] = x.astype(BF16)
        o_ref[...] = (2.0 * alpha) * x

    def chunk(wg_ref, wu_ref, wo_ref):
        xb = xb_ref[...]
        gate = jnp.dot(xb, wg_ref[...], preferred_element_type=F32)
        up = jnp.dot(xb, wu_ref[...], preferred_element_type=F32)
        h = (gate * _sigmoid(gate) * up).astype(BF16)
        o_ref[...] += jnp.dot(h, wo_ref[...], preferred_element_type=F32)

    if n_chunks % 2 == 0:
        chunk(wg0_ref, wu0_ref, wo0_ref)
        chunk(wg1_ref, wu1_ref, wo1_ref)
    else:
        @pl.when(j < last)
        def _():
            chunk(wg0_ref, wu0_ref, wo0_ref)
            chunk(wg1_ref, wu1_ref, wo1_ref)

        @pl.when(j == last)
        def _():
            chunk(wg0_ref, wu0_ref, wo0_ref)

    @pl.when(j == last)
    def _():
        o_ref[...] = _layer_norm(o_ref[...], g_ref[...], b_ref[...], eps=4.0 * LN_EPS)


def _ffn_ln(x, w_in, w_out, ln_g, ln_b, layer, slot, ln_slot, *, alpha, tm, tf):
    t, d = x.shape
    f = w_out.shape[2]
    nf = f // tf
    assert t % tm == 0 and f % tf == 0 and w_in.shape[2:] == (d, 2 * f) and nf >= 2
    steps = (nf + 1) // 2
    first = lambda j: 2 * j
    second = lambda j: jnp.where(2 * j + 1 < nf, 2 * j + 1, nf - 2)
    ln_spec = pl.BlockSpec((None, None, 1, d), lambda i, j: (layer, ln_slot, 0, 0))

    def weight_specs(chunk_of):
        return [pl.BlockSpec((None, None, d, tf), lambda i, j: (layer, slot, 0, chunk_of(j))),
                pl.BlockSpec((None, None, d, tf), lambda i, j: (layer, slot, 0, nf + chunk_of(j))),
                pl.BlockSpec((None, None, tf, d), lambda i, j: (layer, slot, chunk_of(j), 0))]

    return pl.pallas_call(
        functools.partial(_ffn_ln_kernel, alpha=alpha, n_chunks=nf),
        grid=(t // tm, steps),
        in_specs=[pl.BlockSpec((tm, d), lambda i, j: (i, 0))] + weight_specs(first) + weight_specs(second)
                 + [ln_spec, ln_spec],
        out_specs=pl.BlockSpec((tm, d), lambda i, j: (i, 0)),
        out_shape=jax.ShapeDtypeStruct((t, d), F32),
        scratch_shapes=[pltpu.VMEM((tm, d), BF16)],
        compiler_params=_params("parallel", "arbitrary"),
        name="ffn_ln",
    )(x, w_in, w_in, w_out, w_in, w_in, w_out, ln_g, ln_b)


def _in_proj_kernel(x_ref, w_ref, wgc_ref, wgr_ref, y_ref, gc_ref, gr_ref, xb_ref):
    j = pl.program_id(1)

    @pl.when(j == 0)
    def _():
        xb = x_ref[...].astype(BF16)
        xb_ref[...] = xb
        gc_ref[...] = jnp.dot(xb, wgc_ref[...], preferred_element_type=F32)
        gr_ref[...] = lax.dot_general(wgr_ref[...], xb, NT_DIMS, preferred_element_type=F32)

    y_ref[...] = jnp.dot(xb_ref[...], w_ref[...], preferred_element_type=F32).astype(y_ref.dtype)


def _in_proj(x, w_stack, layer, n_main, *, tm, tn):
    t, d = x.shape
    assert t % tm == 0 and n_main % tn == 0
    w_gates = w_stack[layer, :, n_main:]
    w_gate_cols = _pad_cols(w_gates, LANES)
    w_gate_rows = jnp.transpose(w_gates[:, :SUBLANES])
    return pl.pallas_call(
        _in_proj_kernel,
        grid=(t // tm, n_main // tn),
        in_specs=[
            pl.BlockSpec((tm, d), lambda i, j: (i, 0)),
            pl.BlockSpec((None, d, tn), lambda i, j: (layer, 0, j)),
            pl.BlockSpec((d, LANES), lambda i, j: (0, 0)),
            pl.BlockSpec((SUBLANES, d), lambda i, j: (0, 0)),
        ],
        out_specs=[
            pl.BlockSpec((tm, tn), lambda i, j: (i, j)),
            pl.BlockSpec((tm, LANES), lambda i, j: (i, 0)),
            pl.BlockSpec((SUBLANES, tm), lambda i, j: (0, i)),
        ],
        out_shape=[
            jax.ShapeDtypeStruct((t, n_main), BF16),
            jax.ShapeDtypeStruct((t, LANES), F32),
            jax.ShapeDtypeStruct((SUBLANES, t), F32),
        ],
        scratch_shapes=[pltpu.VMEM((tm, d), BF16)],
        compiler_params=_params("parallel", "arbitrary"),
        name="in_proj",
    )(x, w_stack, w_gate_cols, w_gate_rows)


def _diff_attn_kernel(q_ref, k_ref, v_ref, tab_ref, lam_ref, g_ref, o_ref,
                      vt_ref, bdiag_ref, bnear_ref, m_ref, acc_ref,
                      s0_ref, mx0_ref, sh0_ref, s1_ref, mx1_ref, sh1_ref, *, t, lam_init):
    qi = pl.program_id(2)
    dk, dv = A_QK_DIM, A_V_DIM
    nk = k_ref.shape[0] // t
    bufs = ((s0_ref, mx0_ref, sh0_ref), (s1_ref, mx1_ref, sh1_ref))

    @pl.when(qi == 0)
    def _():
        for c in range(nk):
            vt_ref[c, 0:dv, :] = jnp.transpose(v_ref[c * t:(c + 1) * t, :].astype(F32)).astype(BF16)
            vt_ref[c, dv:, :] = jnp.ones((vt_ref.shape[1] - dv, t), BF16)
        skew = pltpu.roll(jnp.broadcast_to(tab_ref[...], (t, 2 * t)), 0, 1, stride=1, stride_axis=0)
        key = lax.broadcasted_iota(jnp.int32, (t, t), 0)
        qry = lax.broadcasted_iota(jnp.int32, (t, t), 1)
        bdiag_ref[...] = jnp.where(qry >= key, skew[:, :t], -1e30)
        bnear_ref[...] = skew[:, t:]

    q_t = jnp.transpose(q_ref[...].astype(F32)) * (dk ** -0.5)
    feat = lax.broadcasted_iota(jnp.int32, q_t.shape, 0)
    qs_t = jnp.concatenate([jnp.where(feat < dk, q_t, 0.0), jnp.where(feat >= dk, q_t, 0.0)],
                           axis=1).astype(BF16)

    m_ref[...] = jnp.full_like(m_ref, -jnp.inf)
    acc_ref[...] = jnp.zeros_like(acc_ref)

    def score(kj, buf, bias_ref, shift):
        s_ref, mx_ref, sh_ref = bufs[buf]
        off = pl.multiple_of(kj * t, t)
        s = jnp.dot(k_ref[pl.ds(off, t), :], qs_t, preferred_element_type=F32)
        if bias_ref is not None:
            b = bias_ref[...]
            s = s + jnp.concatenate([b, b], axis=1)
        s_ref[...] = s
        mx_ref[...] = jnp.max(s, 0, keepdims=True) + shift
        sh_ref[...] = jnp.zeros_like(sh_ref) + shift

    def absorb(kj, buf):
        s_ref, mx_ref, sh_ref = bufs[buf]
        m_old = m_ref[...]
        m_new = jnp.maximum(m_old, mx_ref[...])
        p = jnp.exp(s_ref[...] - (m_new - sh_ref[...])).astype(BF16)
        acc_ref[...] = (jnp.exp(m_old - m_new) * acc_ref[...]
                        + jnp.dot(vt_ref[kj], p, preferred_element_type=F32))
        m_ref[...] = m_new

    c_far = tab_ref[:, 2 * t - 1:2 * t]
    score(qi, 0, bdiag_ref, 0.0)

    @pl.when(qi >= 1)
    def _():
        score(qi - 1, 1, bnear_ref, 0.0)
        absorb(qi, 0)

    def far_pair(p, carry):
        kj = qi - 2 - 2 * p
        score(kj, 0, None, c_far)
        absorb(kj + 1, 1)
        score(kj - 1, 1, None, c_far)
        absorb(kj, 0)
        return carry

    lax.fori_loop(0, jnp.maximum(qi - 1, 0) // 2, far_pair, 0)

    @pl.when((qi >= 2) & (qi % 2 == 0))
    def _():
        score(0, 0, None, c_far)
        absorb(1, 1)

    @pl.when(qi % 2 == 0)
    def _():
        absorb(0, 0)

    @pl.when(qi % 2 == 1)
    def _():
        absorb(0, 1)

    lv = lam_ref[...]
    lam = (jnp.exp(jnp.sum(lv[0:1] * lv[1:2], -1, keepdims=True))
           - jnp.exp(jnp.sum(lv[2:3] * lv[3:4], -1, keepdims=True)) + lam_init)
    acc = acc_ref[...]
    o_t = acc[0:dv] * (1.0 / acc[dv:dv + 1])
    out = jnp.transpose(o_t[:, :t] - lam * o_t[:, t:])
    hn = out * lax.rsqrt(jnp.mean(out * out, -1, keepdims=True) + NORM_EPS)
    o_ref[...] = (hn * g_ref[...] * (1.0 - lam_init)).astype(o_ref.dtype)


def _t5_bias_by_distance(rel_bias, n):
    r = jnp.arange(n, dtype=jnp.int32)
    max_exact = N_BUCKETS // 2
    rf = jnp.maximum(r, 1).astype(F32)
    large = max_exact + (jnp.log(rf / max_exact) / math.log(MAX_DISTANCE / max_exact)
                         * (N_BUCKETS - max_exact)).astype(jnp.int32)
    large = jnp.minimum(large, N_BUCKETS - 1)
    bucket = jnp.where(r < max_exact, r, large)
    return jnp.transpose(rel_bias[bucket]).astype(F32)


def _diff_attention(y3, rel_bias, lam_vecs, diff_g, *, lam_init, t):
    bsz, s, _ = y3.shape
    hd = 2 * A_QK_DIM
    dv = A_V_DIM
    assert hd == dv == LANES and s % t == 0 and t >= MAX_DISTANCE
    nq = s // t
    table = _t5_bias_by_distance(rel_bias, 2 * t).reshape(A_HEADS, 1, 2 * t)
    return pl.pallas_call(
        functools.partial(_diff_attn_kernel, t=t, lam_init=lam_init),
        grid=(bsz, A_HEADS, nq),
        in_specs=[
            pl.BlockSpec((None, t, hd), lambda b, h, q: (b, q, h)),
            pl.BlockSpec((None, s, hd), lambda b, h, q: (b, 0, A_HEADS + h)),
            pl.BlockSpec((None, s, dv), lambda b, h, q: (b, 0, 2 * A_HEADS + h)),
            pl.BlockSpec((None, 1, 2 * t), lambda b, h, q: (h, 0, 0)),
            pl.BlockSpec((4, A_QK_DIM), lambda b, h, q: (0, 0)),
            pl.BlockSpec((1, dv), lambda b, h, q: (0, h)),
        ],
        out_specs=pl.BlockSpec((None, t, dv), lambda b, h, q: (b, q, h)),
        out_shape=jax.ShapeDtypeStruct((bsz, s, A_HEADS * dv), BF16),
        scratch_shapes=[pltpu.VMEM((nq, dv + ATTN_ONES_ROWS, t), BF16),
                        pltpu.VMEM((t, t), F32),
                        pltpu.VMEM((t, t), F32),
                        pltpu.VMEM((1, 2 * t), F32),
                        pltpu.VMEM((dv + ATTN_ONES_ROWS, 2 * t), F32)]
                       + 2 * [pltpu.VMEM((t, 2 * t), F32), pltpu.VMEM((1, 2 * t), F32),
                              pltpu.VMEM((1, 2 * t), F32)],
        compiler_params=_params("parallel", "parallel", "arbitrary"),
        name="diff_attn",
    )(y3, y3, y3, table, lam_vecs, diff_g)


def _mlstm_kernel(qk_ref, v_ref, ob_ref, gc_ref, gr_ref, cw_ref, cb_ref, bc_ref, br_ref, g_ref, y_ref,
                  xp_ref, c_ref, n_ref, m_ref, *, chunk):
    heads, dk, dv = B_HEADS, B_QK_DIM, B_V_DIM
    pad = SUBLANES
    c = pl.program_id(1)

    @pl.when(c == 0)
    def _():
        xp_ref[0:pad, :] = jnp.zeros((pad, xp_ref.shape[1]), F32)
        c_ref[...] = jnp.zeros_like(c_ref)
        n_ref[...] = jnp.zeros_like(n_ref)
        m_ref[...] = jnp.zeros_like(m_ref)

    x = qk_ref[...].astype(F32)
    xp_ref[pad:pad + chunk, :] = x
    cw = cw_ref[...]
    conv = cb_ref[...]
    for j in range(CONV_WIDTH):
        conv = conv + cw[j:j + 1, :] * xp_ref[pl.ds(pad - (CONV_WIDTH - 1) + j, chunk), :]
    xp_ref[0:pad, :] = x[chunk - pad:chunk, :]
    qk = conv * _sigmoid(conv)

    gc = gc_ref[...] + bc_ref[...]
    gr = gr_ref[...] + br_ref[...]
    row = lax.broadcasted_iota(jnp.int32, (chunk, chunk), 0)
    col = lax.broadcasted_iota(jnp.int32, (chunk, chunk), 1)
    causal = row >= col
    tril = causal.astype(F32)
    triu = (row <= col).astype(F32)
    b_c = jnp.dot(tril, _log_sigmoid(gc), precision=HIGHEST, preferred_element_type=F32)
    b_r = jnp.dot(_log_sigmoid(gr), triu, precision=HIGHEST, preferred_element_type=F32)

    ob = ob_ref[...].astype(F32)
    g = g_ref[...]
    for h in range(heads):
        q = qk[:, h * dk:(h + 1) * dk] * (dk ** -0.5)
        k = qk[:, (heads + h) * dk:(heads + h + 1) * dk]
        v = v_ref[:, h * dv:(h + 1) * dv]
        b_col = b_c[:, heads + h:heads + h + 1]
        i_col = gc[:, h:h + 1]
        b_row = b_r[heads + h:heads + h + 1, :]
        i_row = gr[h:h + 1, :]
        m_prev = m_ref[h][0:1, 0:1]

        d = jnp.where(causal, b_col - b_row + i_row, -jnp.inf)
        a_col = b_col + m_prev
        m_t = jnp.maximum(a_col, jnp.max(d, -1, keepdims=True))
        w_inter = jnp.exp(a_col - m_t)
        qb = q.astype(BF16)
        sw = lax.dot_general(qb, k.astype(BF16), NT_DIMS, preferred_element_type=F32) * jnp.exp(d - m_t)
        c_mat = c_ref[h]
        num = (w_inter * jnp.dot(qb, c_mat.astype(BF16), preferred_element_type=F32)
               + jnp.dot(sw.astype(BF16), v, preferred_element_type=F32))
        den = w_inter * jnp.sum(q * n_ref[h], -1, keepdims=True) + jnp.sum(sw, -1, keepdims=True)
        hh = num / jnp.maximum(jnp.abs(den), jnp.exp(-m_t))

        b_last = b_row[:, chunk - 1:chunk]
        g_col = b_last - b_col + i_col
        m_new = jnp.maximum(b_last + m_prev, jnp.max(g_col, 0, keepdims=True))
        decay = jnp.exp(b_last + m_prev - m_new)
        kw = k * jnp.exp(g_col - m_new)
        c_ref[h] = decay * c_mat + lax.dot_general(kw.astype(BF16), v, TN_DIMS, preferred_element_type=F32)
        n_ref[h] = decay * n_ref[h] + jnp.sum(kw, 0, keepdims=True)
        m_ref[h] = jnp.broadcast_to(m_new, m_ref.shape[1:])

        hn = hh * lax.rsqrt(jnp.mean(hh * hh, -1, keepdims=True) + NORM_EPS)
        sl = slice(h * dv, (h + 1) * dv)
        y_ref[:, sl] = (hn * g[:, sl] * _sigmoid(ob[:, sl])).astype(y_ref.dtype)


def _mlstm(y3, gcol3, grow3, conv_w, conv_b, gate_bias_cols, gate_bias_rows, mlstm_g, *, chunk):
    bsz, s, _ = y3.shape
    heads, dk, dv = B_HEADS, B_QK_DIM, B_V_DIM
    w = heads * dv
    assert 2 * heads * dk == w and s % chunk == 0 and chunk % LANES == 0
    nc = s // chunk
    return pl.pallas_call(
        functools.partial(_mlstm_kernel, chunk=chunk),
        grid=(bsz, nc),
        in_specs=[
            pl.BlockSpec((None, chunk, w), lambda b, c: (b, c, 3)),
            pl.BlockSpec((None, chunk, w), lambda b, c: (b, c, 4)),
            pl.BlockSpec((None, chunk, w), lambda b, c: (b, c, 5)),
            pl.BlockSpec((None, chunk, LANES), lambda b, c: (b, c, 0)),
            pl.BlockSpec((SUBLANES, chunk), lambda b, c: (0, b * nc + c)),
            pl.BlockSpec((CONV_WIDTH, w), lambda b, c: (0, 0)),
            pl.BlockSpec((1, w), lambda b, c: (0, 0)),
            pl.BlockSpec((1, LANES), lambda b, c: (0, 0)),
            pl.BlockSpec((SUBLANES, 1), lambda b, c: (0, 0)),
            pl.BlockSpec((1, w), lambda b, c: (0, 0)),
        ],
        out_specs=pl.BlockSpec((None, chunk, w), lambda b, c: (b, c, 0)),
        out_shape=jax.ShapeDtypeStruct((bsz, s, w), BF16),
        scratch_shapes=[
            pltpu.VMEM((SUBLANES + chunk, w), F32),
            pltpu.VMEM((heads, dk, dv), F32),
            pltpu.VMEM((heads, 1, dk), F32),
            pltpu.VMEM((heads, SUBLANES, LANES), F32),
        ],
        compiler_params=_params("parallel", "arbitrary"),
        name="mlstm",
    )(y3, y3, y3, gcol3, grow3, conv_w, conv_b, gate_bias_cols, gate_bias_rows, mlstm_g)


def _gla_kernel(q_ref, k_ref, v_ref, r_ref, a1_ref, wa_ref, ba_ref, g_ref, y_ref, st_ref, *, chunk):
    heads, dk, dv = C_HEADS, C_QK_DIM, C_V_DIM
    c = pl.program_id(1)

    @pl.when(c == 0)
    def _():
        st_ref[...] = jnp.zeros_like(st_ref)

    z = jnp.dot(a1_ref[...], wa_ref[...], precision=HIGHEST, preferred_element_type=F32) + ba_ref[...]
    log_a = _log_sigmoid(z) * (1.0 / C_GATE_TEMP)
    row = lax.broadcasted_iota(jnp.int32, (chunk, chunk), 0)
    col = lax.broadcasted_iota(jnp.int32, (chunk, chunk), 1)
    causal = row >= col
    b_all = jnp.dot(causal.astype(F32), log_a, precision=HIGHEST, preferred_element_type=F32)

    r = r_ref[...].astype(F32)
    g = g_ref[...]
    for h in range(heads):
        ks = slice(h * dk, (h + 1) * dk)
        vs = slice(h * dv, (h + 1) * dv)
        b = b_all[:, ks]
        q = q_ref[:, ks].astype(F32) * (dk ** -0.5)
        k = k_ref[:, ks].astype(F32)
        v = v_ref[:, vs]
        q_dec = (q * jnp.exp(b)).astype(BF16)
        k_dec = (k * jnp.exp(-b)).astype(BF16)
        att = jnp.where(causal, lax.dot_general(q_dec, k_dec, NT_DIMS, preferred_element_type=F32), 0.0)
        st = st_ref[h]
        o = (jnp.dot(att.astype(BF16), v, preferred_element_type=F32)
             + lax.dot_general(q_dec, st.astype(BF16), NT_DIMS, preferred_element_type=F32))
        b_last = b[chunk - 1:chunk, :]
        k_upd = (k * jnp.exp(b_last - b)).astype(BF16)
        st_ref[h] = jnp.exp(b_last) * st + lax.dot_general(v, k_upd, TN_DIMS, preferred_element_type=F32)

        on = o * lax.rsqrt(jnp.mean(o * o, -1, keepdims=True) + NORM_EPS)
        rr = r[:, vs]
        y_ref[:, vs] = (on * g[:, vs] * (rr * _sigmoid(rr))).astype(y_ref.dtype)


def _gla(y3, a13, w_a2, b_a, gla_g, *, chunk):
    bsz, s, _ = y3.shape
    heads, dk, dv = C_HEADS, C_QK_DIM, C_V_DIM
    wk, wv = heads * dk, heads * dv
    assert wv == 2 * wk and s % chunk == 0
    nc = s // chunk
    return pl.pallas_call(
        functools.partial(_gla_kernel, chunk=chunk),
        grid=(bsz, nc),
        in_specs=[
            pl.BlockSpec((None, chunk, wk), lambda b, c: (b, c, 0)),
            pl.BlockSpec((None, chunk, wk), lambda b, c: (b, c, 1)),
            pl.BlockSpec((None, chunk, wv), lambda b, c: (b, c, 1)),
            pl.BlockSpec((None, chunk, wv), lambda b, c: (b, c, 2)),
            pl.BlockSpec((None, chunk, LANES), lambda b, c: (b, c, 0)),
            pl.BlockSpec((LANES, wk), lambda b, c: (0, 0)),
            pl.BlockSpec((1, wk), lambda b, c: (0, 0)),
            pl.BlockSpec((1, wv), lambda b, c: (0, 0)),
        ],
        out_specs=pl.BlockSpec((None, chunk, wv), lambda b, c: (b, c, 0)),
        out_shape=jax.ShapeDtypeStruct((bsz, s, wv), BF16),
        scratch_shapes=[pltpu.VMEM((heads, dv, dk), F32)],
        compiler_params=_params("parallel", "arbitrary"),
        name="gla",
    )(y3, y3, y3, y3, a13, w_a2, b_a, gla_g)


def _out_ln_kernel(*refs, n_in, alpha):
    y_refs, w_refs = refs[:n_in], refs[n_in:2 * n_in]
    x_ref, g_ref, b_ref, o_ref = refs[2 * n_in:]
    rb = o_ref.shape[0] // ROW_SPLIT
    for r in range(ROW_SPLIT):
        rows = slice(r * rb, (r + 1) * rb)
        acc = jnp.dot(y_refs[0][rows, :], w_refs[0][...], preferred_element_type=F32)
        for y_ref, w_ref in zip(y_refs[1:], w_refs[1:]):
            acc = acc + jnp.dot(y_ref[rows, :], w_ref[...], preferred_element_type=F32)
        o_ref[rows, :] = _layer_norm(alpha * x_ref[rows, :] + acc, g_ref[...], b_ref[...])


def _out_ln(ys, w, x, ln_g, ln_b, layer, w_layer, ln_slot, *, alpha, tm):
    t, d = x.shape
    n_in = len(ys)
    wd = ys[0].shape[1]
    assert all(y.shape[1] == wd for y in ys) and w.shape[1:] == (n_in * wd, d)
    ln_spec = pl.BlockSpec((None, None, 1, d), lambda i: (layer, ln_slot, 0, 0))
    in_specs = ([pl.BlockSpec((tm, wd), lambda i: (i, 0)) for _ in ys]
                + [pl.BlockSpec((None, wd, d), lambda i, k=k: (w_layer, k, 0)) for k in range(n_in)]
                + [pl.BlockSpec((tm, d), lambda i: (i, 0)), ln_spec, ln_spec])
    return pl.pallas_call(
        functools.partial(_out_ln_kernel, n_in=n_in, alpha=alpha),
        grid=(t // tm,),
        in_specs=in_specs,
        out_specs=pl.BlockSpec((tm, d), lambda i: (i, 0)),
        out_shape=jax.ShapeDtypeStruct((t, d), F32),
        compiler_params=_params("parallel"),
        name="out_ln",
    )(*ys, *([w] * n_in), x, ln_g, ln_b)


def _ple_kernel(x_ref, p_ref, wp_ref, wg_ref, o_ref):
    rb = o_ref.shape[0] // ROW_SPLIT
    for r in range(ROW_SPLIT):
        rows = slice(r * rb, (r + 1) * rb)
        x = x_ref[rows, :]
        e = jnp.dot(p_ref[rows, :].astype(BF16), wp_ref[...], preferred_element_type=F32)
        gate = jnp.dot(x.astype(BF16), wg_ref[...], preferred_element_type=F32)
        o_ref[rows, :] = x + e * _sigmoid(gate)


def _ple(x, p, w_proj, w_gate, layer, *, tm):
    t, d = x.shape
    s, pd = p.shape[2:]
    assert s % tm == 0
    per_seq = s // tm
    return pl.pallas_call(
        _ple_kernel,
        grid=(t // tm,),
        in_specs=[
            pl.BlockSpec((tm, d), lambda i: (i, 0)),
            pl.BlockSpec((None, None, tm, pd), lambda i: (layer, i // per_seq, i % per_seq, 0)),
            pl.BlockSpec((None, pd, d), lambda i: (layer, 0, 0)),
            pl.BlockSpec((None, d, d), lambda i: (layer, 0, 0)),
        ],
        out_specs=pl.BlockSpec((tm, d), lambda i: (i, 0)),
        out_shape=jax.ShapeDtypeStruct((t, d), F32),
        compiler_params=_params("parallel"),
        name="ple",
    )(x, p, w_proj, w_gate)


def _tiles(t, s):
    return dict(
        ffn_tm=min(512, t), ffn_tf=512,
        proj_tm=min(1024, t), proj_tn=1024,
        row_tm=min(512, t),
        attn_t=min(512, s),
        mlstm_chunk=min(128, s),
        gla_chunk=min(64, s),
    )


def _pad_cols(w, n):
    return jnp.pad(w, ((0, 0), (0, n - w.shape[1])))


def kernel(x, p, ln_g, ln_b, w_ffn_in, w_ffn_out, w_in_ab, w_out_ab, rel_bias, lambda_q1, lambda_k1,
           lambda_q2, lambda_k2, diff_norm, conv_w, conv_b, b_igate, b_fgate, mlstm_norm, w_in_c,
           w_alpha2, b_alpha, gla_norm, w_out_c, w_ple_proj, w_ple_gate):
    bsz, s, d = x.shape
    depth = p.shape[0]
    t = bsz * s
    tl = _tiles(t, s)
    alpha = (2 * depth) ** 0.25
    a_w = A_HEADS * A_V_DIM
    ab_main = 3 * a_w + 3 * B_HEADS * B_V_DIM
    c_main = 2 * C_HEADS * C_QK_DIM + 2 * C_HEADS * C_V_DIM
    row = lambda a: a.reshape(1, -1)

    wfi, wfo = w_ffn_in.astype(BF16), w_ffn_out.astype(BF16)
    w_ab, w_c = w_in_ab.astype(BF16), w_in_c.astype(BF16)
    wo_ab, wo_c = w_out_ab.astype(BF16), w_out_c.astype(BF16)
    wpp, wpg = w_ple_proj.astype(BF16), w_ple_gate.astype(BF16)
    g4, b4 = ln_g.reshape(depth, 3, 1, d), ln_b.reshape(depth, 3, 1, d)

    xf = x.reshape(t, d)
    for i in range(depth):
        xf = _ffn_ln(xf, wfi, wfo, g4, b4, i, 0, 0, alpha=alpha, tm=tl["ffn_tm"], tf=tl["ffn_tf"])
        if i % 2 == 0:
            e = i // 2
            lam_init = 0.8 - 0.6 * math.exp(-0.3 * i)
            y, gcol, grow = _in_proj(xf, w_ab, e, ab_main, tm=tl["proj_tm"], tn=tl["proj_tn"])
            y3 = y.reshape(bsz, s, ab_main)
            lam_vecs = jnp.stack([lambda_q1[e], lambda_k1[e], lambda_q2[e], lambda_k2[e]])
            ya = _diff_attention(y3, rel_bias, lam_vecs, row(diff_norm[e]), lam_init=lam_init, t=tl["attn_t"])
            gate_bias = jnp.concatenate([b_igate[e], b_fgate[e]])
            yb = _mlstm(y3, gcol.reshape(bsz, s, LANES), grow,
                        conv_w[e], row(conv_b[e]), _pad_cols(row(gate_bias), LANES), gate_bias.reshape(-1, 1),
                        row(mlstm_norm[e]), chunk=tl["mlstm_chunk"])
            ys, w_out, w_layer = [ya.reshape(t, a_w), yb.reshape(t, -1)], wo_ab, e
        else:
            o = i // 2
            y, a1, _ = _in_proj(xf, w_c, o, c_main, tm=tl["proj_tm"], tn=tl["proj_tn"])
            w_a2 = jnp.pad(w_alpha2[o], ((0, LANES - w_alpha2.shape[1]), (0, 0)))
            yc = _gla(y.reshape(bsz, s, c_main), a1.reshape(bsz, s, LANES), w_a2, row(b_alpha[o]),
                      row(gla_norm[o]), chunk=tl["gla_chunk"])
            ys, w_out, w_layer = [yc.reshape(t, -1)], wo_c, o
        xf = _out_ln(ys, w_out, xf, g4, b4, i, w_layer, 1, alpha=alpha, tm=tl["row_tm"])
        xf = _ffn_ln(xf, wfi, wfo, g4, b4, i, 1, 2, alpha=alpha, tm=tl["ffn_tm"], tf=tl["ffn_tf"])
        xf = _ple(xf, p, wpp, wpg, i, tm=tl["row_tm"])
    return xf.reshape(bsz, s, d)
```

```python
import functools
import math

import jax
import jax.numpy as jnp
from jax import lax
from jax.experimental import pallas as pl
from jax.experimental.pallas import tpu as pltpu

F32 = jnp.float32
BF16 = jnp.bfloat16
HIGHEST = lax.Precision.HIGHEST

A_HEADS = 8
A_QK_DIM = 64
A_V_DIM = 128
B_HEADS = 4
B_QK_DIM = 128
B_V_DIM = 256
CONV_WIDTH = 4
C_HEADS = 4
C_QK_DIM = 256
C_V_DIM = 512
C_GATE_TEMP = 16.0
N_BUCKETS = 32
MAX_DISTANCE = 128
LN_EPS = 1e-5
NORM_EPS = 1e-6

V7X_VMEM_LIMIT_BYTES = 56 * 1024 * 1024
LANES = 128
SUBLANES = 8
ATTN_ONES_ROWS = 16
ROW_SPLIT = 4

NT_DIMS = (((1,), (1,)), ((), ()))
TN_DIMS = (((0,), (0,)), ((), ()))


def _params(*sem):
    return pltpu.CompilerParams(dimension_semantics=sem, vmem_limit_bytes=V7X_VMEM_LIMIT_BYTES)


def _layer_norm(y, g, b, eps=LN_EPS):
    mu = jnp.mean(y, -1, keepdims=True)
    yc = y - mu
    var = jnp.mean(yc * yc, -1, keepdims=True)
    return yc * lax.rsqrt(var + eps) * g + b


def _sigmoid(x):
    return 1.0 / (1.0 + jnp.exp(-x))


def _log_sigmoid(x):
    return jnp.minimum(x, 0.0) - jnp.log(1.0 + jnp.exp(-jnp.abs(x)))


def _ffn_ln_kernel(x_ref, wg0_ref, wu0_ref, wo0_ref, wg1_ref, wu1_ref, wo1_ref, g_ref, b_ref, o_ref, xb_ref,
                   *, alpha, n_chunks):
    j = pl.program_id(1)
    last = pl.num_programs(1) - 1

    @pl.when(j == 0)
    def _():
        x = x_ref[...]
        xb_ref[...] = x.astype(BF16)
        o_ref[...] = (2.0 * alpha) * x

    def chunk(wg_ref, wu_ref, wo_ref):
        xb = xb_ref[...]
        gate = jnp.dot(xb, wg_ref[...], preferred_element_type=F32)
        up = jnp.dot(xb, wu_ref[...], preferred_element_type=F32)
        h = (gate * _sigmoid(gate) * up).astype(BF16)
        o_ref[...] += jnp.dot(h, wo_ref[...], preferred_element_type=F32)

    if n_chunks % 2 == 0:
        chunk(wg0_ref, wu0_ref, wo0_ref)
        chunk(wg1_ref, wu1_ref, wo1_ref)
    else:
        @pl.when(j < last)
        def _():
            chunk(wg0_ref, wu0_ref, wo0_ref)
            chunk(wg1_ref, wu1_ref, wo1_ref)

        @pl.when(j == last)
        def _():
            chunk(wg0_ref, wu0_ref, wo0_ref)

    @pl.when(j == last)
    def _():
        o_ref[...] = _layer_norm(o_ref[...], g_ref[...], b_ref[...], eps=4.0 * LN_EPS)


def _ffn_ln(x, w_in, w_out, ln_g, ln_b, layer, slot, ln_slot, *, alpha, tm, tf):
    t, d = x.shape
    f = w_out.shape[2]
    nf = f // tf
    assert t % tm == 0 and f % tf == 0 and w_in.shape[2:] == (d, 2 * f) and nf >= 2
    steps = (nf + 1) // 2
    first = lambda j: 2 * j
    second = lambda j: jnp.where(2 * j + 1 < nf, 2 * j + 1, nf - 2)
    ln_spec = pl.BlockSpec((None, None, 1, d), lambda i, j: (layer, ln_slot, 0, 0))

    def weight_specs(chunk_of):
        return [pl.BlockSpec((None, None, d, tf), lambda i, j: (layer, slot, 0, chunk_of(j))),
                pl.BlockSpec((None, None, d, tf), lambda i, j: (layer, slot, 0, nf + chunk_of(j))),
                pl.BlockSpec((None, None, tf, d), lambda i, j: (layer, slot, chunk_of(j), 0))]

    return pl.pallas_call(
        functools.partial(_ffn_ln_kernel, alpha=alpha, n_chunks=nf),
        grid=(t // tm, steps),
        in_specs=[pl.BlockSpec((tm, d), lambda i, j: (i, 0))] + weight_specs(first) + weight_specs(second)
                 + [ln_spec, ln_spec],
        out_specs=pl.BlockSpec((tm, d), lambda i, j: (i, 0)),
        out_shape=jax.ShapeDtypeStruct((t, d), F32),
        scratch_shapes=[pltpu.VMEM((tm, d), BF16)],
        compiler_params=_params("parallel", "arbitrary"),
        name="ffn_ln",
    )(x, w_in, w_in, w_out, w_in, w_in, w_out, ln_g, ln_b)


def _in_proj_kernel(x_ref, w_ref, wgc_ref, wgr_ref, y_ref, gc_ref, gr_ref, xb_ref):
    j = pl.program_id(1)

    @pl.when(j == 0)
    def _():
        xb = x_ref[...].astype(BF16)
        xb_ref[...] = xb
        gc_ref[...] = jnp.dot(xb, wgc_ref[...], preferred_element_type=F32)
        gr_ref[...] = lax.dot_general(wgr_ref[...], xb, NT_DIMS, preferred_element_type=F32)

    y_ref[...] = jnp.dot(xb_ref[...], w_ref[...], preferred_element_type=F32).astype(y_ref.dtype)


def _in_proj(x, w_stack, layer, n_main, *, tm, tn):
    t, d = x.shape
    assert t % tm == 0 and n_main % tn == 0
    w_main = w_stack[:, :, :n_main].astype(BF16)
    w_gates = w_stack[layer, :, n_main:].astype(BF16)
    w_gate_cols = _pad_cols(w_gates, LANES)
    w_gate_rows = jnp.transpose(w_gates[:, :SUBLANES])
    return pl.pallas_call(
        _in_proj_kernel,
        grid=(t // tm, n_main // tn),
        in_specs=[
            pl.BlockSpec((tm, d), lambda i, j: (i, 0)),
            pl.BlockSpec((None, d, tn), lambda i, j: (layer, 0, j)),
            pl.BlockSpec((d, LANES), lambda i, j: (0, 0)),
            pl.BlockSpec((SUBLANES, d), lambda i, j: (0, 0)),
        ],
        out_specs=[
            pl.BlockSpec((tm, tn), lambda i, j: (i, j)),
            pl.BlockSpec((tm, LANES), lambda i, j: (i, 0)),
            pl.BlockSpec((SUBLANES, tm), lambda i, j: (0, i)),
        ],
        out_shape=[
            jax.ShapeDtypeStruct((t, n_main), BF16),
            jax.ShapeDtypeStruct((t, LANES), F32),
            jax.ShapeDtypeStruct((SUBLANES, t), F32),
        ],
        scratch_shapes=[pltpu.VMEM((tm, d), BF16)],
        compiler_params=_params("parallel", "arbitrary"),
        name="in_proj",
    )(x, w_main, w_gate_cols, w_gate_rows)


def _diff_attn_kernel(q_ref, k_ref, v_ref, tab_ref, lam_ref, g_ref, o_ref,
                      vt_ref, bdiag_ref, bnear_ref, m_ref, acc_ref,
                      s0_ref, mx0_ref, sh0_ref, s1_ref, mx1_ref, sh1_ref, *, t, lam_init):
    qi = pl.program_id(2)
    dk, dv = A_QK_DIM, A_V_DIM
    nk = k_ref.shape[0] // t
    bufs = ((s0_ref, mx0_ref, sh0_ref), (s1_ref, mx1_ref, sh1_ref))

    @pl.when(qi == 0)
    def _():
        for c in range(nk):
            vt_ref[c, 0:dv, :] = jnp.transpose(v_ref[c * t:(c + 1) * t, :].astype(F32)).astype(BF16)
            vt_ref[c, dv:, :] = jnp.ones((vt_ref.shape[1] - dv, t), BF16)
        skew = pltpu.roll(jnp.broadcast_to(tab_ref[...], (t, 2 * t)), 0, 1, stride=1, stride_axis=0)
        key = lax.broadcasted_iota(jnp.int32, (t, t), 0)
        qry = lax.broadcasted_iota(jnp.int32, (t, t), 1)
        bdiag_ref[...] = jnp.where(qry >= key, skew[:, :t], -1e30)
        bnear_ref[...] = skew[:, t:]

    q_t = jnp.transpose(q_ref[...].astype(F32)) * (dk ** -0.5)
    feat = lax.broadcasted_iota(jnp.int32, q_t.shape, 0)
    qs_t = jnp.concatenate([jnp.where(feat < dk, q_t, 0.0), jnp.where(feat >= dk, q_t, 0.0)],
                           axis=1).astype(BF16)

    m_ref[...] = jnp.full_like(m_ref, -jnp.inf)
    acc_ref[...] = jnp.zeros_like(acc_ref)

    def score(kj, buf, bias_ref, shift):
        s_ref, mx_ref, sh_ref = bufs[buf]
        off = pl.multiple_of(kj * t, t)
        s = jnp.dot(k_ref[pl.ds(off, t), :], qs_t, preferred_element_type=F32)
        if bias_ref is not None:
            b = bias_ref[...]
            s = s + jnp.concatenate([b, b], axis=1)
        s_ref[...] = s
        mx_ref[...] = jnp.max(s, 0, keepdims=True) + shift
        sh_ref[...] = jnp.zeros_like(sh_ref) + shift

    def absorb(kj, buf):
        s_ref, mx_ref, sh_ref = bufs[buf]
        m_old = m_ref[...]
        m_new = jnp.maximum(m_old, mx_ref[...])
        p = jnp.exp(s_ref[...] - (m_new - sh_ref[...])).astype(BF16)
        acc_ref[...] = (jnp.exp(m_old - m_new) * acc_ref[...]
                        + jnp.dot(vt_ref[kj], p, preferred_element_type=F32))
        m_ref[...] = m_new

    c_far = tab_ref[:, 2 * t - 1:2 * t]
    score(qi, 0, bdiag_ref, 0.0)

    @pl.when(qi >= 1)
    def _():
        score(qi - 1, 1, bnear_ref, 0.0)
        absorb(qi, 0)

    def far_pair(p, carry):
        kj = qi - 2 - 2 * p
        score(kj, 0, None, c_far)
        absorb(kj + 1, 1)
        score(kj - 1, 1, None, c_far)
        absorb(kj, 0)
        return carry

    lax.fori_loop(0, jnp.maximum(qi - 1, 0) // 2, far_pair, 0)

    @pl.when((qi >= 2) & (qi % 2 == 0))
    def _():
        score(0, 0, None, c_far)
        absorb(1, 1)

    @pl.when(qi % 2 == 0)
    def _():
        absorb(0, 0)

    @pl.when(qi % 2 == 1)
    def _():
        absorb(0, 1)

    lv = lam_ref[...]
    lam = (jnp.exp(jnp.sum(lv[0:1] * lv[1:2], -1, keepdims=True))
           - jnp.exp(jnp.sum(lv[2:3] * lv[3:4], -1, keepdims=True)) + lam_init)
    acc = acc_ref[...]
    o_t = acc[0:dv] * (1.0 / acc[dv:dv + 1])
    out = jnp.transpose(o_t[:, :t] - lam * o_t[:, t:])
    hn = out * lax.rsqrt(jnp.mean(out * out, -1, keepdims=True) + NORM_EPS)
    o_ref[...] = (hn * g_ref[...] * (1.0 - lam_init)).astype(o_ref.dtype)


def _t5_bias_by_distance(rel_bias, n):
    r = jnp.arange(n, dtype=jnp.int32)
    max_exact = N_BUCKETS // 2
    rf = jnp.maximum(r, 1).astype(F32)
    large = max_exact + (jnp.log(rf / max_exact) / math.log(MAX_DISTANCE / max_exact)
                         * (N_BUCKETS - max_exact)).astype(jnp.int32)
    large = jnp.minimum(large, N_BUCKETS - 1)
    bucket = jnp.where(r < max_exact, r, large)
    return jnp.transpose(rel_bias[bucket]).astype(F32)


def _diff_attention(y3, rel_bias, lam_vecs, diff_g, *, lam_init, t):
    bsz, s, _ = y3.shape
    hd = 2 * A_QK_DIM
    dv = A_V_DIM
    assert hd == dv == LANES and s % t == 0 and t >= MAX_DISTANCE
    nq = s // t
    table = _t5_bias_by_distance(rel_bias, 2 * t).reshape(A_HEADS, 1, 2 * t)
    return pl.pallas_call(
        functools.partial(_diff_attn_kernel, t=t, lam_init=lam_init),
        grid=(bsz, A_HEADS, nq),
        in_specs=[
            pl.BlockSpec((None, t, hd), lambda b, h, q: (b, q, h)),
            pl.BlockSpec((None, s, hd), lambda b, h, q: (b, 0, A_HEADS + h)),
            pl.BlockSpec((None, s, dv), lambda b, h, q: (b, 0, 2 * A_HEADS + h)),
            pl.BlockSpec((None, 1, 2 * t), lambda b, h, q: (h, 0, 0)),
            pl.BlockSpec((4, A_QK_DIM), lambda b, h, q: (0, 0)),
            pl.BlockSpec((1, dv), lambda b, h, q: (0, h)),
        ],
        out_specs=pl.BlockSpec((None, t, dv), lambda b, h, q: (b, q, h)),
        out_shape=jax.ShapeDtypeStruct((bsz, s, A_HEADS * dv), BF16),
        scratch_shapes=[pltpu.VMEM((nq, dv + ATTN_ONES_ROWS, t), BF16),
                        pltpu.VMEM((t, t), F32),
                        pltpu.VMEM((t, t), F32),
                        pltpu.VMEM((1, 2 * t), F32),
                        pltpu.VMEM((dv + ATTN_ONES_ROWS, 2 * t), F32)]
                       + 2 * [pltpu.VMEM((t, 2 * t), F32), pltpu.VMEM((1, 2 * t), F32),
                              pltpu.VMEM((1, 2 * t), F32)],
        compiler_params=_params("parallel", "parallel", "arbitrary"),
        name="diff_attn",
    )(y3, y3, y3, table, lam_vecs, diff_g)


def _mlstm_kernel(qk_ref, v_ref, ob_ref, gc_ref, gr_ref, cw_ref, cb_ref, bc_ref, br_ref, g_ref, y_ref,
                  xp_ref, c_ref, n_ref, m_ref, *, chunk):
    heads, dk, dv = B_HEADS, B_QK_DIM, B_V_DIM
    pad = SUBLANES
    c = pl.program_id(1)

    @pl.when(c == 0)
    def _():
        xp_ref[0:pad, :] = jnp.zeros((pad, xp_ref.shape[1]), F32)
        c_ref[...] = jnp.zeros_like(c_ref)
        n_ref[...] = jnp.zeros_like(n_ref)
        m_ref[...] = jnp.zeros_like(m_ref)

    x = qk_ref[...].astype(F32)
    xp_ref[pad:pad + chunk, :] = x
    cw = cw_ref[...]
    conv = cb_ref[...]
    for j in range(CONV_WIDTH):
        conv = conv + cw[j:j + 1, :] * xp_ref[pl.ds(pad - (CONV_WIDTH - 1) + j, chunk), :]
    xp_ref[0:pad, :] = x[chunk - pad:chunk, :]
    qk = conv * _sigmoid(conv)

    gc = gc_ref[...] + bc_ref[...]
    gr = gr_ref[...] + br_ref[...]
    row = lax.broadcasted_iota(jnp.int32, (chunk, chunk), 0)
    col = lax.broadcasted_iota(jnp.int32, (chunk, chunk), 1)
    causal = row >= col
    tril = causal.astype(F32)
    triu = (row <= col).astype(F32)
    b_c = jnp.dot(tril, _log_sigmoid(gc), precision=HIGHEST, preferred_element_type=F32)
    b_r = jnp.dot(_log_sigmoid(gr), triu, precision=HIGHEST, preferred_element_type=F32)

    ob = ob_ref[...].astype(F32)
    g = g_ref[...]
    for h in range(heads):
        q = qk[:, h * dk:(h + 1) * dk] * (dk ** -0.5)
        k = qk[:, (heads + h) * dk:(heads + h + 1) * dk]
        v = v_ref[:, h * dv:(h + 1) * dv]
        b_col = b_c[:, heads + h:heads + h + 1]
        i_col = gc[:, h:h + 1]
        b_row = b_r[heads + h:heads + h + 1, :]
        i_row = gr[h:h + 1, :]
        m_prev = m_ref[h][0:1, 0:1]

        d = jnp.where(causal, b_col - b_row + i_row, -jnp.inf)
        a_col = b_col + m_prev
        m_t = jnp.maximum(a_col, jnp.max(d, -1, keepdims=True))
        w_inter = jnp.exp(a_col - m_t)
        qb = q.astype(BF16)
        sw = lax.dot_general(qb, k.astype(BF16), NT_DIMS, preferred_element_type=F32) * jnp.exp(d - m_t)
        c_mat = c_ref[h]
        num = (w_inter * jnp.dot(qb, c_mat.astype(BF16), preferred_element_type=F32)
               + jnp.dot(sw.astype(BF16), v, preferred_element_type=F32))
        den = w_inter * jnp.sum(q * n_ref[h], -1, keepdims=True) + jnp.sum(sw, -1, keepdims=True)
        hh = num / jnp.maximum(jnp.abs(den), jnp.exp(-m_t))

        b_last = b_row[:, chunk - 1:chunk]
        g_col = b_last - b_col + i_col
        m_new = jnp.maximum(b_last + m_prev, jnp.max(g_col, 0, keepdims=True))
        decay = jnp.exp(b_last + m_prev - m_new)
        kw = k * jnp.exp(g_col - m_new)
        c_ref[h] = decay * c_mat + lax.dot_general(kw.astype(BF16), v, TN_DIMS, preferred_element_type=F32)
        n_ref[h] = decay * n_ref[h] + jnp.sum(kw, 0, keepdims=True)
        m_ref[h] = jnp.broadcast_to(m_new, m_ref.shape[1:])

        hn = hh * lax.rsqrt(jnp.mean(hh * hh, -1, keepdims=True) + NORM_EPS)
        sl = slice(h * dv, (h + 1) * dv)
        y_ref[:, sl] = (hn * g[:, sl] * _sigmoid(ob[:, sl])).astype(y_ref.dtype)


def _mlstm(y3, gcol3, grow3, conv_w, conv_b, gate_bias_cols, gate_bias_rows, mlstm_g, *, chunk):
    bsz, s, _ = y3.shape
    heads, dk, dv = B_HEADS, B_QK_DIM, B_V_DIM
    w = heads * dv
    assert 2 * heads * dk == w and s % chunk == 0 and chunk % LANES == 0
    nc = s // chunk
    return pl.pallas_call(
        functools.partial(_mlstm_kernel, chunk=chunk),
        grid=(bsz, nc),
        in_specs=[
            pl.BlockSpec((None, chunk, w), lambda b, c: (b, c, 3)),
            pl.BlockSpec((None, chunk, w), lambda b, c: (b, c, 4)),
            pl.BlockSpec((None, chunk, w), lambda b, c: (b, c, 5)),
            pl.BlockSpec((None, chunk, LANES), lambda b, c: (b, c, 0)),
            pl.BlockSpec((SUBLANES, chunk), lambda b, c: (0, b * nc + c)),
            pl.BlockSpec((CONV_WIDTH, w), lambda b, c: (0, 0)),
            pl.BlockSpec((1, w), lambda b, c: (0, 0)),
            pl.BlockSpec((1, LANES), lambda b, c: (0, 0)),
            pl.BlockSpec((SUBLANES, 1), lambda b, c: (0, 0)),
            pl.BlockSpec((1, w), lambda b, c: (0, 0)),
        ],
        out_specs=pl.BlockSpec((None, chunk, w), lambda b, c: (b, c, 0)),
        out_shape=jax.ShapeDtypeStruct((bsz, s, w), BF16),
        scratch_shapes=[
            pltpu.VMEM((SUBLANES + chunk, w), F32),
            pltpu.VMEM((heads, dk, dv), F32),
            pltpu.VMEM((heads, 1, dk), F32),
            pltpu.VMEM((heads, SUBLANES, LANES), F32),
        ],
        compiler_params=_params("parallel", "arbitrary"),
        name="mlstm",
    )(y3, y3, y3, gcol3, grow3, conv_w, conv_b, gate_bias_cols, gate_bias_rows, mlstm_g)


def _gla_kernel(q_ref, k_ref, v_ref, r_ref, a1_ref, wa_ref, ba_ref, g_ref, y_ref, st_ref, *, chunk):
    heads, dk, dv = C_HEADS, C_QK_DIM, C_V_DIM
    c = pl.program_id(1)

    @pl.when(c == 0)
    def _():
        st_ref[...] = jnp.zeros_like(st_ref)

    z = jnp.dot(a1_ref[...], wa_ref[...], precision=HIGHEST, preferred_element_type=F32) + ba_ref[...]
    log_a = _log_sigmoid(z) * (1.0 / C_GATE_TEMP)
    row = lax.broadcasted_iota(jnp.int32, (chunk, chunk), 0)
    col = lax.broadcasted_iota(jnp.int32, (chunk, chunk), 1)
    causal = row >= col
    b_all = jnp.dot(causal.astype(F32), log_a, precision=HIGHEST, preferred_element_type=F32)

    r = r_ref[...].astype(F32)
    g = g_ref[...]
    for h in range(heads):
        ks = slice(h * dk, (h + 1) * dk)
        vs = slice(h * dv, (h + 1) * dv)
        b = b_all[:, ks]
        q = q_ref[:, ks].astype(F32) * (dk ** -0.5)
        k = k_ref[:, ks].astype(F32)
        v = v_ref[:, vs]
        q_dec = (q * jnp.exp(b)).astype(BF16)
        k_dec = (k * jnp.exp(-b)).astype(BF16)
        att = jnp.where(causal, lax.dot_general(q_dec, k_dec, NT_DIMS, preferred_element_type=F32), 0.0)
        st = st_ref[h]
        o = (jnp.dot(att.astype(BF16), v, preferred_element_type=F32)
             + lax.dot_general(q_dec, st.astype(BF16), NT_DIMS, preferred_element_type=F32))
        b_last = b[chunk - 1:chunk, :]
        k_upd = (k * jnp.exp(b_last - b)).astype(BF16)
        st_ref[h] = jnp.exp(b_last) * st + lax.dot_general(v, k_upd, TN_DIMS, preferred_element_type=F32)

        on = o * lax.rsqrt(jnp.mean(o * o, -1, keepdims=True) + NORM_EPS)
        rr = r[:, vs]
        y_ref[:, vs] = (on * g[:, vs] * (rr * _sigmoid(rr))).astype(y_ref.dtype)


def _gla(y3, a13, w_a2, b_a, gla_g, *, chunk):
    bsz, s, _ = y3.shape
    heads, dk, dv = C_HEADS, C_QK_DIM, C_V_DIM
    wk, wv = heads * dk, heads * dv
    assert wv == 2 * wk and s % chunk == 0
    nc = s // chunk
    return pl.pallas_call(
        functools.partial(_gla_kernel, chunk=chunk),
        grid=(bsz, nc),
        in_specs=[
            pl.BlockSpec((None, chunk, wk), lambda b, c: (b, c, 0)),
            pl.BlockSpec((None, chunk, wk), lambda b, c: (b, c, 1)),
            pl.BlockSpec((None, chunk, wv), lambda b, c: (b, c, 1)),
            pl.BlockSpec((None, chunk, wv), lambda b, c: (b, c, 2)),
            pl.BlockSpec((None, chunk, LANES), lambda b, c: (b, c, 0)),
            pl.BlockSpec((LANES, wk), lambda b, c: (0, 0)),
            pl.BlockSpec((1, wk), lambda b, c: (0, 0)),
            pl.BlockSpec((1, wv), lambda b, c: (0, 0)),
        ],
        out_specs=pl.BlockSpec((None, chunk, wv), lambda b, c: (b, c, 0)),
        out_shape=jax.ShapeDtypeStruct((bsz, s, wv), BF16),
        scratch_shapes=[pltpu.VMEM((heads, dv, dk), F32)],
        compiler_params=_params("parallel", "arbitrary"),
        name="gla",
    )(y3, y3, y3, y3, a13, w_a2, b_a, gla_g)


def _out_ln_kernel(*refs, n_in, alpha):
    y_refs, w_refs = refs[:n_in], refs[n_in:2 * n_in]
    x_ref, g_ref, b_ref, o_ref = refs[2 * n_in:]
    rb = o_ref.shape[0] // ROW_SPLIT
    for r in range(ROW_SPLIT):
        rows = slice(r * rb, (r + 1) * rb)
        acc = jnp.dot(y_refs[0][rows, :], w_refs[0][...], preferred_element_type=F32)
        for y_ref, w_ref in zip(y_refs[1:], w_refs[1:]):
            acc = acc + jnp.dot(y_ref[rows, :], w_ref[...], preferred_element_type=F32)
        o_ref[rows, :] = _layer_norm(alpha * x_ref[rows, :] + acc, g_ref[...], b_ref[...])


def _out_ln(ys, w, x, ln_g, ln_b, layer, w_layer, ln_slot, *, alpha, tm):
    t, d = x.shape
    n_in = len(ys)
    wd = ys[0].shape[1]
    assert all(y.shape[1] == wd for y in ys) and w.shape[1:] == (n_in * wd, d)
    ln_spec = pl.BlockSpec((None, None, 1, d), lambda i: (layer, ln_slot, 0, 0))
    in_specs = ([pl.BlockSpec((tm, wd), lambda i: (i, 0)) for _ in ys]
                + [pl.BlockSpec((None, wd, d), lambda i, k=k: (w_layer, k, 0)) for k in range(n_in)]
                + [pl.BlockSpec((tm, d), lambda i: (i, 0)), ln_spec, ln_spec])
    return pl.pallas_call(
        functools.partial(_out_ln_kernel, n_in=n_in, alpha=alpha),
        grid=(t // tm,),
        in_specs=in_specs,
        out_specs=pl.BlockSpec((tm, d), lambda i: (i, 0)),
        out_shape=jax.ShapeDtypeStruct((t, d), F32),
        compiler_params=_params("parallel"),
        name="out_ln",
    )(*ys, *([w] * n_in), x, ln_g, ln_b)


def _ple_kernel(x_ref, p_ref, wp_ref, wg_ref, o_ref):
    rb = o_ref.shape[0] // ROW_SPLIT
    for r in range(ROW_SPLIT):
        rows = slice(r * rb, (r + 1) * rb)
        x = x_ref[rows, :]
        e = jnp.dot(p_ref[rows, :].astype(BF16), wp_ref[...], preferred_element_type=F32)
        gate = jnp.dot(x.astype(BF16), wg_ref[...], preferred_element_type=F32)
        o_ref[rows, :] = x + e * _sigmoid(gate)


def _ple(x, p, w_proj, w_gate, layer, *, tm):
    t, d = x.shape
    s, pd = p.shape[2:]
    assert s % tm == 0
    per_seq = s // tm
    return pl.pallas_call(
        _ple_kernel,
        grid=(t // tm,),
        in_specs=[
            pl.BlockSpec((tm, d), lambda i: (i, 0)),
            pl.BlockSpec((None, None, tm, pd), lambda i: (layer, i // per_seq, i % per_seq, 0)),
            pl.BlockSpec((None, pd, d), lambda i: (layer, 0, 0)),
            pl.BlockSpec((None, d, d), lambda i: (layer, 0, 0)),
        ],
        out_specs=pl.BlockSpec((tm, d), lambda i: (i, 0)),
        out_shape=jax.ShapeDtypeStruct((t, d), F32),
        compiler_params=_params("parallel"),
        name="ple",
    )(x, p, w_proj, w_gate)


def _tiles(t, s):
    return dict(
        ffn_tm=min(256, t), ffn_tf=512,
        proj_tm=min(1024, t), proj_tn=1024,
        row_tm=min(512, t),
        attn_t=min(512, s),
        mlstm_chunk=min(128, s),
        gla_chunk=min(64, s),
    )


def _pad_cols(w, n):
    return jnp.pad(w, ((0, 0), (0, n - w.shape[1])))


def kernel(x, p, ln_g, ln_b, w_ffn_in, w_ffn_out, w_in_ab, w_out_ab, rel_bias, lambda_q1, lambda_k1,
           lambda_q2, lambda_k2, diff_norm, conv_w, conv_b, b_igate, b_fgate, mlstm_norm, w_in_c,
           w_alpha2, b_alpha, gla_norm, w_out_c, w_ple_proj, w_ple_gate):
    bsz, s, d = x.shape
    depth = p.shape[0]
    t = bsz * s
    tl = _tiles(t, s)
    alpha = (2 * depth) ** 0.25
    a_w = A_HEADS * A_V_DIM
    ab_main = 3 * a_w + 3 * B_HEADS * B_V_DIM
    c_main = 2 * C_HEADS * C_QK_DIM + 2 * C_HEADS * C_V_DIM
    row = lambda a: a.reshape(1, -1)

    wfi, wfo = w_ffn_in.astype(BF16), w_ffn_out.astype(BF16)
    wo_ab, wo_c = w_out_ab.astype(BF16), w_out_c.astype(BF16)
    wpp, wpg = w_ple_proj.astype(BF16), w_ple_gate.astype(BF16)
    g4, b4 = ln_g.reshape(depth, 3, 1, d), ln_b.reshape(depth, 3, 1, d)

    xf = x.reshape(t, d)
    for i in range(depth):
        xf = _ffn_ln(xf, wfi, wfo, g4, b4, i, 0, 0, alpha=alpha, tm=tl["ffn_tm"], tf=tl["ffn_tf"])
        if i % 2 == 0:
            e = i // 2
            lam_init = 0.8 - 0.6 * math.exp(-0.3 * i)
            y, gcol, grow = _in_proj(xf, w_in_ab, e, ab_main, tm=tl["proj_tm"], tn=tl["proj_tn"])
            y3 = y.reshape(bsz, s, ab_main)
            lam_vecs = jnp.stack([lambda_q1[e], lambda_k1[e], lambda_q2[e], lambda_k2[e]])
            ya = _diff_attention(y3, rel_bias, lam_vecs, row(diff_norm[e]), lam_init=lam_init, t=tl["attn_t"])
            gate_bias = jnp.concatenate([b_igate[e], b_fgate[e]])
            yb = _mlstm(y3, gcol.reshape(bsz, s, LANES), grow,
                        conv_w[e], row(conv_b[e]), _pad_cols(row(gate_bias), LANES), gate_bias.reshape(-1, 1),
                        row(mlstm_norm[e]), chunk=tl["mlstm_chunk"])
            ys, w_out, w_layer = [ya.reshape(t, a_w), yb.reshape(t, -1)], wo_ab, e
        else:
            o = i // 2
            y, a1, _ = _in_proj(xf, w_in_c, o, c_main, tm=tl["proj_tm"], tn=tl["proj_tn"])
            w_a2 = jnp.pad(w_alpha2[o], ((0, LANES - w_alpha2.shape[1]), (0, 0)))
            yc = _gla(y.reshape(bsz, s, c_main), a1.reshape(bsz, s, LANES), w_a2, row(b_alpha[o]),
                      row(gla_norm[o]), chunk=tl["gla_chunk"])
            ys, w_out, w_layer = [yc.reshape(t, -1)], wo_c, o
        xf = _out_ln(ys, w_out, xf, g4, b4, i, w_layer, 1, alpha=alpha, tm=tl["row_tm"])
        xf = _ffn_ln(xf, wfi, wfo, g4, b4, i, 1, 2, alpha=alpha, tm=tl["ffn_tm"], tf=tl["ffn_tf"])
        xf = _ple(xf, p, wpp, wpg, i, tm=tl["row_tm"])
    return xf.reshape(bsz, s, d)
```

```python
import functools
import math

import jax
import jax.numpy as jnp
from jax import lax
from jax.experimental import pallas as pl
from jax.experimental.pallas import tpu as pltpu

F32 = jnp.float32
BF16 = jnp.bfloat16
HIGHEST = lax.Precision.HIGHEST

A_HEADS = 8
A_QK_DIM = 64
A_V_DIM = 128
B_HEADS = 4
B_QK_DIM = 128
B_V_DIM = 256
CONV_WIDTH = 4
C_HEADS = 4
C_QK_DIM = 256
C_V_DIM = 512
C_GATE_TEMP = 16.0
N_BUCKETS = 32
MAX_DISTANCE = 128
LN_EPS = 1e-5
NORM_EPS = 1e-6

V7X_VMEM_LIMIT_BYTES = 56 * 1024 * 1024
LANES = 128
SUBLANES = 8
BF16_SUBLANES = 16
ATTN_ONES_ROWS = 16
ROW_SPLIT = 4

NT_DIMS = (((1,), (1,)), ((), ()))
TN_DIMS = (((0,), (0,)), ((), ()))


def _params(*sem):
    return pltpu.CompilerParams(dimension_semantics=sem, vmem_limit_bytes=V7X_VMEM_LIMIT_BYTES)


def _layer_norm(y, g, b, eps=LN_EPS):
    mu = jnp.mean(y, -1, keepdims=True)
    yc = y - mu
    var = jnp.mean(yc * yc, -1, keepdims=True)
    return yc * lax.rsqrt(var + eps) * g + b


def _sigmoid(x):
    return 1.0 / (1.0 + jnp.exp(-x))


def _log_sigmoid(x):
    return jnp.minimum(x, 0.0) - jnp.log(1.0 + jnp.exp(-jnp.abs(x)))


def _ffn_ln_kernel(x_ref, wg0_ref, wu0_ref, wo0_ref, wg1_ref, wu1_ref, wo1_ref, g_ref, b_ref, o_ref, xb_ref,
                   *, alpha, n_chunks):
    j = pl.program_id(1)
    last = pl.num_programs(1) - 1

    @pl.when(j == 0)
    def _():
        x = x_ref[...]
        xb_ref[...] = x.astype(BF16)
        o_ref[...] = (2.0 * alpha) * x

    def chunk(wg_ref, wu_ref, wo_ref):
        xb = xb_ref[...]
        gate = jnp.dot(xb, wg_ref[...], preferred_element_type=F32)
        up = jnp.dot(xb, wu_ref[...], preferred_element_type=F32)
        h = (gate * _sigmoid(gate) * up).astype(BF16)
        o_ref[...] += jnp.dot(h, wo_ref[...], preferred_element_type=F32)

    if n_chunks % 2 == 0:
        chunk(wg0_ref, wu0_ref, wo0_ref)
        chunk(wg1_ref, wu1_ref, wo1_ref)
    else:
        @pl.when(j < last)
        def _():
            chunk(wg0_ref, wu0_ref, wo0_ref)
            chunk(wg1_ref, wu1_ref, wo1_ref)

        @pl.when(j == last)
        def _():
            chunk(wg0_ref, wu0_ref, wo0_ref)

    @pl.when(j == last)
    def _():
        o_ref[...] = _layer_norm(o_ref[...], g_ref[...], b_ref[...], eps=4.0 * LN_EPS)


def _ffn_ln(x, w_in, w_out, ln_g, ln_b, layer, slot, ln_slot, *, alpha, tm, tf):
    t, d = x.shape
    f = w_out.shape[2]
    nf = f // tf
    assert t % tm == 0 and f % tf == 0 and w_in.shape[2:] == (d, 2 * f) and nf >= 2
    steps = (nf + 1) // 2
    first = lambda j: 2 * j
    second = lambda j: jnp.where(2 * j + 1 < nf, 2 * j + 1, nf - 2)
    ln_spec = pl.BlockSpec((None, None, 1, d), lambda i, j: (layer, ln_slot, 0, 0))

    def weight_specs(chunk_of):
        return [pl.BlockSpec((None, None, d, tf), lambda i, j: (layer, slot, 0, chunk_of(j))),
                pl.BlockSpec((None, None, d, tf), lambda i, j: (layer, slot, 0, nf + chunk_of(j))),
                pl.BlockSpec((None, None, tf, d), lambda i, j: (layer, slot, chunk_of(j), 0))]

    return pl.pallas_call(
        functools.partial(_ffn_ln_kernel, alpha=alpha, n_chunks=nf),
        grid=(t // tm, steps),
        in_specs=[pl.BlockSpec((tm, d), lambda i, j: (i, 0))] + weight_specs(first) + weight_specs(second)
                 + [ln_spec, ln_spec],
        out_specs=pl.BlockSpec((tm, d), lambda i, j: (i, 0)),
        out_shape=jax.ShapeDtypeStruct((t, d), F32),
        scratch_shapes=[pltpu.VMEM((tm, d), BF16)],
        compiler_params=_params("parallel", "arbitrary"),
        name="ffn_ln",
    )(x, w_in, w_in, w_out, w_in, w_in, w_out, ln_g, ln_b)


def _in_proj_kernel(x_ref, w_ref, wgc_ref, wgr_ref, y_ref, gc_ref, gr_ref, xb_ref):
    j = pl.program_id(1)

    @pl.when(j == 0)
    def _():
        xb = x_ref[...].astype(BF16)
        xb_ref[...] = xb
        gc_ref[...] = jnp.dot(xb, wgc_ref[...], preferred_element_type=F32)
        gr_ref[...] = lax.dot_general(wgr_ref[...], xb, NT_DIMS, preferred_element_type=F32)

    y_ref[...] = jnp.dot(xb_ref[...], w_ref[...], preferred_element_type=F32).astype(y_ref.dtype)


def _in_proj(x, w_stack, layer, n_main, *, tm, tn):
    t, d = x.shape
    assert t % tm == 0 and n_main % tn == 0
    w_main = w_stack[:, :, :n_main].astype(BF16)
    w_gates = w_stack[layer, :, n_main:].astype(BF16)
    w_gate_cols = _pad_cols(w_gates, LANES)
    w_gate_rows = jnp.transpose(w_gates[:, :SUBLANES])
    return pl.pallas_call(
        _in_proj_kernel,
        grid=(t // tm, n_main // tn),
        in_specs=[
            pl.BlockSpec((tm, d), lambda i, j: (i, 0)),
            pl.BlockSpec((None, d, tn), lambda i, j: (layer, 0, j)),
            pl.BlockSpec((d, LANES), lambda i, j: (0, 0)),
            pl.BlockSpec((SUBLANES, d), lambda i, j: (0, 0)),
        ],
        out_specs=[
            pl.BlockSpec((tm, tn), lambda i, j: (i, j)),
            pl.BlockSpec((tm, LANES), lambda i, j: (i, 0)),
            pl.BlockSpec((SUBLANES, tm), lambda i, j: (0, i)),
        ],
        out_shape=[
            jax.ShapeDtypeStruct((t, n_main), BF16),
            jax.ShapeDtypeStruct((t, LANES), F32),
            jax.ShapeDtypeStruct((SUBLANES, t), F32),
        ],
        scratch_shapes=[pltpu.VMEM((tm, d), BF16)],
        compiler_params=_params("parallel", "arbitrary"),
        name="in_proj",
    )(x, w_main, w_gate_cols, w_gate_rows)


def _diff_attn_kernel(q_ref, k_ref, v_ref, tab_ref, lam_ref, g_ref, o_ref,
                      vt_ref, bdiag_ref, bnear_ref, m_ref, acc_ref,
                      s0_ref, mx0_ref, sh0_ref, s1_ref, mx1_ref, sh1_ref, *, t, lam_init):
    qi = pl.program_id(2)
    dk, dv = A_QK_DIM, A_V_DIM
    nk = k_ref.shape[0] // t
    bufs = ((s0_ref, mx0_ref, sh0_ref), (s1_ref, mx1_ref, sh1_ref))

    @pl.when(qi == 0)
    def _():
        for c in range(nk):
            vt_ref[c, 0:dv, :] = jnp.transpose(v_ref[c * t:(c + 1) * t, :].astype(F32)).astype(BF16)
            vt_ref[c, dv:, :] = jnp.ones((vt_ref.shape[1] - dv, t), BF16)
        skew = pltpu.roll(jnp.broadcast_to(tab_ref[...], (t, 2 * t)), 0, 1, stride=1, stride_axis=0)
        key = lax.broadcasted_iota(jnp.int32, (t, t), 0)
        qry = lax.broadcasted_iota(jnp.int32, (t, t), 1)
        bdiag_ref[...] = jnp.where(qry >= key, skew[:, :t], -1e30)
        bnear_ref[...] = skew[:, t:]

    q_t = jnp.transpose(q_ref[...].astype(F32)) * (dk ** -0.5)
    feat = lax.broadcasted_iota(jnp.int32, q_t.shape, 0)
    qs_t = jnp.concatenate([jnp.where(feat < dk, q_t, 0.0), jnp.where(feat >= dk, q_t, 0.0)],
                           axis=1).astype(BF16)

    m_ref[...] = jnp.full_like(m_ref, -jnp.inf)
    acc_ref[...] = jnp.zeros_like(acc_ref)

    def score(kj, buf, bias_ref, shift):
        s_ref, mx_ref, sh_ref = bufs[buf]
        off = pl.multiple_of(kj * t, t)
        s = jnp.dot(k_ref[pl.ds(off, t), :], qs_t, preferred_element_type=F32)
        if bias_ref is not None:
            b = bias_ref[...]
            s = s + jnp.concatenate([b, b], axis=1)
        s_ref[...] = s
        mx_ref[...] = jnp.max(s, 0, keepdims=True) + shift
        sh_ref[...] = jnp.zeros_like(sh_ref) + shift

    def absorb(kj, buf):
        s_ref, mx_ref, sh_ref = bufs[buf]
        m_old = m_ref[...]
        m_new = jnp.maximum(m_old, mx_ref[...])
        p = jnp.exp(s_ref[...] - (m_new - sh_ref[...])).astype(BF16)
        acc_ref[...] = (jnp.exp(m_old - m_new) * acc_ref[...]
                        + jnp.dot(vt_ref[kj], p, preferred_element_type=F32))
        m_ref[...] = m_new

    c_far = tab_ref[:, 2 * t - 1:2 * t]
    score(qi, 0, bdiag_ref, 0.0)

    @pl.when(qi >= 1)
    def _():
        score(qi - 1, 1, bnear_ref, 0.0)
        absorb(qi, 0)

    def far_pair(p, carry):
        kj = qi - 2 - 2 * p
        score(kj, 0, None, c_far)
        absorb(kj + 1, 1)
        score(kj - 1, 1, None, c_far)
        absorb(kj, 0)
        return carry

    lax.fori_loop(0, jnp.maximum(qi - 1, 0) // 2, far_pair, 0)

    @pl.when((qi >= 2) & (qi % 2 == 0))
    def _():
        score(0, 0, None, c_far)
        absorb(1, 1)

    @pl.when(qi % 2 == 0)
    def _():
        absorb(0, 0)

    @pl.when(qi % 2 == 1)
    def _():
        absorb(0, 1)

    lv = lam_ref[...]
    lam = (jnp.exp(jnp.sum(lv[0:1] * lv[1:2], -1, keepdims=True))
           - jnp.exp(jnp.sum(lv[2:3] * lv[3:4], -1, keepdims=True)) + lam_init)
    acc = acc_ref[...]
    o_t = acc[0:dv] * (1.0 / acc[dv:dv + 1])
    out = jnp.transpose(o_t[:, :t] - lam * o_t[:, t:])
    hn = out * lax.rsqrt(jnp.mean(out * out, -1, keepdims=True) + NORM_EPS)
    o_ref[...] = (hn * g_ref[...] * (1.0 - lam_init)).astype(o_ref.dtype)


def _t5_bias_by_distance(rel_bias, n):
    r = jnp.arange(n, dtype=jnp.int32)
    max_exact = N_BUCKETS // 2
    rf = jnp.maximum(r, 1).astype(F32)
    large = max_exact + (jnp.log(rf / max_exact) / math.log(MAX_DISTANCE / max_exact)
                         * (N_BUCKETS - max_exact)).astype(jnp.int32)
    large = jnp.minimum(large, N_BUCKETS - 1)
    bucket = jnp.where(r < max_exact, r, large)
    return jnp.transpose(rel_bias[bucket]).astype(F32)


def _diff_attention(y3, rel_bias, lam_vecs, diff_g, *, lam_init, t):
    bsz, s, _ = y3.shape
    hd = 2 * A_QK_DIM
    dv = A_V_DIM
    assert hd == dv == LANES and s % t == 0 and t >= MAX_DISTANCE
    nq = s // t
    table = _t5_bias_by_distance(rel_bias, 2 * t).reshape(A_HEADS, 1, 2 * t)
    return pl.pallas_call(
        functools.partial(_diff_attn_kernel, t=t, lam_init=lam_init),
        grid=(bsz, A_HEADS, nq),
        in_specs=[
            pl.BlockSpec((None, t, hd), lambda b, h, q: (b, q, h)),
            pl.BlockSpec((None, s, hd), lambda b, h, q: (b, 0, A_HEADS + h)),
            pl.BlockSpec((None, s, dv), lambda b, h, q: (b, 0, 2 * A_HEADS + h)),
            pl.BlockSpec((None, 1, 2 * t), lambda b, h, q: (h, 0, 0)),
            pl.BlockSpec((4, A_QK_DIM), lambda b, h, q: (0, 0)),
            pl.BlockSpec((1, dv), lambda b, h, q: (0, h)),
        ],
        out_specs=pl.BlockSpec((None, t, dv), lambda b, h, q: (b, q, h)),
        out_shape=jax.ShapeDtypeStruct((bsz, s, A_HEADS * dv), BF16),
        scratch_shapes=[pltpu.VMEM((nq, dv + ATTN_ONES_ROWS, t), BF16),
                        pltpu.VMEM((t, t), F32),
                        pltpu.VMEM((t, t), F32),
                        pltpu.VMEM((1, 2 * t), F32),
                        pltpu.VMEM((dv + ATTN_ONES_ROWS, 2 * t), F32)]
                       + 2 * [pltpu.VMEM((t, 2 * t), F32), pltpu.VMEM((1, 2 * t), F32),
                              pltpu.VMEM((1, 2 * t), F32)],
        compiler_params=_params("parallel", "parallel", "arbitrary"),
        name="diff_attn",
    )(y3, y3, y3, table, lam_vecs, diff_g)


def _mlstm_kernel(qk_ref, v_ref, ob_ref, gc_ref, gr_ref, cw_ref, cb_ref, bc_ref, br_ref, g_ref, y_ref,
                  xp_ref, *state_refs, chunk):
    heads, dk, dv = B_HEADS, B_QK_DIM, B_V_DIM
    c_refs, n_refs, m_refs = state_refs[0::3], state_refs[1::3], state_refs[2::3]
    pad = xp_ref.shape[0] - chunk
    c = pl.program_id(1)

    @pl.when(c == 0)
    def _():
        xp_ref[0:pad, :] = jnp.zeros((pad, xp_ref.shape[1]), xp_ref.dtype)
        for ref in state_refs:
            ref[...] = jnp.zeros_like(ref)

    x = qk_ref[...]
    xp_ref[pad:pad + chunk, :] = x
    xp = xp_ref[...]
    cw = cw_ref[...]
    dst = lax.broadcasted_iota(jnp.int32, (chunk, pad + chunk), 0)
    src = lax.broadcasted_iota(jnp.int32, (chunk, pad + chunk), 1)
    conv = cb_ref[...] + cw[CONV_WIDTH - 1:CONV_WIDTH, :] * x.astype(F32)
    for j in range(CONV_WIDTH - 1):
        shift = (src == dst + (pad - (CONV_WIDTH - 1) + j)).astype(BF16)
        conv = conv + cw[j:j + 1, :] * jnp.dot(shift, xp, preferred_element_type=F32)
    xp_ref[0:pad, :] = x[chunk - pad:chunk, :]
    qk = conv * _sigmoid(conv)

    gc = gc_ref[...] + bc_ref[...]
    gr = gr_ref[...] + br_ref[...]
    row = lax.broadcasted_iota(jnp.int32, (chunk, chunk), 0)
    col = lax.broadcasted_iota(jnp.int32, (chunk, chunk), 1)
    causal = row >= col
    tril = causal.astype(F32)
    triu = (row <= col).astype(F32)
    b_c = jnp.dot(tril, _log_sigmoid(gc), precision=HIGHEST, preferred_element_type=F32)
    b_r = jnp.dot(_log_sigmoid(gr), triu, precision=HIGHEST, preferred_element_type=F32)

    ob = ob_ref[...].astype(F32)
    g = g_ref[...]
    hs = range(heads)
    q = [qk[:, h * dk:(h + 1) * dk] * (dk ** -0.5) for h in hs]
    k = [qk[:, (heads + h) * dk:(heads + h + 1) * dk] for h in hs]
    v = [v_ref[:, h * dv:(h + 1) * dv] for h in hs]
    b_col = [b_c[:, heads + h:heads + h + 1] for h in hs]
    i_col = [gc[:, h:h + 1] for h in hs]
    b_row = [b_r[heads + h:heads + h + 1, :] for h in hs]
    i_row = [gr[h:h + 1, :] for h in hs]
    m_prev = [m_refs[h][0:1, 0:1] for h in hs]
    c_mat = [c_refs[h][...] for h in hs]
    n_vec = [n_refs[h][...] for h in hs]

    b_last = [b_row[h][:, chunk - 1:chunk] for h in hs]
    g_col = [b_last[h] - b_col[h] + i_col[h] for h in hs]
    m_new = [jnp.maximum(b_last[h] + m_prev[h], jnp.max(g_col[h], 0, keepdims=True)) for h in hs]
    decay = [jnp.exp(b_last[h] + m_prev[h] - m_new[h]) for h in hs]
    kw = [k[h] * jnp.exp(g_col[h] - m_new[h]) for h in hs]
    qb = [q[h].astype(BF16) for h in hs]
    qk_t = [lax.dot_general(qb[h], k[h].astype(BF16), NT_DIMS, preferred_element_type=F32) for h in hs]
    q_c = [jnp.dot(qb[h], c_mat[h].astype(BF16), preferred_element_type=F32) for h in hs]
    kw_v = [lax.dot_general(kw[h].astype(BF16), v[h], TN_DIMS, preferred_element_type=F32) for h in hs]

    d = [jnp.where(causal, b_col[h] - b_row[h] + i_row[h], -jnp.inf) for h in hs]
    a_col = [b_col[h] + m_prev[h] for h in hs]
    m_t = [jnp.maximum(a_col[h], jnp.max(d[h], -1, keepdims=True)) for h in hs]
    w_inter = [jnp.exp(a_col[h] - m_t[h]) for h in hs]
    sw = [qk_t[h] * jnp.exp(d[h] - m_t[h]) for h in hs]
    sw_v = [jnp.dot(sw[h].astype(BF16), v[h], preferred_element_type=F32) for h in hs]

    for h in hs:
        c_refs[h][...] = decay[h] * c_mat[h] + kw_v[h]
        n_refs[h][...] = decay[h] * n_vec[h] + jnp.sum(kw[h], 0, keepdims=True)
        m_refs[h][...] = jnp.broadcast_to(m_new[h], m_refs[h].shape)

    for h in hs:
        num = w_inter[h] * q_c[h] + sw_v[h]
        den = (w_inter[h] * jnp.sum(q[h] * n_vec[h], -1, keepdims=True)
               + jnp.sum(sw[h], -1, keepdims=True))
        hh = num / jnp.maximum(jnp.abs(den), jnp.exp(-m_t[h]))
        hn = hh * lax.rsqrt(jnp.mean(hh * hh, -1, keepdims=True) + NORM_EPS)
        sl = slice(h * dv, (h + 1) * dv)
        y_ref[:, sl] = (hn * g[:, sl] * _sigmoid(ob[:, sl])).astype(y_ref.dtype)


def _mlstm(y3, gcol3, grow3, conv_w, conv_b, gate_bias_cols, gate_bias_rows, mlstm_g, *, chunk):
    bsz, s, _ = y3.shape
    heads, dk, dv = B_HEADS, B_QK_DIM, B_V_DIM
    w = heads * dv
    assert 2 * heads * dk == w and s % chunk == 0 and chunk % LANES == 0
    nc = s // chunk
    return pl.pallas_call(
        functools.partial(_mlstm_kernel, chunk=chunk),
        grid=(bsz, nc),
        in_specs=[
            pl.BlockSpec((None, chunk, w), lambda b, c: (b, c, 3)),
            pl.BlockSpec((None, chunk, w), lambda b, c: (b, c, 4)),
            pl.BlockSpec((None, chunk, w), lambda b, c: (b, c, 5)),
            pl.BlockSpec((None, chunk, LANES), lambda b, c: (b, c, 0)),
            pl.BlockSpec((SUBLANES, chunk), lambda b, c: (0, b * nc + c)),
            pl.BlockSpec((CONV_WIDTH, w), lambda b, c: (0, 0)),
            pl.BlockSpec((1, w), lambda b, c: (0, 0)),
            pl.BlockSpec((1, LANES), lambda b, c: (0, 0)),
            pl.BlockSpec((SUBLANES, 1), lambda b, c: (0, 0)),
            pl.BlockSpec((1, w), lambda b, c: (0, 0)),
        ],
        out_specs=pl.BlockSpec((None, chunk, w), lambda b, c: (b, c, 0)),
        out_shape=jax.ShapeDtypeStruct((bsz, s, w), BF16),
        scratch_shapes=[
            pltpu.VMEM((BF16_SUBLANES + chunk, w), BF16),
        ] + heads * [pltpu.VMEM((dk, dv), F32), pltpu.VMEM((1, dk), F32), pltpu.VMEM((SUBLANES, LANES), F32)],
        compiler_params=_params("parallel", "arbitrary"),
        name="mlstm",
    )(y3, y3, y3, gcol3, grow3, conv_w, conv_b, gate_bias_cols, gate_bias_rows, mlstm_g)


def _gla_kernel(q_ref, k_ref, v_ref, r_ref, a1_ref, a1n_ref, wah_ref, wal_ref, ba_ref, g_ref, y_ref,
                b_ref, *st_refs, chunk):
    heads, dk, dv = C_HEADS, C_QK_DIM, C_V_DIM
    c = pl.program_id(1)
    row = lax.broadcasted_iota(jnp.int32, (chunk, chunk), 0)
    col = lax.broadcasted_iota(jnp.int32, (chunk, chunk), 1)
    causal = row >= col

    def cum_log_decay(a1):
        a_hi = a1.astype(BF16)
        a_lo = (a1 - a_hi.astype(F32)).astype(BF16)
        w_hi = wah_ref[...]
        z = (jnp.dot(a_hi, w_hi, preferred_element_type=F32)
             + (jnp.dot(a_lo, w_hi, preferred_element_type=F32)
                + jnp.dot(a_hi, wal_ref[...], preferred_element_type=F32))) + ba_ref[...]
        log_a = _log_sigmoid(z) * (1.0 / C_GATE_TEMP)
        return jnp.dot(causal.astype(F32), log_a, precision=HIGHEST, preferred_element_type=F32)

    @pl.when(c == 0)
    def _():
        for ref in st_refs:
            ref[...] = jnp.zeros_like(ref)
        b_ref[...] = cum_log_decay(a1_ref[...])

    b_all = b_ref[...]
    b_next = cum_log_decay(a1n_ref[...])

    r = r_ref[...].astype(F32)
    g = g_ref[...]
    for h in range(heads):
        ks = slice(h * dk, (h + 1) * dk)
        vs = slice(h * dv, (h + 1) * dv)
        b = b_all[:, ks]
        q = q_ref[:, ks].astype(F32) * (dk ** -0.5)
        k = k_ref[:, ks].astype(F32)
        v = v_ref[:, vs]
        b_last = b[chunk - 1:chunk, :]
        q_dec = (q * jnp.exp(b)).astype(BF16)
        k_dec = (k * jnp.exp(-b)).astype(BF16)
        k_upd = (k * jnp.exp(b_last - b)).astype(BF16)
        st = st_refs[h][...]
        qk_t = lax.dot_general(q_dec, k_dec, NT_DIMS, preferred_element_type=F32)
        inter = lax.dot_general(q_dec, st.astype(BF16), NT_DIMS, preferred_element_type=F32)
        st_refs[h][...] = jnp.exp(b_last) * st + lax.dot_general(v, k_upd, TN_DIMS, preferred_element_type=F32)
        att = jnp.where(causal, qk_t, 0.0)
        o = jnp.dot(att.astype(BF16), v, preferred_element_type=F32) + inter

        on = o * lax.rsqrt(jnp.mean(o * o, -1, keepdims=True) + NORM_EPS)
        rr = r[:, vs]
        y_ref[:, vs] = (on * g[:, vs] * (rr * _sigmoid(rr))).astype(y_ref.dtype)

    b_ref[...] = b_next


def _gla(y3, a13, w_a2_hi, w_a2_lo, b_a, gla_g, *, chunk):
    bsz, s, _ = y3.shape
    heads, dk, dv = C_HEADS, C_QK_DIM, C_V_DIM
    wk, wv = heads * dk, heads * dv
    assert wv == 2 * wk and s % chunk == 0
    nc = s // chunk
    return pl.pallas_call(
        functools.partial(_gla_kernel, chunk=chunk),
        grid=(bsz, nc),
        in_specs=[
            pl.BlockSpec((None, chunk, wk), lambda b, c: (b, c, 0)),
            pl.BlockSpec((None, chunk, wk), lambda b, c: (b, c, 1)),
            pl.BlockSpec((None, chunk, wv), lambda b, c: (b, c, 1)),
            pl.BlockSpec((None, chunk, wv), lambda b, c: (b, c, 2)),
            pl.BlockSpec((None, chunk, LANES), lambda b, c: (b, c, 0)),
            pl.BlockSpec((None, chunk, LANES), lambda b, c: (b, jnp.minimum(c + 1, nc - 1), 0)),
            pl.BlockSpec((LANES, wk), lambda b, c: (0, 0)),
            pl.BlockSpec((LANES, wk), lambda b, c: (0, 0)),
            pl.BlockSpec((1, wk), lambda b, c: (0, 0)),
            pl.BlockSpec((1, wv), lambda b, c: (0, 0)),
        ],
        out_specs=pl.BlockSpec((None, chunk, wv), lambda b, c: (b, c, 0)),
        out_shape=jax.ShapeDtypeStruct((bsz, s, wv), BF16),
        scratch_shapes=[pltpu.VMEM((chunk, wk), F32)] + heads * [pltpu.VMEM((dv, dk), F32)],
        compiler_params=_params("parallel", "arbitrary"),
        name="gla",
    )(y3, y3, y3, y3, a13, a13, w_a2_hi, w_a2_lo, b_a, gla_g)


def _out_ln_kernel(*refs, n_in, alpha):
    y_refs, w_refs = refs[:n_in], refs[n_in:2 * n_in]
    x_ref, g_ref, b_ref, o_ref = refs[2 * n_in:]
    rb = o_ref.shape[0] // ROW_SPLIT
    for r in range(ROW_SPLIT):
        rows = slice(r * rb, (r + 1) * rb)
        acc = jnp.dot(y_refs[0][rows, :], w_refs[0][...], preferred_element_type=F32)
        for y_ref, w_ref in zip(y_refs[1:], w_refs[1:]):
            acc = acc + jnp.dot(y_ref[rows, :], w_ref[...], preferred_element_type=F32)
        o_ref[rows, :] = _layer_norm(alpha * x_ref[rows, :] + acc, g_ref[...], b_ref[...])


def _out_ln(ys, w, x, ln_g, ln_b, layer, w_layer, ln_slot, *, alpha, tm):
    t, d = x.shape
    n_in = len(ys)
    wd = ys[0].shape[1]
    assert all(y.shape[1] == wd for y in ys) and w.shape[1:] == (n_in * wd, d)
    ln_spec = pl.BlockSpec((None, None, 1, d), lambda i: (layer, ln_slot, 0, 0))
    in_specs = ([pl.BlockSpec((tm, wd), lambda i: (i, 0)) for _ in ys]
                + [pl.BlockSpec((None, wd, d), lambda i, k=k: (w_layer, k, 0)) for k in range(n_in)]
                + [pl.BlockSpec((tm, d), lambda i: (i, 0)), ln_spec, ln_spec])
    return pl.pallas_call(
        functools.partial(_out_ln_kernel, n_in=n_in, alpha=alpha),
        grid=(t // tm,),
        in_specs=in_specs,
        out_specs=pl.BlockSpec((tm, d), lambda i: (i, 0)),
        out_shape=jax.ShapeDtypeStruct((t, d), F32),
        compiler_params=_params("parallel"),
        name="out_ln",
    )(*ys, *([w] * n_in), x, ln_g, ln_b)


def _ple_kernel(x_ref, p_ref, wp_ref, wg_ref, o_ref):
    rb = o_ref.shape[0] // ROW_SPLIT
    for r in range(ROW_SPLIT):
        rows = slice(r * rb, (r + 1) * rb)
        x = x_ref[rows, :]
        e = jnp.dot(p_ref[rows, :].astype(BF16), wp_ref[...], preferred_element_type=F32)
        gate = jnp.dot(x.astype(BF16), wg_ref[...], preferred_element_type=F32)
        o_ref[rows, :] = x + e * _sigmoid(gate)


def _ple(x, p, w_proj, w_gate, layer, *, tm):
    t, d = x.shape
    s, pd = p.shape[2:]
    assert s % tm == 0
    per_seq = s // tm
    return pl.pallas_call(
        _ple_kernel,
        grid=(t // tm,),
        in_specs=[
            pl.BlockSpec((tm, d), lambda i: (i, 0)),
            pl.BlockSpec((None, None, tm, pd), lambda i: (layer, i // per_seq, i % per_seq, 0)),
            pl.BlockSpec((None, pd, d), lambda i: (layer, 0, 0)),
            pl.BlockSpec((None, d, d), lambda i: (layer, 0, 0)),
        ],
        out_specs=pl.BlockSpec((tm, d), lambda i: (i, 0)),
        out_shape=jax.ShapeDtypeStruct((t, d), F32),
        compiler_params=_params("parallel"),
        name="ple",
    )(x, p, w_proj, w_gate)


def _tiles(t, s):
    return dict(
        ffn_tm=min(512, t), ffn_tf=512,
        proj_tm=min(1024, t), proj_tn=1024,
        row_tm=min(512, t),
        attn_t=min(512, s),
        mlstm_chunk=min(128, s),
        gla_chunk=min(64, s),
    )


def _pad_cols(w, n):
    return jnp.pad(w, ((0, 0), (0, n - w.shape[1])))


def kernel(x, p, ln_g, ln_b, w_ffn_in, w_ffn_out, w_in_ab, w_out_ab, rel_bias, lambda_q1, lambda_k1,
           lambda_q2, lambda_k2, diff_norm, conv_w, conv_b, b_igate, b_fgate, mlstm_norm, w_in_c,
           w_alpha2, b_alpha, gla_norm, w_out_c, w_ple_proj, w_ple_gate):
    bsz, s, d = x.shape
    depth = p.shape[0]
    t = bsz * s
    tl = _tiles(t, s)
    alpha = (2 * depth) ** 0.25
    a_w = A_HEADS * A_V_DIM
    ab_main = 3 * a_w + 3 * B_HEADS * B_V_DIM
    c_main = 2 * C_HEADS * C_QK_DIM + 2 * C_HEADS * C_V_DIM
    row = lambda a: a.reshape(1, -1)

    wfi, wfo = w_ffn_in.astype(BF16), w_ffn_out.astype(BF16)
    wo_ab, wo_c = w_out_ab.astype(BF16), w_out_c.astype(BF16)
    wpp, wpg = w_ple_proj.astype(BF16), w_ple_gate.astype(BF16)
    g4, b4 = ln_g.reshape(depth, 3, 1, d), ln_b.reshape(depth, 3, 1, d)

    xf = x.reshape(t, d)
    for i in range(depth):
        xf = _ffn_ln(xf, wfi, wfo, g4, b4, i, 0, 0, alpha=alpha, tm=tl["ffn_tm"], tf=tl["ffn_tf"])
        if i % 2 == 0:
            e = i // 2
            lam_init = 0.8 - 0.6 * math.exp(-0.3 * i)
            y, gcol, grow = _in_proj(xf, w_in_ab, e, ab_main, tm=tl["proj_tm"], tn=tl["proj_tn"])
            y3 = y.reshape(bsz, s, ab_main)
            lam_vecs = jnp.stack([lambda_q1[e], lambda_k1[e], lambda_q2[e], lambda_k2[e]])
            ya = _diff_attention(y3, rel_bias, lam_vecs, row(diff_norm[e]), lam_init=lam_init, t=tl["attn_t"])
            gate_bias = jnp.concatenate([b_igate[e], b_fgate[e]])
            yb = _mlstm(y3, gcol.reshape(bsz, s, LANES), grow,
                        conv_w[e], row(conv_b[e]), _pad_cols(row(gate_bias), LANES), gate_bias.reshape(-1, 1),
                        row(mlstm_norm[e]), chunk=tl["mlstm_chunk"])
            ys, w_out, w_layer = [ya.reshape(t, a_w), yb.reshape(t, -1)], wo_ab, e
        else:
            o = i // 2
            y, a1, _ = _in_proj(xf, w_in_c, o, c_main, tm=tl["proj_tm"], tn=tl["proj_tn"])
            w_a2 = jnp.pad(w_alpha2[o], ((0, LANES - w_alpha2.shape[1]), (0, 0)))
            w_a2_hi = lax.reduce_precision(w_a2, exponent_bits=8, mantissa_bits=7)
            w_a2_lo = (w_a2 - w_a2_hi).astype(BF16)
            yc = _gla(y.reshape(bsz, s, c_main), a1.reshape(bsz, s, LANES), w_a2_hi.astype(BF16), w_a2_lo,
                      row(b_alpha[o]),
                      row(gla_norm[o]), chunk=tl["gla_chunk"])
            ys, w_out, w_layer = [yc.reshape(t, -1)], wo_c, o
        xf = _out_ln(ys, w_out, xf, g4, b4, i, w_layer, 1, alpha=alpha, tm=tl["row_tm"])
        xf = _ffn_ln(xf, wfi, wfo, g4, b4, i, 1, 2, alpha=alpha, tm=tl["ffn_tm"], tf=tl["ffn_tf"])
        xf = _ple(xf, p, wpp, wpg, i, tm=tl["row_tm"])
    return xf.reshape(bsz, s, d)
```

```python
import functools
import math

import jax
import jax.numpy as jnp
from jax import lax
from jax.experimental import pallas as pl
from jax.experimental.pallas import tpu as pltpu

F32 = jnp.float32
BF16 = jnp.bfloat16
HIGHEST = lax.Precision.HIGHEST

A_HEADS = 8
A_QK_DIM = 64
A_V_DIM = 128
B_HEADS = 4
B_QK_DIM = 128
B_V_DIM = 256
CONV_WIDTH = 4
C_HEADS = 4
C_QK_DIM = 256
C_V_DIM = 512
C_GATE_TEMP = 16.0
N_BUCKETS = 32
MAX_DISTANCE = 128
LN_EPS = 1e-5
NORM_EPS = 1e-6

V7X_VMEM_LIMIT_BYTES = 56 * 1024 * 1024
LANES = 128
SUBLANES = 8
BF16_SUBLANES = 16
ATTN_ONES_ROWS = 16
ROW_SPLIT = 4

NT_DIMS = (((1,), (1,)), ((), ()))
TN_DIMS = (((0,), (0,)), ((), ()))


def _params(*sem):
    return pltpu.CompilerParams(dimension_semantics=sem, vmem_limit_bytes=V7X_VMEM_LIMIT_BYTES)


def _layer_norm(y, g, b, eps=LN_EPS):
    mu = jnp.mean(y, -1, keepdims=True)
    yc = y - mu
    var = jnp.mean(yc * yc, -1, keepdims=True)
    return yc * lax.rsqrt(var + eps) * g + b


def _sigmoid(x):
    return 1.0 / (1.0 + jnp.exp(-x))


def _log_sigmoid(x):
    return jnp.minimum(x, 0.0) - jnp.log(1.0 + jnp.exp(-jnp.abs(x)))


def _ffn_ln_kernel(x_ref, wg0_ref, wu0_ref, wo0_ref, wg1_ref, wu1_ref, wo1_ref, g_ref, b_ref, o_ref, xb_ref,
                   *, alpha, n_chunks):
    j = pl.program_id(1)
    last = pl.num_programs(1) - 1

    @pl.when(j == 0)
    def _():
        x = x_ref[...]
        xb_ref[...] = x.astype(BF16)
        o_ref[...] = (2.0 * alpha) * x

    def chunk(wg_ref, wu_ref, wo_ref):
        xb = xb_ref[...]
        gate = jnp.dot(xb, wg_ref[...], preferred_element_type=F32)
        up = jnp.dot(xb, wu_ref[...], preferred_element_type=F32)
        h = (gate * _sigmoid(gate) * up).astype(BF16)
        o_ref[...] += jnp.dot(h, wo_ref[...], preferred_element_type=F32)

    if n_chunks % 2 == 0:
        chunk(wg0_ref, wu0_ref, wo0_ref)
        chunk(wg1_ref, wu1_ref, wo1_ref)
    else:
        @pl.when(j < last)
        def _():
            chunk(wg0_ref, wu0_ref, wo0_ref)
            chunk(wg1_ref, wu1_ref, wo1_ref)

        @pl.when(j == last)
        def _():
            chunk(wg0_ref, wu0_ref, wo0_ref)

    @pl.when(j == last)
    def _():
        o_ref[...] = _layer_norm(o_ref[...], g_ref[...], b_ref[...], eps=4.0 * LN_EPS)


def _chunk_major(w_in, tf):
    *lead, d, f2 = w_in.shape
    nl = len(lead)
    w = w_in.reshape(*lead, d, 2, f2 // (2 * tf), tf)
    return jnp.transpose(w, (*range(nl), nl + 1, nl + 2, nl, nl + 3))


def _ffn_ln(x, w_in, w_out, ln_g, ln_b, layer, slot, ln_slot, *, alpha, tm):
    t, d = x.shape
    f = w_out.shape[2]
    nf, tf = w_in.shape[3], w_in.shape[5]
    assert t % tm == 0 and nf * tf == f and w_in.shape[2:] == (2, nf, d, tf) and nf >= 2
    steps = (nf + 1) // 2
    first = lambda j: 2 * j
    second = lambda j: jnp.where(2 * j + 1 < nf, 2 * j + 1, nf - 2)
    ln_spec = pl.BlockSpec((None, None, 1, d), lambda i, j: (layer, ln_slot, 0, 0))

    def weight_specs(chunk_of):
        return [pl.BlockSpec((None, None, None, None, d, tf), lambda i, j: (layer, slot, 0, chunk_of(j), 0, 0)),
                pl.BlockSpec((None, None, None, None, d, tf), lambda i, j: (layer, slot, 1, chunk_of(j), 0, 0)),
                pl.BlockSpec((None, None, tf, d), lambda i, j: (layer, slot, chunk_of(j), 0))]

    return pl.pallas_call(
        functools.partial(_ffn_ln_kernel, alpha=alpha, n_chunks=nf),
        grid=(t // tm, steps),
        in_specs=[pl.BlockSpec((tm, d), lambda i, j: (i, 0))] + weight_specs(first) + weight_specs(second)
                 + [ln_spec, ln_spec],
        out_specs=pl.BlockSpec((tm, d), lambda i, j: (i, 0)),
        out_shape=jax.ShapeDtypeStruct((t, d), F32),
        scratch_shapes=[pltpu.VMEM((tm, d), BF16)],
        compiler_params=_params("parallel", "arbitrary"),
        name="ffn_ln",
    )(x, w_in, w_in, w_out, w_in, w_in, w_out, ln_g, ln_b)


def _in_proj_kernel(x_ref, w_ref, wgc_ref, wgr_ref, y_ref, gc_ref, gr_ref, xb_ref):
    j = pl.program_id(1)

    @pl.when(j == 0)
    def _():
        xb = x_ref[...].astype(BF16)
        xb_ref[...] = xb
        gc_ref[...] = jnp.dot(xb, wgc_ref[...], preferred_element_type=F32)
        gr_ref[...] = lax.dot_general(wgr_ref[...], xb, NT_DIMS, preferred_element_type=F32)

    y_ref[...] = jnp.dot(xb_ref[...], w_ref[...], preferred_element_type=F32).astype(y_ref.dtype)


def _in_proj(x, w_stack, layer, n_main, *, tm, tn):
    t, d = x.shape
    assert t % tm == 0 and n_main % tn == 0
    w_main = w_stack[:, :, :n_main].astype(BF16)
    w_gates = w_stack[layer, :, n_main:].astype(BF16)
    w_gate_cols = _pad_cols(w_gates, LANES)
    w_gate_rows = jnp.transpose(w_gates[:, :SUBLANES])
    return pl.pallas_call(
        _in_proj_kernel,
        grid=(t // tm, n_main // tn),
        in_specs=[
            pl.BlockSpec((tm, d), lambda i, j: (i, 0)),
            pl.BlockSpec((None, d, tn), lambda i, j: (layer, 0, j)),
            pl.BlockSpec((d, LANES), lambda i, j: (0, 0)),
            pl.BlockSpec((SUBLANES, d), lambda i, j: (0, 0)),
        ],
        out_specs=[
            pl.BlockSpec((tm, tn), lambda i, j: (i, j)),
            pl.BlockSpec((tm, LANES), lambda i, j: (i, 0)),
            pl.BlockSpec((SUBLANES, tm), lambda i, j: (0, i)),
        ],
        out_shape=[
            jax.ShapeDtypeStruct((t, n_main), BF16),
            jax.ShapeDtypeStruct((t, LANES), F32),
            jax.ShapeDtypeStruct((SUBLANES, t), F32),
        ],
        scratch_shapes=[pltpu.VMEM((tm, d), BF16)],
        compiler_params=_params("parallel", "arbitrary"),
        name="in_proj",
    )(x, w_main, w_gate_cols, w_gate_rows)


def _diff_attn_kernel(q_ref, k_ref, v_ref, tab_ref, lam_ref, g_ref, o_ref,
                      vt_ref, bdiag_ref, bnear_ref, m_ref, acc_ref,
                      s0_ref, mx0_ref, sh0_ref, s1_ref, mx1_ref, sh1_ref, *, t, lam_init):
    qi = pl.program_id(2)
    dk, dv = A_QK_DIM, A_V_DIM
    nk = k_ref.shape[0] // t
    bufs = ((s0_ref, mx0_ref, sh0_ref), (s1_ref, mx1_ref, sh1_ref))

    @pl.when(qi == 0)
    def _():
        for c in range(nk):
            vt_ref[c, 0:dv, :] = jnp.transpose(v_ref[c * t:(c + 1) * t, :].astype(F32)).astype(BF16)
            vt_ref[c, dv:, :] = jnp.ones((vt_ref.shape[1] - dv, t), BF16)
        skew = pltpu.roll(jnp.broadcast_to(tab_ref[...], (t, 2 * t)), 0, 1, stride=1, stride_axis=0)
        key = lax.broadcasted_iota(jnp.int32, (t, t), 0)
        qry = lax.broadcasted_iota(jnp.int32, (t, t), 1)
        bdiag_ref[...] = jnp.where(qry >= key, skew[:, :t], -1e30)
        bnear_ref[...] = skew[:, t:]

    q_t = jnp.transpose(q_ref[...].astype(F32)) * (dk ** -0.5)
    feat = lax.broadcasted_iota(jnp.int32, q_t.shape, 0)
    qs_t = jnp.concatenate([jnp.where(feat < dk, q_t, 0.0), jnp.where(feat >= dk, q_t, 0.0)],
                           axis=1).astype(BF16)

    m_ref[...] = jnp.full_like(m_ref, -jnp.inf)
    acc_ref[...] = jnp.zeros_like(acc_ref)

    def score(kj, buf, bias_ref, shift):
        s_ref, mx_ref, sh_ref = bufs[buf]
        off = pl.multiple_of(kj * t, t)
        s = jnp.dot(k_ref[pl.ds(off, t), :], qs_t, preferred_element_type=F32)
        if bias_ref is not None:
            b = bias_ref[...]
            s = s + jnp.concatenate([b, b], axis=1)
        s_ref[...] = s
        mx_ref[...] = jnp.max(s, 0, keepdims=True) + shift
        sh_ref[...] = jnp.zeros_like(sh_ref) + shift

    def absorb(kj, buf):
        s_ref, mx_ref, sh_ref = bufs[buf]
        m_old = m_ref[...]
        m_new = jnp.maximum(m_old, mx_ref[...])
        p = jnp.exp(s_ref[...] - (m_new - sh_ref[...])).astype(BF16)
        acc_ref[...] = (jnp.exp(m_old - m_new) * acc_ref[...]
                        + jnp.dot(vt_ref[kj], p, preferred_element_type=F32))
        m_ref[...] = m_new

    c_far = tab_ref[:, 2 * t - 1:2 * t]
    score(qi, 0, bdiag_ref, 0.0)

    @pl.when(qi >= 1)
    def _():
        score(qi - 1, 1, bnear_ref, 0.0)
        absorb(qi, 0)

    def far_pair(p, carry):
        kj = qi - 2 - 2 * p
        score(kj, 0, None, c_far)
        absorb(kj + 1, 1)
        score(kj - 1, 1, None, c_far)
        absorb(kj, 0)
        return carry

    lax.fori_loop(0, jnp.maximum(qi - 1, 0) // 2, far_pair, 0)

    @pl.when((qi >= 2) & (qi % 2 == 0))
    def _():
        score(0, 0, None, c_far)
        absorb(1, 1)

    @pl.when(qi % 2 == 0)
    def _():
        absorb(0, 0)

    @pl.when(qi % 2 == 1)
    def _():
        absorb(0, 1)

    lv = lam_ref[...]
    lam = (jnp.exp(jnp.sum(lv[0:1] * lv[1:2], -1, keepdims=True))
           - jnp.exp(jnp.sum(lv[2:3] * lv[3:4], -1, keepdims=True)) + lam_init)
    acc = acc_ref[...]
    o_t = acc[0:dv] * (1.0 / acc[dv:dv + 1])
    out = jnp.transpose(o_t[:, :t] - lam * o_t[:, t:])
    hn = out * lax.rsqrt(jnp.mean(out * out, -1, keepdims=True) + NORM_EPS)
    o_ref[...] = (hn * g_ref[...] * (1.0 - lam_init)).astype(o_ref.dtype)


def _t5_bias_by_distance(rel_bias, n):
    r = jnp.arange(n, dtype=jnp.int32)
    max_exact = N_BUCKETS // 2
    rf = jnp.maximum(r, 1).astype(F32)
    large = max_exact + (jnp.log(rf / max_exact) / math.log(MAX_DISTANCE / max_exact)
                         * (N_BUCKETS - max_exact)).astype(jnp.int32)
    large = jnp.minimum(large, N_BUCKETS - 1)
    bucket = jnp.where(r < max_exact, r, large)
    return jnp.transpose(rel_bias[bucket]).astype(F32)


def _diff_attention(y3, rel_bias, lam_vecs, diff_g, *, lam_init, t):
    bsz, s, _ = y3.shape
    hd = 2 * A_QK_DIM
    dv = A_V_DIM
    assert hd == dv == LANES and s % t == 0 and t >= MAX_DISTANCE
    nq = s // t
    table = _t5_bias_by_distance(rel_bias, 2 * t).reshape(A_HEADS, 1, 2 * t)
    return pl.pallas_call(
        functools.partial(_diff_attn_kernel, t=t, lam_init=lam_init),
        grid=(bsz, A_HEADS, nq),
        in_specs=[
            pl.BlockSpec((None, t, hd), lambda b, h, q: (b, q, h)),
            pl.BlockSpec((None, s, hd), lambda b, h, q: (b, 0, A_HEADS + h)),
            pl.BlockSpec((None, s, dv), lambda b, h, q: (b, 0, 2 * A_HEADS + h)),
            pl.BlockSpec((None, 1, 2 * t), lambda b, h, q: (h, 0, 0)),
            pl.BlockSpec((4, A_QK_DIM), lambda b, h, q: (0, 0)),
            pl.BlockSpec((1, dv), lambda b, h, q: (0, h)),
        ],
        out_specs=pl.BlockSpec((None, t, dv), lambda b, h, q: (b, q, h)),
        out_shape=jax.ShapeDtypeStruct((bsz, s, A_HEADS * dv), BF16),
        scratch_shapes=[pltpu.VMEM((nq, dv + ATTN_ONES_ROWS, t), BF16),
                        pltpu.VMEM((t, t), F32),
                        pltpu.VMEM((t, t), F32),
                        pltpu.VMEM((1, 2 * t), F32),
                        pltpu.VMEM((dv + ATTN_ONES_ROWS, 2 * t), F32)]
                       + 2 * [pltpu.VMEM((t, 2 * t), F32), pltpu.VMEM((1, 2 * t), F32),
                              pltpu.VMEM((1, 2 * t), F32)],
        compiler_params=_params("parallel", "parallel", "arbitrary"),
        name="diff_attn",
    )(y3, y3, y3, table, lam_vecs, diff_g)


def _mlstm_kernel(qk_ref, v_ref, ob_ref, gc_ref, gr_ref, cw_ref, cb_ref, bc_ref, br_ref, g_ref, y_ref,
                  xp_ref, *state_refs, chunk):
    heads, dk, dv = B_HEADS, B_QK_DIM, B_V_DIM
    c_refs, n_refs, m_refs = state_refs[0::3], state_refs[1::3], state_refs[2::3]
    pad = xp_ref.shape[0] - chunk
    c = pl.program_id(1)

    @pl.when(c == 0)
    def _():
        xp_ref[0:pad, :] = jnp.zeros((pad, xp_ref.shape[1]), xp_ref.dtype)
        for ref in state_refs:
            ref[...] = jnp.zeros_like(ref)

    x = qk_ref[...]
    xp_ref[pad:pad + chunk, :] = x
    xp = xp_ref[...]
    cw = cw_ref[...]
    dst = lax.broadcasted_iota(jnp.int32, (chunk, pad + chunk), 0)
    src = lax.broadcasted_iota(jnp.int32, (chunk, pad + chunk), 1)
    conv = cb_ref[...] + cw[CONV_WIDTH - 1:CONV_WIDTH, :] * x.astype(F32)
    for j in range(CONV_WIDTH - 1):
        shift = (src == dst + (pad - (CONV_WIDTH - 1) + j)).astype(BF16)
        conv = conv + cw[j:j + 1, :] * jnp.dot(shift, xp, preferred_element_type=F32)
    xp_ref[0:pad, :] = x[chunk - pad:chunk, :]
    qk = conv * _sigmoid(conv)

    gc = gc_ref[...] + bc_ref[...]
    gr = gr_ref[...] + br_ref[...]
    row = lax.broadcasted_iota(jnp.int32, (chunk, chunk), 0)
    col = lax.broadcasted_iota(jnp.int32, (chunk, chunk), 1)
    causal = row >= col
    tril = causal.astype(F32)
    triu = (row <= col).astype(F32)
    b_c = jnp.dot(tril, _log_sigmoid(gc), precision=HIGHEST, preferred_element_type=F32)
    b_r = jnp.dot(_log_sigmoid(gr), triu, precision=HIGHEST, preferred_element_type=F32)

    ob = ob_ref[...].astype(F32)
    g = g_ref[...]
    hs = range(heads)
    q = [qk[:, h * dk:(h + 1) * dk] * (dk ** -0.5) for h in hs]
    k = [qk[:, (heads + h) * dk:(heads + h + 1) * dk] for h in hs]
    v = [v_ref[:, h * dv:(h + 1) * dv] for h in hs]
    b_col = [b_c[:, heads + h:heads + h + 1] for h in hs]
    i_col = [gc[:, h:h + 1] for h in hs]
    b_row = [b_r[heads + h:heads + h + 1, :] for h in hs]
    i_row = [gr[h:h + 1, :] for h in hs]
    m_prev = [m_refs[h][0:1, 0:1] for h in hs]
    c_mat = [c_refs[h][...] for h in hs]
    n_vec = [n_refs[h][...] for h in hs]

    b_last = [b_row[h][:, chunk - 1:chunk] for h in hs]
    g_col = [b_last[h] - b_col[h] + i_col[h] for h in hs]
    m_new = [jnp.maximum(b_last[h] + m_prev[h], jnp.max(g_col[h], 0, keepdims=True)) for h in hs]
    decay = [jnp.exp(b_last[h] + m_prev[h] - m_new[h]) for h in hs]
    kw = [k[h] * jnp.exp(g_col[h] - m_new[h]) for h in hs]
    qb = [q[h].astype(BF16) for h in hs]
    qk_t = [lax.dot_general(qb[h], k[h].astype(BF16), NT_DIMS, preferred_element_type=F32) for h in hs]
    q_c = [jnp.dot(qb[h], c_mat[h].astype(BF16), preferred_element_type=F32) for h in hs]
    kw_v = [lax.dot_general(kw[h].astype(BF16), v[h], TN_DIMS, preferred_element_type=F32) for h in hs]

    d = [jnp.where(causal, b_col[h] - b_row[h] + i_row[h], -jnp.inf) for h in hs]
    a_col = [b_col[h] + m_prev[h] for h in hs]
    m_t = [jnp.maximum(a_col[h], jnp.max(d[h], -1, keepdims=True)) for h in hs]
    w_inter = [jnp.exp(a_col[h] - m_t[h]) for h in hs]
    sw = [qk_t[h] * jnp.exp(d[h] - m_t[h]) for h in hs]
    sw_v = [jnp.dot(sw[h].astype(BF16), v[h], preferred_element_type=F32) for h in hs]

    for h in hs:
        c_refs[h][...] = decay[h] * c_mat[h] + kw_v[h]
        n_refs[h][...] = decay[h] * n_vec[h] + jnp.sum(kw[h], 0, keepdims=True)
        m_refs[h][...] = jnp.broadcast_to(m_new[h], m_refs[h].shape)

    for h in hs:
        num = w_inter[h] * q_c[h] + sw_v[h]
        den = (w_inter[h] * jnp.sum(q[h] * n_vec[h], -1, keepdims=True)
               + jnp.sum(sw[h], -1, keepdims=True))
        hh = num / jnp.maximum(jnp.abs(den), jnp.exp(-m_t[h]))
        hn = hh * lax.rsqrt(jnp.mean(hh * hh, -1, keepdims=True) + NORM_EPS)
        sl = slice(h * dv, (h + 1) * dv)
        y_ref[:, sl] = (hn * g[:, sl] * _sigmoid(ob[:, sl])).astype(y_ref.dtype)


def _mlstm(y3, gcol3, grow3, conv_w, conv_b, gate_bias_cols, gate_bias_rows, mlstm_g, *, chunk):
    bsz, s, _ = y3.shape
    heads, dk, dv = B_HEADS, B_QK_DIM, B_V_DIM
    w = heads * dv
    assert 2 * heads * dk == w and s % chunk == 0 and chunk % LANES == 0
    nc = s // chunk
    return pl.pallas_call(
        functools.partial(_mlstm_kernel, chunk=chunk),
        grid=(bsz, nc),
        in_specs=[
            pl.BlockSpec((None, chunk, w), lambda b, c: (b, c, 3)),
            pl.BlockSpec((None, chunk, w), lambda b, c: (b, c, 4)),
            pl.BlockSpec((None, chunk, w), lambda b, c: (b, c, 5)),
            pl.BlockSpec((None, chunk, LANES), lambda b, c: (b, c, 0)),
            pl.BlockSpec((SUBLANES, chunk), lambda b, c: (0, b * nc + c)),
            pl.BlockSpec((CONV_WIDTH, w), lambda b, c: (0, 0)),
            pl.BlockSpec((1, w), lambda b, c: (0, 0)),
            pl.BlockSpec((1, LANES), lambda b, c: (0, 0)),
            pl.BlockSpec((SUBLANES, 1), lambda b, c: (0, 0)),
            pl.BlockSpec((1, w), lambda b, c: (0, 0)),
        ],
        out_specs=pl.BlockSpec((None, chunk, w), lambda b, c: (b, c, 0)),
        out_shape=jax.ShapeDtypeStruct((bsz, s, w), BF16),
        scratch_shapes=[
            pltpu.VMEM((BF16_SUBLANES + chunk, w), BF16),
        ] + heads * [pltpu.VMEM((dk, dv), F32), pltpu.VMEM((1, dk), F32), pltpu.VMEM((SUBLANES, LANES), F32)],
        compiler_params=_params("parallel", "arbitrary"),
        name="mlstm",
    )(y3, y3, y3, gcol3, grow3, conv_w, conv_b, gate_bias_cols, gate_bias_rows, mlstm_g)


def _gla_kernel(q_ref, k_ref, v_ref, r_ref, a1_ref, a1n_ref, wah_ref, wal_ref, ba_ref, g_ref, y_ref,
                b_ref, *st_refs, chunk):
    heads, dk, dv = C_HEADS, C_QK_DIM, C_V_DIM
    c = pl.program_id(1)
    row = lax.broadcasted_iota(jnp.int32, (chunk, chunk), 0)
    col = lax.broadcasted_iota(jnp.int32, (chunk, chunk), 1)
    causal = row >= col

    def cum_log_decay(a1):
        a_hi = a1.astype(BF16)
        a_lo = (a1 - a_hi.astype(F32)).astype(BF16)
        w_hi = wah_ref[...]
        z = (jnp.dot(a_hi, w_hi, preferred_element_type=F32)
             + (jnp.dot(a_lo, w_hi, preferred_element_type=F32)
                + jnp.dot(a_hi, wal_ref[...], preferred_element_type=F32))) + ba_ref[...]
        log_a = _log_sigmoid(z) * (1.0 / C_GATE_TEMP)
        return jnp.dot(causal.astype(F32), log_a, precision=HIGHEST, preferred_element_type=F32)

    @pl.when(c == 0)
    def _():
        for ref in st_refs:
            ref[...] = jnp.zeros_like(ref)
        b_ref[...] = cum_log_decay(a1_ref[...])

    b_all = b_ref[...]
    b_next = cum_log_decay(a1n_ref[...])

    r = r_ref[...].astype(F32)
    g = g_ref[...]
    for h in range(heads):
        ks = slice(h * dk, (h + 1) * dk)
        vs = slice(h * dv, (h + 1) * dv)
        b = b_all[:, ks]
        q = q_ref[:, ks].astype(F32) * (dk ** -0.5)
        k = k_ref[:, ks].astype(F32)
        v = v_ref[:, vs]
        b_last = b[chunk - 1:chunk, :]
        q_dec = (q * jnp.exp(b)).astype(BF16)
        k_dec = (k * jnp.exp(-b)).astype(BF16)
        k_upd = (k * jnp.exp(b_last - b)).astype(BF16)
        st = st_refs[h][...]
        qk_t = lax.dot_general(q_dec, k_dec, NT_DIMS, preferred_element_type=F32)
        inter = lax.dot_general(q_dec, st.astype(BF16), NT_DIMS, preferred_element_type=F32)
        st_refs[h][...] = jnp.exp(b_last) * st + lax.dot_general(v, k_upd, TN_DIMS, preferred_element_type=F32)
        att = jnp.where(causal, qk_t, 0.0)
        o = jnp.dot(att.astype(BF16), v, preferred_element_type=F32) + inter

        on = o * lax.rsqrt(jnp.mean(o * o, -1, keepdims=True) + NORM_EPS)
        rr = r[:, vs]
        y_ref[:, vs] = (on * g[:, vs] * (rr * _sigmoid(rr))).astype(y_ref.dtype)

    b_ref[...] = b_next


def _gla(y3, a13, w_a2_hi, w_a2_lo, b_a, gla_g, *, chunk):
    bsz, s, _ = y3.shape
    heads, dk, dv = C_HEADS, C_QK_DIM, C_V_DIM
    wk, wv = heads * dk, heads * dv
    assert wv == 2 * wk and s % chunk == 0
    nc = s // chunk
    return pl.pallas_call(
        functools.partial(_gla_kernel, chunk=chunk),
        grid=(bsz, nc),
        in_specs=[
            pl.BlockSpec((None, chunk, wk), lambda b, c: (b, c, 0)),
            pl.BlockSpec((None, chunk, wk), lambda b, c: (b, c, 1)),
            pl.BlockSpec((None, chunk, wv), lambda b, c: (b, c, 1)),
            pl.BlockSpec((None, chunk, wv), lambda b, c: (b, c, 2)),
            pl.BlockSpec((None, chunk, LANES), lambda b, c: (b, c, 0)),
            pl.BlockSpec((None, chunk, LANES), lambda b, c: (b, jnp.minimum(c + 1, nc - 1), 0)),
            pl.BlockSpec((LANES, wk), lambda b, c: (0, 0)),
            pl.BlockSpec((LANES, wk), lambda b, c: (0, 0)),
            pl.BlockSpec((1, wk), lambda b, c: (0, 0)),
            pl.BlockSpec((1, wv), lambda b, c: (0, 0)),
        ],
        out_specs=pl.BlockSpec((None, chunk, wv), lambda b, c: (b, c, 0)),
        out_shape=jax.ShapeDtypeStruct((bsz, s, wv), BF16),
        scratch_shapes=[pltpu.VMEM((chunk, wk), F32)] + heads * [pltpu.VMEM((dv, dk), F32)],
        compiler_params=_params("parallel", "arbitrary"),
        name="gla",
    )(y3, y3, y3, y3, a13, a13, w_a2_hi, w_a2_lo, b_a, gla_g)


def _out_ln_kernel(*refs, n_in, alpha):
    y_refs, w_refs = refs[:n_in], refs[n_in:2 * n_in]
    x_ref, g_ref, b_ref, o_ref = refs[2 * n_in:]
    rb = o_ref.shape[0] // ROW_SPLIT
    for r in range(ROW_SPLIT):
        rows = slice(r * rb, (r + 1) * rb)
        acc = jnp.dot(y_refs[0][rows, :], w_refs[0][...], preferred_element_type=F32)
        for y_ref, w_ref in zip(y_refs[1:], w_refs[1:]):
            acc = acc + jnp.dot(y_ref[rows, :], w_ref[...], preferred_element_type=F32)
        o_ref[rows, :] = _layer_norm(alpha * x_ref[rows, :] + acc, g_ref[...], b_ref[...])


def _out_ln(ys, w, x, ln_g, ln_b, layer, w_layer, ln_slot, *, alpha, tm):
    t, d = x.shape
    n_in = len(ys)
    wd = ys[0].shape[1]
    assert all(y.shape[1] == wd for y in ys) and w.shape[1:] == (n_in * wd, d)
    ln_spec = pl.BlockSpec((None, None, 1, d), lambda i: (layer, ln_slot, 0, 0))
    in_specs = ([pl.BlockSpec((tm, wd), lambda i: (i, 0)) for _ in ys]
                + [pl.BlockSpec((None, wd, d), lambda i, k=k: (w_layer, k, 0)) for k in range(n_in)]
                + [pl.BlockSpec((tm, d), lambda i: (i, 0)), ln_spec, ln_spec])
    return pl.pallas_call(
        functools.partial(_out_ln_kernel, n_in=n_in, alpha=alpha),
        grid=(t // tm,),
        in_specs=in_specs,
        out_specs=pl.BlockSpec((tm, d), lambda i: (i, 0)),
        out_shape=jax.ShapeDtypeStruct((t, d), F32),
        compiler_params=_params("parallel"),
        name="out_ln",
    )(*ys, *([w] * n_in), x, ln_g, ln_b)


def _ple_kernel(x_ref, p_ref, wp_ref, wg_ref, o_ref):
    rb = o_ref.shape[0] // ROW_SPLIT
    for r in range(ROW_SPLIT):
        rows = slice(r * rb, (r + 1) * rb)
        x = x_ref[rows, :]
        e = jnp.dot(p_ref[rows, :].astype(BF16), wp_ref[...], preferred_element_type=F32)
        gate = jnp.dot(x.astype(BF16), wg_ref[...], preferred_element_type=F32)
        o_ref[rows, :] = x + e * _sigmoid(gate)


def _ple(x, p, w_proj, w_gate, layer, *, tm):
    t, d = x.shape
    s, pd = p.shape[2:]
    assert s % tm == 0
    per_seq = s // tm
    return pl.pallas_call(
        _ple_kernel,
        grid=(t // tm,),
        in_specs=[
            pl.BlockSpec((tm, d), lambda i: (i, 0)),
            pl.BlockSpec((None, None, tm, pd), lambda i: (layer, i // per_seq, i % per_seq, 0)),
            pl.BlockSpec((None, pd, d), lambda i: (layer, 0, 0)),
            pl.BlockSpec((None, d, d), lambda i: (layer, 0, 0)),
        ],
        out_specs=pl.BlockSpec((tm, d), lambda i: (i, 0)),
        out_shape=jax.ShapeDtypeStruct((t, d), F32),
        compiler_params=_params("parallel"),
        name="ple",
    )(x, p, w_proj, w_gate)


def _tiles(t, s):
    return dict(
        ffn_tm=min(512, t), ffn_tf=512,
        proj_tm=min(1024, t), proj_tn=1024,
        row_tm=min(512, t),
        attn_t=min(512, s),
        mlstm_chunk=min(128, s),
        gla_chunk=min(64, s),
    )


def _pad_cols(w, n):
    return jnp.pad(w, ((0, 0), (0, n - w.shape[1])))


def kernel(x, p, ln_g, ln_b, w_ffn_in, w_ffn_out, w_in_ab, w_out_ab, rel_bias, lambda_q1, lambda_k1,
           lambda_q2, lambda_k2, diff_norm, conv_w, conv_b, b_igate, b_fgate, mlstm_norm, w_in_c,
           w_alpha2, b_alpha, gla_norm, w_out_c, w_ple_proj, w_ple_gate):
    bsz, s, d = x.shape
    depth = p.shape[0]
    t = bsz * s
    tl = _tiles(t, s)
    alpha = (2 * depth) ** 0.25
    a_w = A_HEADS * A_V_DIM
    ab_main = 3 * a_w + 3 * B_HEADS * B_V_DIM
    c_main = 2 * C_HEADS * C_QK_DIM + 2 * C_HEADS * C_V_DIM
    row = lambda a: a.reshape(1, -1)

    wfi, wfo = _chunk_major(w_ffn_in, tl["ffn_tf"]).astype(BF16), w_ffn_out.astype(BF16)
    wo_ab, wo_c = w_out_ab.astype(BF16), w_out_c.astype(BF16)
    wpp, wpg = w_ple_proj.astype(BF16), w_ple_gate.astype(BF16)
    g4, b4 = ln_g.reshape(depth, 3, 1, d), ln_b.reshape(depth, 3, 1, d)

    xf = x.reshape(t, d)
    for i in range(depth):
        xf = _ffn_ln(xf, wfi, wfo, g4, b4, i, 0, 0, alpha=alpha, tm=tl["ffn_tm"])
        if i % 2 == 0:
            e = i // 2
            lam_init = 0.8 - 0.6 * math.exp(-0.3 * i)
            y, gcol, grow = _in_proj(xf, w_in_ab, e, ab_main, tm=tl["proj_tm"], tn=tl["proj_tn"])
            y3 = y.reshape(bsz, s, ab_main)
            lam_vecs = jnp.stack([lambda_q1[e], lambda_k1[e], lambda_q2[e], lambda_k2[e]])
            ya = _diff_attention(y3, rel_bias, lam_vecs, row(diff_norm[e]), lam_init=lam_init, t=tl["attn_t"])
            gate_bias = jnp.concatenate([b_igate[e], b_fgate[e]])
            yb = _mlstm(y3, gcol.reshape(bsz, s, LANES), grow,
                        conv_w[e], row(conv_b[e]), _pad_cols(row(gate_bias), LANES), gate_bias.reshape(-1, 1),
                        row(mlstm_norm[e]), chunk=tl["mlstm_chunk"])
            ys, w_out, w_layer = [ya.reshape(t, a_w), yb.reshape(t, -1)], wo_ab, e
        else:
            o = i // 2
            y, a1, _ = _in_proj(xf, w_in_c, o, c_main, tm=tl["proj_tm"], tn=tl["proj_tn"])
            w_a2 = jnp.pad(w_alpha2[o], ((0, LANES - w_alpha2.shape[1]), (0, 0)))
            w_a2_hi = lax.reduce_precision(w_a2, exponent_bits=8, mantissa_bits=7)
            w_a2_lo = (w_a2 - w_a2_hi).astype(BF16)
            yc = _gla(y.reshape(bsz, s, c_main), a1.reshape(bsz, s, LANES), w_a2_hi.astype(BF16), w_a2_lo,
                      row(b_alpha[o]),
                      row(gla_norm[o]), chunk=tl["gla_chunk"])
            ys, w_out, w_layer = [yc.reshape(t, -1)], wo_c, o
        xf = _out_ln(ys, w_out, xf, g4, b4, i, w_layer, 1, alpha=alpha, tm=tl["row_tm"])
        xf = _ffn_ln(xf, wfi, wfo, g4, b4, i, 1, 2, alpha=alpha, tm=tl["ffn_tm"])
        xf = _ple(xf, p, wpp, wpg, i, tm=tl["row_tm"])
    return xf.reshape(bsz, s, d)
```

```python
import functools
import math

import jax
import jax.numpy as jnp
from jax import lax
from jax.experimental import pallas as pl
from jax.experimental.pallas import tpu as pltpu

F32 = jnp.float32
BF16 = jnp.bfloat16
HIGHEST = lax.Precision.HIGHEST

A_HEADS = 8
A_QK_DIM = 64
A_V_DIM = 128
B_HEADS = 4
B_QK_DIM = 128
B_V_DIM = 256
CONV_WIDTH = 4
C_HEADS = 4
C_QK_DIM = 256
C_V_DIM = 512
C_GATE_TEMP = 16.0
GLA_SUB = 64
N_BUCKETS = 32
MAX_DISTANCE = 128
LN_EPS = 1e-5
NORM_EPS = 1e-6

V7X_VMEM_LIMIT_BYTES = 56 * 1024 * 1024
LANES = 128
SUBLANES = 8
BF16_SUBLANES = 16
ATTN_ONES_ROWS = 16
ROW_SPLIT = 4

NT_DIMS = (((1,), (1,)), ((), ()))
TN_DIMS = (((0,), (0,)), ((), ()))


def _params(*sem):
    return pltpu.CompilerParams(dimension_semantics=sem, vmem_limit_bytes=V7X_VMEM_LIMIT_BYTES)


def _layer_norm(y, g, b, eps=LN_EPS):
    mu = jnp.mean(y, -1, keepdims=True)
    yc = y - mu
    var = jnp.mean(yc * yc, -1, keepdims=True)
    return yc * lax.rsqrt(var + eps) * g + b


def _sigmoid(x):
    return 1.0 / (1.0 + jnp.exp(-x))


def _log_sigmoid(x):
    return jnp.minimum(x, 0.0) - jnp.log(1.0 + jnp.exp(-jnp.abs(x)))


def _ffn_ln_kernel(x_ref, wg0_ref, wu0_ref, wo0_ref, wg1_ref, wu1_ref, wo1_ref, g_ref, b_ref, o_ref, xb_ref,
                   *, alpha, n_chunks):
    j = pl.program_id(1)
    last = pl.num_programs(1) - 1

    @pl.when(j == 0)
    def _():
        x = x_ref[...]
        xb_ref[...] = x.astype(BF16)
        o_ref[...] = (2.0 * alpha) * x

    def chunk(wg_ref, wu_ref, wo_ref):
        xb = xb_ref[...]
        gate = jnp.dot(xb, wg_ref[...], preferred_element_type=F32)
        up = jnp.dot(xb, wu_ref[...], preferred_element_type=F32)
        h = (gate * _sigmoid(gate) * up).astype(BF16)
        o_ref[...] += jnp.dot(h, wo_ref[...], preferred_element_type=F32)

    if n_chunks % 2 == 0:
        chunk(wg0_ref, wu0_ref, wo0_ref)
        chunk(wg1_ref, wu1_ref, wo1_ref)
    else:
        @pl.when(j < last)
        def _():
            chunk(wg0_ref, wu0_ref, wo0_ref)
            chunk(wg1_ref, wu1_ref, wo1_ref)

        @pl.when(j == last)
        def _():
            chunk(wg0_ref, wu0_ref, wo0_ref)

    @pl.when(j == last)
    def _():
        o_ref[...] = _layer_norm(o_ref[...], g_ref[...], b_ref[...], eps=4.0 * LN_EPS)


def _ffn_ln(x, w_in, w_out, ln_g, ln_b, layer, slot, ln_slot, *, alpha, tm, tf):
    t, d = x.shape
    f = w_out.shape[2]
    nf = f // tf
    assert t % tm == 0 and f % tf == 0 and w_in.shape[2:] == (d, 2 * f) and nf >= 2
    steps = (nf + 1) // 2
    first = lambda j: 2 * j
    second = lambda j: jnp.where(2 * j + 1 < nf, 2 * j + 1, nf - 2)
    ln_spec = pl.BlockSpec((None, None, 1, d), lambda i, j: (layer, ln_slot, 0, 0))

    def weight_specs(chunk_of):
        return [pl.BlockSpec((None, None, d, tf), lambda i, j: (layer, slot, 0, chunk_of(j))),
                pl.BlockSpec((None, None, d, tf), lambda i, j: (layer, slot, 0, nf + chunk_of(j))),
                pl.BlockSpec((None, None, tf, d), lambda i, j: (layer, slot, chunk_of(j), 0))]

    return pl.pallas_call(
        functools.partial(_ffn_ln_kernel, alpha=alpha, n_chunks=nf),
        grid=(t // tm, steps),
        in_specs=[pl.BlockSpec((tm, d), lambda i, j: (i, 0))] + weight_specs(first) + weight_specs(second)
                 + [ln_spec, ln_spec],
        out_specs=pl.BlockSpec((tm, d), lambda i, j: (i, 0)),
        out_shape=jax.ShapeDtypeStruct((t, d), F32),
        scratch_shapes=[pltpu.VMEM((tm, d), BF16)],
        compiler_params=_params("parallel", "arbitrary"),
        name="ffn_ln",
    )(x, w_in, w_in, w_out, w_in, w_in, w_out, ln_g, ln_b)


def _in_proj_kernel(x_ref, w_ref, wgc_ref, wgr_ref, y_ref, gc_ref, gr_ref, xb_ref):
    j = pl.program_id(1)

    @pl.when(j == 0)
    def _():
        xb = x_ref[...].astype(BF16)
        xb_ref[...] = xb
        gc_ref[...] = jnp.dot(xb, wgc_ref[...], preferred_element_type=F32)
        gr_ref[...] = lax.dot_general(wgr_ref[...], xb, NT_DIMS, preferred_element_type=F32)

    y_ref[...] = jnp.dot(xb_ref[...], w_ref[...], preferred_element_type=F32).astype(y_ref.dtype)


def _in_proj(x, w_stack, layer, n_main, *, tm, tn):
    t, d = x.shape
    assert t % tm == 0 and n_main % tn == 0
    w_main = w_stack[:, :, :n_main].astype(BF16)
    w_gates = w_stack[layer, :, n_main:].astype(BF16)
    w_gate_cols = _pad_cols(w_gates, LANES)
    w_gate_rows = jnp.transpose(w_gates[:, :SUBLANES])
    return pl.pallas_call(
        _in_proj_kernel,
        grid=(t // tm, n_main // tn),
        in_specs=[
            pl.BlockSpec((tm, d), lambda i, j: (i, 0)),
            pl.BlockSpec((None, d, tn), lambda i, j: (layer, 0, j)),
            pl.BlockSpec((d, LANES), lambda i, j: (0, 0)),
            pl.BlockSpec((SUBLANES, d), lambda i, j: (0, 0)),
        ],
        out_specs=[
            pl.BlockSpec((tm, tn), lambda i, j: (i, j)),
            pl.BlockSpec((tm, LANES), lambda i, j: (i, 0)),
            pl.BlockSpec((SUBLANES, tm), lambda i, j: (0, i)),
        ],
        out_shape=[
            jax.ShapeDtypeStruct((t, n_main), BF16),
            jax.ShapeDtypeStruct((t, LANES), F32),
            jax.ShapeDtypeStruct((SUBLANES, t), F32),
        ],
        scratch_shapes=[pltpu.VMEM((tm, d), BF16)],
        compiler_params=_params("parallel", "arbitrary"),
        name="in_proj",
    )(x, w_main, w_gate_cols, w_gate_rows)


def _diff_attn_kernel(q_ref, k_ref, v_ref, tab_ref, lam_ref, g_ref, o_ref,
                      vt_ref, bdiag_ref, bnear_ref, m_ref, acc_ref,
                      s0_ref, mx0_ref, sh0_ref, s1_ref, mx1_ref, sh1_ref, *, t, lam_init):
    qi = pl.program_id(2)
    dk, dv = A_QK_DIM, A_V_DIM
    nk = k_ref.shape[0] // t
    bufs = ((s0_ref, mx0_ref, sh0_ref), (s1_ref, mx1_ref, sh1_ref))

    @pl.when(qi == 0)
    def _():
        for c in range(nk):
            vt_ref[c, 0:dv, :] = jnp.transpose(v_ref[c * t:(c + 1) * t, :].astype(F32)).astype(BF16)
            vt_ref[c, dv:, :] = jnp.ones((vt_ref.shape[1] - dv, t), BF16)
        skew = pltpu.roll(jnp.broadcast_to(tab_ref[...], (t, 2 * t)), 0, 1, stride=1, stride_axis=0)
        key = lax.broadcasted_iota(jnp.int32, (t, t), 0)
        qry = lax.broadcasted_iota(jnp.int32, (t, t), 1)
        bdiag_ref[...] = jnp.where(qry >= key, skew[:, :t], -1e30)
        bnear_ref[...] = skew[:, t:]

    q_t = jnp.transpose(q_ref[...].astype(F32)) * (dk ** -0.5)
    feat = lax.broadcasted_iota(jnp.int32, q_t.shape, 0)
    qs_t = jnp.concatenate([jnp.where(feat < dk, q_t, 0.0), jnp.where(feat >= dk, q_t, 0.0)],
                           axis=1).astype(BF16)

    m_ref[...] = jnp.full_like(m_ref, -jnp.inf)
    acc_ref[...] = jnp.zeros_like(acc_ref)

    def score(kj, buf, bias_ref, shift):
        s_ref, mx_ref, sh_ref = bufs[buf]
        off = pl.multiple_of(kj * t, t)
        s = jnp.dot(k_ref[pl.ds(off, t), :], qs_t, preferred_element_type=F32)
        if bias_ref is not None:
            b = bias_ref[...]
            s = s + jnp.concatenate([b, b], axis=1)
        s_ref[...] = s
        mx_ref[...] = jnp.max(s, 0, keepdims=True) + shift
        sh_ref[...] = jnp.zeros_like(sh_ref) + shift

    def absorb(kj, buf):
        s_ref, mx_ref, sh_ref = bufs[buf]
        m_old = m_ref[...]
        m_new = jnp.maximum(m_old, mx_ref[...])
        p = jnp.exp(s_ref[...] - (m_new - sh_ref[...])).astype(BF16)
        acc_ref[...] = (jnp.exp(m_old - m_new) * acc_ref[...]
                        + jnp.dot(vt_ref[kj], p, preferred_element_type=F32))
        m_ref[...] = m_new

    c_far = tab_ref[:, 2 * t - 1:2 * t]
    score(qi, 0, bdiag_ref, 0.0)

    @pl.when(qi >= 1)
    def _():
        score(qi - 1, 1, bnear_ref, 0.0)
        absorb(qi, 0)

    def far_pair(p, carry):
        kj = qi - 2 - 2 * p
        score(kj, 0, None, c_far)
        absorb(kj + 1, 1)
        score(kj - 1, 1, None, c_far)
        absorb(kj, 0)
        return carry

    lax.fori_loop(0, jnp.maximum(qi - 1, 0) // 2, far_pair, 0)

    @pl.when((qi >= 2) & (qi % 2 == 0))
    def _():
        score(0, 0, None, c_far)
        absorb(1, 1)

    @pl.when(qi % 2 == 0)
    def _():
        absorb(0, 0)

    @pl.when(qi % 2 == 1)
    def _():
        absorb(0, 1)

    lv = lam_ref[...]
    lam = (jnp.exp(jnp.sum(lv[0:1] * lv[1:2], -1, keepdims=True))
           - jnp.exp(jnp.sum(lv[2:3] * lv[3:4], -1, keepdims=True)) + lam_init)
    acc = acc_ref[...]
    o_t = acc[0:dv] * (1.0 / acc[dv:dv + 1])
    out = jnp.transpose(o_t[:, :t] - lam * o_t[:, t:])
    hn = out * lax.rsqrt(jnp.mean(out * out, -1, keepdims=True) + NORM_EPS)
    o_ref[...] = (hn * g_ref[...] * (1.0 - lam_init)).astype(o_ref.dtype)


def _t5_bias_by_distance(rel_bias, n):
    r = jnp.arange(n, dtype=jnp.int32)
    max_exact = N_BUCKETS // 2
    rf = jnp.maximum(r, 1).astype(F32)
    large = max_exact + (jnp.log(rf / max_exact) / math.log(MAX_DISTANCE / max_exact)
                         * (N_BUCKETS - max_exact)).astype(jnp.int32)
    large = jnp.minimum(large, N_BUCKETS - 1)
    bucket = jnp.where(r < max_exact, r, large)
    return jnp.transpose(rel_bias[bucket]).astype(F32)


def _diff_attention(y3, rel_bias, lam_vecs, diff_g, *, lam_init, t):
    bsz, s, _ = y3.shape
    hd = 2 * A_QK_DIM
    dv = A_V_DIM
    assert hd == dv == LANES and s % t == 0 and t >= MAX_DISTANCE
    nq = s // t
    table = _t5_bias_by_distance(rel_bias, 2 * t).reshape(A_HEADS, 1, 2 * t)
    return pl.pallas_call(
        functools.partial(_diff_attn_kernel, t=t, lam_init=lam_init),
        grid=(bsz, A_HEADS, nq),
        in_specs=[
            pl.BlockSpec((None, t, hd), lambda b, h, q: (b, q, h)),
            pl.BlockSpec((None, s, hd), lambda b, h, q: (b, 0, A_HEADS + h)),
            pl.BlockSpec((None, s, dv), lambda b, h, q: (b, 0, 2 * A_HEADS + h)),
            pl.BlockSpec((None, 1, 2 * t), lambda b, h, q: (h, 0, 0)),
            pl.BlockSpec((4, A_QK_DIM), lambda b, h, q: (0, 0)),
            pl.BlockSpec((1, dv), lambda b, h, q: (0, h)),
        ],
        out_specs=pl.BlockSpec((None, t, dv), lambda b, h, q: (b, q, h)),
        out_shape=jax.ShapeDtypeStruct((bsz, s, A_HEADS * dv), BF16),
        scratch_shapes=[pltpu.VMEM((nq, dv + ATTN_ONES_ROWS, t), BF16),
                        pltpu.VMEM((t, t), F32),
                        pltpu.VMEM((t, t), F32),
                        pltpu.VMEM((1, 2 * t), F32),
                        pltpu.VMEM((dv + ATTN_ONES_ROWS, 2 * t), F32)]
                       + 2 * [pltpu.VMEM((t, 2 * t), F32), pltpu.VMEM((1, 2 * t), F32),
                              pltpu.VMEM((1, 2 * t), F32)],
        compiler_params=_params("parallel", "parallel", "arbitrary"),
        name="diff_attn",
    )(y3, y3, y3, table, lam_vecs, diff_g)


def _mlstm_kernel(qk_ref, v_ref, ob_ref, gc_ref, gr_ref, cw_ref, cb_ref, bc_ref, br_ref, g_ref, y_ref,
                  xp_ref, *state_refs, chunk):
    heads, dk, dv = B_HEADS, B_QK_DIM, B_V_DIM
    c_refs, n_refs, m_refs = state_refs[0::3], state_refs[1::3], state_refs[2::3]
    pad = xp_ref.shape[0] - chunk
    c = pl.program_id(1)

    @pl.when(c == 0)
    def _():
        xp_ref[0:pad, :] = jnp.zeros((pad, xp_ref.shape[1]), xp_ref.dtype)
        for ref in state_refs:
            ref[...] = jnp.zeros_like(ref)

    x = qk_ref[...]
    xp_ref[pad:pad + chunk, :] = x
    xp = xp_ref[...]
    cw = cw_ref[...]
    dst = lax.broadcasted_iota(jnp.int32, (chunk, pad + chunk), 0)
    src = lax.broadcasted_iota(jnp.int32, (chunk, pad + chunk), 1)
    conv = cb_ref[...] + cw[CONV_WIDTH - 1:CONV_WIDTH, :] * x.astype(F32)
    for j in range(CONV_WIDTH - 1):
        shift = (src == dst + (pad - (CONV_WIDTH - 1) + j)).astype(BF16)
        conv = conv + cw[j:j + 1, :] * jnp.dot(shift, xp, preferred_element_type=F32)
    xp_ref[0:pad, :] = x[chunk - pad:chunk, :]
    qk = conv * _sigmoid(conv)

    gc = gc_ref[...] + bc_ref[...]
    gr = gr_ref[...] + br_ref[...]
    row = lax.broadcasted_iota(jnp.int32, (chunk, chunk), 0)
    col = lax.broadcasted_iota(jnp.int32, (chunk, chunk), 1)
    causal = row >= col
    tril = causal.astype(F32)
    triu = (row <= col).astype(F32)
    b_c = jnp.dot(tril, _log_sigmoid(gc), precision=HIGHEST, preferred_element_type=F32)
    b_r = jnp.dot(_log_sigmoid(gr), triu, precision=HIGHEST, preferred_element_type=F32)

    ob = ob_ref[...].astype(F32)
    g = g_ref[...]
    hs = range(heads)
    q = [qk[:, h * dk:(h + 1) * dk] * (dk ** -0.5) for h in hs]
    k = [qk[:, (heads + h) * dk:(heads + h + 1) * dk] for h in hs]
    v = [v_ref[:, h * dv:(h + 1) * dv] for h in hs]
    b_col = [b_c[:, heads + h:heads + h + 1] for h in hs]
    i_col = [gc[:, h:h + 1] for h in hs]
    b_row = [b_r[heads + h:heads + h + 1, :] for h in hs]
    i_row = [gr[h:h + 1, :] for h in hs]
    m_prev = [m_refs[h][0:1, 0:1] for h in hs]
    c_mat = [c_refs[h][...] for h in hs]
    n_vec = [n_refs[h][...] for h in hs]

    b_last = [b_row[h][:, chunk - 1:chunk] for h in hs]
    g_col = [b_last[h] - b_col[h] + i_col[h] for h in hs]
    m_new = [jnp.maximum(b_last[h] + m_prev[h], jnp.max(g_col[h], 0, keepdims=True)) for h in hs]
    decay = [jnp.exp(b_last[h] + m_prev[h] - m_new[h]) for h in hs]
    kw = [k[h] * jnp.exp(g_col[h] - m_new[h]) for h in hs]
    qb = [q[h].astype(BF16) for h in hs]
    qk_t = [lax.dot_general(qb[h], k[h].astype(BF16), NT_DIMS, preferred_element_type=F32) for h in hs]
    q_c = [jnp.dot(qb[h], c_mat[h].astype(BF16), preferred_element_type=F32) for h in hs]
    kw_v = [lax.dot_general(kw[h].astype(BF16), v[h], TN_DIMS, preferred_element_type=F32) for h in hs]

    d = [jnp.where(causal, b_col[h] - b_row[h] + i_row[h], -jnp.inf) for h in hs]
    a_col = [b_col[h] + m_prev[h] for h in hs]
    m_t = [jnp.maximum(a_col[h], jnp.max(d[h], -1, keepdims=True)) for h in hs]
    w_inter = [jnp.exp(a_col[h] - m_t[h]) for h in hs]
    sw = [qk_t[h] * jnp.exp(d[h] - m_t[h]) for h in hs]
    sw_v = [jnp.dot(sw[h].astype(BF16), v[h], preferred_element_type=F32) for h in hs]

    for h in hs:
        c_refs[h][...] = decay[h] * c_mat[h] + kw_v[h]
        n_refs[h][...] = decay[h] * n_vec[h] + jnp.sum(kw[h], 0, keepdims=True)
        m_refs[h][...] = jnp.broadcast_to(m_new[h], m_refs[h].shape)

    for h in hs:
        num = w_inter[h] * q_c[h] + sw_v[h]
        den = (w_inter[h] * jnp.sum(q[h] * n_vec[h], -1, keepdims=True)
               + jnp.sum(sw[h], -1, keepdims=True))
        hh = num / jnp.maximum(jnp.abs(den), jnp.exp(-m_t[h]))
        hn = hh * lax.rsqrt(jnp.mean(hh * hh, -1, keepdims=True) + NORM_EPS)
        sl = slice(h * dv, (h + 1) * dv)
        y_ref[:, sl] = (hn * g[:, sl] * _sigmoid(ob[:, sl])).astype(y_ref.dtype)


def _mlstm(y3, gcol3, grow3, conv_w, conv_b, gate_bias_cols, gate_bias_rows, mlstm_g, *, chunk):
    bsz, s, _ = y3.shape
    heads, dk, dv = B_HEADS, B_QK_DIM, B_V_DIM
    w = heads * dv
    assert 2 * heads * dk == w and s % chunk == 0 and chunk % LANES == 0
    nc = s // chunk
    return pl.pallas_call(
        functools.partial(_mlstm_kernel, chunk=chunk),
        grid=(bsz, nc),
        in_specs=[
            pl.BlockSpec((None, chunk, w), lambda b, c: (b, c, 3)),
            pl.BlockSpec((None, chunk, w), lambda b, c: (b, c, 4)),
            pl.BlockSpec((None, chunk, w), lambda b, c: (b, c, 5)),
            pl.BlockSpec((None, chunk, LANES), lambda b, c: (b, c, 0)),
            pl.BlockSpec((SUBLANES, chunk), lambda b, c: (0, b * nc + c)),
            pl.BlockSpec((CONV_WIDTH, w), lambda b, c: (0, 0)),
            pl.BlockSpec((1, w), lambda b, c: (0, 0)),
            pl.BlockSpec((1, LANES), lambda b, c: (0, 0)),
            pl.BlockSpec((SUBLANES, 1), lambda b, c: (0, 0)),
            pl.BlockSpec((1, w), lambda b, c: (0, 0)),
        ],
        out_specs=pl.BlockSpec((None, chunk, w), lambda b, c: (b, c, 0)),
        out_shape=jax.ShapeDtypeStruct((bsz, s, w), BF16),
        scratch_shapes=[
            pltpu.VMEM((BF16_SUBLANES + chunk, w), BF16),
        ] + heads * [pltpu.VMEM((dk, dv), F32), pltpu.VMEM((1, dk), F32), pltpu.VMEM((SUBLANES, LANES), F32)],
        compiler_params=_params("parallel", "arbitrary"),
        name="mlstm",
    )(y3, y3, y3, gcol3, grow3, conv_w, conv_b, gate_bias_cols, gate_bias_rows, mlstm_g)


def _gla_kernel(q_ref, k_ref, v_ref, r_ref, a1_ref, a1n_ref, wah_ref, wal_ref, ba_ref, g_ref, y_ref,
                b_ref, *st_refs, chunk):
    heads, dk, dv = C_HEADS, C_QK_DIM, C_V_DIM
    sub = GLA_SUB
    assert chunk == 2 * sub
    c = pl.program_id(1)
    row = lax.broadcasted_iota(jnp.int32, (chunk, chunk), 0)
    col = lax.broadcasted_iota(jnp.int32, (chunk, chunk), 1)
    causal = (row >= col) & ((row < sub) == (col < sub))
    first = lax.broadcasted_iota(jnp.int32, (chunk, 1), 0) < sub

    def cum_log_decay(a1):
        a_hi = a1.astype(BF16)
        a_lo = (a1 - a_hi.astype(F32)).astype(BF16)
        w_hi = wah_ref[...]
        z = (jnp.dot(a_hi, w_hi, preferred_element_type=F32)
             + (jnp.dot(a_lo, w_hi, preferred_element_type=F32)
                + jnp.dot(a_hi, wal_ref[...], preferred_element_type=F32))) + ba_ref[...]
        log_a = _log_sigmoid(z) * (1.0 / C_GATE_TEMP)
        return jnp.dot(causal.astype(F32), log_a, precision=HIGHEST, preferred_element_type=F32)

    @pl.when(c == 0)
    def _():
        for ref in st_refs:
            ref[...] = jnp.zeros_like(ref)
        b_ref[...] = cum_log_decay(a1_ref[...])

    b_all = b_ref[...]
    b_next = cum_log_decay(a1n_ref[...])

    r = r_ref[...].astype(F32)
    g = g_ref[...]
    for h in range(heads):
        ks = slice(h * dk, (h + 1) * dk)
        vs = slice(h * dv, (h + 1) * dv)
        b = b_all[:, ks]
        q = q_ref[:, ks].astype(F32) * (dk ** -0.5)
        k = k_ref[:, ks].astype(F32)
        v = v_ref[:, vs]
        tot0 = b[sub - 1:sub, :]
        tot1 = b[chunk - 1:chunk, :]
        q_dec = q * jnp.exp(b)
        k_dec = (k * jnp.exp(-b)).astype(BF16)
        k_end = k * jnp.exp(jnp.where(first, tot0, tot1) - b)
        k_new = (k_end * jnp.where(first, jnp.exp(tot1), 1.0)).astype(BF16)
        q_int = (q_dec * jnp.where(first, 1.0, jnp.exp(tot0))).astype(BF16)
        q_dec = q_dec.astype(BF16)
        st = st_refs[h][...]
        qk_t = lax.dot_general(q_dec, k_dec, NT_DIMS, preferred_element_type=F32)
        inter = lax.dot_general(q_int, st.astype(BF16), NT_DIMS, preferred_element_type=F32)
        st_refs[h][...] = (jnp.exp(tot0 + tot1) * st
                           + lax.dot_general(v, k_new, TN_DIMS, preferred_element_type=F32))
        cross = lax.dot_general(q_dec[sub:], k_end[:sub].astype(BF16), NT_DIMS, preferred_element_type=F32)
        att = jnp.where(causal, qk_t, 0.0)
        o = jnp.dot(att.astype(BF16), v, preferred_element_type=F32) + inter
        o_cross = jnp.dot(cross.astype(BF16), v[:sub], preferred_element_type=F32)
        o = jnp.concatenate([o[:sub], o[sub:] + o_cross], axis=0)

        on = o * lax.rsqrt(jnp.mean(o * o, -1, keepdims=True) + NORM_EPS)
        rr = r[:, vs]
        y_ref[:, vs] = (on * g[:, vs] * (rr * _sigmoid(rr))).astype(y_ref.dtype)

    b_ref[...] = b_next


def _gla(y3, a13, w_a2_hi, w_a2_lo, b_a, gla_g, *, chunk):
    bsz, s, _ = y3.shape
    heads, dk, dv = C_HEADS, C_QK_DIM, C_V_DIM
    wk, wv = heads * dk, heads * dv
    assert wv == 2 * wk and s % chunk == 0
    nc = s // chunk
    return pl.pallas_call(
        functools.partial(_gla_kernel, chunk=chunk),
        grid=(bsz, nc),
        in_specs=[
            pl.BlockSpec((None, chunk, wk), lambda b, c: (b, c, 0)),
            pl.BlockSpec((None, chunk, wk), lambda b, c: (b, c, 1)),
            pl.BlockSpec((None, chunk, wv), lambda b, c: (b, c, 1)),
            pl.BlockSpec((None, chunk, wv), lambda b, c: (b, c, 2)),
            pl.BlockSpec((None, chunk, LANES), lambda b, c: (b, c, 0)),
            pl.BlockSpec((None, chunk, LANES), lambda b, c: (b, jnp.minimum(c + 1, nc - 1), 0)),
            pl.BlockSpec((LANES, wk), lambda b, c: (0, 0)),
            pl.BlockSpec((LANES, wk), lambda b, c: (0, 0)),
            pl.BlockSpec((1, wk), lambda b, c: (0, 0)),
            pl.BlockSpec((1, wv), lambda b, c: (0, 0)),
        ],
        out_specs=pl.BlockSpec((None, chunk, wv), lambda b, c: (b, c, 0)),
        out_shape=jax.ShapeDtypeStruct((bsz, s, wv), BF16),
        scratch_shapes=[pltpu.VMEM((chunk, wk), F32)] + heads * [pltpu.VMEM((dv, dk), F32)],
        compiler_params=_params("parallel", "arbitrary"),
        name="gla",
    )(y3, y3, y3, y3, a13, a13, w_a2_hi, w_a2_lo, b_a, gla_g)


def _out_ln_kernel(*refs, n_in, alpha):
    y_refs, w_refs = refs[:n_in], refs[n_in:2 * n_in]
    x_ref, g_ref, b_ref, o_ref = refs[2 * n_in:]
    rb = o_ref.shape[0] // ROW_SPLIT
    for r in range(ROW_SPLIT):
        rows = slice(r * rb, (r + 1) * rb)
        acc = jnp.dot(y_refs[0][rows, :], w_refs[0][...], preferred_element_type=F32)
        for y_ref, w_ref in zip(y_refs[1:], w_refs[1:]):
            acc = acc + jnp.dot(y_ref[rows, :], w_ref[...], preferred_element_type=F32)
        o_ref[rows, :] = _layer_norm(alpha * x_ref[rows, :] + acc, g_ref[...], b_ref[...])


def _out_ln(ys, w, x, ln_g, ln_b, layer, w_layer, ln_slot, *, alpha, tm):
    t, d = x.shape
    n_in = len(ys)
    wd = ys[0].shape[1]
    assert all(y.shape[1] == wd for y in ys) and w.shape[1:] == (n_in * wd, d)
    ln_spec = pl.BlockSpec((None, None, 1, d), lambda i: (layer, ln_slot, 0, 0))
    in_specs = ([pl.BlockSpec((tm, wd), lambda i: (i, 0)) for _ in ys]
                + [pl.BlockSpec((None, wd, d), lambda i, k=k: (w_layer, k, 0)) for k in range(n_in)]
                + [pl.BlockSpec((tm, d), lambda i: (i, 0)), ln_spec, ln_spec])
    return pl.pallas_call(
        functools.partial(_out_ln_kernel, n_in=n_in, alpha=alpha),
        grid=(t // tm,),
        in_specs=in_specs,
        out_specs=pl.BlockSpec((tm, d), lambda i: (i, 0)),
        out_shape=jax.ShapeDtypeStruct((t, d), F32),
        compiler_params=_params("parallel"),
        name="out_ln",
    )(*ys, *([w] * n_in), x, ln_g, ln_b)


def _ple_kernel(x_ref, p_ref, wp_ref, wg_ref, o_ref):
    rb = o_ref.shape[0] // ROW_SPLIT
    for r in range(ROW_SPLIT):
        rows = slice(r * rb, (r + 1) * rb)
        x = x_ref[rows, :]
        e = jnp.dot(p_ref[rows, :].astype(BF16), wp_ref[...], preferred_element_type=F32)
        gate = jnp.dot(x.astype(BF16), wg_ref[...], preferred_element_type=F32)
        o_ref[rows, :] = x + e * _sigmoid(gate)


def _ple(x, p, w_proj, w_gate, layer, *, tm):
    t, d = x.shape
    s, pd = p.shape[2:]
    assert s % tm == 0
    per_seq = s // tm
    return pl.pallas_call(
        _ple_kernel,
        grid=(t // tm,),
        in_specs=[
            pl.BlockSpec((tm, d), lambda i: (i, 0)),
            pl.BlockSpec((None, None, tm, pd), lambda i: (layer, i // per_seq, i % per_seq, 0)),
            pl.BlockSpec((None, pd, d), lambda i: (layer, 0, 0)),
            pl.BlockSpec((None, d, d), lambda i: (layer, 0, 0)),
        ],
        out_specs=pl.BlockSpec((tm, d), lambda i: (i, 0)),
        out_shape=jax.ShapeDtypeStruct((t, d), F32),
        compiler_params=_params("parallel"),
        name="ple",
    )(x, p, w_proj, w_gate)


def _tiles(t, s):
    return dict(
        ffn_tm=min(512, t), ffn_tf=512,
        proj_tm=min(1024, t), proj_tn=1024,
        row_tm=min(512, t),
        attn_t=min(512, s),
        mlstm_chunk=min(128, s),
        gla_chunk=min(2 * GLA_SUB, s),
    )


def _pad_cols(w, n):
    return jnp.pad(w, ((0, 0), (0, n - w.shape[1])))


def kernel(x, p, ln_g, ln_b, w_ffn_in, w_ffn_out, w_in_ab, w_out_ab, rel_bias, lambda_q1, lambda_k1,
           lambda_q2, lambda_k2, diff_norm, conv_w, conv_b, b_igate, b_fgate, mlstm_norm, w_in_c,
           w_alpha2, b_alpha, gla_norm, w_out_c, w_ple_proj, w_ple_gate):
    bsz, s, d = x.shape
    depth = p.shape[0]
    t = bsz * s
    tl = _tiles(t, s)
    alpha = (2 * depth) ** 0.25
    a_w = A_HEADS * A_V_DIM
    ab_main = 3 * a_w + 3 * B_HEADS * B_V_DIM
    c_main = 2 * C_HEADS * C_QK_DIM + 2 * C_HEADS * C_V_DIM
    row = lambda a: a.reshape(1, -1)

    wfi, wfo = w_ffn_in.astype(BF16), w_ffn_out.astype(BF16)
    wo_ab, wo_c = w_out_ab.astype(BF16), w_out_c.astype(BF16)
    wpp, wpg = w_ple_proj.astype(BF16), w_ple_gate.astype(BF16)
    g4, b4 = ln_g.reshape(depth, 3, 1, d), ln_b.reshape(depth, 3, 1, d)

    xf = x.reshape(t, d)
    for i in range(depth):
        xf = _ffn_ln(xf, wfi, wfo, g4, b4, i, 0, 0, alpha=alpha, tm=tl["ffn_tm"], tf=tl["ffn_tf"])
        if i % 2 == 0:
            e = i // 2
            lam_init = 0.8 - 0.6 * math.exp(-0.3 * i)
            y, gcol, grow = _in_proj(xf, w_in_ab, e, ab_main, tm=tl["proj_tm"], tn=tl["proj_tn"])
            y3 = y.reshape(bsz, s, ab_main)
            lam_vecs = jnp.stack([lambda_q1[e], lambda_k1[e], lambda_q2[e], lambda_k2[e]])
            ya = _diff_attention(y3, rel_bias, lam_vecs, row(diff_norm[e]), lam_init=lam_init, t=tl["attn_t"])
            gate_bias = jnp.concatenate([b_igate[e], b_fgate[e]])
            yb = _mlstm(y3, gcol.reshape(bsz, s, LANES), grow,
                        conv_w[e], row(conv_b[e]), _pad_cols(row(gate_bias), LANES), gate_bias.reshape(-1, 1),
                        row(mlstm_norm[e]), chunk=tl["mlstm_chunk"])
            ys, w_out, w_layer = [ya.reshape(t, a_w), yb.reshape(t, -1)], wo_ab, e
        else:
            o = i // 2
            y, a1, _ = _in_proj(xf, w_in_c, o, c_main, tm=tl["proj_tm"], tn=tl["proj_tn"])
            w_a2 = jnp.pad(w_alpha2[o], ((0, LANES - w_alpha2.shape[1]), (0, 0)))
            w_a2_hi = lax.reduce_precision(w_a2, exponent_bits=8, mantissa_bits=7)
            w_a2_lo = (w_a2 - w_a2_hi).astype(BF16)
            yc = _gla(y.reshape(bsz, s, c_main), a1.reshape(bsz, s, LANES), w_a2_hi.astype(BF16), w_a2_lo,
                      row(b_alpha[o]),
                      row(gla_norm[o]), chunk=tl["gla_chunk"])
            ys, w_out, w_layer = [yc.reshape(t, -1)], wo_c, o
        xf = _out_ln(ys, w_out, xf, g4, b4, i, w_layer, 1, alpha=alpha, tm=tl["row_tm"])
        xf = _ffn_ln(xf, wfi, wfo, g4, b4, i, 1, 2, alpha=alpha, tm=tl["ffn_tm"], tf=tl["ffn_tf"])
        xf = _ple(xf, p, wpp, wpg, i, tm=tl["row_tm"])
    return xf.reshape(bsz, s, d)
```

```python
import functools
import math

import jax
import jax.numpy as jnp
from jax import lax
from jax.experimental import pallas as pl
from jax.experimental.pallas import tpu as pltpu

F32 = jnp.float32
BF16 = jnp.bfloat16
HIGHEST = lax.Precision.HIGHEST

A_HEADS = 8
A_QK_DIM = 64
A_V_DIM = 128
B_HEADS = 4
B_QK_DIM = 128
B_V_DIM = 256
CONV_WIDTH = 4
C_HEADS = 4
C_QK_DIM = 256
C_V_DIM = 512
C_GATE_TEMP = 16.0
GLA_SUB = 64
N_BUCKETS = 32
MAX_DISTANCE = 128
LN_EPS = 1e-5
NORM_EPS = 1e-6

V7X_VMEM_LIMIT_BYTES = 56 * 1024 * 1024
LANES = 128
SUBLANES = 8
BF16_SUBLANES = 16
ATTN_ONES_ROWS = 16
ROW_SPLIT = 4

NT_DIMS = (((1,), (1,)), ((), ()))
TN_DIMS = (((0,), (0,)), ((), ()))


def _params(*sem):
    return pltpu.CompilerParams(dimension_semantics=sem, vmem_limit_bytes=V7X_VMEM_LIMIT_BYTES)


def _layer_norm(y, g, b, eps=LN_EPS):
    mu = jnp.mean(y, -1, keepdims=True)
    yc = y - mu
    var = jnp.mean(yc * yc, -1, keepdims=True)
    return yc * lax.rsqrt(var + eps) * g + b


def _sigmoid(x):
    return 1.0 / (1.0 + jnp.exp(-x))


def _log_sigmoid(x):
    return jnp.minimum(x, 0.0) - jnp.log(1.0 + jnp.exp(-jnp.abs(x)))


def _ffn_ln_kernel(x_ref, wg0_ref, wu0_ref, wo0_ref, wg1_ref, wu1_ref, wo1_ref, g_ref, b_ref, o_ref, xb_ref,
                   *, alpha, n_chunks):
    j = pl.program_id(1)
    last = pl.num_programs(1) - 1

    @pl.when(j == 0)
    def _():
        x = x_ref[...]
        xb_ref[...] = x.astype(BF16)
        o_ref[...] = (2.0 * alpha) * x

    def chunk(wg_ref, wu_ref, wo_ref):
        xb = xb_ref[...]
        gate = jnp.dot(xb, wg_ref[...], preferred_element_type=F32)
        up = jnp.dot(xb, wu_ref[...], preferred_element_type=F32)
        h = (gate * _sigmoid(gate) * up).astype(BF16)
        o_ref[...] += jnp.dot(h, wo_ref[...], preferred_element_type=F32)

    if n_chunks % 2 == 0:
        chunk(wg0_ref, wu0_ref, wo0_ref)
        chunk(wg1_ref, wu1_ref, wo1_ref)
    else:
        @pl.when(j < last)
        def _():
            chunk(wg0_ref, wu0_ref, wo0_ref)
            chunk(wg1_ref, wu1_ref, wo1_ref)

        @pl.when(j == last)
        def _():
            chunk(wg0_ref, wu0_ref, wo0_ref)

    @pl.when(j == last)
    def _():
        o_ref[...] = _layer_norm(o_ref[...], g_ref[...], b_ref[...], eps=4.0 * LN_EPS)


def _ffn_ln(x, w_in, w_out, ln_g, ln_b, layer, slot, ln_slot, *, alpha, tm, tf):
    t, d = x.shape
    f = w_out.shape[2]
    nf = f // tf
    assert t % tm == 0 and f % tf == 0 and w_in.shape[2:] == (d, 2 * f) and nf >= 2
    steps = (nf + 1) // 2
    first = lambda j: 2 * j
    second = lambda j: jnp.where(2 * j + 1 < nf, 2 * j + 1, nf - 2)
    ln_spec = pl.BlockSpec((None, None, 1, d), lambda i, j: (layer, ln_slot, 0, 0))

    def weight_specs(chunk_of):
        return [pl.BlockSpec((None, None, d, tf), lambda i, j: (layer, slot, 0, chunk_of(j))),
                pl.BlockSpec((None, None, d, tf), lambda i, j: (layer, slot, 0, nf + chunk_of(j))),
                pl.BlockSpec((None, None, tf, d), lambda i, j: (layer, slot, chunk_of(j), 0))]

    return pl.pallas_call(
        functools.partial(_ffn_ln_kernel, alpha=alpha, n_chunks=nf),
        grid=(t // tm, steps),
        in_specs=[pl.BlockSpec((tm, d), lambda i, j: (i, 0))] + weight_specs(first) + weight_specs(second)
                 + [ln_spec, ln_spec],
        out_specs=pl.BlockSpec((tm, d), lambda i, j: (i, 0)),
        out_shape=jax.ShapeDtypeStruct((t, d), F32),
        scratch_shapes=[pltpu.VMEM((tm, d), BF16)],
        compiler_params=_params("parallel", "arbitrary"),
        name="ffn_ln",
    )(x, w_in, w_in, w_out, w_in, w_in, w_out, ln_g, ln_b)


def _in_proj_kernel(x_ref, w_ref, wgc_ref, wgr_ref, y_ref, gc_ref, gr_ref, xb_ref):
    j = pl.program_id(1)

    @pl.when(j == 0)
    def _():
        xb = x_ref[...].astype(BF16)
        xb_ref[...] = xb
        gc_ref[...] = jnp.dot(xb, wgc_ref[...], preferred_element_type=F32)
        gr_ref[...] = lax.dot_general(wgr_ref[...], xb, NT_DIMS, preferred_element_type=F32)

    y_ref[...] = lax.dot_general(xb_ref[...], w_ref[...], NT_DIMS,
                                 preferred_element_type=F32).astype(y_ref.dtype)


def _in_proj(x, w_stack, layer, n_main, *, tm, tn):
    t, d = x.shape
    assert t % tm == 0 and n_main % tn == 0
    w_main = jnp.swapaxes(w_stack, 1, 2)[:, :n_main, :].astype(BF16)
    w_gates = w_stack[layer, :, n_main:].astype(BF16)
    w_gate_cols = _pad_cols(w_gates, LANES)
    w_gate_rows = jnp.transpose(w_gates[:, :SUBLANES])
    return pl.pallas_call(
        _in_proj_kernel,
        grid=(t // tm, n_main // tn),
        in_specs=[
            pl.BlockSpec((tm, d), lambda i, j: (i, 0)),
            pl.BlockSpec((None, tn, d), lambda i, j: (layer, j, 0)),
            pl.BlockSpec((d, LANES), lambda i, j: (0, 0)),
            pl.BlockSpec((SUBLANES, d), lambda i, j: (0, 0)),
        ],
        out_specs=[
            pl.BlockSpec((tm, tn), lambda i, j: (i, j)),
            pl.BlockSpec((tm, LANES), lambda i, j: (i, 0)),
            pl.BlockSpec((SUBLANES, tm), lambda i, j: (0, i)),
        ],
        out_shape=[
            jax.ShapeDtypeStruct((t, n_main), BF16),
            jax.ShapeDtypeStruct((t, LANES), F32),
            jax.ShapeDtypeStruct((SUBLANES, t), F32),
        ],
        scratch_shapes=[pltpu.VMEM((tm, d), BF16)],
        compiler_params=_params("parallel", "arbitrary"),
        name="in_proj",
    )(x, w_main, w_gate_cols, w_gate_rows)


def _diff_attn_kernel(q_ref, k_ref, v_ref, tab_ref, lam_ref, g_ref, o_ref,
                      vt_ref, bdiag_ref, bnear_ref, m_ref, acc_ref,
                      s0_ref, mx0_ref, sh0_ref, s1_ref, mx1_ref, sh1_ref, *, t, lam_init):
    qi = pl.program_id(2)
    dk, dv = A_QK_DIM, A_V_DIM
    nk = k_ref.shape[0] // t
    bufs = ((s0_ref, mx0_ref, sh0_ref), (s1_ref, mx1_ref, sh1_ref))

    @pl.when(qi == 0)
    def _():
        for c in range(nk):
            vt_ref[c, 0:dv, :] = jnp.transpose(v_ref[c * t:(c + 1) * t, :].astype(F32)).astype(BF16)
            vt_ref[c, dv:, :] = jnp.ones((vt_ref.shape[1] - dv, t), BF16)
        skew = pltpu.roll(jnp.broadcast_to(tab_ref[...], (t, 2 * t)), 0, 1, stride=1, stride_axis=0)
        key = lax.broadcasted_iota(jnp.int32, (t, t), 0)
        qry = lax.broadcasted_iota(jnp.int32, (t, t), 1)
        bdiag_ref[...] = jnp.where(qry >= key, skew[:, :t], -1e30)
        bnear_ref[...] = skew[:, t:]

    q_t = jnp.transpose(q_ref[...].astype(F32)) * (dk ** -0.5)
    feat = lax.broadcasted_iota(jnp.int32, q_t.shape, 0)
    qs_t = jnp.concatenate([jnp.where(feat < dk, q_t, 0.0), jnp.where(feat >= dk, q_t, 0.0)],
                           axis=1).astype(BF16)

    m_ref[...] = jnp.full_like(m_ref, -jnp.inf)
    acc_ref[...] = jnp.zeros_like(acc_ref)

    def score(kj, buf, bias_ref, shift):
        s_ref, mx_ref, sh_ref = bufs[buf]
        off = pl.multiple_of(kj * t, t)
        s = jnp.dot(k_ref[pl.ds(off, t), :], qs_t, preferred_element_type=F32)
        if bias_ref is not None:
            b = bias_ref[...]
            s = s + jnp.concatenate([b, b], axis=1)
        s_ref[...] = s
        mx_ref[...] = jnp.max(s, 0, keepdims=True) + shift
        sh_ref[...] = jnp.zeros_like(sh_ref) + shift

    def absorb(kj, buf):
        s_ref, mx_ref, sh_ref = bufs[buf]
        m_old = m_ref[...]
        m_new = jnp.maximum(m_old, mx_ref[...])
        p = jnp.exp(s_ref[...] - (m_new - sh_ref[...])).astype(BF16)
        acc_ref[...] = (jnp.exp(m_old - m_new) * acc_ref[...]
                        + jnp.dot(vt_ref[kj], p, preferred_element_type=F32))
        m_ref[...] = m_new

    c_far = tab_ref[:, 2 * t - 1:2 * t]
    score(qi, 0, bdiag_ref, 0.0)

    @pl.when(qi >= 1)
    def _():
        score(qi - 1, 1, bnear_ref, 0.0)
        absorb(qi, 0)

    def far_pair(p, carry):
        kj = qi - 2 - 2 * p
        score(kj, 0, None, c_far)
        absorb(kj + 1, 1)
        score(kj - 1, 1, None, c_far)
        absorb(kj, 0)
        return carry

    lax.fori_loop(0, jnp.maximum(qi - 1, 0) // 2, far_pair, 0)

    @pl.when((qi >= 2) & (qi % 2 == 0))
    def _():
        score(0, 0, None, c_far)
        absorb(1, 1)

    @pl.when(qi % 2 == 0)
    def _():
        absorb(0, 0)

    @pl.when(qi % 2 == 1)
    def _():
        absorb(0, 1)

    lv = lam_ref[...]
    lam = (jnp.exp(jnp.sum(lv[0:1] * lv[1:2], -1, keepdims=True))
           - jnp.exp(jnp.sum(lv[2:3] * lv[3:4], -1, keepdims=True)) + lam_init)
    acc = acc_ref[...]
    o_t = acc[0:dv] * (1.0 / acc[dv:dv + 1])
    out = jnp.transpose(o_t[:, :t] - lam * o_t[:, t:])
    hn = out * lax.rsqrt(jnp.mean(out * out, -1, keepdims=True) + NORM_EPS)
    o_ref[...] = (hn * g_ref[...] * (1.0 - lam_init)).astype(o_ref.dtype)


def _t5_bias_by_distance(rel_bias, n):
    r = jnp.arange(n, dtype=jnp.int32)
    max_exact = N_BUCKETS // 2
    rf = jnp.maximum(r, 1).astype(F32)
    large = max_exact + (jnp.log(rf / max_exact) / math.log(MAX_DISTANCE / max_exact)
                         * (N_BUCKETS - max_exact)).astype(jnp.int32)
    large = jnp.minimum(large, N_BUCKETS - 1)
    bucket = jnp.where(r < max_exact, r, large)
    return jnp.transpose(rel_bias[bucket]).astype(F32)


def _diff_attention(y3, rel_bias, lam_vecs, diff_g, *, lam_init, t):
    bsz, s, _ = y3.shape
    hd = 2 * A_QK_DIM
    dv = A_V_DIM
    assert hd == dv == LANES and s % t == 0 and t >= MAX_DISTANCE
    nq = s // t
    table = _t5_bias_by_distance(rel_bias, 2 * t).reshape(A_HEADS, 1, 2 * t)
    return pl.pallas_call(
        functools.partial(_diff_attn_kernel, t=t, lam_init=lam_init),
        grid=(bsz, A_HEADS, nq),
        in_specs=[
            pl.BlockSpec((None, t, hd), lambda b, h, q: (b, q, h)),
            pl.BlockSpec((None, s, hd), lambda b, h, q: (b, 0, A_HEADS + h)),
            pl.BlockSpec((None, s, dv), lambda b, h, q: (b, 0, 2 * A_HEADS + h)),
            pl.BlockSpec((None, 1, 2 * t), lambda b, h, q: (h, 0, 0)),
            pl.BlockSpec((4, A_QK_DIM), lambda b, h, q: (0, 0)),
            pl.BlockSpec((1, dv), lambda b, h, q: (0, h)),
        ],
        out_specs=pl.BlockSpec((None, t, dv), lambda b, h, q: (b, q, h)),
        out_shape=jax.ShapeDtypeStruct((bsz, s, A_HEADS * dv), BF16),
        scratch_shapes=[pltpu.VMEM((nq, dv + ATTN_ONES_ROWS, t), BF16),
                        pltpu.VMEM((t, t), F32),
                        pltpu.VMEM((t, t), F32),
                        pltpu.VMEM((1, 2 * t), F32),
                        pltpu.VMEM((dv + ATTN_ONES_ROWS, 2 * t), F32)]
                       + 2 * [pltpu.VMEM((t, 2 * t), F32), pltpu.VMEM((1, 2 * t), F32),
                              pltpu.VMEM((1, 2 * t), F32)],
        compiler_params=_params("parallel", "parallel", "arbitrary"),
        name="diff_attn",
    )(y3, y3, y3, table, lam_vecs, diff_g)


def _mlstm_kernel(qk_ref, v_ref, ob_ref, gc_ref, gr_ref, cw_ref, cb_ref, bc_ref, br_ref, g_ref, y_ref,
                  xp_ref, *state_refs, chunk):
    heads, dk, dv = B_HEADS, B_QK_DIM, B_V_DIM
    c_refs, n_refs, m_refs = state_refs[0::3], state_refs[1::3], state_refs[2::3]
    pad = xp_ref.shape[0] - chunk
    c = pl.program_id(1)

    @pl.when(c == 0)
    def _():
        xp_ref[0:pad, :] = jnp.zeros((pad, xp_ref.shape[1]), xp_ref.dtype)
        for ref in state_refs:
            ref[...] = jnp.zeros_like(ref)

    x = qk_ref[...]
    xp_ref[pad:pad + chunk, :] = x
    xp = xp_ref[...]
    cw = cw_ref[...]
    dst = lax.broadcasted_iota(jnp.int32, (chunk, pad + chunk), 0)
    src = lax.broadcasted_iota(jnp.int32, (chunk, pad + chunk), 1)
    conv = cb_ref[...] + cw[CONV_WIDTH - 1:CONV_WIDTH, :] * x.astype(F32)
    for j in range(CONV_WIDTH - 1):
        shift = (src == dst + (pad - (CONV_WIDTH - 1) + j)).astype(BF16)
        conv = conv + cw[j:j + 1, :] * jnp.dot(shift, xp, preferred_element_type=F32)
    xp_ref[0:pad, :] = x[chunk - pad:chunk, :]
    qk = conv * _sigmoid(conv)

    gc = gc_ref[...] + bc_ref[...]
    gr = gr_ref[...] + br_ref[...]
    row = lax.broadcasted_iota(jnp.int32, (chunk, chunk), 0)
    col = lax.broadcasted_iota(jnp.int32, (chunk, chunk), 1)
    causal = row >= col
    tril = causal.astype(F32)
    triu = (row <= col).astype(F32)
    b_c = jnp.dot(tril, _log_sigmoid(gc), precision=HIGHEST, preferred_element_type=F32)
    b_r = jnp.dot(_log_sigmoid(gr), triu, precision=HIGHEST, preferred_element_type=F32)

    ob = ob_ref[...].astype(F32)
    g = g_ref[...]
    hs = range(heads)
    q = [qk[:, h * dk:(h + 1) * dk] * (dk ** -0.5) for h in hs]
    k = [qk[:, (heads + h) * dk:(heads + h + 1) * dk] for h in hs]
    v = [v_ref[:, h * dv:(h + 1) * dv] for h in hs]
    b_col = [b_c[:, heads + h:heads + h + 1] for h in hs]
    i_col = [gc[:, h:h + 1] for h in hs]
    b_row = [b_r[heads + h:heads + h + 1, :] for h in hs]
    i_row = [gr[h:h + 1, :] for h in hs]
    m_prev = [m_refs[h][0:1, 0:1] for h in hs]
    c_mat = [c_refs[h][...] for h in hs]
    n_vec = [n_refs[h][...] for h in hs]

    b_last = [b_row[h][:, chunk - 1:chunk] for h in hs]
    g_col = [b_last[h] - b_col[h] + i_col[h] for h in hs]
    m_new = [jnp.maximum(b_last[h] + m_prev[h], jnp.max(g_col[h], 0, keepdims=True)) for h in hs]
    decay = [jnp.exp(b_last[h] + m_prev[h] - m_new[h]) for h in hs]
    kw = [k[h] * jnp.exp(g_col[h] - m_new[h]) for h in hs]
    qb = [q[h].astype(BF16) for h in hs]
    qk_t = [lax.dot_general(qb[h], k[h].astype(BF16), NT_DIMS, preferred_element_type=F32) for h in hs]
    q_c = [jnp.dot(qb[h], c_mat[h].astype(BF16), preferred_element_type=F32) for h in hs]
    kw_v = [lax.dot_general(kw[h].astype(BF16), v[h], TN_DIMS, preferred_element_type=F32) for h in hs]

    d = [jnp.where(causal, b_col[h] - b_row[h] + i_row[h], -jnp.inf) for h in hs]
    a_col = [b_col[h] + m_prev[h] for h in hs]
    m_t = [jnp.maximum(a_col[h], jnp.max(d[h], -1, keepdims=True)) for h in hs]
    w_inter = [jnp.exp(a_col[h] - m_t[h]) for h in hs]
    sw = [qk_t[h] * jnp.exp(d[h] - m_t[h]) for h in hs]
    sw_v = [jnp.dot(sw[h].astype(BF16), v[h], preferred_element_type=F32) for h in hs]

    for h in hs:
        c_refs[h][...] = decay[h] * c_mat[h] + kw_v[h]
        n_refs[h][...] = decay[h] * n_vec[h] + jnp.sum(kw[h], 0, keepdims=True)
        m_refs[h][...] = jnp.broadcast_to(m_new[h], m_refs[h].shape)

    for h in hs:
        num = w_inter[h] * q_c[h] + sw_v[h]
        den = (w_inter[h] * jnp.sum(q[h] * n_vec[h], -1, keepdims=True)
               + jnp.sum(sw[h], -1, keepdims=True))
        hh = num / jnp.maximum(jnp.abs(den), jnp.exp(-m_t[h]))
        hn = hh * lax.rsqrt(jnp.mean(hh * hh, -1, keepdims=True) + NORM_EPS)
        sl = slice(h * dv, (h + 1) * dv)
        y_ref[:, sl] = (hn * g[:, sl] * _sigmoid(ob[:, sl])).astype(y_ref.dtype)


def _mlstm(y3, gcol3, grow3, conv_w, conv_b, gate_bias_cols, gate_bias_rows, mlstm_g, *, chunk):
    bsz, s, _ = y3.shape
    heads, dk, dv = B_HEADS, B_QK_DIM, B_V_DIM
    w = heads * dv
    assert 2 * heads * dk == w and s % chunk == 0 and chunk % LANES == 0
    nc = s // chunk
    return pl.pallas_call(
        functools.partial(_mlstm_kernel, chunk=chunk),
        grid=(bsz, nc),
        in_specs=[
            pl.BlockSpec((None, chunk, w), lambda b, c: (b, c, 3)),
            pl.BlockSpec((None, chunk, w), lambda b, c: (b, c, 4)),
            pl.BlockSpec((None, chunk, w), lambda b, c: (b, c, 5)),
            pl.BlockSpec((None, chunk, LANES), lambda b, c: (b, c, 0)),
            pl.BlockSpec((SUBLANES, chunk), lambda b, c: (0, b * nc + c)),
            pl.BlockSpec((CONV_WIDTH, w), lambda b, c: (0, 0)),
            pl.BlockSpec((1, w), lambda b, c: (0, 0)),
            pl.BlockSpec((1, LANES), lambda b, c: (0, 0)),
            pl.BlockSpec((SUBLANES, 1), lambda b, c: (0, 0)),
            pl.BlockSpec((1, w), lambda b, c: (0, 0)),
        ],
        out_specs=pl.BlockSpec((None, chunk, w), lambda b, c: (b, c, 0)),
        out_shape=jax.ShapeDtypeStruct((bsz, s, w), BF16),
        scratch_shapes=[
            pltpu.VMEM((BF16_SUBLANES + chunk, w), BF16),
        ] + heads * [pltpu.VMEM((dk, dv), F32), pltpu.VMEM((1, dk), F32), pltpu.VMEM((SUBLANES, LANES), F32)],
        compiler_params=_params("parallel", "arbitrary"),
        name="mlstm",
    )(y3, y3, y3, gcol3, grow3, conv_w, conv_b, gate_bias_cols, gate_bias_rows, mlstm_g)


def _gla_kernel(q_ref, k_ref, v_ref, r_ref, a1_ref, a1n_ref, wah_ref, wal_ref, ba_ref, g_ref, y_ref,
                b_ref, *st_refs, chunk):
    heads, dk, dv = C_HEADS, C_QK_DIM, C_V_DIM
    sub = GLA_SUB
    assert chunk == 2 * sub
    c = pl.program_id(1)
    row = lax.broadcasted_iota(jnp.int32, (chunk, chunk), 0)
    col = lax.broadcasted_iota(jnp.int32, (chunk, chunk), 1)
    causal = (row >= col) & ((row < sub) == (col < sub))
    first = lax.broadcasted_iota(jnp.int32, (chunk, 1), 0) < sub

    def cum_log_decay(a1):
        a_hi = a1.astype(BF16)
        a_lo = (a1 - a_hi.astype(F32)).astype(BF16)
        w_hi = wah_ref[...]
        z = (jnp.dot(a_hi, w_hi, preferred_element_type=F32)
             + (jnp.dot(a_lo, w_hi, preferred_element_type=F32)
                + jnp.dot(a_hi, wal_ref[...], preferred_element_type=F32))) + ba_ref[...]
        log_a = _log_sigmoid(z) * (1.0 / C_GATE_TEMP)
        return jnp.dot(causal.astype(F32), log_a, precision=HIGHEST, preferred_element_type=F32)

    @pl.when(c == 0)
    def _():
        for ref in st_refs:
            ref[...] = jnp.zeros_like(ref)
        b_ref[...] = cum_log_decay(a1_ref[...])

    b_all = b_ref[...]
    b_next = cum_log_decay(a1n_ref[...])

    r = r_ref[...].astype(F32)
    g = g_ref[...]
    for h in range(heads):
        ks = slice(h * dk, (h + 1) * dk)
        vs = slice(h * dv, (h + 1) * dv)
        b = b_all[:, ks]
        q = q_ref[:, ks].astype(F32) * (dk ** -0.5)
        k = k_ref[:, ks].astype(F32)
        v = v_ref[:, vs]
        tot0 = b[sub - 1:sub, :]
        tot1 = b[chunk - 1:chunk, :]
        q_dec = q * jnp.exp(b)
        k_dec = (k * jnp.exp(-b)).astype(BF16)
        k_end = k * jnp.exp(jnp.where(first, tot0, tot1) - b)
        k_new = (k_end * jnp.where(first, jnp.exp(tot1), 1.0)).astype(BF16)
        q_int = (q_dec * jnp.where(first, 1.0, jnp.exp(tot0))).astype(BF16)
        q_dec = q_dec.astype(BF16)
        st = st_refs[h][...]
        qk_t = lax.dot_general(q_dec, k_dec, NT_DIMS, preferred_element_type=F32)
        inter = lax.dot_general(q_int, st.astype(BF16), NT_DIMS, preferred_element_type=F32)
        st_refs[h][...] = (jnp.exp(tot0 + tot1) * st
                           + lax.dot_general(v, k_new, TN_DIMS, preferred_element_type=F32))
        cross = lax.dot_general(q_dec[sub:], k_end[:sub].astype(BF16), NT_DIMS, preferred_element_type=F32)
        att = jnp.where(causal, qk_t, 0.0)
        o = jnp.dot(att.astype(BF16), v, preferred_element_type=F32) + inter
        o_cross = jnp.dot(cross.astype(BF16), v[:sub], preferred_element_type=F32)
        o = jnp.concatenate([o[:sub], o[sub:] + o_cross], axis=0)

        on = o * lax.rsqrt(jnp.mean(o * o, -1, keepdims=True) + NORM_EPS)
        rr = r[:, vs]
        y_ref[:, vs] = (on * g[:, vs] * (rr * _sigmoid(rr))).astype(y_ref.dtype)

    b_ref[...] = b_next


def _gla(y3, a13, w_a2_hi, w_a2_lo, b_a, gla_g, *, chunk):
    bsz, s, _ = y3.shape
    heads, dk, dv = C_HEADS, C_QK_DIM, C_V_DIM
    wk, wv = heads * dk, heads * dv
    assert wv == 2 * wk and s % chunk == 0
    nc = s // chunk
    return pl.pallas_call(
        functools.partial(_gla_kernel, chunk=chunk),
        grid=(bsz, nc),
        in_specs=[
            pl.BlockSpec((None, chunk, wk), lambda b, c: (b, c, 0)),
            pl.BlockSpec((None, chunk, wk), lambda b, c: (b, c, 1)),
            pl.BlockSpec((None, chunk, wv), lambda b, c: (b, c, 1)),
            pl.BlockSpec((None, chunk, wv), lambda b, c: (b, c, 2)),
            pl.BlockSpec((None, chunk, LANES), lambda b, c: (b, c, 0)),
            pl.BlockSpec((None, chunk, LANES), lambda b, c: (b, jnp.minimum(c + 1, nc - 1), 0)),
            pl.BlockSpec((LANES, wk), lambda b, c: (0, 0)),
            pl.BlockSpec((LANES, wk), lambda b, c: (0, 0)),
            pl.BlockSpec((1, wk), lambda b, c: (0, 0)),
            pl.BlockSpec((1, wv), lambda b, c: (0, 0)),
        ],
        out_specs=pl.BlockSpec((None, chunk, wv), lambda b, c: (b, c, 0)),
        out_shape=jax.ShapeDtypeStruct((bsz, s, wv), BF16),
        scratch_shapes=[pltpu.VMEM((chunk, wk), F32)] + heads * [pltpu.VMEM((dv, dk), F32)],
        compiler_params=_params("parallel", "arbitrary"),
        name="gla",
    )(y3, y3, y3, y3, a13, a13, w_a2_hi, w_a2_lo, b_a, gla_g)


def _out_ln_kernel(*refs, n_in, alpha):
    y_refs, w_refs = refs[:n_in], refs[n_in:2 * n_in]
    x_ref, g_ref, b_ref, o_ref = refs[2 * n_in:]
    rb = o_ref.shape[0] // ROW_SPLIT
    for r in range(ROW_SPLIT):
        rows = slice(r * rb, (r + 1) * rb)
        acc = jnp.dot(y_refs[0][rows, :], w_refs[0][...], preferred_element_type=F32)
        for y_ref, w_ref in zip(y_refs[1:], w_refs[1:]):
            acc = acc + jnp.dot(y_ref[rows, :], w_ref[...], preferred_element_type=F32)
        o_ref[rows, :] = _layer_norm(alpha * x_ref[rows, :] + acc, g_ref[...], b_ref[...])


def _out_ln(ys, w, x, ln_g, ln_b, layer, w_layer, ln_slot, *, alpha, tm):
    t, d = x.shape
    n_in = len(ys)
    wd = ys[0].shape[1]
    assert all(y.shape[1] == wd for y in ys) and w.shape[1:] == (n_in * wd, d)
    ln_spec = pl.BlockSpec((None, None, 1, d), lambda i: (layer, ln_slot, 0, 0))
    in_specs = ([pl.BlockSpec((tm, wd), lambda i: (i, 0)) for _ in ys]
                + [pl.BlockSpec((None, wd, d), lambda i, k=k: (w_layer, k, 0)) for k in range(n_in)]
                + [pl.BlockSpec((tm, d), lambda i: (i, 0)), ln_spec, ln_spec])
    return pl.pallas_call(
        functools.partial(_out_ln_kernel, n_in=n_in, alpha=alpha),
        grid=(t // tm,),
        in_specs=in_specs,
        out_specs=pl.BlockSpec((tm, d), lambda i: (i, 0)),
        out_shape=jax.ShapeDtypeStruct((t, d), F32),
        compiler_params=_params("parallel"),
        name="out_ln",
    )(*ys, *([w] * n_in), x, ln_g, ln_b)


def _ple_kernel(x_ref, p_ref, wp_ref, wg_ref, o_ref):
    rb = o_ref.shape[0] // ROW_SPLIT
    for r in range(ROW_SPLIT):
        rows = slice(r * rb, (r + 1) * rb)
        x = x_ref[rows, :]
        e = jnp.dot(p_ref[rows, :].astype(BF16), wp_ref[...], preferred_element_type=F32)
        gate = jnp.dot(x.astype(BF16), wg_ref[...], preferred_element_type=F32)
        o_ref[rows, :] = x + e * _sigmoid(gate)


def _ple(x, p, w_proj, w_gate, layer, *, tm):
    t, d = x.shape
    s, pd = p.shape[2:]
    assert s % tm == 0
    per_seq = s // tm
    return pl.pallas_call(
        _ple_kernel,
        grid=(t // tm,),
        in_specs=[
            pl.BlockSpec((tm, d), lambda i: (i, 0)),
            pl.BlockSpec((None, None, tm, pd), lambda i: (layer, i // per_seq, i % per_seq, 0)),
            pl.BlockSpec((None, pd, d), lambda i: (layer, 0, 0)),
            pl.BlockSpec((None, d, d), lambda i: (layer, 0, 0)),
        ],
        out_specs=pl.BlockSpec((tm, d), lambda i: (i, 0)),
        out_shape=jax.ShapeDtypeStruct((t, d), F32),
        compiler_params=_params("parallel"),
        name="ple",
    )(x, p, w_proj, w_gate)


def _tiles(t, s):
    return dict(
        ffn_tm=min(512, t), ffn_tf=512,
        proj_tm=min(1024, t), proj_tn=1024,
        row_tm=min(512, t),
        attn_t=min(512, s),
        mlstm_chunk=min(128, s),
        gla_chunk=min(2 * GLA_SUB, s),
    )


def _pad_cols(w, n):
    return jnp.pad(w, ((0, 0), (0, n - w.shape[1])))


def kernel(x, p, ln_g, ln_b, w_ffn_in, w_ffn_out, w_in_ab, w_out_ab, rel_bias, lambda_q1, lambda_k1,
           lambda_q2, lambda_k2, diff_norm, conv_w, conv_b, b_igate, b_fgate, mlstm_norm, w_in_c,
           w_alpha2, b_alpha, gla_norm, w_out_c, w_ple_proj, w_ple_gate):
    bsz, s, d = x.shape
    depth = p.shape[0]
    t = bsz * s
    tl = _tiles(t, s)
    alpha = (2 * depth) ** 0.25
    a_w = A_HEADS * A_V_DIM
    ab_main = 3 * a_w + 3 * B_HEADS * B_V_DIM
    c_main = 2 * C_HEADS * C_QK_DIM + 2 * C_HEADS * C_V_DIM
    row = lambda a: a.reshape(1, -1)

    wfi, wfo = w_ffn_in.astype(BF16), w_ffn_out.astype(BF16)
    wo_ab, wo_c = w_out_ab.astype(BF16), w_out_c.astype(BF16)
    wpp, wpg = w_ple_proj.astype(BF16), w_ple_gate.astype(BF16)
    g4, b4 = ln_g.reshape(depth, 3, 1, d), ln_b.reshape(depth, 3, 1, d)

    xf = x.reshape(t, d)
    for i in range(depth):
        xf = _ffn_ln(xf, wfi, wfo, g4, b4, i, 0, 0, alpha=alpha, tm=tl["ffn_tm"], tf=tl["ffn_tf"])
        if i % 2 == 0:
            e = i // 2
            lam_init = 0.8 - 0.6 * math.exp(-0.3 * i)
            y, gcol, grow = _in_proj(xf, w_in_ab, e, ab_main, tm=tl["proj_tm"], tn=tl["proj_tn"])
            y3 = y.reshape(bsz, s, ab_main)
            lam_vecs = jnp.stack([lambda_q1[e], lambda_k1[e], lambda_q2[e], lambda_k2[e]])
            ya = _diff_attention(y3, rel_bias, lam_vecs, row(diff_norm[e]), lam_init=lam_init, t=tl["attn_t"])
            gate_bias = jnp.concatenate([b_igate[e], b_fgate[e]])
            yb = _mlstm(y3, gcol.reshape(bsz, s, LANES), grow,
                        conv_w[e], row(conv_b[e]), _pad_cols(row(gate_bias), LANES), gate_bias.reshape(-1, 1),
                        row(mlstm_norm[e]), chunk=tl["mlstm_chunk"])
            ys, w_out, w_layer = [ya.reshape(t, a_w), yb.reshape(t, -1)], wo_ab, e
        else:
            o = i // 2
            y, a1, _ = _in_proj(xf, w_in_c, o, c_main, tm=tl["proj_tm"], tn=tl["proj_tn"])
            w_a2 = jnp.pad(w_alpha2[o], ((0, LANES - w_alpha2.shape[1]), (0, 0)))
            w_a2_hi = lax.reduce_precision(w_a2, exponent_bits=8, mantissa_bits=7)
            w_a2_lo = (w_a2 - w_a2_hi).astype(BF16)
            yc = _gla(y.reshape(bsz, s, c_main), a1.reshape(bsz, s, LANES), w_a2_hi.astype(BF16), w_a2_lo,
                      row(b_alpha[o]),
                      row(gla_norm[o]), chunk=tl["gla_chunk"])
            ys, w_out, w_layer = [yc.reshape(t, -1)], wo_c, o
        xf = _out_ln(ys, w_out, xf, g4, b4, i, w_layer, 1, alpha=alpha, tm=tl["row_tm"])
        xf = _ffn_ln(xf, wfi, wfo, g4, b4, i, 1, 2, alpha=alpha, tm=tl["ffn_tm"], tf=tl["ffn_tf"])
        xf = _ple(xf, p, wpp, wpg, i, tm=tl["row_tm"])
    return xf.reshape(bsz, s, d)
```

```python
import functools
import math

import jax
import jax.numpy as jnp
from jax import lax
from jax.experimental import pallas as pl
from jax.experimental.pallas import tpu as pltpu

F32 = jnp.float32
BF16 = jnp.bfloat16
HIGHEST = lax.Precision.HIGHEST

A_HEADS = 8
A_QK_DIM = 64
A_V_DIM = 128
B_HEADS = 4
B_QK_DIM = 128
B_V_DIM = 256
CONV_WIDTH = 4
C_HEADS = 4
C_QK_DIM = 256
C_V_DIM = 512
C_GATE_TEMP = 16.0
GLA_SUB = 64
N_BUCKETS = 32
MAX_DISTANCE = 128
LN_EPS = 1e-5
NORM_EPS = 1e-6

V7X_VMEM_LIMIT_BYTES = 56 * 1024 * 1024
LANES = 128
SUBLANES = 8
BF16_SUBLANES = 16
ATTN_ONES_ROWS = 16
ROW_SPLIT = 4

NT_DIMS = (((1,), (1,)), ((), ()))
TN_DIMS = (((0,), (0,)), ((), ()))


def _params(*sem):
    return pltpu.CompilerParams(dimension_semantics=sem, vmem_limit_bytes=V7X_VMEM_LIMIT_BYTES)


def _layer_norm(y, g, b, eps=LN_EPS):
    mu = jnp.mean(y, -1, keepdims=True)
    yc = y - mu
    var = jnp.mean(yc * yc, -1, keepdims=True)
    return yc * lax.rsqrt(var + eps) * g + b


def _sigmoid(x):
    return 1.0 / (1.0 + jnp.exp(-x))


def _log_sigmoid(x):
    return jnp.minimum(x, 0.0) - jnp.log(1.0 + jnp.exp(-jnp.abs(x)))


def _ffn_ln_kernel(x_ref, wg0_ref, wu0_ref, wo0_ref, wg1_ref, wu1_ref, wo1_ref, g_ref, b_ref, o_ref, xb_ref,
                   *, alpha, n_chunks):
    j = pl.program_id(1)
    last = pl.num_programs(1) - 1

    @pl.when(j == 0)
    def _():
        x = x_ref[...]
        xb_ref[...] = x.astype(BF16)
        o_ref[...] = (2.0 * alpha) * x

    def chunk(wg_ref, wu_ref, wo_ref):
        xb = xb_ref[...]
        gate = jnp.dot(xb, wg_ref[...], preferred_element_type=F32)
        up = jnp.dot(xb, wu_ref[...], preferred_element_type=F32)
        h = (gate * _sigmoid(gate) * up).astype(BF16)
        o_ref[...] += jnp.dot(h, wo_ref[...], preferred_element_type=F32)

    if n_chunks % 2 == 0:
        chunk(wg0_ref, wu0_ref, wo0_ref)
        chunk(wg1_ref, wu1_ref, wo1_ref)
    else:
        @pl.when(j < last)
        def _():
            chunk(wg0_ref, wu0_ref, wo0_ref)
            chunk(wg1_ref, wu1_ref, wo1_ref)

        @pl.when(j == last)
        def _():
            chunk(wg0_ref, wu0_ref, wo0_ref)

    @pl.when(j == last)
    def _():
        o_ref[...] = _layer_norm(o_ref[...], g_ref[...], b_ref[...], eps=4.0 * LN_EPS)


def _ffn_ln(x, w_in, w_out, ln_g, ln_b, layer, slot, ln_slot, *, alpha, tm, tf):
    t, d = x.shape
    f = w_out.shape[2]
    nf = f // tf
    assert t % tm == 0 and f % tf == 0 and w_in.shape[2:] == (d, 2 * f) and nf >= 2
    steps = (nf + 1) // 2
    first = lambda j: 2 * j
    second = lambda j: jnp.where(2 * j + 1 < nf, 2 * j + 1, nf - 2)
    ln_spec = pl.BlockSpec((None, None, 1, d), lambda i, j: (layer, ln_slot, 0, 0))

    def weight_specs(chunk_of):
        return [pl.BlockSpec((None, None, d, tf), lambda i, j: (layer, slot, 0, chunk_of(j))),
                pl.BlockSpec((None, None, d, tf), lambda i, j: (layer, slot, 0, nf + chunk_of(j))),
                pl.BlockSpec((None, None, tf, d), lambda i, j: (layer, slot, chunk_of(j), 0))]

    return pl.pallas_call(
        functools.partial(_ffn_ln_kernel, alpha=alpha, n_chunks=nf),
        grid=(t // tm, steps),
        in_specs=[pl.BlockSpec((tm, d), lambda i, j: (i, 0))] + weight_specs(first) + weight_specs(second)
                 + [ln_spec, ln_spec],
        out_specs=pl.BlockSpec((tm, d), lambda i, j: (i, 0)),
        out_shape=jax.ShapeDtypeStruct((t, d), F32),
        scratch_shapes=[pltpu.VMEM((tm, d), BF16)],
        compiler_params=_params("parallel", "arbitrary"),
        name="ffn_ln",
    )(x, w_in, w_in, w_out, w_in, w_in, w_out, ln_g, ln_b)


def _in_proj_kernel(x_ref, w_ref, wgc_ref, wgr_ref, y_ref, gc_ref, gr_ref, xb_ref):
    j = pl.program_id(1)

    @pl.when(j == 0)
    def _():
        xb = x_ref[...].astype(BF16)
        xb_ref[...] = xb
        gc_ref[...] = jnp.dot(xb, wgc_ref[...], preferred_element_type=F32)
        gr_ref[...] = lax.dot_general(wgr_ref[...], xb, NT_DIMS, preferred_element_type=F32)

    y_ref[...] = jnp.dot(xb_ref[...], w_ref[...], preferred_element_type=F32).astype(y_ref.dtype)


def _in_proj(x, w_stack, layer, n_main, *, tm, tn):
    t, d = x.shape
    assert t % tm == 0 and n_main % tn == 0
    w_main = w_stack[:, :, :n_main].astype(BF16)
    w_gates = w_stack[layer, :, n_main:].astype(BF16)
    w_gate_cols = _pad_cols(w_gates, LANES)
    w_gate_rows = jnp.transpose(w_gates[:, :SUBLANES])
    return pl.pallas_call(
        _in_proj_kernel,
        grid=(t // tm, n_main // tn),
        in_specs=[
            pl.BlockSpec((tm, d), lambda i, j: (i, 0)),
            pl.BlockSpec((None, d, tn), lambda i, j: (layer, 0, j)),
            pl.BlockSpec((d, LANES), lambda i, j: (0, 0)),
            pl.BlockSpec((SUBLANES, d), lambda i, j: (0, 0)),
        ],
        out_specs=[
            pl.BlockSpec((tm, tn), lambda i, j: (i, j)),
            pl.BlockSpec((tm, LANES), lambda i, j: (i, 0)),
            pl.BlockSpec((SUBLANES, tm), lambda i, j: (0, i)),
        ],
        out_shape=[
            jax.ShapeDtypeStruct((t, n_main), BF16),
            jax.ShapeDtypeStruct((t, LANES), F32),
            jax.ShapeDtypeStruct((SUBLANES, t), F32),
        ],
        scratch_shapes=[pltpu.VMEM((tm, d), BF16)],
        compiler_params=_params("parallel", "arbitrary"),
        name="in_proj",
    )(x, w_main, w_gate_cols, w_gate_rows)


def _diff_attn_kernel(q_ref, k_ref, v_ref, tab_ref, lam_ref, g_ref, o_ref,
                      vt_ref, bdiag_ref, bnear_ref, m_ref, acc_ref,
                      s0_ref, mx0_ref, sh0_ref, s1_ref, mx1_ref, sh1_ref, *, t, lam_init):
    qi = pl.program_id(2)
    dk, dv = A_QK_DIM, A_V_DIM
    nk = k_ref.shape[0] // t
    bufs = ((s0_ref, mx0_ref, sh0_ref), (s1_ref, mx1_ref, sh1_ref))

    @pl.when(qi == 0)
    def _():
        for c in range(nk):
            vt_ref[c, 0:dv, :] = jnp.transpose(v_ref[c * t:(c + 1) * t, :].astype(F32)).astype(BF16)
            vt_ref[c, dv:, :] = jnp.ones((vt_ref.shape[1] - dv, t), BF16)
        skew = pltpu.roll(jnp.broadcast_to(tab_ref[...], (t, 2 * t)), 0, 1, stride=1, stride_axis=0)
        key = lax.broadcasted_iota(jnp.int32, (t, t), 0)
        qry = lax.broadcasted_iota(jnp.int32, (t, t), 1)
        bdiag_ref[...] = jnp.where(qry >= key, skew[:, :t], -1e30)
        bnear_ref[...] = skew[:, t:]

    q_t = jnp.transpose(q_ref[...].astype(F32)) * (dk ** -0.5)
    feat = lax.broadcasted_iota(jnp.int32, q_t.shape, 0)
    qs_t = jnp.concatenate([jnp.where(feat < dk, q_t, 0.0), jnp.where(feat >= dk, q_t, 0.0)],
                           axis=1).astype(BF16)

    m_ref[...] = jnp.full_like(m_ref, -jnp.inf)
    acc_ref[...] = jnp.zeros_like(acc_ref)

    def score(kj, buf, bias_ref, shift):
        s_ref, mx_ref, sh_ref = bufs[buf]
        off = pl.multiple_of(kj * t, t)
        s = jnp.dot(k_ref[pl.ds(off, t), :], qs_t, preferred_element_type=F32)
        if bias_ref is not None:
            b = bias_ref[...]
            s = s + jnp.concatenate([b, b], axis=1)
        s_ref[...] = s
        mx_ref[...] = jnp.max(s, 0, keepdims=True) + shift
        sh_ref[...] = jnp.zeros_like(sh_ref) + shift

    def absorb(kj, buf):
        s_ref, mx_ref, sh_ref = bufs[buf]
        m_old = m_ref[...]
        m_new = jnp.maximum(m_old, mx_ref[...])
        p = jnp.exp(s_ref[...] - (m_new - sh_ref[...])).astype(BF16)
        acc_ref[...] = (jnp.exp(m_old - m_new) * acc_ref[...]
                        + jnp.dot(vt_ref[kj], p, preferred_element_type=F32))
        m_ref[...] = m_new

    c_far = tab_ref[:, 2 * t - 1:2 * t]
    score(qi, 0, bdiag_ref, 0.0)

    @pl.when(qi >= 1)
    def _():
        score(qi - 1, 1, bnear_ref, 0.0)
        absorb(qi, 0)

    def far_pair(p, carry):
        kj = qi - 2 - 2 * p
        score(kj, 0, None, c_far)
        absorb(kj + 1, 1)
        score(kj - 1, 1, None, c_far)
        absorb(kj, 0)
        return carry

    lax.fori_loop(0, jnp.maximum(qi - 1, 0) // 2, far_pair, 0)

    @pl.when((qi >= 2) & (qi % 2 == 0))
    def _():
        score(0, 0, None, c_far)
        absorb(1, 1)

    @pl.when(qi % 2 == 0)
    def _():
        absorb(0, 0)

    @pl.when(qi % 2 == 1)
    def _():
        absorb(0, 1)

    lv = lam_ref[...]
    lam = (jnp.exp(jnp.sum(lv[0:1] * lv[1:2], -1, keepdims=True))
           - jnp.exp(jnp.sum(lv[2:3] * lv[3:4], -1, keepdims=True)) + lam_init)
    acc = acc_ref[...]
    o_t = acc[0:dv] * (1.0 / acc[dv:dv + 1])
    out = jnp.transpose(o_t[:, :t] - lam * o_t[:, t:])
    hn = out * lax.rsqrt(jnp.mean(out * out, -1, keepdims=True) + NORM_EPS)
    o_ref[...] = (hn * g_ref[...] * (1.0 - lam_init)).astype(o_ref.dtype)


def _t5_bias_by_distance(rel_bias, n):
    r = jnp.arange(n, dtype=jnp.int32)
    max_exact = N_BUCKETS // 2
    rf = jnp.maximum(r, 1).astype(F32)
    large = max_exact + (jnp.log(rf / max_exact) / math.log(MAX_DISTANCE / max_exact)
                         * (N_BUCKETS - max_exact)).astype(jnp.int32)
    large = jnp.minimum(large, N_BUCKETS - 1)
    bucket = jnp.where(r < max_exact, r, large)
    return jnp.transpose(rel_bias[bucket]).astype(F32)


def _diff_attention(y3, rel_bias, lam_vecs, diff_g, *, lam_init, t):
    bsz, s, _ = y3.shape
    hd = 2 * A_QK_DIM
    dv = A_V_DIM
    assert hd == dv == LANES and s % t == 0 and t >= MAX_DISTANCE
    nq = s // t
    table = _t5_bias_by_distance(rel_bias, 2 * t).reshape(A_HEADS, 1, 2 * t)
    return pl.pallas_call(
        functools.partial(_diff_attn_kernel, t=t, lam_init=lam_init),
        grid=(bsz, A_HEADS, nq),
        in_specs=[
            pl.BlockSpec((None, t, hd), lambda b, h, q: (b, q, h)),
            pl.BlockSpec((None, s, hd), lambda b, h, q: (b, 0, A_HEADS + h)),
            pl.BlockSpec((None, s, dv), lambda b, h, q: (b, 0, 2 * A_HEADS + h)),
            pl.BlockSpec((None, 1, 2 * t), lambda b, h, q: (h, 0, 0)),
            pl.BlockSpec((4, A_QK_DIM), lambda b, h, q: (0, 0)),
            pl.BlockSpec((1, dv), lambda b, h, q: (0, h)),
        ],
        out_specs=pl.BlockSpec((None, t, dv), lambda b, h, q: (b, q, h)),
        out_shape=jax.ShapeDtypeStruct((bsz, s, A_HEADS * dv), BF16),
        scratch_shapes=[pltpu.VMEM((nq, dv + ATTN_ONES_ROWS, t), BF16),
                        pltpu.VMEM((t, t), F32),
                        pltpu.VMEM((t, t), F32),
                        pltpu.VMEM((1, 2 * t), F32),
                        pltpu.VMEM((dv + ATTN_ONES_ROWS, 2 * t), F32)]
                       + 2 * [pltpu.VMEM((t, 2 * t), F32), pltpu.VMEM((1, 2 * t), F32),
                              pltpu.VMEM((1, 2 * t), F32)],
        compiler_params=_params("parallel", "parallel", "arbitrary"),
        name="diff_attn",
    )(y3, y3, y3, table, lam_vecs, diff_g)


def _mlstm_kernel(qk_ref, v_ref, ob_ref, gc_ref, gr_ref, cw_ref, cb_ref, bc_ref, br_ref, g_ref, y_ref,
                  xp_ref, *state_refs, chunk):
    heads, dk, dv = B_HEADS, B_QK_DIM, B_V_DIM
    c_refs, n_refs, m_refs = state_refs[0::3], state_refs[1::3], state_refs[2::3]
    pad = xp_ref.shape[0] - chunk
    c = pl.program_id(1)

    @pl.when(c == 0)
    def _():
        xp_ref[0:pad, :] = jnp.zeros((pad, xp_ref.shape[1]), xp_ref.dtype)
        for ref in state_refs:
            ref[...] = jnp.zeros_like(ref)

    x = qk_ref[...]
    xp_ref[pad:pad + chunk, :] = x
    xp = xp_ref[...]
    cw = cw_ref[...]
    dst = lax.broadcasted_iota(jnp.int32, (chunk, pad + chunk), 0)
    src = lax.broadcasted_iota(jnp.int32, (chunk, pad + chunk), 1)
    conv = cb_ref[...] + cw[CONV_WIDTH - 1:CONV_WIDTH, :] * x.astype(F32)
    for j in range(CONV_WIDTH - 1):
        shift = (src == dst + (pad - (CONV_WIDTH - 1) + j)).astype(BF16)
        conv = conv + cw[j:j + 1, :] * jnp.dot(shift, xp, preferred_element_type=F32)
    xp_ref[0:pad, :] = x[chunk - pad:chunk, :]
    qk = conv * _sigmoid(conv)

    gc = gc_ref[...] + bc_ref[...]
    gr = gr_ref[...] + br_ref[...]
    row = lax.broadcasted_iota(jnp.int32, (chunk, chunk), 0)
    col = lax.broadcasted_iota(jnp.int32, (chunk, chunk), 1)
    causal = row >= col
    tril = causal.astype(F32)
    triu = (row <= col).astype(F32)
    b_c = jnp.dot(tril, _log_sigmoid(gc), precision=HIGHEST, preferred_element_type=F32)
    b_r = jnp.dot(_log_sigmoid(gr), triu, precision=HIGHEST, preferred_element_type=F32)

    ob = ob_ref[...].astype(F32)
    g = g_ref[...]
    hs = range(heads)
    q = [qk[:, h * dk:(h + 1) * dk] * (dk ** -0.5) for h in hs]
    k = [qk[:, (heads + h) * dk:(heads + h + 1) * dk] for h in hs]
    v = [v_ref[:, h * dv:(h + 1) * dv] for h in hs]
    b_col = [b_c[:, heads + h:heads + h + 1] for h in hs]
    i_col = [gc[:, h:h + 1] for h in hs]
    b_row = [b_r[heads + h:heads + h + 1, :] for h in hs]
    i_row = [gr[h:h + 1, :] for h in hs]
    m_prev = [m_refs[h][0:1, 0:1] for h in hs]
    c_mat = [c_refs[h][...] for h in hs]
    n_vec = [n_refs[h][...] for h in hs]

    b_last = [b_row[h][:, chunk - 1:chunk] for h in hs]
    g_col = [b_last[h] - b_col[h] + i_col[h] for h in hs]
    m_new = [jnp.maximum(b_last[h] + m_prev[h], jnp.max(g_col[h], 0, keepdims=True)) for h in hs]
    decay = [jnp.exp(b_last[h] + m_prev[h] - m_new[h]) for h in hs]
    kw = [k[h] * jnp.exp(g_col[h] - m_new[h]) for h in hs]
    qb = [q[h].astype(BF16) for h in hs]
    qk_t = [lax.dot_general(qb[h], k[h].astype(BF16), NT_DIMS, preferred_element_type=F32) for h in hs]
    q_c = [jnp.dot(qb[h], c_mat[h].astype(BF16), preferred_element_type=F32) for h in hs]
    kw_v = [lax.dot_general(kw[h].astype(BF16), v[h], TN_DIMS, preferred_element_type=F32) for h in hs]

    d = [jnp.where(causal, b_col[h] - b_row[h] + i_row[h], -jnp.inf) for h in hs]
    a_col = [b_col[h] + m_prev[h] for h in hs]
    m_t = [jnp.maximum(a_col[h], jnp.max(d[h], -1, keepdims=True)) for h in hs]
    w_inter = [jnp.exp(a_col[h] - m_t[h]) for h in hs]
    sw = [qk_t[h] * jnp.exp(d[h] - m_t[h]) for h in hs]
    sw_v = [jnp.dot(sw[h].astype(BF16), v[h], preferred_element_type=F32) for h in hs]

    for h in hs:
        c_refs[h][...] = decay[h] * c_mat[h] + kw_v[h]
        n_refs[h][...] = decay[h] * n_vec[h] + jnp.sum(kw[h], 0, keepdims=True)
        m_refs[h][...] = jnp.broadcast_to(m_new[h], m_refs[h].shape)

    for h in hs:
        num = w_inter[h] * q_c[h] + sw_v[h]
        den = (w_inter[h] * jnp.sum(q[h] * n_vec[h], -1, keepdims=True)
               + jnp.sum(sw[h], -1, keepdims=True))
        hh = num / jnp.maximum(jnp.abs(den), jnp.exp(-m_t[h]))
        hn = hh * lax.rsqrt(jnp.mean(hh * hh, -1, keepdims=True) + NORM_EPS)
        sl = slice(h * dv, (h + 1) * dv)
        y_ref[:, sl] = (hn * g[:, sl] * _sigmoid(ob[:, sl])).astype(y_ref.dtype)


def _mlstm(y3, gcol3, grow3, conv_w, conv_b, gate_bias_cols, gate_bias_rows, mlstm_g, *, chunk):
    bsz, s, _ = y3.shape
    heads, dk, dv = B_HEADS, B_QK_DIM, B_V_DIM
    w = heads * dv
    assert 2 * heads * dk == w and s % chunk == 0 and chunk % LANES == 0
    nc = s // chunk
    return pl.pallas_call(
        functools.partial(_mlstm_kernel, chunk=chunk),
        grid=(bsz, nc),
        in_specs=[
            pl.BlockSpec((None, chunk, w), lambda b, c: (b, c, 3)),
            pl.BlockSpec((None, chunk, w), lambda b, c: (b, c, 4)),
            pl.BlockSpec((None, chunk, w), lambda b, c: (b, c, 5)),
            pl.BlockSpec((None, chunk, LANES), lambda b, c: (b, c, 0)),
            pl.BlockSpec((SUBLANES, chunk), lambda b, c: (0, b * nc + c)),
            pl.BlockSpec((CONV_WIDTH, w), lambda b, c: (0, 0)),
            pl.BlockSpec((1, w), lambda b, c: (0, 0)),
            pl.BlockSpec((1, LANES), lambda b, c: (0, 0)),
            pl.BlockSpec((SUBLANES, 1), lambda b, c: (0, 0)),
            pl.BlockSpec((1, w), lambda b, c: (0, 0)),
        ],
        out_specs=pl.BlockSpec((None, chunk, w), lambda b, c: (b, c, 0)),
        out_shape=jax.ShapeDtypeStruct((bsz, s, w), BF16),
        scratch_shapes=[
            pltpu.VMEM((BF16_SUBLANES + chunk, w), BF16),
        ] + heads * [pltpu.VMEM((dk, dv), F32), pltpu.VMEM((1, dk), F32), pltpu.VMEM((SUBLANES, LANES), F32)],
        compiler_params=_params("parallel", "arbitrary"),
        name="mlstm",
    )(y3, y3, y3, gcol3, grow3, conv_w, conv_b, gate_bias_cols, gate_bias_rows, mlstm_g)


def _gla_kernel(q_ref, k_ref, v_ref, r_ref, a1_ref, a1n_ref, wah_ref, wal_ref, ba_ref, g_ref, y_ref,
                b_ref, *st_refs, chunk):
    heads, dk, dv = C_HEADS, C_QK_DIM, C_V_DIM
    sub = GLA_SUB
    assert chunk == 2 * sub
    c = pl.program_id(1)
    row = lax.broadcasted_iota(jnp.int32, (chunk, chunk), 0)
    col = lax.broadcasted_iota(jnp.int32, (chunk, chunk), 1)
    causal = (row >= col) & ((row < sub) == (col < sub))
    first = lax.broadcasted_iota(jnp.int32, (chunk, 1), 0) < sub

    def cum_log_decay(a1):
        a_hi = a1.astype(BF16)
        a_lo = (a1 - a_hi.astype(F32)).astype(BF16)
        w_hi = wah_ref[...]
        z = (jnp.dot(a_hi, w_hi, preferred_element_type=F32)
             + (jnp.dot(a_lo, w_hi, preferred_element_type=F32)
                + jnp.dot(a_hi, wal_ref[...], preferred_element_type=F32))) + ba_ref[...]
        log_a = _log_sigmoid(z) * (1.0 / C_GATE_TEMP)
        return jnp.dot(causal.astype(F32), log_a, precision=HIGHEST, preferred_element_type=F32)

    @pl.when(c == 0)
    def _():
        for ref in st_refs:
            ref[...] = jnp.zeros_like(ref)
        b_ref[...] = cum_log_decay(a1_ref[...])

    b_all = b_ref[...]
    b_next = cum_log_decay(a1n_ref[...])

    r = r_ref[...].astype(F32)
    g = g_ref[...]
    for h in range(heads):
        ks = slice(h * dk, (h + 1) * dk)
        vs = slice(h * dv, (h + 1) * dv)
        b = b_all[:, ks]
        q = q_ref[:, ks].astype(F32) * (dk ** -0.5)
        k = k_ref[:, ks].astype(F32)
        v = v_ref[:, vs]
        tot0 = b[sub - 1:sub, :]
        tot1 = b[chunk - 1:chunk, :]
        q_dec = q * jnp.exp(b)
        k_dec = (k * jnp.exp(-b)).astype(BF16)
        k_end = k * jnp.exp(jnp.where(first, tot0, tot1) - b)
        k_new = (k_end * jnp.where(first, jnp.exp(tot1), 1.0)).astype(BF16)
        q_int = (q_dec * jnp.where(first, 1.0, jnp.exp(tot0))).astype(BF16)
        q_dec = q_dec.astype(BF16)
        st = st_refs[h][...]
        qk_t = lax.dot_general(q_dec, k_dec, NT_DIMS, preferred_element_type=F32)
        inter = lax.dot_general(q_int, st.astype(BF16), NT_DIMS, preferred_element_type=F32)
        st_refs[h][...] = (jnp.exp(tot0 + tot1) * st
                           + lax.dot_general(v, k_new, TN_DIMS, preferred_element_type=F32))
        cross = lax.dot_general(q_dec[sub:], k_end[:sub].astype(BF16), NT_DIMS, preferred_element_type=F32)
        att = jnp.where(causal, qk_t, 0.0)
        o = jnp.dot(att.astype(BF16), v, preferred_element_type=F32) + inter
        o_cross = jnp.dot(cross.astype(BF16), v[:sub], preferred_element_type=F32)
        o = jnp.concatenate([o[:sub], o[sub:] + o_cross], axis=0)

        on = o * lax.rsqrt(jnp.mean(o * o, -1, keepdims=True) + NORM_EPS)
        rr = r[:, vs]
        y_ref[:, vs] = (on * g[:, vs] * (rr * _sigmoid(rr))).astype(y_ref.dtype)

    b_ref[...] = b_next


def _gla(y3, a13, w_a2_hi, w_a2_lo, b_a, gla_g, *, chunk):
    bsz, s, _ = y3.shape
    heads, dk, dv = C_HEADS, C_QK_DIM, C_V_DIM
    wk, wv = heads * dk, heads * dv
    assert wv == 2 * wk and s % chunk == 0
    nc = s // chunk
    return pl.pallas_call(
        functools.partial(_gla_kernel, chunk=chunk),
        grid=(bsz, nc),
        in_specs=[
            pl.BlockSpec((None, chunk, wk), lambda b, c: (b, c, 0)),
            pl.BlockSpec((None, chunk, wk), lambda b, c: (b, c, 1)),
            pl.BlockSpec((None, chunk, wv), lambda b, c: (b, c, 1)),
            pl.BlockSpec((None, chunk, wv), lambda b, c: (b, c, 2)),
            pl.BlockSpec((None, chunk, LANES), lambda b, c: (b, c, 0)),
            pl.BlockSpec((None, chunk, LANES), lambda b, c: (b, jnp.minimum(c + 1, nc - 1), 0)),
            pl.BlockSpec((LANES, wk), lambda b, c: (0, 0)),
            pl.BlockSpec((LANES, wk), lambda b, c: (0, 0)),
            pl.BlockSpec((1, wk), lambda b, c: (0, 0)),
            pl.BlockSpec((1, wv), lambda b, c: (0, 0)),
        ],
        out_specs=pl.BlockSpec((None, chunk, wv), lambda b, c: (b, c, 0)),
        out_shape=jax.ShapeDtypeStruct((bsz, s, wv), BF16),
        scratch_shapes=[pltpu.VMEM((chunk, wk), F32)] + heads * [pltpu.VMEM((dv, dk), F32)],
        compiler_params=_params("parallel", "arbitrary"),
        name="gla",
    )(y3, y3, y3, y3, a13, a13, w_a2_hi, w_a2_lo, b_a, gla_g)


def _out_ln_kernel(*refs, n_in, alpha):
    y_refs, w_refs = refs[:n_in], refs[n_in:2 * n_in]
    x_ref, g_ref, b_ref, o_ref = refs[2 * n_in:]
    rb = o_ref.shape[0] // ROW_SPLIT
    for r in range(ROW_SPLIT):
        rows = slice(r * rb, (r + 1) * rb)
        acc = jnp.dot(y_refs[0][rows, :], w_refs[0][...], preferred_element_type=F32)
        for y_ref, w_ref in zip(y_refs[1:], w_refs[1:]):
            acc = acc + jnp.dot(y_ref[rows, :], w_ref[...], preferred_element_type=F32)
        o_ref[rows, :] = _layer_norm(alpha * x_ref[rows, :] + acc, g_ref[...], b_ref[...])


def _out_ln(ys, w, x, ln_g, ln_b, layer, w_layer, ln_slot, *, alpha, tm):
    t, d = x.shape
    n_in = len(ys)
    wd = ys[0].shape[1]
    assert all(y.shape[1] == wd for y in ys) and w.shape[1:] == (n_in * wd, d)
    ln_spec = pl.BlockSpec((None, None, 1, d), lambda i: (layer, ln_slot, 0, 0))
    in_specs = ([pl.BlockSpec((tm, wd), lambda i: (i, 0)) for _ in ys]
                + [pl.BlockSpec((None, wd, d), lambda i, k=k: (w_layer, k, 0)) for k in range(n_in)]
                + [pl.BlockSpec((tm, d), lambda i: (i, 0)), ln_spec, ln_spec])
    return pl.pallas_call(
        functools.partial(_out_ln_kernel, n_in=n_in, alpha=alpha),
        grid=(t // tm,),
        in_specs=in_specs,
        out_specs=pl.BlockSpec((tm, d), lambda i: (i, 0)),
        out_shape=jax.ShapeDtypeStruct((t, d), F32),
        compiler_params=_params("parallel"),
        name="out_ln",
    )(*ys, *([w] * n_in), x, ln_g, ln_b)


def _ple_kernel(x_ref, p_ref, wp_ref, wg_ref, o_ref):
    rb = o_ref.shape[0] // ROW_SPLIT
    for r in range(ROW_SPLIT):
        rows = slice(r * rb, (r + 1) * rb)
        x = x_ref[rows, :]
        e = jnp.dot(p_ref[rows, :].astype(BF16), wp_ref[...], preferred_element_type=F32)
        gate = jnp.dot(x.astype(BF16), wg_ref[...], preferred_element_type=F32)
        o_ref[rows, :] = x + e * _sigmoid(gate)


def _ple(x, p, w_proj, w_gate, layer, *, tm):
    t, d = x.shape
    s, pd = p.shape[2:]
    assert s % tm == 0
    per_seq = s // tm
    return pl.pallas_call(
        _ple_kernel,
        grid=(t // tm,),
        in_specs=[
            pl.BlockSpec((tm, d), lambda i: (i, 0)),
            pl.BlockSpec((None, None, tm, pd), lambda i: (layer, i // per_seq, i % per_seq, 0)),
            pl.BlockSpec((None, pd, d), lambda i: (layer, 0, 0)),
            pl.BlockSpec((None, d, d), lambda i: (layer, 0, 0)),
        ],
        out_specs=pl.BlockSpec((tm, d), lambda i: (i, 0)),
        out_shape=jax.ShapeDtypeStruct((t, d), F32),
        compiler_params=_params("parallel"),
        name="ple",
    )(x, p, w_proj, w_gate)


def _tiles(t, s):
    return dict(
        ffn_tm=min(1024, t), ffn_tf=256,
        proj_tm=min(1024, t), proj_tn=1024,
        row_tm=min(512, t),
        attn_t=min(512, s),
        mlstm_chunk=min(128, s),
        gla_chunk=min(2 * GLA_SUB, s),
    )


def _pad_cols(w, n):
    return jnp.pad(w, ((0, 0), (0, n - w.shape[1])))


def kernel(x, p, ln_g, ln_b, w_ffn_in, w_ffn_out, w_in_ab, w_out_ab, rel_bias, lambda_q1, lambda_k1,
           lambda_q2, lambda_k2, diff_norm, conv_w, conv_b, b_igate, b_fgate, mlstm_norm, w_in_c,
           w_alpha2, b_alpha, gla_norm, w_out_c, w_ple_proj, w_ple_gate):
    bsz, s, d = x.shape
    depth = p.shape[0]
    t = bsz * s
    tl = _tiles(t, s)
    alpha = (2 * depth) ** 0.25
    a_w = A_HEADS * A_V_DIM
    ab_main = 3 * a_w + 3 * B_HEADS * B_V_DIM
    c_main = 2 * C_HEADS * C_QK_DIM + 2 * C_HEADS * C_V_DIM
    row = lambda a: a.reshape(1, -1)

    wfi, wfo = w_ffn_in.astype(BF16), w_ffn_out.astype(BF16)
    wo_ab, wo_c = w_out_ab.astype(BF16), w_out_c.astype(BF16)
    wpp, wpg = w_ple_proj.astype(BF16), w_ple_gate.astype(BF16)
    g4, b4 = ln_g.reshape(depth, 3, 1, d), ln_b.reshape(depth, 3, 1, d)

    xf = x.reshape(t, d)
    for i in range(depth):
        xf = _ffn_ln(xf, wfi, wfo, g4, b4, i, 0, 0, alpha=alpha, tm=tl["ffn_tm"], tf=tl["ffn_tf"])
        if i % 2 == 0:
            e = i // 2
            lam_init = 0.8 - 0.6 * math.exp(-0.3 * i)
            y, gcol, grow = _in_proj(xf, w_in_ab, e, ab_main, tm=tl["proj_tm"], tn=tl["proj_tn"])
            y3 = y.reshape(bsz, s, ab_main)
            lam_vecs = jnp.stack([lambda_q1[e], lambda_k1[e], lambda_q2[e], lambda_k2[e]])
            ya = _diff_attention(y3, rel_bias, lam_vecs, row(diff_norm[e]), lam_init=lam_init, t=tl["attn_t"])
            gate_bias = jnp.concatenate([b_igate[e], b_fgate[e]])
            yb = _mlstm(y3, gcol.reshape(bsz, s, LANES), grow,
                        conv_w[e], row(conv_b[e]), _pad_cols(row(gate_bias), LANES), gate_bias.reshape(-1, 1),
                        row(mlstm_norm[e]), chunk=tl["mlstm_chunk"])
            ys, w_out, w_layer = [ya.reshape(t, a_w), yb.reshape(t, -1)], wo_ab, e
        else:
            o = i // 2
            y, a1, _ = _in_proj(xf, w_in_c, o, c_main, tm=tl["proj_tm"], tn=tl["proj_tn"])
            w_a2 = jnp.pad(w_alpha2[o], ((0, LANES - w_alpha2.shape[1]), (0, 0)))
            w_a2_hi = lax.reduce_precision(w_a2, exponent_bits=8, mantissa_bits=7)
            w_a2_lo = (w_a2 - w_a2_hi).astype(BF16)
            yc = _gla(y.reshape(bsz, s, c_main), a1.reshape(bsz, s, LANES), w_a2_hi.astype(BF16), w_a2_lo,
                      row(b_alpha[o]),
                      row(gla_norm[o]), chunk=tl["gla_chunk"])
            ys, w_out, w_layer = [yc.reshape(t, -1)], wo_c, o
        xf = _out_ln(ys, w_out, xf, g4, b4, i, w_layer, 1, alpha=alpha, tm=tl["row_tm"])
        xf = _ffn_ln(xf, wfi, wfo, g4, b4, i, 1, 2, alpha=alpha, tm=tl["ffn_tm"], tf=tl["ffn_tf"])
        xf = _ple(xf, p, wpp, wpg, i, tm=tl["row_tm"])
    return xf.reshape(bsz, s, d)
```

```python
import functools
import math

import jax
import jax.numpy as jnp
from jax import lax
from jax.experimental import pallas as pl
from jax.experimental.pallas import tpu as pltpu

F32 = jnp.float32
BF16 = jnp.bfloat16
HIGHEST = lax.Precision.HIGHEST

A_HEADS = 8
A_QK_DIM = 64
A_V_DIM = 128
B_HEADS = 4
B_QK_DIM = 128
B_V_DIM = 256
CONV_WIDTH = 4
C_HEADS = 4
C_QK_DIM = 256
C_V_DIM = 512
C_GATE_TEMP = 16.0
GLA_SUB = 64
N_BUCKETS = 32
MAX_DISTANCE = 128
LN_EPS = 1e-5
NORM_EPS = 1e-6

V7X_VMEM_LIMIT_BYTES = 56 * 1024 * 1024
LANES = 128
SUBLANES = 8
BF16_SUBLANES = 16
ATTN_ONES_ROWS = BF16_SUBLANES
MASKED_LOGIT = -1e30
ROW_SPLIT = 4

NT_DIMS = (((1,), (1,)), ((), ()))
TN_DIMS = (((0,), (0,)), ((), ()))


def _params(*sem):
    return pltpu.CompilerParams(dimension_semantics=sem, vmem_limit_bytes=V7X_VMEM_LIMIT_BYTES)


def _layer_norm(y, g, b, eps=LN_EPS):
    mu = jnp.mean(y, -1, keepdims=True)
    yc = y - mu
    var = jnp.mean(yc * yc, -1, keepdims=True)
    return yc * lax.rsqrt(var + eps) * g + b


def _sigmoid(x):
    return 1.0 / (1.0 + jnp.exp(-x))


def _log_sigmoid(x):
    return jnp.minimum(x, 0.0) - jnp.log(1.0 + jnp.exp(-jnp.abs(x)))


def _ffn_ln_kernel(x_ref, wg0_ref, wu0_ref, wo0_ref, wg1_ref, wu1_ref, wo1_ref, g_ref, b_ref, o_ref, xb_ref,
                   *, alpha, n_chunks):
    j = pl.program_id(1)
    last = pl.num_programs(1) - 1

    @pl.when(j == 0)
    def _():
        x = x_ref[...]
        xb_ref[...] = x.astype(BF16)
        o_ref[...] = (2.0 * alpha) * x

    def chunk(wg_ref, wu_ref, wo_ref):
        xb = xb_ref[...]
        gate = jnp.dot(xb, wg_ref[...], preferred_element_type=F32)
        up = jnp.dot(xb, wu_ref[...], preferred_element_type=F32)
        h = (gate * _sigmoid(gate) * up).astype(BF16)
        o_ref[...] += jnp.dot(h, wo_ref[...], preferred_element_type=F32)

    if n_chunks % 2 == 0:
        chunk(wg0_ref, wu0_ref, wo0_ref)
        chunk(wg1_ref, wu1_ref, wo1_ref)
    else:
        @pl.when(j < last)
        def _():
            chunk(wg0_ref, wu0_ref, wo0_ref)
            chunk(wg1_ref, wu1_ref, wo1_ref)

        @pl.when(j == last)
        def _():
            chunk(wg0_ref, wu0_ref, wo0_ref)

    @pl.when(j == last)
    def _():
        o_ref[...] = _layer_norm(o_ref[...], g_ref[...], b_ref[...], eps=4.0 * LN_EPS)


def _ffn_ln(x, w_in, w_out, ln_g, ln_b, layer, slot, ln_slot, *, alpha, tm, tf):
    t, d = x.shape
    f = w_out.shape[2]
    nf = f // tf
    assert t % tm == 0 and f % tf == 0 and w_in.shape[2:] == (d, 2 * f) and nf >= 2
    steps = (nf + 1) // 2
    first = lambda j: 2 * j
    second = lambda j: jnp.where(2 * j + 1 < nf, 2 * j + 1, nf - 2)
    ln_spec = pl.BlockSpec((None, None, 1, d), lambda i, j: (layer, ln_slot, 0, 0))

    def weight_specs(chunk_of):
        return [pl.BlockSpec((None, None, d, tf), lambda i, j: (layer, slot, 0, chunk_of(j))),
                pl.BlockSpec((None, None, d, tf), lambda i, j: (layer, slot, 0, nf + chunk_of(j))),
                pl.BlockSpec((None, None, tf, d), lambda i, j: (layer, slot, chunk_of(j), 0))]

    return pl.pallas_call(
        functools.partial(_ffn_ln_kernel, alpha=alpha, n_chunks=nf),
        grid=(t // tm, steps),
        in_specs=[pl.BlockSpec((tm, d), lambda i, j: (i, 0))] + weight_specs(first) + weight_specs(second)
                 + [ln_spec, ln_spec],
        out_specs=pl.BlockSpec((tm, d), lambda i, j: (i, 0)),
        out_shape=jax.ShapeDtypeStruct((t, d), F32),
        scratch_shapes=[pltpu.VMEM((tm, d), BF16)],
        compiler_params=_params("parallel", "arbitrary"),
        name="ffn_ln",
    )(x, w_in, w_in, w_out, w_in, w_in, w_out, ln_g, ln_b)


def _in_proj_kernel(x_ref, w_ref, wgc_ref, wgr_ref, y_ref, gc_ref, gr_ref, xb_ref):
    j = pl.program_id(1)

    @pl.when(j == 0)
    def _():
        xb = x_ref[...].astype(BF16)
        xb_ref[...] = xb
        gc_ref[...] = jnp.dot(xb, wgc_ref[...], preferred_element_type=F32)
        gr_ref[...] = lax.dot_general(wgr_ref[...], xb, NT_DIMS, preferred_element_type=F32)

    y_ref[...] = jnp.dot(xb_ref[...], w_ref[...], preferred_element_type=F32).astype(y_ref.dtype)


def _in_proj(x, w_stack, layer, n_main, *, tm, tn):
    t, d = x.shape
    assert t % tm == 0 and n_main % tn == 0
    w_main = w_stack[:, :, :n_main].astype(BF16)
    w_gates = w_stack[layer, :, n_main:].astype(BF16)
    w_gate_cols = _pad_cols(w_gates, LANES)
    w_gate_rows = jnp.transpose(w_gates[:, :SUBLANES])
    return pl.pallas_call(
        _in_proj_kernel,
        grid=(t // tm, n_main // tn),
        in_specs=[
            pl.BlockSpec((tm, d), lambda i, j: (i, 0)),
            pl.BlockSpec((None, d, tn), lambda i, j: (layer, 0, j)),
            pl.BlockSpec((d, LANES), lambda i, j: (0, 0)),
            pl.BlockSpec((SUBLANES, d), lambda i, j: (0, 0)),
        ],
        out_specs=[
            pl.BlockSpec((tm, tn), lambda i, j: (i, j)),
            pl.BlockSpec((tm, LANES), lambda i, j: (i, 0)),
            pl.BlockSpec((SUBLANES, tm), lambda i, j: (0, i)),
        ],
        out_shape=[
            jax.ShapeDtypeStruct((t, n_main), BF16),
            jax.ShapeDtypeStruct((t, LANES), F32),
            jax.ShapeDtypeStruct((SUBLANES, t), F32),
        ],
        scratch_shapes=[pltpu.VMEM((tm, d), BF16)],
        compiler_params=_params("parallel", "arbitrary"),
        name="in_proj",
    )(x, w_main, w_gate_cols, w_gate_rows)


def _diff_attn_kernel(q_ref, k_ref, v_ref, tab_ref, lam_ref, g_ref, o_ref,
                      vt_ref, bdiag_ref, bnear_ref, m_ref, acc_ref,
                      s0_ref, mx0_ref, sh0_ref, s1_ref, mx1_ref, sh1_ref, *, t, lam_init):
    qi = pl.program_id(2)
    dk, dv = A_QK_DIM, A_V_DIM
    nk = k_ref.shape[0] // t
    bufs = ((s0_ref, mx0_ref, sh0_ref), (s1_ref, mx1_ref, sh1_ref))

    @pl.when(qi == 0)
    def _():
        for c in range(nk):
            vt_ref[c, 0:dv, :] = jnp.transpose(v_ref[c * t:(c + 1) * t, :].astype(F32)).astype(BF16)
            vt_ref[c, dv:, :] = jnp.ones((vt_ref.shape[1] - dv, t), BF16)
        skew = pltpu.roll(jnp.broadcast_to(tab_ref[...], (t, 2 * t)), 0, 1, stride=1, stride_axis=0)
        key = lax.broadcasted_iota(jnp.int32, (t, t), 0)
        qry = lax.broadcasted_iota(jnp.int32, (t, t), 1)
        bdiag_ref[...] = jnp.where(qry >= key, skew[:, :t], MASKED_LOGIT)
        bnear_ref[...] = skew[:, t:]

    q_t = jnp.transpose(q_ref[...].astype(F32)) * (dk ** -0.5)
    feat = lax.broadcasted_iota(jnp.int32, q_t.shape, 0)
    qs_t = jnp.concatenate([jnp.where(feat < dk, q_t, 0.0), jnp.where(feat >= dk, q_t, 0.0)],
                           axis=1).astype(BF16)

    m_ref[...] = jnp.full_like(m_ref, -jnp.inf)
    acc_ref[...] = jnp.zeros_like(acc_ref)

    def score(kj, buf, bias_ref, shift):
        s_ref, mx_ref, sh_ref = bufs[buf]
        off = pl.multiple_of(kj * t, t)
        s = jnp.dot(k_ref[pl.ds(off, t), :], qs_t, preferred_element_type=F32)
        if bias_ref is not None:
            b = bias_ref[...]
            s = s + jnp.concatenate([b, b], axis=1)
        s_ref[...] = s
        mx_ref[...] = jnp.max(s, 0, keepdims=True) + shift
        sh_ref[...] = jnp.zeros_like(sh_ref) + shift

    def absorb(kj, buf):
        s_ref, mx_ref, sh_ref = bufs[buf]
        m_old = m_ref[...]
        m_new = jnp.maximum(m_old, mx_ref[...])
        p = jnp.exp(s_ref[...] - (m_new - sh_ref[...])).astype(BF16)
        acc_ref[...] = (jnp.exp(m_old - m_new) * acc_ref[...]
                        + jnp.dot(vt_ref[kj], p, preferred_element_type=F32))
        m_ref[...] = m_new

    c_far = tab_ref[:, 2 * t - 1:2 * t]
    score(qi, 0, bdiag_ref, 0.0)

    @pl.when(qi >= 1)
    def _():
        score(qi - 1, 1, bnear_ref, 0.0)
        absorb(qi, 0)

    def far_pair(p, carry):
        kj = qi - 2 - 2 * p
        score(kj, 0, None, c_far)
        absorb(kj + 1, 1)
        score(kj - 1, 1, None, c_far)
        absorb(kj, 0)
        return carry

    lax.fori_loop(0, jnp.maximum(qi - 1, 0) // 2, far_pair, 0)

    @pl.when((qi >= 2) & (qi % 2 == 0))
    def _():
        score(0, 0, None, c_far)
        absorb(1, 1)

    @pl.when(qi % 2 == 0)
    def _():
        absorb(0, 0)

    @pl.when(qi % 2 == 1)
    def _():
        absorb(0, 1)

    lv = lam_ref[...]
    lam = (jnp.exp(jnp.sum(lv[0:1] * lv[1:2], -1, keepdims=True))
           - jnp.exp(jnp.sum(lv[2:3] * lv[3:4], -1, keepdims=True)) + lam_init)
    acc = acc_ref[...]
    o_t = acc[0:dv] * (1.0 / acc[dv:dv + 1])
    out = jnp.transpose(o_t[:, :t] - lam * o_t[:, t:])
    hn = out * lax.rsqrt(jnp.mean(out * out, -1, keepdims=True) + NORM_EPS)
    o_ref[...] = (hn * g_ref[...] * (1.0 - lam_init)).astype(o_ref.dtype)


def _t5_bias_by_distance(rel_bias, n):
    r = jnp.arange(n, dtype=jnp.int32)
    max_exact = N_BUCKETS // 2
    rf = jnp.maximum(r, 1).astype(F32)
    large = max_exact + (jnp.log(rf / max_exact) / math.log(MAX_DISTANCE / max_exact)
                         * (N_BUCKETS - max_exact)).astype(jnp.int32)
    large = jnp.minimum(large, N_BUCKETS - 1)
    bucket = jnp.where(r < max_exact, r, large)
    return jnp.transpose(rel_bias[bucket]).astype(F32)


def _diff_attention(y3, rel_bias, lam_vecs, diff_g, *, lam_init, t):
    bsz, s, _ = y3.shape
    hd = 2 * A_QK_DIM
    dv = A_V_DIM
    assert hd == dv == LANES and s % t == 0 and t >= MAX_DISTANCE
    nq = s // t
    table = _t5_bias_by_distance(rel_bias, 2 * t).reshape(A_HEADS, 1, 2 * t)
    return pl.pallas_call(
        functools.partial(_diff_attn_kernel, t=t, lam_init=lam_init),
        grid=(bsz, A_HEADS, nq),
        in_specs=[
            pl.BlockSpec((None, t, hd), lambda b, h, q: (b, q, h)),
            pl.BlockSpec((None, s, hd), lambda b, h, q: (b, 0, A_HEADS + h)),
            pl.BlockSpec((None, s, dv), lambda b, h, q: (b, 0, 2 * A_HEADS + h)),
            pl.BlockSpec((None, 1, 2 * t), lambda b, h, q: (h, 0, 0)),
            pl.BlockSpec((4, A_QK_DIM), lambda b, h, q: (0, 0)),
            pl.BlockSpec((1, dv), lambda b, h, q: (0, h)),
        ],
        out_specs=pl.BlockSpec((None, t, dv), lambda b, h, q: (b, q, h)),
        out_shape=jax.ShapeDtypeStruct((bsz, s, A_HEADS * dv), BF16),
        scratch_shapes=[pltpu.VMEM((nq, dv + ATTN_ONES_ROWS, t), BF16),
                        pltpu.VMEM((t, t), F32),
                        pltpu.VMEM((t, t), F32),
                        pltpu.VMEM((1, 2 * t), F32),
                        pltpu.VMEM((dv + ATTN_ONES_ROWS, 2 * t), F32)]
                       + 2 * [pltpu.VMEM((t, 2 * t), F32), pltpu.VMEM((1, 2 * t), F32),
                              pltpu.VMEM((1, 2 * t), F32)],
        compiler_params=_params("parallel", "parallel", "arbitrary"),
        name="diff_attn",
    )(y3, y3, y3, table, lam_vecs, diff_g)


def _mlstm_kernel(qk_ref, v_ref, ob_ref, gc_ref, gr_ref, cw_ref, cb_ref, bc_ref, br_ref, g_ref, y_ref,
                  xp_ref, *state_refs, chunk):
    heads, dk, dv = B_HEADS, B_QK_DIM, B_V_DIM
    c_refs, n_refs, m_refs = state_refs[0::3], state_refs[1::3], state_refs[2::3]
    pad = xp_ref.shape[0] - chunk
    c = pl.program_id(1)

    @pl.when(c == 0)
    def _():
        xp_ref[0:pad, :] = jnp.zeros((pad, xp_ref.shape[1]), xp_ref.dtype)
        for ref in state_refs:
            ref[...] = jnp.zeros_like(ref)

    x = qk_ref[...]
    xp_ref[pad:pad + chunk, :] = x
    xp = xp_ref[...]
    cw = cw_ref[...]
    dst = lax.broadcasted_iota(jnp.int32, (chunk, pad + chunk), 0)
    src = lax.broadcasted_iota(jnp.int32, (chunk, pad + chunk), 1)
    conv = cb_ref[...] + cw[CONV_WIDTH - 1:CONV_WIDTH, :] * x.astype(F32)
    for j in range(CONV_WIDTH - 1):
        shift = (src == dst + (pad - (CONV_WIDTH - 1) + j)).astype(BF16)
        conv = conv + cw[j:j + 1, :] * jnp.dot(shift, xp, preferred_element_type=F32)
    xp_ref[0:pad, :] = x[chunk - pad:chunk, :]
    qk = conv * _sigmoid(conv)

    gc = gc_ref[...] + bc_ref[...]
    gr = gr_ref[...] + br_ref[...]
    row = lax.broadcasted_iota(jnp.int32, (chunk, chunk), 0)
    col = lax.broadcasted_iota(jnp.int32, (chunk, chunk), 1)
    causal = row >= col
    tril = causal.astype(F32)
    triu = (row <= col).astype(F32)
    b_c = jnp.dot(tril, _log_sigmoid(gc), precision=HIGHEST, preferred_element_type=F32)
    b_r = jnp.dot(_log_sigmoid(gr), triu, precision=HIGHEST, preferred_element_type=F32)

    ob = ob_ref[...].astype(F32)
    g = g_ref[...]
    hs = range(heads)
    q = [qk[:, h * dk:(h + 1) * dk] * (dk ** -0.5) for h in hs]
    k = [qk[:, (heads + h) * dk:(heads + h + 1) * dk] for h in hs]
    v = [v_ref[:, h * dv:(h + 1) * dv] for h in hs]
    b_col = [b_c[:, heads + h:heads + h + 1] for h in hs]
    i_col = [gc[:, h:h + 1] for h in hs]
    b_row = [b_r[heads + h:heads + h + 1, :] for h in hs]
    i_row = [gr[h:h + 1, :] for h in hs]
    m_prev = [m_refs[h][0:1, 0:1] for h in hs]
    c_mat = [c_refs[h][...] for h in hs]
    n_vec = [n_refs[h][...] for h in hs]

    b_last = [b_row[h][:, chunk - 1:chunk] for h in hs]
    g_col = [b_last[h] - b_col[h] + i_col[h] for h in hs]
    m_new = [jnp.maximum(b_last[h] + m_prev[h], jnp.max(g_col[h], 0, keepdims=True)) for h in hs]
    decay = [jnp.exp(b_last[h] + m_prev[h] - m_new[h]) for h in hs]
    kw = [k[h] * jnp.exp(g_col[h] - m_new[h]) for h in hs]
    qb = [q[h].astype(BF16) for h in hs]
    qk_t = [lax.dot_general(qb[h], k[h].astype(BF16), NT_DIMS, preferred_element_type=F32) for h in hs]
    q_c = [jnp.dot(qb[h], c_mat[h].astype(BF16), preferred_element_type=F32) for h in hs]
    kw_v = [lax.dot_general(kw[h].astype(BF16), v[h], TN_DIMS, preferred_element_type=F32) for h in hs]

    d = [jnp.where(causal, b_col[h] - b_row[h] + i_row[h], -jnp.inf) for h in hs]
    a_col = [b_col[h] + m_prev[h] for h in hs]
    m_t = [jnp.maximum(a_col[h], jnp.max(d[h], -1, keepdims=True)) for h in hs]
    w_inter = [jnp.exp(a_col[h] - m_t[h]) for h in hs]
    sw = [qk_t[h] * jnp.exp(d[h] - m_t[h]) for h in hs]
    sw_v = [jnp.dot(sw[h].astype(BF16), v[h], preferred_element_type=F32) for h in hs]

    for h in hs:
        c_refs[h][...] = decay[h] * c_mat[h] + kw_v[h]
        n_refs[h][...] = decay[h] * n_vec[h] + jnp.sum(kw[h], 0, keepdims=True)
        m_refs[h][...] = jnp.broadcast_to(m_new[h], m_refs[h].shape)

    for h in hs:
        num = w_inter[h] * q_c[h] + sw_v[h]
        den = (w_inter[h] * jnp.sum(q[h] * n_vec[h], -1, keepdims=True)
               + jnp.sum(sw[h], -1, keepdims=True))
        hh = num / jnp.maximum(jnp.abs(den), jnp.exp(-m_t[h]))
        hn = hh * lax.rsqrt(jnp.mean(hh * hh, -1, keepdims=True) + NORM_EPS)
        sl = slice(h * dv, (h + 1) * dv)
        y_ref[:, sl] = (hn * g[:, sl] * _sigmoid(ob[:, sl])).astype(y_ref.dtype)


def _mlstm(y3, gcol3, grow3, conv_w, conv_b, gate_bias_cols, gate_bias_rows, mlstm_g, *, chunk):
    bsz, s, _ = y3.shape
    heads, dk, dv = B_HEADS, B_QK_DIM, B_V_DIM
    w = heads * dv
    assert 2 * heads * dk == w and s % chunk == 0 and chunk % LANES == 0
    nc = s // chunk
    return pl.pallas_call(
        functools.partial(_mlstm_kernel, chunk=chunk),
        grid=(bsz, nc),
        in_specs=[
            pl.BlockSpec((None, chunk, w), lambda b, c: (b, c, 3)),
            pl.BlockSpec((None, chunk, w), lambda b, c: (b, c, 4)),
            pl.BlockSpec((None, chunk, w), lambda b, c: (b, c, 5)),
            pl.BlockSpec((None, chunk, LANES), lambda b, c: (b, c, 0)),
            pl.BlockSpec((SUBLANES, chunk), lambda b, c: (0, b * nc + c)),
            pl.BlockSpec((CONV_WIDTH, w), lambda b, c: (0, 0)),
            pl.BlockSpec((1, w), lambda b, c: (0, 0)),
            pl.BlockSpec((1, LANES), lambda b, c: (0, 0)),
            pl.BlockSpec((SUBLANES, 1), lambda b, c: (0, 0)),
            pl.BlockSpec((1, w), lambda b, c: (0, 0)),
        ],
        out_specs=pl.BlockSpec((None, chunk, w), lambda b, c: (b, c, 0)),
        out_shape=jax.ShapeDtypeStruct((bsz, s, w), BF16),
        scratch_shapes=[
            pltpu.VMEM((BF16_SUBLANES + chunk, w), BF16),
        ] + heads * [pltpu.VMEM((dk, dv), F32), pltpu.VMEM((1, dk), F32), pltpu.VMEM((SUBLANES, LANES), F32)],
        compiler_params=_params("parallel", "arbitrary"),
        name="mlstm",
    )(y3, y3, y3, gcol3, grow3, conv_w, conv_b, gate_bias_cols, gate_bias_rows, mlstm_g)


def _gla_kernel(q_ref, k_ref, v_ref, r_ref, a1_ref, a1n_ref, wah_ref, wal_ref, ba_ref, g_ref, y_ref,
                b_ref, *st_refs, chunk):
    heads, dk, dv = C_HEADS, C_QK_DIM, C_V_DIM
    sub = GLA_SUB
    assert chunk == 2 * sub
    c = pl.program_id(1)
    row = lax.broadcasted_iota(jnp.int32, (chunk, chunk), 0)
    col = lax.broadcasted_iota(jnp.int32, (chunk, chunk), 1)
    causal = (row >= col) & ((row < sub) == (col < sub))
    first = lax.broadcasted_iota(jnp.int32, (chunk, 1), 0) < sub

    def cum_log_decay(a1):
        a_hi = a1.astype(BF16)
        a_lo = (a1 - a_hi.astype(F32)).astype(BF16)
        w_hi = wah_ref[...]
        z = (jnp.dot(a_hi, w_hi, preferred_element_type=F32)
             + (jnp.dot(a_lo, w_hi, preferred_element_type=F32)
                + jnp.dot(a_hi, wal_ref[...], preferred_element_type=F32))) + ba_ref[...]
        log_a = _log_sigmoid(z) * (1.0 / C_GATE_TEMP)
        return jnp.dot(causal.astype(F32), log_a, precision=HIGHEST, preferred_element_type=F32)

    @pl.when(c == 0)
    def _():
        for ref in st_refs:
            ref[...] = jnp.zeros_like(ref)
        b_ref[...] = cum_log_decay(a1_ref[...])

    b_all = b_ref[...]
    b_next = cum_log_decay(a1n_ref[...])

    r = r_ref[...].astype(F32)
    g = g_ref[...]
    for h in range(heads):
        ks = slice(h * dk, (h + 1) * dk)
        vs = slice(h * dv, (h + 1) * dv)
        b = b_all[:, ks]
        q = q_ref[:, ks].astype(F32) * (dk ** -0.5)
        k = k_ref[:, ks].astype(F32)
        v = v_ref[:, vs]
        tot0 = b[sub - 1:sub, :]
        tot1 = b[chunk - 1:chunk, :]
        q_dec = q * jnp.exp(b)
        k_dec = (k * jnp.exp(-b)).astype(BF16)
        k_end = k * jnp.exp(jnp.where(first, tot0, tot1) - b)
        k_new = (k_end * jnp.where(first, jnp.exp(tot1), 1.0)).astype(BF16)
        q_int = (q_dec * jnp.where(first, 1.0, jnp.exp(tot0))).astype(BF16)
        q_dec = q_dec.astype(BF16)
        st = st_refs[h][...]
        qk_t = lax.dot_general(q_dec, k_dec, NT_DIMS, preferred_element_type=F32)
        inter = lax.dot_general(q_int, st.astype(BF16), NT_DIMS, preferred_element_type=F32)
        st_refs[h][...] = (jnp.exp(tot0 + tot1) * st
                           + lax.dot_general(v, k_new, TN_DIMS, preferred_element_type=F32))
        cross = lax.dot_general(q_dec[sub:], k_end[:sub].astype(BF16), NT_DIMS, preferred_element_type=F32)
        att = jnp.where(causal, qk_t, 0.0)
        o = jnp.dot(att.astype(BF16), v, preferred_element_type=F32) + inter
        o_cross = jnp.dot(cross.astype(BF16), v[:sub], preferred_element_type=F32)
        o = jnp.concatenate([o[:sub], o[sub:] + o_cross], axis=0)

        on = o * lax.rsqrt(jnp.mean(o * o, -1, keepdims=True) + NORM_EPS)
        rr = r[:, vs]
        y_ref[:, vs] = (on * g[:, vs] * (rr * _sigmoid(rr))).astype(y_ref.dtype)

    b_ref[...] = b_next


def _gla(y3, a13, w_a2_hi, w_a2_lo, b_a, gla_g, *, chunk):
    bsz, s, _ = y3.shape
    heads, dk, dv = C_HEADS, C_QK_DIM, C_V_DIM
    wk, wv = heads * dk, heads * dv
    assert wv == 2 * wk and s % chunk == 0
    nc = s // chunk
    return pl.pallas_call(
        functools.partial(_gla_kernel, chunk=chunk),
        grid=(bsz, nc),
        in_specs=[
            pl.BlockSpec((None, chunk, wk), lambda b, c: (b, c, 0)),
            pl.BlockSpec((None, chunk, wk), lambda b, c: (b, c, 1)),
            pl.BlockSpec((None, chunk, wv), lambda b, c: (b, c, 1)),
            pl.BlockSpec((None, chunk, wv), lambda b, c: (b, c, 2)),
            pl.BlockSpec((None, chunk, LANES), lambda b, c: (b, c, 0)),
            pl.BlockSpec((None, chunk, LANES), lambda b, c: (b, jnp.minimum(c + 1, nc - 1), 0)),
            pl.BlockSpec((LANES, wk), lambda b, c: (0, 0)),
            pl.BlockSpec((LANES, wk), lambda b, c: (0, 0)),
            pl.BlockSpec((1, wk), lambda b, c: (0, 0)),
            pl.BlockSpec((1, wv), lambda b, c: (0, 0)),
        ],
        out_specs=pl.BlockSpec((None, chunk, wv), lambda b, c: (b, c, 0)),
        out_shape=jax.ShapeDtypeStruct((bsz, s, wv), BF16),
        scratch_shapes=[pltpu.VMEM((chunk, wk), F32)] + heads * [pltpu.VMEM((dv, dk), F32)],
        compiler_params=_params("parallel", "arbitrary"),
        name="gla",
    )(y3, y3, y3, y3, a13, a13, w_a2_hi, w_a2_lo, b_a, gla_g)


def _out_ln_kernel(*refs, n_in, alpha):
    y_refs, w_refs = refs[:n_in], refs[n_in:2 * n_in]
    x_ref, g_ref, b_ref, o_ref = refs[2 * n_in:]
    rb = o_ref.shape[0] // ROW_SPLIT
    for r in range(ROW_SPLIT):
        rows = slice(r * rb, (r + 1) * rb)
        acc = jnp.dot(y_refs[0][rows, :], w_refs[0][...], preferred_element_type=F32)
        for y_ref, w_ref in zip(y_refs[1:], w_refs[1:]):
            acc = acc + jnp.dot(y_ref[rows, :], w_ref[...], preferred_element_type=F32)
        o_ref[rows, :] = _layer_norm(alpha * x_ref[rows, :] + acc, g_ref[...], b_ref[...])


def _out_ln(ys, w, x, ln_g, ln_b, layer, w_layer, ln_slot, *, alpha, tm):
    t, d = x.shape
    n_in = len(ys)
    wd = ys[0].shape[1]
    assert all(y.shape[1] == wd for y in ys) and w.shape[1:] == (n_in * wd, d)
    ln_spec = pl.BlockSpec((None, None, 1, d), lambda i: (layer, ln_slot, 0, 0))
    in_specs = ([pl.BlockSpec((tm, wd), lambda i: (i, 0)) for _ in ys]
                + [pl.BlockSpec((None, wd, d), lambda i, k=k: (w_layer, k, 0)) for k in range(n_in)]
                + [pl.BlockSpec((tm, d), lambda i: (i, 0)), ln_spec, ln_spec])
    return pl.pallas_call(
        functools.partial(_out_ln_kernel, n_in=n_in, alpha=alpha),
        grid=(t // tm,),
        in_specs=in_specs,
        out_specs=pl.BlockSpec((tm, d), lambda i: (i, 0)),
        out_shape=jax.ShapeDtypeStruct((t, d), F32),
        compiler_params=_params("parallel"),
        name="out_ln",
    )(*ys, *([w] * n_in), x, ln_g, ln_b)


def _ple_kernel(x_ref, p_ref, wp_ref, wg_ref, o_ref):
    rb = o_ref.shape[0] // ROW_SPLIT
    for r in range(ROW_SPLIT):
        rows = slice(r * rb, (r + 1) * rb)
        x = x_ref[rows, :]
        e = jnp.dot(p_ref[rows, :].astype(BF16), wp_ref[...], preferred_element_type=F32)
        gate = jnp.dot(x.astype(BF16), wg_ref[...], preferred_element_type=F32)
        o_ref[rows, :] = x + e * _sigmoid(gate)


def _ple(x, p, w_proj, w_gate, layer, *, tm):
    t, d = x.shape
    s, pd = p.shape[2:]
    assert s % tm == 0
    per_seq = s // tm
    return pl.pallas_call(
        _ple_kernel,
        grid=(t // tm,),
        in_specs=[
            pl.BlockSpec((tm, d), lambda i: (i, 0)),
            pl.BlockSpec((None, None, tm, pd), lambda i: (layer, i // per_seq, i % per_seq, 0)),
            pl.BlockSpec((None, pd, d), lambda i: (layer, 0, 0)),
            pl.BlockSpec((None, d, d), lambda i: (layer, 0, 0)),
        ],
        out_specs=pl.BlockSpec((tm, d), lambda i: (i, 0)),
        out_shape=jax.ShapeDtypeStruct((t, d), F32),
        compiler_params=_params("parallel"),
        name="ple",
    )(x, p, w_proj, w_gate)


def _tiles(t, s):
    return dict(
        ffn_tm=min(1024, t), ffn_tf=256,
        proj_tm=min(1024, t), proj_tn=1024,
        row_tm=min(512, t),
        attn_t=min(512, s),
        mlstm_chunk=min(128, s),
        gla_chunk=min(2 * GLA_SUB, s),
    )


def _pad_cols(w, n):
    return jnp.pad(w, ((0, 0), (0, n - w.shape[1])))


def kernel(x, p, ln_g, ln_b, w_ffn_in, w_ffn_out, w_in_ab, w_out_ab, rel_bias, lambda_q1, lambda_k1,
           lambda_q2, lambda_k2, diff_norm, conv_w, conv_b, b_igate, b_fgate, mlstm_norm, w_in_c,
           w_alpha2, b_alpha, gla_norm, w_out_c, w_ple_proj, w_ple_gate):
    bsz, s, d = x.shape
    depth = p.shape[0]
    t = bsz * s
    tl = _tiles(t, s)
    alpha = (2 * depth) ** 0.25
    a_w = A_HEADS * A_V_DIM
    ab_main = 3 * a_w + 3 * B_HEADS * B_V_DIM
    c_main = 2 * C_HEADS * C_QK_DIM + 2 * C_HEADS * C_V_DIM
    row = lambda a: a.reshape(1, -1)

    wfi, wfo = w_ffn_in.astype(BF16), w_ffn_out.astype(BF16)
    wo_ab, wo_c = w_out_ab.astype(BF16), w_out_c.astype(BF16)
    wpp, wpg = w_ple_proj.astype(BF16), w_ple_gate.astype(BF16)
    g4, b4 = ln_g.reshape(depth, 3, 1, d), ln_b.reshape(depth, 3, 1, d)

    xf = x.reshape(t, d)
    for i in range(depth):
        xf = _ffn_ln(xf, wfi, wfo, g4, b4, i, 0, 0, alpha=alpha, tm=tl["ffn_tm"], tf=tl["ffn_tf"])
        if i % 2 == 0:
            e = i // 2
            lam_init = 0.8 - 0.6 * math.exp(-0.3 * i)
            y, gcol, grow = _in_proj(xf, w_in_ab, e, ab_main, tm=tl["proj_tm"], tn=tl["proj_tn"])
            y3 = y.reshape(bsz, s, ab_main)
            lam_vecs = jnp.stack([lambda_q1[e], lambda_k1[e], lambda_q2[e], lambda_k2[e]])
            ya = _diff_attention(y3, rel_bias, lam_vecs, row(diff_norm[e]), lam_init=lam_init, t=tl["attn_t"])
            gate_bias = jnp.concatenate([b_igate[e], b_fgate[e]])
            yb = _mlstm(y3, gcol.reshape(bsz, s, LANES), grow,
                        conv_w[e], row(conv_b[e]), _pad_cols(row(gate_bias), LANES), gate_bias.reshape(-1, 1),
                        row(mlstm_norm[e]), chunk=tl["mlstm_chunk"])
            ys, w_out, w_layer = [ya.reshape(t, a_w), yb.reshape(t, -1)], wo_ab, e
        else:
            o = i // 2
            y, a1, _ = _in_proj(xf, w_in_c, o, c_main, tm=tl["proj_tm"], tn=tl["proj_tn"])
            w_a2 = jnp.pad(w_alpha2[o], ((0, LANES - w_alpha2.shape[1]), (0, 0)))
            w_a2_hi = lax.reduce_precision(w_a2, exponent_bits=8, mantissa_bits=7)
            w_a2_lo = (w_a2 - w_a2_hi).astype(BF16)
            yc = _gla(y.reshape(bsz, s, c_main), a1.reshape(bsz, s, LANES), w_a2_hi.astype(BF16), w_a2_lo,
                      row(b_alpha[o]),
                      row(gla_norm[o]), chunk=tl["gla_chunk"])
            ys, w_out, w_layer = [yc.reshape(t, -1)], wo_c, o
        xf = _out_ln(ys, w_out, xf, g4, b4, i, w_layer, 1, alpha=alpha, tm=tl["row_tm"])
        xf = _ffn_ln(xf, wfi, wfo, g4, b4, i, 1, 2, alpha=alpha, tm=tl["ffn_tm"], tf=tl["ffn_tf"])
        xf = _ple(xf, p, wpp, wpg, i, tm=tl["row_tm"])
    return xf.reshape(bsz, s, d)
```

```python
import functools
import math

import jax
import jax.numpy as jnp
from jax import lax
from jax.experimental import pallas as pl
from jax.experimental.pallas import tpu as pltpu

F32 = jnp.float32
BF16 = jnp.bfloat16
HIGHEST = lax.Precision.HIGHEST

A_HEADS = 8
A_QK_DIM = 64
A_V_DIM = 128
B_HEADS = 4
B_QK_DIM = 128
B_V_DIM = 256
CONV_WIDTH = 4
C_HEADS = 4
C_QK_DIM = 256
C_V_DIM = 512
C_GATE_TEMP = 16.0
GLA_SUB = 64
N_BUCKETS = 32
MAX_DISTANCE = 128
LN_EPS = 1e-5
NORM_EPS = 1e-6

V7X_VMEM_LIMIT_BYTES = 56 * 1024 * 1024
LANES = 128
SUBLANES = 8
BF16_SUBLANES = 16
ATTN_ONES_ROWS = BF16_SUBLANES
MASKED_LOGIT = -1e30
ROW_SPLIT = 4

NT_DIMS = (((1,), (1,)), ((), ()))
TN_DIMS = (((0,), (0,)), ((), ()))


def _params(*sem):
    return pltpu.CompilerParams(dimension_semantics=sem, vmem_limit_bytes=V7X_VMEM_LIMIT_BYTES)


def _layer_norm(y, g, b, eps=LN_EPS):
    mu = jnp.mean(y, -1, keepdims=True)
    yc = y - mu
    var = jnp.mean(yc * yc, -1, keepdims=True)
    return yc * lax.rsqrt(var + eps) * g + b


def _sigmoid(x):
    return 1.0 / (1.0 + jnp.exp(-x))


def _log_sigmoid(x):
    return jnp.minimum(x, 0.0) - jnp.log(1.0 + jnp.exp(-jnp.abs(x)))


def _ffn_ln_kernel(x_ref, wg0_ref, wu0_ref, wo0_ref, wg1_ref, wu1_ref, wo1_ref, g_ref, b_ref, o_ref, xb_ref,
                   *, alpha, n_chunks):
    j = pl.program_id(1)
    last = pl.num_programs(1) - 1

    @pl.when(j == 0)
    def _():
        x = x_ref[...]
        xb_ref[...] = x.astype(BF16)
        o_ref[...] = (2.0 * alpha) * x

    def chunk(wg_ref, wu_ref, wo_ref):
        xb = xb_ref[...]
        gate = jnp.dot(xb, wg_ref[...], preferred_element_type=F32)
        up = jnp.dot(xb, wu_ref[...], preferred_element_type=F32)
        h = (gate * _sigmoid(gate) * up).astype(BF16)
        o_ref[...] += jnp.dot(h, wo_ref[...], preferred_element_type=F32)

    if n_chunks % 2 == 0:
        chunk(wg0_ref, wu0_ref, wo0_ref)
        chunk(wg1_ref, wu1_ref, wo1_ref)
    else:
        @pl.when(j < last)
        def _():
            chunk(wg0_ref, wu0_ref, wo0_ref)
            chunk(wg1_ref, wu1_ref, wo1_ref)

        @pl.when(j == last)
        def _():
            chunk(wg0_ref, wu0_ref, wo0_ref)

    @pl.when(j == last)
    def _():
        o_ref[...] = _layer_norm(o_ref[...], g_ref[...], b_ref[...], eps=4.0 * LN_EPS)


def _ffn_ln(x, w_in, w_out, ln_g, ln_b, layer, slot, ln_slot, *, alpha, tm, tf):
    t, d = x.shape
    f = w_out.shape[2]
    nf = f // tf
    assert t % tm == 0 and f % tf == 0 and w_in.shape[2:] == (d, 2 * f) and nf >= 2
    steps = (nf + 1) // 2
    first = lambda j: 2 * j
    second = lambda j: jnp.where(2 * j + 1 < nf, 2 * j + 1, nf - 2)
    ln_spec = pl.BlockSpec((None, None, 1, d), lambda i, j: (layer, ln_slot, 0, 0))

    def weight_specs(chunk_of):
        return [pl.BlockSpec((None, None, d, tf), lambda i, j: (layer, slot, 0, chunk_of(j))),
                pl.BlockSpec((None, None, d, tf), lambda i, j: (layer, slot, 0, nf + chunk_of(j))),
                pl.BlockSpec((None, None, tf, d), lambda i, j: (layer, slot, chunk_of(j), 0))]

    return pl.pallas_call(
        functools.partial(_ffn_ln_kernel, alpha=alpha, n_chunks=nf),
        grid=(t // tm, steps),
        in_specs=[pl.BlockSpec((tm, d), lambda i, j: (i, 0))] + weight_specs(first) + weight_specs(second)
                 + [ln_spec, ln_spec],
        out_specs=pl.BlockSpec((tm, d), lambda i, j: (i, 0)),
        out_shape=jax.ShapeDtypeStruct((t, d), F32),
        scratch_shapes=[pltpu.VMEM((tm, d), BF16)],
        compiler_params=_params("parallel", "arbitrary"),
        name="ffn_ln",
    )(x, w_in, w_in, w_out, w_in, w_in, w_out, ln_g, ln_b)


def _in_proj_kernel(x_ref, w_ref, wgc_ref, wgr_ref, y_ref, gc_ref, gr_ref, xb_ref):
    j = pl.program_id(1)

    @pl.when(j == 0)
    def _():
        xb = x_ref[...].astype(BF16)
        xb_ref[...] = xb
        gc_ref[...] = jnp.dot(xb, wgc_ref[...], preferred_element_type=F32)
        gr_ref[...] = lax.dot_general(wgr_ref[...], xb, NT_DIMS, preferred_element_type=F32)

    y_ref[...] = lax.dot_general(xb_ref[...], w_ref[...].astype(BF16), NT_DIMS,
                                 preferred_element_type=F32).astype(y_ref.dtype)


def _in_proj(x, w_stack, layer, n_main, *, tm, tn):
    t, d = x.shape
    assert t % tm == 0 and n_main % tn == 0
    w_t = jnp.swapaxes(w_stack, 1, 2)
    w_gates = w_stack[layer, :, n_main:].astype(BF16)
    w_gate_cols = _pad_cols(w_gates, LANES)
    w_gate_rows = jnp.transpose(w_gates[:, :SUBLANES])
    return pl.pallas_call(
        _in_proj_kernel,
        grid=(t // tm, n_main // tn),
        in_specs=[
            pl.BlockSpec((tm, d), lambda i, j: (i, 0)),
            pl.BlockSpec((None, tn, d), lambda i, j: (layer, j, 0)),
            pl.BlockSpec((d, LANES), lambda i, j: (0, 0)),
            pl.BlockSpec((SUBLANES, d), lambda i, j: (0, 0)),
        ],
        out_specs=[
            pl.BlockSpec((tm, tn), lambda i, j: (i, j)),
            pl.BlockSpec((tm, LANES), lambda i, j: (i, 0)),
            pl.BlockSpec((SUBLANES, tm), lambda i, j: (0, i)),
        ],
        out_shape=[
            jax.ShapeDtypeStruct((t, n_main), BF16),
            jax.ShapeDtypeStruct((t, LANES), F32),
            jax.ShapeDtypeStruct((SUBLANES, t), F32),
        ],
        scratch_shapes=[pltpu.VMEM((tm, d), BF16)],
        compiler_params=_params("parallel", "arbitrary"),
        name="in_proj",
    )(x, w_t, w_gate_cols, w_gate_rows)


def _diff_attn_kernel(q_ref, k_ref, v_ref, tab_ref, lam_ref, g_ref, o_ref,
                      vt_ref, bdiag_ref, bnear_ref, m_ref, acc_ref,
                      s0_ref, mx0_ref, sh0_ref, s1_ref, mx1_ref, sh1_ref, *, t, lam_init):
    qi = pl.program_id(2)
    dk, dv = A_QK_DIM, A_V_DIM
    nk = k_ref.shape[0] // t
    bufs = ((s0_ref, mx0_ref, sh0_ref), (s1_ref, mx1_ref, sh1_ref))

    @pl.when(qi == 0)
    def _():
        for c in range(nk):
            vt_ref[c, 0:dv, :] = jnp.transpose(v_ref[c * t:(c + 1) * t, :].astype(F32)).astype(BF16)
            vt_ref[c, dv:, :] = jnp.ones((vt_ref.shape[1] - dv, t), BF16)
        skew = pltpu.roll(jnp.broadcast_to(tab_ref[...], (t, 2 * t)), 0, 1, stride=1, stride_axis=0)
        key = lax.broadcasted_iota(jnp.int32, (t, t), 0)
        qry = lax.broadcasted_iota(jnp.int32, (t, t), 1)
        bdiag_ref[...] = jnp.where(qry >= key, skew[:, :t], MASKED_LOGIT)
        bnear_ref[...] = skew[:, t:]

    q_t = jnp.transpose(q_ref[...].astype(F32)) * (dk ** -0.5)
    feat = lax.broadcasted_iota(jnp.int32, q_t.shape, 0)
    qs_t = jnp.concatenate([jnp.where(feat < dk, q_t, 0.0), jnp.where(feat >= dk, q_t, 0.0)],
                           axis=1).astype(BF16)

    m_ref[...] = jnp.full_like(m_ref, -jnp.inf)
    acc_ref[...] = jnp.zeros_like(acc_ref)

    def score(kj, buf, bias_ref, shift):
        s_ref, mx_ref, sh_ref = bufs[buf]
        off = pl.multiple_of(kj * t, t)
        s = jnp.dot(k_ref[pl.ds(off, t), :], qs_t, preferred_element_type=F32)
        if bias_ref is not None:
            b = bias_ref[...]
            s = s + jnp.concatenate([b, b], axis=1)
        s_ref[...] = s
        mx_ref[...] = jnp.max(s, 0, keepdims=True) + shift
        sh_ref[...] = jnp.zeros_like(sh_ref) + shift

    def absorb(kj, buf):
        s_ref, mx_ref, sh_ref = bufs[buf]
        m_old = m_ref[...]
        m_new = jnp.maximum(m_old, mx_ref[...])
        p = jnp.exp(s_ref[...] - (m_new - sh_ref[...])).astype(BF16)
        acc_ref[...] = (jnp.exp(m_old - m_new) * acc_ref[...]
                        + jnp.dot(vt_ref[kj], p, preferred_element_type=F32))
        m_ref[...] = m_new

    c_far = tab_ref[:, 2 * t - 1:2 * t]
    score(qi, 0, bdiag_ref, 0.0)

    @pl.when(qi >= 1)
    def _():
        score(qi - 1, 1, bnear_ref, 0.0)
        absorb(qi, 0)

    def far_pair(p, carry):
        kj = qi - 2 - 2 * p
        score(kj, 0, None, c_far)
        absorb(kj + 1, 1)
        score(kj - 1, 1, None, c_far)
        absorb(kj, 0)
        return carry

    lax.fori_loop(0, jnp.maximum(qi - 1, 0) // 2, far_pair, 0)

    @pl.when((qi >= 2) & (qi % 2 == 0))
    def _():
        score(0, 0, None, c_far)
        absorb(1, 1)

    @pl.when(qi % 2 == 0)
    def _():
        absorb(0, 0)

    @pl.when(qi % 2 == 1)
    def _():
        absorb(0, 1)

    lv = lam_ref[...]
    lam = (jnp.exp(jnp.sum(lv[0:1] * lv[1:2], -1, keepdims=True))
           - jnp.exp(jnp.sum(lv[2:3] * lv[3:4], -1, keepdims=True)) + lam_init)
    acc = acc_ref[...]
    o_t = acc[0:dv] * (1.0 / acc[dv:dv + 1])
    out = jnp.transpose(o_t[:, :t] - lam * o_t[:, t:])
    hn = out * lax.rsqrt(jnp.mean(out * out, -1, keepdims=True) + NORM_EPS)
    o_ref[...] = (hn * g_ref[...] * (1.0 - lam_init)).astype(o_ref.dtype)


def _t5_bias_by_distance(rel_bias, n):
    r = jnp.arange(n, dtype=jnp.int32)
    max_exact = N_BUCKETS // 2
    rf = jnp.maximum(r, 1).astype(F32)
    large = max_exact + (jnp.log(rf / max_exact) / math.log(MAX_DISTANCE / max_exact)
                         * (N_BUCKETS - max_exact)).astype(jnp.int32)
    large = jnp.minimum(large, N_BUCKETS - 1)
    bucket = jnp.where(r < max_exact, r, large)
    return jnp.transpose(rel_bias[bucket]).astype(F32)


def _diff_attention(y3, rel_bias, lam_vecs, diff_g, *, lam_init, t):
    bsz, s, _ = y3.shape
    hd = 2 * A_QK_DIM
    dv = A_V_DIM
    assert hd == dv == LANES and s % t == 0 and t >= MAX_DISTANCE
    nq = s // t
    table = _t5_bias_by_distance(rel_bias, 2 * t).reshape(A_HEADS, 1, 2 * t)
    return pl.pallas_call(
        functools.partial(_diff_attn_kernel, t=t, lam_init=lam_init),
        grid=(bsz, A_HEADS, nq),
        in_specs=[
            pl.BlockSpec((None, t, hd), lambda b, h, q: (b, q, h)),
            pl.BlockSpec((None, s, hd), lambda b, h, q: (b, 0, A_HEADS + h)),
            pl.BlockSpec((None, s, dv), lambda b, h, q: (b, 0, 2 * A_HEADS + h)),
            pl.BlockSpec((None, 1, 2 * t), lambda b, h, q: (h, 0, 0)),
            pl.BlockSpec((4, A_QK_DIM), lambda b, h, q: (0, 0)),
            pl.BlockSpec((1, dv), lambda b, h, q: (0, h)),
        ],
        out_specs=pl.BlockSpec((None, t, dv), lambda b, h, q: (b, q, h)),
        out_shape=jax.ShapeDtypeStruct((bsz, s, A_HEADS * dv), BF16),
        scratch_shapes=[pltpu.VMEM((nq, dv + ATTN_ONES_ROWS, t), BF16),
                        pltpu.VMEM((t, t), F32),
                        pltpu.VMEM((t, t), F32),
                        pltpu.VMEM((1, 2 * t), F32),
                        pltpu.VMEM((dv + ATTN_ONES_ROWS, 2 * t), F32)]
                       + 2 * [pltpu.VMEM((t, 2 * t), F32), pltpu.VMEM((1, 2 * t), F32),
                              pltpu.VMEM((1, 2 * t), F32)],
        compiler_params=_params("parallel", "parallel", "arbitrary"),
        name="diff_attn",
    )(y3, y3, y3, table, lam_vecs, diff_g)


def _mlstm_kernel(qk_ref, v_ref, ob_ref, gc_ref, gr_ref, cw_ref, cb_ref, bc_ref, br_ref, g_ref, y_ref,
                  xp_ref, *state_refs, chunk):
    heads, dk, dv = B_HEADS, B_QK_DIM, B_V_DIM
    c_refs, n_refs, m_refs = state_refs[0::3], state_refs[1::3], state_refs[2::3]
    pad = xp_ref.shape[0] - chunk
    c = pl.program_id(1)

    @pl.when(c == 0)
    def _():
        xp_ref[0:pad, :] = jnp.zeros((pad, xp_ref.shape[1]), xp_ref.dtype)
        for ref in state_refs:
            ref[...] = jnp.zeros_like(ref)

    x = qk_ref[...]
    xp_ref[pad:pad + chunk, :] = x
    xp = xp_ref[...]
    cw = cw_ref[...]
    dst = lax.broadcasted_iota(jnp.int32, (chunk, pad + chunk), 0)
    src = lax.broadcasted_iota(jnp.int32, (chunk, pad + chunk), 1)
    conv = cb_ref[...] + cw[CONV_WIDTH - 1:CONV_WIDTH, :] * x.astype(F32)
    for j in range(CONV_WIDTH - 1):
        shift = (src == dst + (pad - (CONV_WIDTH - 1) + j)).astype(BF16)
        conv = conv + cw[j:j + 1, :] * jnp.dot(shift, xp, preferred_element_type=F32)
    xp_ref[0:pad, :] = x[chunk - pad:chunk, :]
    qk = conv * _sigmoid(conv)

    gc = gc_ref[...] + bc_ref[...]
    gr = gr_ref[...] + br_ref[...]
    row = lax.broadcasted_iota(jnp.int32, (chunk, chunk), 0)
    col = lax.broadcasted_iota(jnp.int32, (chunk, chunk), 1)
    causal = row >= col
    tril = causal.astype(F32)
    triu = (row <= col).astype(F32)
    b_c = jnp.dot(tril, _log_sigmoid(gc), precision=HIGHEST, preferred_element_type=F32)
    b_r = jnp.dot(_log_sigmoid(gr), triu, precision=HIGHEST, preferred_element_type=F32)

    ob = ob_ref[...].astype(F32)
    g = g_ref[...]
    hs = range(heads)
    q = [qk[:, h * dk:(h + 1) * dk] * (dk ** -0.5) for h in hs]
    k = [qk[:, (heads + h) * dk:(heads + h + 1) * dk] for h in hs]
    v = [v_ref[:, h * dv:(h + 1) * dv] for h in hs]
    b_col = [b_c[:, heads + h:heads + h + 1] for h in hs]
    i_col = [gc[:, h:h + 1] for h in hs]
    b_row = [b_r[heads + h:heads + h + 1, :] for h in hs]
    i_row = [gr[h:h + 1, :] for h in hs]
    m_prev = [m_refs[h][0:1, 0:1] for h in hs]
    c_mat = [c_refs[h][...] for h in hs]
    n_vec = [n_refs[h][...] for h in hs]

    b_last = [b_row[h][:, chunk - 1:chunk] for h in hs]
    g_col = [b_last[h] - b_col[h] + i_col[h] for h in hs]
    m_new = [jnp.maximum(b_last[h] + m_prev[h], jnp.max(g_col[h], 0, keepdims=True)) for h in hs]
    decay = [jnp.exp(b_last[h] + m_prev[h] - m_new[h]) for h in hs]
    kw = [k[h] * jnp.exp(g_col[h] - m_new[h]) for h in hs]
    qb = [q[h].astype(BF16) for h in hs]
    qk_t = [lax.dot_general(qb[h], k[h].astype(BF16), NT_DIMS, preferred_element_type=F32) for h in hs]
    q_c = [jnp.dot(qb[h], c_mat[h].astype(BF16), preferred_element_type=F32) for h in hs]
    kw_v = [lax.dot_general(kw[h].astype(BF16), v[h], TN_DIMS, preferred_element_type=F32) for h in hs]

    d = [jnp.where(causal, b_col[h] - b_row[h] + i_row[h], -jnp.inf) for h in hs]
    a_col = [b_col[h] + m_prev[h] for h in hs]
    m_t = [jnp.maximum(a_col[h], jnp.max(d[h], -1, keepdims=True)) for h in hs]
    w_inter = [jnp.exp(a_col[h] - m_t[h]) for h in hs]
    sw = [qk_t[h] * jnp.exp(d[h] - m_t[h]) for h in hs]
    sw_v = [jnp.dot(sw[h].astype(BF16), v[h], preferred_element_type=F32) for h in hs]

    for h in hs:
        c_refs[h][...] = decay[h] * c_mat[h] + kw_v[h]
        n_refs[h][...] = decay[h] * n_vec[h] + jnp.sum(kw[h], 0, keepdims=True)
        m_refs[h][...] = jnp.broadcast_to(m_new[h], m_refs[h].shape)

    for h in hs:
        num = w_inter[h] * q_c[h] + sw_v[h]
        den = (w_inter[h] * jnp.sum(q[h] * n_vec[h], -1, keepdims=True)
               + jnp.sum(sw[h], -1, keepdims=True))
        hh = num / jnp.maximum(jnp.abs(den), jnp.exp(-m_t[h]))
        hn = hh * lax.rsqrt(jnp.mean(hh * hh, -1, keepdims=True) + NORM_EPS)
        sl = slice(h * dv, (h + 1) * dv)
        y_ref[:, sl] = (hn * g[:, sl] * _sigmoid(ob[:, sl])).astype(y_ref.dtype)


def _mlstm(y3, gcol3, grow3, conv_w, conv_b, gate_bias_cols, gate_bias_rows, mlstm_g, *, chunk):
    bsz, s, _ = y3.shape
    heads, dk, dv = B_HEADS, B_QK_DIM, B_V_DIM
    w = heads * dv
    assert 2 * heads * dk == w and s % chunk == 0 and chunk % LANES == 0
    nc = s // chunk
    return pl.pallas_call(
        functools.partial(_mlstm_kernel, chunk=chunk),
        grid=(bsz, nc),
        in_specs=[
            pl.BlockSpec((None, chunk, w), lambda b, c: (b, c, 3)),
            pl.BlockSpec((None, chunk, w), lambda b, c: (b, c, 4)),
            pl.BlockSpec((None, chunk, w), lambda b, c: (b, c, 5)),
            pl.BlockSpec((None, chunk, LANES), lambda b, c: (b, c, 0)),
            pl.BlockSpec((SUBLANES, chunk), lambda b, c: (0, b * nc + c)),
            pl.BlockSpec((CONV_WIDTH, w), lambda b, c: (0, 0)),
            pl.BlockSpec((1, w), lambda b, c: (0, 0)),
            pl.BlockSpec((1, LANES), lambda b, c: (0, 0)),
            pl.BlockSpec((SUBLANES, 1), lambda b, c: (0, 0)),
            pl.BlockSpec((1, w), lambda b, c: (0, 0)),
        ],
        out_specs=pl.BlockSpec((None, chunk, w), lambda b, c: (b, c, 0)),
        out_shape=jax.ShapeDtypeStruct((bsz, s, w), BF16),
        scratch_shapes=[
            pltpu.VMEM((BF16_SUBLANES + chunk, w), BF16),
        ] + heads * [pltpu.VMEM((dk, dv), F32), pltpu.VMEM((1, dk), F32), pltpu.VMEM((SUBLANES, LANES), F32)],
        compiler_params=_params("parallel", "arbitrary"),
        name="mlstm",
    )(y3, y3, y3, gcol3, grow3, conv_w, conv_b, gate_bias_cols, gate_bias_rows, mlstm_g)


def _gla_kernel(q_ref, k_ref, v_ref, r_ref, a1_ref, a1n_ref, wah_ref, wal_ref, ba_ref, g_ref, y_ref,
                b_ref, *st_refs, chunk):
    heads, dk, dv = C_HEADS, C_QK_DIM, C_V_DIM
    sub = GLA_SUB
    assert chunk == 2 * sub
    c = pl.program_id(1)
    row = lax.broadcasted_iota(jnp.int32, (chunk, chunk), 0)
    col = lax.broadcasted_iota(jnp.int32, (chunk, chunk), 1)
    causal = (row >= col) & ((row < sub) == (col < sub))
    first = lax.broadcasted_iota(jnp.int32, (chunk, 1), 0) < sub

    def cum_log_decay(a1):
        a_hi = a1.astype(BF16)
        a_lo = (a1 - a_hi.astype(F32)).astype(BF16)
        w_hi = wah_ref[...]
        z = (jnp.dot(a_hi, w_hi, preferred_element_type=F32)
             + (jnp.dot(a_lo, w_hi, preferred_element_type=F32)
                + jnp.dot(a_hi, wal_ref[...], preferred_element_type=F32))) + ba_ref[...]
        log_a = _log_sigmoid(z) * (1.0 / C_GATE_TEMP)
        return jnp.dot(causal.astype(F32), log_a, precision=HIGHEST, preferred_element_type=F32)

    @pl.when(c == 0)
    def _():
        for ref in st_refs:
            ref[...] = jnp.zeros_like(ref)
        b_ref[...] = cum_log_decay(a1_ref[...])

    b_all = b_ref[...]
    b_next = cum_log_decay(a1n_ref[...])

    r = r_ref[...].astype(F32)
    g = g_ref[...]
    for h in range(heads):
        ks = slice(h * dk, (h + 1) * dk)
        vs = slice(h * dv, (h + 1) * dv)
        b = b_all[:, ks]
        q = q_ref[:, ks].astype(F32) * (dk ** -0.5)
        k = k_ref[:, ks].astype(F32)
        v = v_ref[:, vs]
        tot0 = b[sub - 1:sub, :]
        tot1 = b[chunk - 1:chunk, :]
        q_dec = q * jnp.exp(b)
        k_dec = (k * jnp.exp(-b)).astype(BF16)
        k_end = k * jnp.exp(jnp.where(first, tot0, tot1) - b)
        k_new = (k_end * jnp.where(first, jnp.exp(tot1), 1.0)).astype(BF16)
        q_int = (q_dec * jnp.where(first, 1.0, jnp.exp(tot0))).astype(BF16)
        q_dec = q_dec.astype(BF16)
        st = st_refs[h][...]
        qk_t = lax.dot_general(q_dec, k_dec, NT_DIMS, preferred_element_type=F32)
        inter = lax.dot_general(q_int, st.astype(BF16), NT_DIMS, preferred_element_type=F32)
        st_refs[h][...] = (jnp.exp(tot0 + tot1) * st
                           + lax.dot_general(v, k_new, TN_DIMS, preferred_element_type=F32))
        cross = lax.dot_general(q_dec[sub:], k_end[:sub].astype(BF16), NT_DIMS, preferred_element_type=F32)
        att = jnp.where(causal, qk_t, 0.0)
        o = jnp.dot(att.astype(BF16), v, preferred_element_type=F32) + inter
        o_cross = jnp.dot(cross.astype(BF16), v[:sub], preferred_element_type=F32)
        o = jnp.concatenate([o[:sub], o[sub:] + o_cross], axis=0)

        on = o * lax.rsqrt(jnp.mean(o * o, -1, keepdims=True) + NORM_EPS)
        rr = r[:, vs]
        y_ref[:, vs] = (on * g[:, vs] * (rr * _sigmoid(rr))).astype(y_ref.dtype)

    b_ref[...] = b_next


def _gla(y3, a13, w_a2_hi, w_a2_lo, b_a, gla_g, *, chunk):
    bsz, s, _ = y3.shape
    heads, dk, dv = C_HEADS, C_QK_DIM, C_V_DIM
    wk, wv = heads * dk, heads * dv
    assert wv == 2 * wk and s % chunk == 0
    nc = s // chunk
    return pl.pallas_call(
        functools.partial(_gla_kernel, chunk=chunk),
        grid=(bsz, nc),
        in_specs=[
            pl.BlockSpec((None, chunk, wk), lambda b, c: (b, c, 0)),
            pl.BlockSpec((None, chunk, wk), lambda b, c: (b, c, 1)),
            pl.BlockSpec((None, chunk, wv), lambda b, c: (b, c, 1)),
            pl.BlockSpec((None, chunk, wv), lambda b, c: (b, c, 2)),
            pl.BlockSpec((None, chunk, LANES), lambda b, c: (b, c, 0)),
            pl.BlockSpec((None, chunk, LANES), lambda b, c: (b, jnp.minimum(c + 1, nc - 1), 0)),
            pl.BlockSpec((LANES, wk), lambda b, c: (0, 0)),
            pl.BlockSpec((LANES, wk), lambda b, c: (0, 0)),
            pl.BlockSpec((1, wk), lambda b, c: (0, 0)),
            pl.BlockSpec((1, wv), lambda b, c: (0, 0)),
        ],
        out_specs=pl.BlockSpec((None, chunk, wv), lambda b, c: (b, c, 0)),
        out_shape=jax.ShapeDtypeStruct((bsz, s, wv), BF16),
        scratch_shapes=[pltpu.VMEM((chunk, wk), F32)] + heads * [pltpu.VMEM((dv, dk), F32)],
        compiler_params=_params("parallel", "arbitrary"),
        name="gla",
    )(y3, y3, y3, y3, a13, a13, w_a2_hi, w_a2_lo, b_a, gla_g)


def _out_ln_kernel(*refs, n_in, alpha):
    y_refs, w_refs = refs[:n_in], refs[n_in:2 * n_in]
    x_ref, g_ref, b_ref, o_ref = refs[2 * n_in:]
    rb = o_ref.shape[0] // ROW_SPLIT
    for r in range(ROW_SPLIT):
        rows = slice(r * rb, (r + 1) * rb)
        acc = jnp.dot(y_refs[0][rows, :], w_refs[0][...], preferred_element_type=F32)
        for y_ref, w_ref in zip(y_refs[1:], w_refs[1:]):
            acc = acc + jnp.dot(y_ref[rows, :], w_ref[...], preferred_element_type=F32)
        o_ref[rows, :] = _layer_norm(alpha * x_ref[rows, :] + acc, g_ref[...], b_ref[...])


def _out_ln(ys, w, x, ln_g, ln_b, layer, w_layer, ln_slot, *, alpha, tm):
    t, d = x.shape
    n_in = len(ys)
    wd = ys[0].shape[1]
    assert all(y.shape[1] == wd for y in ys) and w.shape[1:] == (n_in * wd, d)
    ln_spec = pl.BlockSpec((None, None, 1, d), lambda i: (layer, ln_slot, 0, 0))
    in_specs = ([pl.BlockSpec((tm, wd), lambda i: (i, 0)) for _ in ys]
                + [pl.BlockSpec((None, wd, d), lambda i, k=k: (w_layer, k, 0)) for k in range(n_in)]
                + [pl.BlockSpec((tm, d), lambda i: (i, 0)), ln_spec, ln_spec])
    return pl.pallas_call(
        functools.partial(_out_ln_kernel, n_in=n_in, alpha=alpha),
        grid=(t // tm,),
        in_specs=in_specs,
        out_specs=pl.BlockSpec((tm, d), lambda i: (i, 0)),
        out_shape=jax.ShapeDtypeStruct((t, d), F32),
        compiler_params=_params("parallel"),
        name="out_ln",
    )(*ys, *([w] * n_in), x, ln_g, ln_b)


def _ple_kernel(x_ref, p_ref, wp_ref, wg_ref, o_ref):
    rb = o_ref.shape[0] // ROW_SPLIT
    for r in range(ROW_SPLIT):
        rows = slice(r * rb, (r + 1) * rb)
        x = x_ref[rows, :]
        e = jnp.dot(p_ref[rows, :].astype(BF16), wp_ref[...], preferred_element_type=F32)
        gate = jnp.dot(x.astype(BF16), wg_ref[...], preferred_element_type=F32)
        o_ref[rows, :] = x + e * _sigmoid(gate)


def _ple(x, p, w_proj, w_gate, layer, *, tm):
    t, d = x.shape
    s, pd = p.shape[2:]
    assert s % tm == 0
    per_seq = s // tm
    return pl.pallas_call(
        _ple_kernel,
        grid=(t // tm,),
        in_specs=[
            pl.BlockSpec((tm, d), lambda i: (i, 0)),
            pl.BlockSpec((None, None, tm, pd), lambda i: (layer, i // per_seq, i % per_seq, 0)),
            pl.BlockSpec((None, pd, d), lambda i: (layer, 0, 0)),
            pl.BlockSpec((None, d, d), lambda i: (layer, 0, 0)),
        ],
        out_specs=pl.BlockSpec((tm, d), lambda i: (i, 0)),
        out_shape=jax.ShapeDtypeStruct((t, d), F32),
        compiler_params=_params("parallel"),
        name="ple",
    )(x, p, w_proj, w_gate)


def _tiles(t, s):
    return dict(
        ffn_tm=min(1024, t), ffn_tf=256,
        proj_tm=min(1024, t), proj_tn=1024,
        row_tm=min(512, t),
        attn_t=min(512, s),
        mlstm_chunk=min(128, s),
        gla_chunk=min(2 * GLA_SUB, s),
    )


def _pad_cols(w, n):
    return jnp.pad(w, ((0, 0), (0, n - w.shape[1])))


def kernel(x, p, ln_g, ln_b, w_ffn_in, w_ffn_out, w_in_ab, w_out_ab, rel_bias, lambda_q1, lambda_k1,
           lambda_q2, lambda_k2, diff_norm, conv_w, conv_b, b_igate, b_fgate, mlstm_norm, w_in_c,
           w_alpha2, b_alpha, gla_norm, w_out_c, w_ple_proj, w_ple_gate):
    bsz, s, d = x.shape
    depth = p.shape[0]
    t = bsz * s
    tl = _tiles(t, s)
    alpha = (2 * depth) ** 0.25
    a_w = A_HEADS * A_V_DIM
    ab_main = 3 * a_w + 3 * B_HEADS * B_V_DIM
    c_main = 2 * C_HEADS * C_QK_DIM + 2 * C_HEADS * C_V_DIM
    row = lambda a: a.reshape(1, -1)

    wfi, wfo = w_ffn_in.astype(BF16), w_ffn_out.astype(BF16)
    wo_ab, wo_c = w_out_ab.astype(BF16), w_out_c.astype(BF16)
    wpp, wpg = w_ple_proj.astype(BF16), w_ple_gate.astype(BF16)
    g4, b4 = ln_g.reshape(depth, 3, 1, d), ln_b.reshape(depth, 3, 1, d)

    xf = x.reshape(t, d)
    for i in range(depth):
        xf = _ffn_ln(xf, wfi, wfo, g4, b4, i, 0, 0, alpha=alpha, tm=tl["ffn_tm"], tf=tl["ffn_tf"])
        if i % 2 == 0:
            e = i // 2
            lam_init = 0.8 - 0.6 * math.exp(-0.3 * i)
            y, gcol, grow = _in_proj(xf, w_in_ab, e, ab_main, tm=tl["proj_tm"], tn=tl["proj_tn"])
            y3 = y.reshape(bsz, s, ab_main)
            lam_vecs = jnp.stack([lambda_q1[e], lambda_k1[e], lambda_q2[e], lambda_k2[e]])
            ya = _diff_attention(y3, rel_bias, lam_vecs, row(diff_norm[e]), lam_init=lam_init, t=tl["attn_t"])
            gate_bias = jnp.concatenate([b_igate[e], b_fgate[e]])
            yb = _mlstm(y3, gcol.reshape(bsz, s, LANES), grow,
                        conv_w[e], row(conv_b[e]), _pad_cols(row(gate_bias), LANES), gate_bias.reshape(-1, 1),
                        row(mlstm_norm[e]), chunk=tl["mlstm_chunk"])
            ys, w_out, w_layer = [ya.reshape(t, a_w), yb.reshape(t, -1)], wo_ab, e
        else:
            o = i // 2
            y, a1, _ = _in_proj(xf, w_in_c, o, c_main, tm=tl["proj_tm"], tn=tl["proj_tn"])
            w_a2 = jnp.pad(w_alpha2[o], ((0, LANES - w_alpha2.shape[1]), (0, 0)))
            w_a2_hi = lax.reduce_precision(w_a2, exponent_bits=8, mantissa_bits=7)
            w_a2_lo = (w_a2 - w_a2_hi).astype(BF16)
            yc = _gla(y.reshape(bsz, s, c_main), a1.reshape(bsz, s, LANES), w_a2_hi.astype(BF16), w_a2_lo,
                      row(b_alpha[o]),
                      row(gla_norm[o]), chunk=tl["gla_chunk"])
            ys, w_out, w_layer = [yc.reshape(t, -1)], wo_c, o
        xf = _out_ln(ys, w_out, xf, g4, b4, i, w_layer, 1, alpha=alpha, tm=tl["row_tm"])
        xf = _ffn_ln(xf, wfi, wfo, g4, b4, i, 1, 2, alpha=alpha, tm=tl["ffn_tm"], tf=tl["ffn_tf"])
        xf = _ple(xf, p, wpp, wpg, i, tm=tl["row_tm"])
    return xf.reshape(bsz, s, d)
```

```python
import functools
import math

import jax
import jax.numpy as jnp
from jax import lax
from jax.experimental import pallas as pl
from jax.experimental.pallas import tpu as pltpu

F32 = jnp.float32
BF16 = jnp.bfloat16
HIGHEST = lax.Precision.HIGHEST

A_HEADS = 8
A_QK_DIM = 64
A_V_DIM = 128
B_HEADS = 4
B_QK_DIM = 128
B_V_DIM = 256
CONV_WIDTH = 4
C_HEADS = 4
C_QK_DIM = 256
C_V_DIM = 512
C_GATE_TEMP = 16.0
GLA_SUB = 64
N_BUCKETS = 32
MAX_DISTANCE = 128
LN_EPS = 1e-5
NORM_EPS = 1e-6

V7X_VMEM_LIMIT_BYTES = 56 * 1024 * 1024
LANES = 128
SUBLANES = 8
BF16_SUBLANES = 16
ATTN_ONES_ROWS = BF16_SUBLANES
MASKED_LOGIT = -1e30
ROW_SPLIT = 4

NT_DIMS = (((1,), (1,)), ((), ()))
TN_DIMS = (((0,), (0,)), ((), ()))


def _params(*sem):
    return pltpu.CompilerParams(dimension_semantics=sem, vmem_limit_bytes=V7X_VMEM_LIMIT_BYTES)


def _layer_norm(y, g, b, eps=LN_EPS):
    mu = jnp.mean(y, -1, keepdims=True)
    yc = y - mu
    var = jnp.mean(yc * yc, -1, keepdims=True)
    return yc * lax.rsqrt(var + eps) * g + b


def _sigmoid(x):
    return 1.0 / (1.0 + jnp.exp(-x))


def _log_sigmoid(x):
    return jnp.minimum(x, 0.0) - jnp.log(1.0 + jnp.exp(-jnp.abs(x)))


def _ffn_ln_kernel(x_ref, wg0_ref, wu0_ref, wo0_ref, wg1_ref, wu1_ref, wo1_ref, g_ref, b_ref, o_ref, xb_ref,
                   *, alpha, n_chunks):
    j = pl.program_id(1)
    last = pl.num_programs(1) - 1

    @pl.when(j == 0)
    def _():
        x = x_ref[...]
        xb_ref[...] = x.astype(BF16)
        o_ref[...] = (2.0 * alpha) * x

    def chunk(wg_ref, wu_ref, wo_ref):
        xb = xb_ref[...]
        gate = jnp.dot(xb, wg_ref[...], preferred_element_type=F32)
        up = jnp.dot(xb, wu_ref[...], preferred_element_type=F32)
        h = (gate * _sigmoid(gate) * up).astype(BF16)
        o_ref[...] += jnp.dot(h, wo_ref[...], preferred_element_type=F32)

    if n_chunks % 2 == 0:
        chunk(wg0_ref, wu0_ref, wo0_ref)
        chunk(wg1_ref, wu1_ref, wo1_ref)
    else:
        @pl.when(j < last)
        def _():
            chunk(wg0_ref, wu0_ref, wo0_ref)
            chunk(wg1_ref, wu1_ref, wo1_ref)

        @pl.when(j == last)
        def _():
            chunk(wg0_ref, wu0_ref, wo0_ref)

    @pl.when(j == last)
    def _():
        o_ref[...] = _layer_norm(o_ref[...], g_ref[...], b_ref[...], eps=4.0 * LN_EPS)


def _ffn_ln(x, w_in, w_out, w_idx, ln_g, ln_b, layer, ln_slot, *, alpha, tm, tf):
    t, d = x.shape
    f = w_out.shape[1]
    nf = f // tf
    assert t % tm == 0 and f % tf == 0 and w_in.shape[1:] == (d, 2 * f) and nf >= 2
    steps = (nf + 1) // 2
    first = lambda j: 2 * j
    second = lambda j: jnp.where(2 * j + 1 < nf, 2 * j + 1, nf - 2)
    ln_spec = pl.BlockSpec((None, None, 1, d), lambda i, j: (layer, ln_slot, 0, 0))

    def weight_specs(chunk_of):
        return [pl.BlockSpec((None, d, tf), lambda i, j: (w_idx, 0, chunk_of(j))),
                pl.BlockSpec((None, d, tf), lambda i, j: (w_idx, 0, nf + chunk_of(j))),
                pl.BlockSpec((None, tf, d), lambda i, j: (w_idx, chunk_of(j), 0))]

    return pl.pallas_call(
        functools.partial(_ffn_ln_kernel, alpha=alpha, n_chunks=nf),
        grid=(t // tm, steps),
        in_specs=[pl.BlockSpec((tm, d), lambda i, j: (i, 0))] + weight_specs(first) + weight_specs(second)
                 + [ln_spec, ln_spec],
        out_specs=pl.BlockSpec((tm, d), lambda i, j: (i, 0)),
        out_shape=jax.ShapeDtypeStruct((t, d), F32),
        scratch_shapes=[pltpu.VMEM((tm, d), BF16)],
        compiler_params=_params("parallel", "arbitrary"),
        name="ffn_ln",
    )(x, w_in, w_in, w_out, w_in, w_in, w_out, ln_g, ln_b)


def _in_proj_kernel(x_ref, w_ref, wgc_ref, wgr_ref, y_ref, gc_ref, gr_ref, xb_ref):
    j = pl.program_id(1)

    @pl.when(j == 0)
    def _():
        xb = x_ref[...].astype(BF16)
        xb_ref[...] = xb
        gc_ref[...] = jnp.dot(xb, wgc_ref[...], preferred_element_type=F32)
        gr_ref[...] = lax.dot_general(wgr_ref[...], xb, NT_DIMS, preferred_element_type=F32)

    y_ref[...] = lax.dot_general(xb_ref[...], w_ref[...].astype(BF16), NT_DIMS,
                                 preferred_element_type=F32).astype(y_ref.dtype)


def _in_proj(x, w_stack, layer, n_main, *, tm, tn):
    t, d = x.shape
    assert t % tm == 0 and n_main % tn == 0
    w_t = jnp.swapaxes(w_stack, 1, 2)
    w_gates = w_stack[layer, :, n_main:].astype(BF16)
    w_gate_cols = _pad_cols(w_gates, LANES)
    w_gate_rows = jnp.transpose(w_gates[:, :SUBLANES])
    return pl.pallas_call(
        _in_proj_kernel,
        grid=(t // tm, n_main // tn),
        in_specs=[
            pl.BlockSpec((tm, d), lambda i, j: (i, 0)),
            pl.BlockSpec((None, tn, d), lambda i, j: (layer, j, 0)),
            pl.BlockSpec((d, LANES), lambda i, j: (0, 0)),
            pl.BlockSpec((SUBLANES, d), lambda i, j: (0, 0)),
        ],
        out_specs=[
            pl.BlockSpec((tm, tn), lambda i, j: (i, j)),
            pl.BlockSpec((tm, LANES), lambda i, j: (i, 0)),
            pl.BlockSpec((SUBLANES, tm), lambda i, j: (0, i)),
        ],
        out_shape=[
            jax.ShapeDtypeStruct((t, n_main), BF16),
            jax.ShapeDtypeStruct((t, LANES), F32),
            jax.ShapeDtypeStruct((SUBLANES, t), F32),
        ],
        scratch_shapes=[pltpu.VMEM((tm, d), BF16)],
        compiler_params=_params("parallel", "arbitrary"),
        name="in_proj",
    )(x, w_t, w_gate_cols, w_gate_rows)


def _diff_attn_kernel(q_ref, k_ref, v_ref, tab_ref, lam_ref, g_ref, wa_ref, wb_ref, o_ref, wa_bf_ref, wb_bf_ref,
                      vt_ref, bdiag_ref, bnear_ref, m_ref, acc_ref,
                      s0_ref, mx0_ref, sh0_ref, s1_ref, mx1_ref, sh1_ref, *, t, lam_init):
    qi = pl.program_id(2)
    dk, dv = A_QK_DIM, A_V_DIM
    nk = k_ref.shape[0] // t

    wa_bf_ref[...] = wa_ref[...].astype(BF16)
    wb_bf_ref[...] = wb_ref[...].astype(BF16)
    bufs = ((s0_ref, mx0_ref, sh0_ref), (s1_ref, mx1_ref, sh1_ref))

    @pl.when(qi == 0)
    def _():
        for c in range(nk):
            vt_ref[c, 0:dv, :] = jnp.transpose(v_ref[c * t:(c + 1) * t, :].astype(F32)).astype(BF16)
            vt_ref[c, dv:, :] = jnp.ones((vt_ref.shape[1] - dv, t), BF16)
        skew = pltpu.roll(jnp.broadcast_to(tab_ref[...], (t, 2 * t)), 0, 1, stride=1, stride_axis=0)
        key = lax.broadcasted_iota(jnp.int32, (t, t), 0)
        qry = lax.broadcasted_iota(jnp.int32, (t, t), 1)
        bdiag_ref[...] = jnp.where(qry >= key, skew[:, :t], MASKED_LOGIT)
        bnear_ref[...] = skew[:, t:]

    q_t = jnp.transpose(q_ref[...].astype(F32)) * (dk ** -0.5)
    feat = lax.broadcasted_iota(jnp.int32, q_t.shape, 0)
    qs_t = jnp.concatenate([jnp.where(feat < dk, q_t, 0.0), jnp.where(feat >= dk, q_t, 0.0)],
                           axis=1).astype(BF16)

    m_ref[...] = jnp.full_like(m_ref, -jnp.inf)
    acc_ref[...] = jnp.zeros_like(acc_ref)

    def score(kj, buf, bias_ref, shift):
        s_ref, mx_ref, sh_ref = bufs[buf]
        off = pl.multiple_of(kj * t, t)
        s = jnp.dot(k_ref[pl.ds(off, t), :], qs_t, preferred_element_type=F32)
        if bias_ref is not None:
            b = bias_ref[...]
            s = s + jnp.concatenate([b, b], axis=1)
        s_ref[...] = s
        mx_ref[...] = jnp.max(s, 0, keepdims=True) + shift
        sh_ref[...] = jnp.zeros_like(sh_ref) + shift

    def absorb(kj, buf):
        s_ref, mx_ref, sh_ref = bufs[buf]
        m_old = m_ref[...]
        m_new = jnp.maximum(m_old, mx_ref[...])
        p = jnp.exp(s_ref[...] - (m_new - sh_ref[...])).astype(BF16)
        acc_ref[...] = (jnp.exp(m_old - m_new) * acc_ref[...]
                        + jnp.dot(vt_ref[kj], p, preferred_element_type=F32))
        m_ref[...] = m_new

    c_far = tab_ref[:, 2 * t - 1:2 * t]
    score(qi, 0, bdiag_ref, 0.0)

    @pl.when(qi >= 1)
    def _():
        score(qi - 1, 1, bnear_ref, 0.0)
        absorb(qi, 0)

    def far_pair(p, carry):
        kj = qi - 2 - 2 * p
        score(kj, 0, None, c_far)
        absorb(kj + 1, 1)
        score(kj - 1, 1, None, c_far)
        absorb(kj, 0)
        return carry

    lax.fori_loop(0, jnp.maximum(qi - 1, 0) // 2, far_pair, 0)

    @pl.when((qi >= 2) & (qi % 2 == 0))
    def _():
        score(0, 0, None, c_far)
        absorb(1, 1)

    @pl.when(qi % 2 == 0)
    def _():
        absorb(0, 0)

    @pl.when(qi % 2 == 1)
    def _():
        absorb(0, 1)

    lv = lam_ref[...]
    lam = (jnp.exp(jnp.sum(lv[0:1] * lv[1:2], -1, keepdims=True))
           - jnp.exp(jnp.sum(lv[2:3] * lv[3:4], -1, keepdims=True)) + lam_init)
    acc = acc_ref[...]
    o_t = acc[0:dv] * (1.0 / acc[dv:dv + 1])
    out = jnp.transpose(o_t[:, :t] - lam * o_t[:, t:])
    hn = out * lax.rsqrt(jnp.mean(out * out, -1, keepdims=True) + NORM_EPS)
    o_ref[...] = (hn * g_ref[...] * (1.0 - lam_init)).astype(o_ref.dtype)


def _t5_bias_by_distance(rel_bias, n):
    r = jnp.arange(n, dtype=jnp.int32)
    max_exact = N_BUCKETS // 2
    rf = jnp.maximum(r, 1).astype(F32)
    large = max_exact + (jnp.log(rf / max_exact) / math.log(MAX_DISTANCE / max_exact)
                         * (N_BUCKETS - max_exact)).astype(jnp.int32)
    large = jnp.minimum(large, N_BUCKETS - 1)
    bucket = jnp.where(r < max_exact, r, large)
    return jnp.transpose(rel_bias[bucket]).astype(F32)


def _diff_attention(y3, rel_bias, lam_vecs, diff_g, cast_a, cast_b, *, lam_init, t):
    bsz, s, _ = y3.shape
    hd = 2 * A_QK_DIM
    dv = A_V_DIM
    assert hd == dv == LANES and s % t == 0 and t >= MAX_DISTANCE
    nq = s // t
    table = _t5_bias_by_distance(rel_bias, 2 * t).reshape(A_HEADS, 1, 2 * t)
    jobs = (cast_a, cast_b)
    assert all(blocks <= bsz * A_HEADS * nq for _, _, _, blocks in jobs)

    def job_block(blocks):
        return lambda b, h, q: jnp.minimum((b * A_HEADS + h) * nq + q, blocks - 1)

    cast_in = [pl.BlockSpec((rows, w.shape[1]), lambda b, h, q, f=first, blk=job_block(blocks): (f + blk(b, h, q), 0))
               for w, rows, first, blocks in jobs]
    cast_out = [pl.BlockSpec((rows, w.shape[1]), lambda b, h, q, blk=job_block(blocks): (blk(b, h, q), 0))
                for w, rows, first, blocks in jobs]
    cast_shapes = [jax.ShapeDtypeStruct((rows * blocks, w.shape[1]), BF16) for w, rows, first, blocks in jobs]
    return pl.pallas_call(
        functools.partial(_diff_attn_kernel, t=t, lam_init=lam_init),
        grid=(bsz, A_HEADS, nq),
        in_specs=[
            pl.BlockSpec((None, t, hd), lambda b, h, q: (b, q, h)),
            pl.BlockSpec((None, s, hd), lambda b, h, q: (b, 0, A_HEADS + h)),
            pl.BlockSpec((None, s, dv), lambda b, h, q: (b, 0, 2 * A_HEADS + h)),
            pl.BlockSpec((None, 1, 2 * t), lambda b, h, q: (h, 0, 0)),
            pl.BlockSpec((4, A_QK_DIM), lambda b, h, q: (0, 0)),
            pl.BlockSpec((1, dv), lambda b, h, q: (0, h)),
        ] + cast_in,
        out_specs=[pl.BlockSpec((None, t, dv), lambda b, h, q: (b, q, h))] + cast_out,
        out_shape=[jax.ShapeDtypeStruct((bsz, s, A_HEADS * dv), BF16)] + cast_shapes,
        scratch_shapes=[pltpu.VMEM((nq, dv + ATTN_ONES_ROWS, t), BF16),
                        pltpu.VMEM((t, t), F32),
                        pltpu.VMEM((t, t), F32),
                        pltpu.VMEM((1, 2 * t), F32),
                        pltpu.VMEM((dv + ATTN_ONES_ROWS, 2 * t), F32)]
                       + 2 * [pltpu.VMEM((t, 2 * t), F32), pltpu.VMEM((1, 2 * t), F32),
                              pltpu.VMEM((1, 2 * t), F32)],
        compiler_params=_params("arbitrary", "arbitrary", "arbitrary"),
        name="diff_attn",
    )(y3, y3, y3, table, lam_vecs, diff_g, cast_a[0], cast_b[0])


def _mlstm_kernel(qk_ref, v_ref, ob_ref, gc_ref, gr_ref, cw_ref, cb_ref, bc_ref, br_ref, g_ref, y_ref,
                  xp_ref, *state_refs, chunk):
    heads, dk, dv = B_HEADS, B_QK_DIM, B_V_DIM
    c_refs, n_refs, m_refs = state_refs[0::3], state_refs[1::3], state_refs[2::3]
    pad = xp_ref.shape[0] - chunk
    c = pl.program_id(1)

    @pl.when(c == 0)
    def _():
        xp_ref[0:pad, :] = jnp.zeros((pad, xp_ref.shape[1]), xp_ref.dtype)
        for ref in state_refs:
            ref[...] = jnp.zeros_like(ref)

    x = qk_ref[...]
    xp_ref[pad:pad + chunk, :] = x
    xp = xp_ref[...]
    cw = cw_ref[...]
    dst = lax.broadcasted_iota(jnp.int32, (chunk, pad + chunk), 0)
    src = lax.broadcasted_iota(jnp.int32, (chunk, pad + chunk), 1)
    conv = cb_ref[...] + cw[CONV_WIDTH - 1:CONV_WIDTH, :] * x.astype(F32)
    for j in range(CONV_WIDTH - 1):
        shift = (src == dst + (pad - (CONV_WIDTH - 1) + j)).astype(BF16)
        conv = conv + cw[j:j + 1, :] * jnp.dot(shift, xp, preferred_element_type=F32)
    xp_ref[0:pad, :] = x[chunk - pad:chunk, :]
    qk = conv * _sigmoid(conv)

    gc = gc_ref[...] + bc_ref[...]
    gr = gr_ref[...] + br_ref[...]
    row = lax.broadcasted_iota(jnp.int32, (chunk, chunk), 0)
    col = lax.broadcasted_iota(jnp.int32, (chunk, chunk), 1)
    causal = row >= col
    tril = causal.astype(F32)
    triu = (row <= col).astype(F32)
    b_c = jnp.dot(tril, _log_sigmoid(gc), precision=HIGHEST, preferred_element_type=F32)
    b_r = jnp.dot(_log_sigmoid(gr), triu, precision=HIGHEST, preferred_element_type=F32)

    ob = ob_ref[...].astype(F32)
    g = g_ref[...]
    hs = range(heads)
    q = [qk[:, h * dk:(h + 1) * dk] * (dk ** -0.5) for h in hs]
    k = [qk[:, (heads + h) * dk:(heads + h + 1) * dk] for h in hs]
    v = [v_ref[:, h * dv:(h + 1) * dv] for h in hs]
    b_col = [b_c[:, heads + h:heads + h + 1] for h in hs]
    i_col = [gc[:, h:h + 1] for h in hs]
    b_row = [b_r[heads + h:heads + h + 1, :] for h in hs]
    i_row = [gr[h:h + 1, :] for h in hs]
    m_prev = [m_refs[h][0:1, 0:1] for h in hs]
    c_mat = [c_refs[h][...] for h in hs]
    n_vec = [n_refs[h][...] for h in hs]

    b_last = [b_row[h][:, chunk - 1:chunk] for h in hs]
    g_col = [b_last[h] - b_col[h] + i_col[h] for h in hs]
    m_new = [jnp.maximum(b_last[h] + m_prev[h], jnp.max(g_col[h], 0, keepdims=True)) for h in hs]
    decay = [jnp.exp(b_last[h] + m_prev[h] - m_new[h]) for h in hs]
    kw = [k[h] * jnp.exp(g_col[h] - m_new[h]) for h in hs]
    qb = [q[h].astype(BF16) for h in hs]
    qk_t = [lax.dot_general(qb[h], k[h].astype(BF16), NT_DIMS, preferred_element_type=F32) for h in hs]
    q_c = [jnp.dot(qb[h], c_mat[h].astype(BF16), preferred_element_type=F32) for h in hs]
    kw_v = [lax.dot_general(kw[h].astype(BF16), v[h], TN_DIMS, preferred_element_type=F32) for h in hs]

    d = [jnp.where(causal, b_col[h] - b_row[h] + i_row[h], -jnp.inf) for h in hs]
    a_col = [b_col[h] + m_prev[h] for h in hs]
    m_t = [jnp.maximum(a_col[h], jnp.max(d[h], -1, keepdims=True)) for h in hs]
    w_inter = [jnp.exp(a_col[h] - m_t[h]) for h in hs]
    sw = [qk_t[h] * jnp.exp(d[h] - m_t[h]) for h in hs]
    sw_v = [jnp.dot(sw[h].astype(BF16), v[h], preferred_element_type=F32) for h in hs]

    for h in hs:
        c_refs[h][...] = decay[h] * c_mat[h] + kw_v[h]
        n_refs[h][...] = decay[h] * n_vec[h] + jnp.sum(kw[h], 0, keepdims=True)
        m_refs[h][...] = jnp.broadcast_to(m_new[h], m_refs[h].shape)

    for h in hs:
        num = w_inter[h] * q_c[h] + sw_v[h]
        den = (w_inter[h] * jnp.sum(q[h] * n_vec[h], -1, keepdims=True)
               + jnp.sum(sw[h], -1, keepdims=True))
        hh = num / jnp.maximum(jnp.abs(den), jnp.exp(-m_t[h]))
        hn = hh * lax.rsqrt(jnp.mean(hh * hh, -1, keepdims=True) + NORM_EPS)
        sl = slice(h * dv, (h + 1) * dv)
        y_ref[:, sl] = (hn * g[:, sl] * _sigmoid(ob[:, sl])).astype(y_ref.dtype)


def _mlstm(y3, gcol3, grow3, conv_w, conv_b, gate_bias_cols, gate_bias_rows, mlstm_g, *, chunk):
    bsz, s, _ = y3.shape
    heads, dk, dv = B_HEADS, B_QK_DIM, B_V_DIM
    w = heads * dv
    assert 2 * heads * dk == w and s % chunk == 0 and chunk % LANES == 0
    nc = s // chunk
    return pl.pallas_call(
        functools.partial(_mlstm_kernel, chunk=chunk),
        grid=(bsz, nc),
        in_specs=[
            pl.BlockSpec((None, chunk, w), lambda b, c: (b, c, 3)),
            pl.BlockSpec((None, chunk, w), lambda b, c: (b, c, 4)),
            pl.BlockSpec((None, chunk, w), lambda b, c: (b, c, 5)),
            pl.BlockSpec((None, chunk, LANES), lambda b, c: (b, c, 0)),
            pl.BlockSpec((SUBLANES, chunk), lambda b, c: (0, b * nc + c)),
            pl.BlockSpec((CONV_WIDTH, w), lambda b, c: (0, 0)),
            pl.BlockSpec((1, w), lambda b, c: (0, 0)),
            pl.BlockSpec((1, LANES), lambda b, c: (0, 0)),
            pl.BlockSpec((SUBLANES, 1), lambda b, c: (0, 0)),
            pl.BlockSpec((1, w), lambda b, c: (0, 0)),
        ],
        out_specs=pl.BlockSpec((None, chunk, w), lambda b, c: (b, c, 0)),
        out_shape=jax.ShapeDtypeStruct((bsz, s, w), BF16),
        scratch_shapes=[
            pltpu.VMEM((BF16_SUBLANES + chunk, w), BF16),
        ] + heads * [pltpu.VMEM((dk, dv), F32), pltpu.VMEM((1, dk), F32), pltpu.VMEM((SUBLANES, LANES), F32)],
        compiler_params=_params("parallel", "arbitrary"),
        name="mlstm",
    )(y3, y3, y3, gcol3, grow3, conv_w, conv_b, gate_bias_cols, gate_bias_rows, mlstm_g)


def _gla_kernel(q_ref, k_ref, v_ref, r_ref, a1_ref, a1n_ref, wah_ref, wal_ref, ba_ref, g_ref, y_ref,
                b_ref, *st_refs, chunk):
    heads, dk, dv = C_HEADS, C_QK_DIM, C_V_DIM
    sub = GLA_SUB
    assert chunk == 2 * sub
    c = pl.program_id(1)
    row = lax.broadcasted_iota(jnp.int32, (chunk, chunk), 0)
    col = lax.broadcasted_iota(jnp.int32, (chunk, chunk), 1)
    causal = (row >= col) & ((row < sub) == (col < sub))
    first = lax.broadcasted_iota(jnp.int32, (chunk, 1), 0) < sub

    def cum_log_decay(a1):
        a_hi = a1.astype(BF16)
        a_lo = (a1 - a_hi.astype(F32)).astype(BF16)
        w_hi = wah_ref[...]
        z = (jnp.dot(a_hi, w_hi, preferred_element_type=F32)
             + (jnp.dot(a_lo, w_hi, preferred_element_type=F32)
                + jnp.dot(a_hi, wal_ref[...], preferred_element_type=F32))) + ba_ref[...]
        log_a = _log_sigmoid(z) * (1.0 / C_GATE_TEMP)
        return jnp.dot(causal.astype(F32), log_a, precision=HIGHEST, preferred_element_type=F32)

    @pl.when(c == 0)
    def _():
        for ref in st_refs:
            ref[...] = jnp.zeros_like(ref)
        b_ref[...] = cum_log_decay(a1_ref[...])

    b_all = b_ref[...]
    b_next = cum_log_decay(a1n_ref[...])

    r = r_ref[...].astype(F32)
    g = g_ref[...]
    for h in range(heads):
        ks = slice(h * dk, (h + 1) * dk)
        vs = slice(h * dv, (h + 1) * dv)
        b = b_all[:, ks]
        q = q_ref[:, ks].astype(F32) * (dk ** -0.5)
        k = k_ref[:, ks].astype(F32)
        v = v_ref[:, vs]
        tot0 = b[sub - 1:sub, :]
        tot1 = b[chunk - 1:chunk, :]
        q_dec = q * jnp.exp(b)
        k_dec = (k * jnp.exp(-b)).astype(BF16)
        k_end = k * jnp.exp(jnp.where(first, tot0, tot1) - b)
        k_new = (k_end * jnp.where(first, jnp.exp(tot1), 1.0)).astype(BF16)
        q_int = (q_dec * jnp.where(first, 1.0, jnp.exp(tot0))).astype(BF16)
        q_dec = q_dec.astype(BF16)
        st = st_refs[h][...]
        qk_t = lax.dot_general(q_dec, k_dec, NT_DIMS, preferred_element_type=F32)
        inter = lax.dot_general(q_int, st.astype(BF16), NT_DIMS, preferred_element_type=F32)
        st_refs[h][...] = (jnp.exp(tot0 + tot1) * st
                           + lax.dot_general(v, k_new, TN_DIMS, preferred_element_type=F32))
        cross = lax.dot_general(q_dec[sub:], k_end[:sub].astype(BF16), NT_DIMS, preferred_element_type=F32)
        att = jnp.where(causal, qk_t, 0.0)
        o = jnp.dot(att.astype(BF16), v, preferred_element_type=F32) + inter
        o_cross = jnp.dot(cross.astype(BF16), v[:sub], preferred_element_type=F32)
        o = jnp.concatenate([o[:sub], o[sub:] + o_cross], axis=0)

        on = o * lax.rsqrt(jnp.mean(o * o, -1, keepdims=True) + NORM_EPS)
        rr = r[:, vs]
        y_ref[:, vs] = (on * g[:, vs] * (rr * _sigmoid(rr))).astype(y_ref.dtype)

    b_ref[...] = b_next


def _gla(y3, a13, w_a2_hi, w_a2_lo, b_a, gla_g, *, chunk):
    bsz, s, _ = y3.shape
    heads, dk, dv = C_HEADS, C_QK_DIM, C_V_DIM
    wk, wv = heads * dk, heads * dv
    assert wv == 2 * wk and s % chunk == 0
    nc = s // chunk
    return pl.pallas_call(
        functools.partial(_gla_kernel, chunk=chunk),
        grid=(bsz, nc),
        in_specs=[
            pl.BlockSpec((None, chunk, wk), lambda b, c: (b, c, 0)),
            pl.BlockSpec((None, chunk, wk), lambda b, c: (b, c, 1)),
            pl.BlockSpec((None, chunk, wv), lambda b, c: (b, c, 1)),
            pl.BlockSpec((None, chunk, wv), lambda b, c: (b, c, 2)),
            pl.BlockSpec((None, chunk, LANES), lambda b, c: (b, c, 0)),
            pl.BlockSpec((None, chunk, LANES), lambda b, c: (b, jnp.minimum(c + 1, nc - 1), 0)),
            pl.BlockSpec((LANES, wk), lambda b, c: (0, 0)),
            pl.BlockSpec((LANES, wk), lambda b, c: (0, 0)),
            pl.BlockSpec((1, wk), lambda b, c: (0, 0)),
            pl.BlockSpec((1, wv), lambda b, c: (0, 0)),
        ],
        out_specs=pl.BlockSpec((None, chunk, wv), lambda b, c: (b, c, 0)),
        out_shape=jax.ShapeDtypeStruct((bsz, s, wv), BF16),
        scratch_shapes=[pltpu.VMEM((chunk, wk), F32)] + heads * [pltpu.VMEM((dv, dk), F32)],
        compiler_params=_params("parallel", "arbitrary"),
        name="gla",
    )(y3, y3, y3, y3, a13, a13, w_a2_hi, w_a2_lo, b_a, gla_g)


def _out_ln_kernel(*refs, n_in, alpha):
    y_refs, w_refs = refs[:n_in], refs[n_in:2 * n_in]
    x_ref, g_ref, b_ref, o_ref = refs[2 * n_in:]
    rb = o_ref.shape[0] // ROW_SPLIT
    for r in range(ROW_SPLIT):
        rows = slice(r * rb, (r + 1) * rb)
        acc = jnp.dot(y_refs[0][rows, :], w_refs[0][...], preferred_element_type=F32)
        for y_ref, w_ref in zip(y_refs[1:], w_refs[1:]):
            acc = acc + jnp.dot(y_ref[rows, :], w_ref[...], preferred_element_type=F32)
        o_ref[rows, :] = _layer_norm(alpha * x_ref[rows, :] + acc, g_ref[...], b_ref[...])


def _out_ln(ys, w, x, ln_g, ln_b, layer, w_layer, ln_slot, *, alpha, tm):
    t, d = x.shape
    n_in = len(ys)
    wd = ys[0].shape[1]
    assert all(y.shape[1] == wd for y in ys) and w.shape[1:] == (n_in * wd, d)
    ln_spec = pl.BlockSpec((None, None, 1, d), lambda i: (layer, ln_slot, 0, 0))
    in_specs = ([pl.BlockSpec((tm, wd), lambda i: (i, 0)) for _ in ys]
                + [pl.BlockSpec((None, wd, d), lambda i, k=k: (w_layer, k, 0)) for k in range(n_in)]
                + [pl.BlockSpec((tm, d), lambda i: (i, 0)), ln_spec, ln_spec])
    return pl.pallas_call(
        functools.partial(_out_ln_kernel, n_in=n_in, alpha=alpha),
        grid=(t // tm,),
        in_specs=in_specs,
        out_specs=pl.BlockSpec((tm, d), lambda i: (i, 0)),
        out_shape=jax.ShapeDtypeStruct((t, d), F32),
        compiler_params=_params("parallel"),
        name="out_ln",
    )(*ys, *([w] * n_in), x, ln_g, ln_b)


def _ple_kernel(x_ref, p_ref, wp_ref, wg_ref, o_ref):
    rb = o_ref.shape[0] // ROW_SPLIT
    for r in range(ROW_SPLIT):
        rows = slice(r * rb, (r + 1) * rb)
        x = x_ref[rows, :]
        e = jnp.dot(p_ref[rows, :].astype(BF16), wp_ref[...], preferred_element_type=F32)
        gate = jnp.dot(x.astype(BF16), wg_ref[...], preferred_element_type=F32)
        o_ref[rows, :] = x + e * _sigmoid(gate)


def _ple(x, p, w_proj, w_gate, layer, *, tm):
    t, d = x.shape
    s, pd = p.shape[2:]
    assert s % tm == 0
    per_seq = s // tm
    return pl.pallas_call(
        _ple_kernel,
        grid=(t // tm,),
        in_specs=[
            pl.BlockSpec((tm, d), lambda i: (i, 0)),
            pl.BlockSpec((None, None, tm, pd), lambda i: (layer, i // per_seq, i % per_seq, 0)),
            pl.BlockSpec((None, pd, d), lambda i: (layer, 0, 0)),
            pl.BlockSpec((None, d, d), lambda i: (layer, 0, 0)),
        ],
        out_specs=pl.BlockSpec((tm, d), lambda i: (i, 0)),
        out_shape=jax.ShapeDtypeStruct((t, d), F32),
        compiler_params=_params("parallel"),
        name="ple",
    )(x, p, w_proj, w_gate)


def _tiles(t, s):
    return dict(
        ffn_tm=min(1024, t), ffn_tf=256,
        proj_tm=min(1024, t), proj_tn=1024,
        row_tm=min(512, t),
        attn_t=min(512, s),
        mlstm_chunk=min(128, s),
        gla_chunk=min(2 * GLA_SUB, s),
    )


def _cast_job(w2d, rows_per_slot, n_slots, steps):
    rows = next(r for r in range(BF16_SUBLANES, rows_per_slot + 1, BF16_SUBLANES)
                if rows_per_slot % r == 0 and (n_slots - 1) * (rows_per_slot // r) <= steps)
    return w2d, rows, rows_per_slot // rows, (n_slots - 1) * (rows_per_slot // rows)


def _pad_cols(w, n):
    return jnp.pad(w, ((0, 0), (0, n - w.shape[1])))


def kernel(x, p, ln_g, ln_b, w_ffn_in, w_ffn_out, w_in_ab, w_out_ab, rel_bias, lambda_q1, lambda_k1,
           lambda_q2, lambda_k2, diff_norm, conv_w, conv_b, b_igate, b_fgate, mlstm_norm, w_in_c,
           w_alpha2, b_alpha, gla_norm, w_out_c, w_ple_proj, w_ple_gate):
    bsz, s, d = x.shape
    depth = p.shape[0]
    t = bsz * s
    tl = _tiles(t, s)
    alpha = (2 * depth) ** 0.25
    a_w = A_HEADS * A_V_DIM
    ab_main = 3 * a_w + 3 * B_HEADS * B_V_DIM
    c_main = 2 * C_HEADS * C_QK_DIM + 2 * C_HEADS * C_V_DIM
    row = lambda a: a.reshape(1, -1)

    n_ffn, f_ff = 2 * depth, w_ffn_out.shape[2]
    attn_steps = bsz * A_HEADS * (s // tl["attn_t"])
    wfi_first, wfo_first = w_ffn_in[0, :1].astype(BF16), w_ffn_out[0, :1].astype(BF16)
    cast_a = _cast_job(w_ffn_in.reshape(n_ffn * d, -1), d, n_ffn, attn_steps)
    cast_b = _cast_job(w_ffn_out.reshape(n_ffn * f_ff, d), f_ff, n_ffn, attn_steps)
    ffn_w = {0: (wfi_first, wfo_first, 0)}
    wo_ab, wo_c = w_out_ab.astype(BF16), w_out_c.astype(BF16)
    wpp, wpg = w_ple_proj.astype(BF16), w_ple_gate.astype(BF16)
    g4, b4 = ln_g.reshape(depth, 3, 1, d), ln_b.reshape(depth, 3, 1, d)

    xf = x.reshape(t, d)
    for i in range(depth):
        xf = _ffn_ln(xf, *ffn_w[2 * i], g4, b4, i, 0, alpha=alpha, tm=tl["ffn_tm"], tf=tl["ffn_tf"])
        if i % 2 == 0:
            e = i // 2
            lam_init = 0.8 - 0.6 * math.exp(-0.3 * i)
            y, gcol, grow = _in_proj(xf, w_in_ab, e, ab_main, tm=tl["proj_tm"], tn=tl["proj_tn"])
            y3 = y.reshape(bsz, s, ab_main)
            lam_vecs = jnp.stack([lambda_q1[e], lambda_k1[e], lambda_q2[e], lambda_k2[e]])
            ya, wfi_rest, wfo_rest = _diff_attention(y3, rel_bias, lam_vecs, row(diff_norm[e]), cast_a, cast_b,
                                                     lam_init=lam_init, t=tl["attn_t"])
            if i == 0:
                wfi_rest = wfi_rest.reshape(n_ffn - 1, d, -1)
                wfo_rest = wfo_rest.reshape(n_ffn - 1, f_ff, d)
                ffn_w.update({n: (wfi_rest, wfo_rest, n - 1) for n in range(1, n_ffn)})
            gate_bias = jnp.concatenate([b_igate[e], b_fgate[e]])
            yb = _mlstm(y3, gcol.reshape(bsz, s, LANES), grow,
                        conv_w[e], row(conv_b[e]), _pad_cols(row(gate_bias), LANES), gate_bias.reshape(-1, 1),
                        row(mlstm_norm[e]), chunk=tl["mlstm_chunk"])
            ys, w_out, w_layer = [ya.reshape(t, a_w), yb.reshape(t, -1)], wo_ab, e
        else:
            o = i // 2
            y, a1, _ = _in_proj(xf, w_in_c, o, c_main, tm=tl["proj_tm"], tn=tl["proj_tn"])
            w_a2 = jnp.pad(w_alpha2[o], ((0, LANES - w_alpha2.shape[1]), (0, 0)))
            w_a2_hi = lax.reduce_precision(w_a2, exponent_bits=8, mantissa_bits=7)
            w_a2_lo = (w_a2 - w_a2_hi).astype(BF16)
            yc = _gla(y.reshape(bsz, s, c_main), a1.reshape(bsz, s, LANES), w_a2_hi.astype(BF16), w_a2_lo,
                      row(b_alpha[o]),
                      row(gla_norm[o]), chunk=tl["gla_chunk"])
            ys, w_out, w_layer = [yc.reshape(t, -1)], wo_c, o
        xf = _out_ln(ys, w_out, xf, g4, b4, i, w_layer, 1, alpha=alpha, tm=tl["row_tm"])
        xf = _ffn_ln(xf, *ffn_w[2 * i + 1], g4, b4, i, 2, alpha=alpha, tm=tl["ffn_tm"], tf=tl["ffn_tf"])
        xf = _ple(xf, p, wpp, wpg, i, tm=tl["row_tm"])
    return xf.reshape(bsz, s, d)
```

```python
import functools
import math

import jax
import jax.numpy as jnp
from jax import lax
from jax.experimental import pallas as pl
from jax.experimental.pallas import tpu as pltpu

F32 = jnp.float32
BF16 = jnp.bfloat16
HIGHEST = lax.Precision.HIGHEST

A_HEADS = 8
A_QK_DIM = 64
A_V_DIM = 128
B_HEADS = 4
B_QK_DIM = 128
B_V_DIM = 256
CONV_WIDTH = 4
C_HEADS = 4
C_QK_DIM = 256
C_V_DIM = 512
C_GATE_TEMP = 16.0
GLA_SUB = 64
N_BUCKETS = 32
MAX_DISTANCE = 128
LN_EPS = 1e-5
NORM_EPS = 1e-6

V7X_VMEM_LIMIT_BYTES = 56 * 1024 * 1024
LANES = 128
SUBLANES = 8
BF16_SUBLANES = 16
ATTN_ONES_ROWS = BF16_SUBLANES
MASKED_LOGIT = -1e30
ROW_SPLIT = 4

NT_DIMS = (((1,), (1,)), ((), ()))
TN_DIMS = (((0,), (0,)), ((), ()))


def _params(*sem):
    return pltpu.CompilerParams(dimension_semantics=sem, vmem_limit_bytes=V7X_VMEM_LIMIT_BYTES)


def _layer_norm(y, g, b, eps=LN_EPS):
    mu = jnp.mean(y, -1, keepdims=True)
    yc = y - mu
    var = jnp.mean(yc * yc, -1, keepdims=True)
    return yc * lax.rsqrt(var + eps) * g + b


def _sigmoid(x):
    return 1.0 / (1.0 + jnp.exp(-x))


def _log_sigmoid(x):
    return jnp.minimum(x, 0.0) - jnp.log(1.0 + jnp.exp(-jnp.abs(x)))


def _ffn_ln_kernel(x_ref, wg0_ref, wu0_ref, wo0_ref, wg1_ref, wu1_ref, wo1_ref, g_ref, b_ref, o_ref, xb_ref,
                   *, alpha, n_chunks):
    j = pl.program_id(1)
    last = pl.num_programs(1) - 1

    @pl.when(j == 0)
    def _():
        x = x_ref[...]
        xb_ref[...] = x.astype(BF16)
        o_ref[...] = (2.0 * alpha) * x

    def chunk(wg_ref, wu_ref, wo_ref):
        xb = xb_ref[...]
        gate = jnp.dot(xb, wg_ref[...], preferred_element_type=F32)
        up = jnp.dot(xb, wu_ref[...], preferred_element_type=F32)
        h = (gate * _sigmoid(gate) * up).astype(BF16)
        o_ref[...] += jnp.dot(h, wo_ref[...], preferred_element_type=F32)

    if n_chunks % 2 == 0:
        chunk(wg0_ref, wu0_ref, wo0_ref)
        chunk(wg1_ref, wu1_ref, wo1_ref)
    else:
        @pl.when(j < last)
        def _():
            chunk(wg0_ref, wu0_ref, wo0_ref)
            chunk(wg1_ref, wu1_ref, wo1_ref)

        @pl.when(j == last)
        def _():
            chunk(wg0_ref, wu0_ref, wo0_ref)

    @pl.when(j == last)
    def _():
        o_ref[...] = _layer_norm(o_ref[...], g_ref[...], b_ref[...], eps=4.0 * LN_EPS)


def _ffn_ln(x, w_in, w_out, w_idx, ln_g, ln_b, layer, ln_slot, *, alpha, tm, tf):
    t, d = x.shape
    f = w_out.shape[1]
    nf = f // tf
    assert t % tm == 0 and f % tf == 0 and w_in.shape[1:] == (d, 2 * f) and nf >= 2
    steps = (nf + 1) // 2
    first = lambda j: 2 * j
    second = lambda j: jnp.where(2 * j + 1 < nf, 2 * j + 1, nf - 2)
    ln_spec = pl.BlockSpec((None, None, 1, d), lambda i, j: (layer, ln_slot, 0, 0))

    def weight_specs(chunk_of):
        return [pl.BlockSpec((None, d, tf), lambda i, j: (w_idx, 0, chunk_of(j))),
                pl.BlockSpec((None, d, tf), lambda i, j: (w_idx, 0, nf + chunk_of(j))),
                pl.BlockSpec((None, tf, d), lambda i, j: (w_idx, chunk_of(j), 0))]

    return pl.pallas_call(
        functools.partial(_ffn_ln_kernel, alpha=alpha, n_chunks=nf),
        grid=(t // tm, steps),
        in_specs=[pl.BlockSpec((tm, d), lambda i, j: (i, 0))] + weight_specs(first) + weight_specs(second)
                 + [ln_spec, ln_spec],
        out_specs=pl.BlockSpec((tm, d), lambda i, j: (i, 0)),
        out_shape=jax.ShapeDtypeStruct((t, d), F32),
        scratch_shapes=[pltpu.VMEM((tm, d), BF16)],
        compiler_params=_params("parallel", "arbitrary"),
        name="ffn_ln",
    )(x, w_in, w_in, w_out, w_in, w_in, w_out, ln_g, ln_b)


def _in_proj_kernel(x_ref, w_ref, wgc_ref, wgr_ref, y_ref, gc_ref, gr_ref, xb_ref):
    j = pl.program_id(1)

    @pl.when(j == 0)
    def _():
        xb = x_ref[...].astype(BF16)
        xb_ref[...] = xb
        gc_ref[...] = jnp.dot(xb, wgc_ref[...], preferred_element_type=F32)
        gr_ref[...] = lax.dot_general(wgr_ref[...], xb, NT_DIMS, preferred_element_type=F32)

    y_ref[...] = lax.dot_general(xb_ref[...], w_ref[...].astype(BF16), NT_DIMS,
                                 preferred_element_type=F32).astype(y_ref.dtype)


def _in_proj(x, w_stack, layer, n_main, *, tm, tn, w_t=None):
    t, d = x.shape
    assert t % tm == 0 and n_main % tn == 0
    if w_t is None:
        w_t = jnp.swapaxes(w_stack, 1, 2)
    w_gates = w_stack[layer, :, n_main:].astype(BF16)
    w_gate_cols = _pad_cols(w_gates, LANES)
    w_gate_rows = jnp.transpose(w_gates[:, :SUBLANES])
    return pl.pallas_call(
        _in_proj_kernel,
        grid=(t // tm, n_main // tn),
        in_specs=[
            pl.BlockSpec((tm, d), lambda i, j: (i, 0)),
            pl.BlockSpec((None, tn, d), lambda i, j: (layer, j, 0)),
            pl.BlockSpec((d, LANES), lambda i, j: (0, 0)),
            pl.BlockSpec((SUBLANES, d), lambda i, j: (0, 0)),
        ],
        out_specs=[
            pl.BlockSpec((tm, tn), lambda i, j: (i, j)),
            pl.BlockSpec((tm, LANES), lambda i, j: (i, 0)),
            pl.BlockSpec((SUBLANES, tm), lambda i, j: (0, i)),
        ],
        out_shape=[
            jax.ShapeDtypeStruct((t, n_main), BF16),
            jax.ShapeDtypeStruct((t, LANES), F32),
            jax.ShapeDtypeStruct((SUBLANES, t), F32),
        ],
        scratch_shapes=[pltpu.VMEM((tm, d), BF16)],
        compiler_params=_params("parallel", "arbitrary"),
        name="in_proj",
    )(x, w_t, w_gate_cols, w_gate_rows)


def _diff_attn_kernel(q_ref, k_ref, v_ref, tab_ref, lam_ref, g_ref, *refs, t, lam_init, n_jobs):
    cast_in, (o_ref, *cast_out) = refs[:n_jobs], refs[n_jobs:2 * n_jobs + 1]
    (vt_ref, bdiag_ref, bnear_ref, m_ref, acc_ref,
     s0_ref, mx0_ref, sh0_ref, s1_ref, mx1_ref, sh1_ref) = refs[2 * n_jobs + 1:]
    qi = pl.program_id(2)
    dk, dv = A_QK_DIM, A_V_DIM
    nk = k_ref.shape[0] // t
    bufs = ((s0_ref, mx0_ref, sh0_ref), (s1_ref, mx1_ref, sh1_ref))

    @pl.when(qi == 0)
    def _():
        for c in range(nk):
            vt_ref[c, 0:dv, :] = jnp.transpose(v_ref[c * t:(c + 1) * t, :].astype(F32)).astype(BF16)
            vt_ref[c, dv:, :] = jnp.ones((vt_ref.shape[1] - dv, t), BF16)
        skew = pltpu.roll(jnp.broadcast_to(tab_ref[...], (t, 2 * t)), 0, 1, stride=1, stride_axis=0)
        key = lax.broadcasted_iota(jnp.int32, (t, t), 0)
        qry = lax.broadcasted_iota(jnp.int32, (t, t), 1)
        bdiag_ref[...] = jnp.where(qry >= key, skew[:, :t], MASKED_LOGIT)
        bnear_ref[...] = skew[:, t:]

    q_t = jnp.transpose(q_ref[...].astype(F32)) * (dk ** -0.5)
    feat = lax.broadcasted_iota(jnp.int32, q_t.shape, 0)
    qs_t = jnp.concatenate([jnp.where(feat < dk, q_t, 0.0), jnp.where(feat >= dk, q_t, 0.0)],
                           axis=1).astype(BF16)

    m_ref[...] = jnp.full_like(m_ref, -jnp.inf)
    acc_ref[...] = jnp.zeros_like(acc_ref)

    def score(kj, buf, bias_ref, shift):
        s_ref, mx_ref, sh_ref = bufs[buf]
        off = pl.multiple_of(kj * t, t)
        s = jnp.dot(k_ref[pl.ds(off, t), :], qs_t, preferred_element_type=F32)
        if bias_ref is not None:
            b = bias_ref[...]
            s = s + jnp.concatenate([b, b], axis=1)
        s_ref[...] = s
        mx_ref[...] = jnp.max(s, 0, keepdims=True) + shift
        sh_ref[...] = jnp.zeros_like(sh_ref) + shift

    def absorb(kj, buf):
        s_ref, mx_ref, sh_ref = bufs[buf]
        m_old = m_ref[...]
        m_new = jnp.maximum(m_old, mx_ref[...])
        p = jnp.exp(s_ref[...] - (m_new - sh_ref[...])).astype(BF16)
        acc_ref[...] = (jnp.exp(m_old - m_new) * acc_ref[...]
                        + jnp.dot(vt_ref[kj], p, preferred_element_type=F32))
        m_ref[...] = m_new

    c_far = tab_ref[:, 2 * t - 1:2 * t]
    for src_ref, dst_ref in zip(cast_in, cast_out):
        dst_ref[...] = src_ref[...].astype(BF16)
    score(qi, 0, bdiag_ref, 0.0)

    @pl.when(qi >= 1)
    def _():
        score(qi - 1, 1, bnear_ref, 0.0)
        absorb(qi, 0)

    def far_pair(p, carry):
        kj = qi - 2 - 2 * p
        score(kj, 0, None, c_far)
        absorb(kj + 1, 1)
        score(kj - 1, 1, None, c_far)
        absorb(kj, 0)
        return carry

    lax.fori_loop(0, jnp.maximum(qi - 1, 0) // 2, far_pair, 0)

    @pl.when((qi >= 2) & (qi % 2 == 0))
    def _():
        score(0, 0, None, c_far)
        absorb(1, 1)

    @pl.when(qi % 2 == 0)
    def _():
        absorb(0, 0)

    @pl.when(qi % 2 == 1)
    def _():
        absorb(0, 1)

    lv = lam_ref[...]
    lam = (jnp.exp(jnp.sum(lv[0:1] * lv[1:2], -1, keepdims=True))
           - jnp.exp(jnp.sum(lv[2:3] * lv[3:4], -1, keepdims=True)) + lam_init)
    acc = acc_ref[...]
    o_t = acc[0:dv] * (1.0 / acc[dv:dv + 1])
    out = jnp.transpose(o_t[:, :t] - lam * o_t[:, t:])
    hn = out * lax.rsqrt(jnp.mean(out * out, -1, keepdims=True) + NORM_EPS)
    o_ref[...] = (hn * g_ref[...] * (1.0 - lam_init)).astype(o_ref.dtype)


def _t5_bias_by_distance(rel_bias, n):
    r = jnp.arange(n, dtype=jnp.int32)
    max_exact = N_BUCKETS // 2
    rf = jnp.maximum(r, 1).astype(F32)
    large = max_exact + (jnp.log(rf / max_exact) / math.log(MAX_DISTANCE / max_exact)
                         * (N_BUCKETS - max_exact)).astype(jnp.int32)
    large = jnp.minimum(large, N_BUCKETS - 1)
    bucket = jnp.where(r < max_exact, r, large)
    return jnp.transpose(rel_bias[bucket]).astype(F32)


def _diff_attention(y3, rel_bias, lam_vecs, diff_g, jobs, *, lam_init, t):
    bsz, s, _ = y3.shape
    hd = 2 * A_QK_DIM
    dv = A_V_DIM
    assert hd == dv == LANES and s % t == 0 and t >= MAX_DISTANCE
    nq = s // t
    table = _t5_bias_by_distance(rel_bias, 2 * t).reshape(A_HEADS, 1, 2 * t)
    assert all(blocks <= bsz * A_HEADS * nq for _, _, _, blocks in jobs)

    def job_block(blocks):
        return lambda b, h, q: jnp.minimum((b * A_HEADS + h) * nq + q, blocks - 1)

    cast_in = [pl.BlockSpec((rows, w.shape[1]), lambda b, h, q, f=first, blk=job_block(blocks): (f + blk(b, h, q), 0))
               for w, rows, first, blocks in jobs]
    cast_out = [pl.BlockSpec((rows, w.shape[1]), lambda b, h, q, blk=job_block(blocks): (blk(b, h, q), 0))
                for w, rows, first, blocks in jobs]
    cast_shapes = [jax.ShapeDtypeStruct((rows * blocks, w.shape[1]), BF16) for w, rows, first, blocks in jobs]
    return pl.pallas_call(
        functools.partial(_diff_attn_kernel, t=t, lam_init=lam_init, n_jobs=len(jobs)),
        grid=(bsz, A_HEADS, nq),
        in_specs=[
            pl.BlockSpec((None, t, hd), lambda b, h, q: (b, q, h)),
            pl.BlockSpec((None, s, hd), lambda b, h, q: (b, 0, A_HEADS + h)),
            pl.BlockSpec((None, s, dv), lambda b, h, q: (b, 0, 2 * A_HEADS + h)),
            pl.BlockSpec((None, 1, 2 * t), lambda b, h, q: (h, 0, 0)),
            pl.BlockSpec((4, A_QK_DIM), lambda b, h, q: (0, 0)),
            pl.BlockSpec((1, dv), lambda b, h, q: (0, h)),
        ] + cast_in,
        out_specs=[pl.BlockSpec((None, t, dv), lambda b, h, q: (b, q, h))] + cast_out,
        out_shape=[jax.ShapeDtypeStruct((bsz, s, A_HEADS * dv), BF16)] + cast_shapes,
        scratch_shapes=[pltpu.VMEM((nq, dv + ATTN_ONES_ROWS, t), BF16),
                        pltpu.VMEM((t, t), F32),
                        pltpu.VMEM((t, t), F32),
                        pltpu.VMEM((1, 2 * t), F32),
                        pltpu.VMEM((dv + ATTN_ONES_ROWS, 2 * t), F32)]
                       + 2 * [pltpu.VMEM((t, 2 * t), F32), pltpu.VMEM((1, 2 * t), F32),
                              pltpu.VMEM((1, 2 * t), F32)],
        compiler_params=_params("arbitrary", "arbitrary", "arbitrary"),
        name="diff_attn",
    )(y3, y3, y3, table, lam_vecs, diff_g, *[w for w, _, _, _ in jobs])


def _mlstm_kernel(qk_ref, v_ref, ob_ref, gc_ref, gr_ref, cw_ref, cb_ref, bc_ref, br_ref, g_ref, y_ref,
                  xp_ref, *state_refs, chunk):
    heads, dk, dv = B_HEADS, B_QK_DIM, B_V_DIM
    c_refs, n_refs, m_refs = state_refs[0::3], state_refs[1::3], state_refs[2::3]
    pad = xp_ref.shape[0] - chunk
    c = pl.program_id(1)

    @pl.when(c == 0)
    def _():
        xp_ref[0:pad, :] = jnp.zeros((pad, xp_ref.shape[1]), xp_ref.dtype)
        for ref in state_refs:
            ref[...] = jnp.zeros_like(ref)

    x = qk_ref[...]
    xp_ref[pad:pad + chunk, :] = x
    xp = xp_ref[...]
    cw = cw_ref[...]
    dst = lax.broadcasted_iota(jnp.int32, (chunk, pad + chunk), 0)
    src = lax.broadcasted_iota(jnp.int32, (chunk, pad + chunk), 1)
    conv = cb_ref[...] + cw[CONV_WIDTH - 1:CONV_WIDTH, :] * x.astype(F32)
    for j in range(CONV_WIDTH - 1):
        shift = (src == dst + (pad - (CONV_WIDTH - 1) + j)).astype(BF16)
        conv = conv + cw[j:j + 1, :] * jnp.dot(shift, xp, preferred_element_type=F32)
    xp_ref[0:pad, :] = x[chunk - pad:chunk, :]
    qk = conv * _sigmoid(conv)

    gc = gc_ref[...] + bc_ref[...]
    gr = gr_ref[...] + br_ref[...]
    row = lax.broadcasted_iota(jnp.int32, (chunk, chunk), 0)
    col = lax.broadcasted_iota(jnp.int32, (chunk, chunk), 1)
    causal = row >= col
    tril = causal.astype(F32)
    triu = (row <= col).astype(F32)
    b_c = jnp.dot(tril, _log_sigmoid(gc), precision=HIGHEST, preferred_element_type=F32)
    b_r = jnp.dot(_log_sigmoid(gr), triu, precision=HIGHEST, preferred_element_type=F32)

    ob = ob_ref[...].astype(F32)
    g = g_ref[...]
    hs = range(heads)
    q = [qk[:, h * dk:(h + 1) * dk] * (dk ** -0.5) for h in hs]
    k = [qk[:, (heads + h) * dk:(heads + h + 1) * dk] for h in hs]
    v = [v_ref[:, h * dv:(h + 1) * dv] for h in hs]
    b_col = [b_c[:, heads + h:heads + h + 1] for h in hs]
    i_col = [gc[:, h:h + 1] for h in hs]
    b_row = [b_r[heads + h:heads + h + 1, :] for h in hs]
    i_row = [gr[h:h + 1, :] for h in hs]
    m_prev = [m_refs[h][0:1, 0:1] for h in hs]
    c_mat = [c_refs[h][...] for h in hs]
    n_vec = [n_refs[h][...] for h in hs]

    b_last = [b_row[h][:, chunk - 1:chunk] for h in hs]
    g_col = [b_last[h] - b_col[h] + i_col[h] for h in hs]
    m_new = [jnp.maximum(b_last[h] + m_prev[h], jnp.max(g_col[h], 0, keepdims=True)) for h in hs]
    decay = [jnp.exp(b_last[h] + m_prev[h] - m_new[h]) for h in hs]
    kw = [k[h] * jnp.exp(g_col[h] - m_new[h]) for h in hs]
    qb = [q[h].astype(BF16) for h in hs]
    qk_t = [lax.dot_general(qb[h], k[h].astype(BF16), NT_DIMS, preferred_element_type=F32) for h in hs]
    q_c = [jnp.dot(qb[h], c_mat[h].astype(BF16), preferred_element_type=F32) for h in hs]
    kw_v = [lax.dot_general(kw[h].astype(BF16), v[h], TN_DIMS, preferred_element_type=F32) for h in hs]

    d = [jnp.where(causal, b_col[h] - b_row[h] + i_row[h], -jnp.inf) for h in hs]
    a_col = [b_col[h] + m_prev[h] for h in hs]
    m_t = [jnp.maximum(a_col[h], jnp.max(d[h], -1, keepdims=True)) for h in hs]
    w_inter = [jnp.exp(a_col[h] - m_t[h]) for h in hs]
    sw = [qk_t[h] * jnp.exp(d[h] - m_t[h]) for h in hs]
    sw_v = [jnp.dot(sw[h].astype(BF16), v[h], preferred_element_type=F32) for h in hs]

    for h in hs:
        c_refs[h][...] = decay[h] * c_mat[h] + kw_v[h]
        n_refs[h][...] = decay[h] * n_vec[h] + jnp.sum(kw[h], 0, keepdims=True)
        m_refs[h][...] = jnp.broadcast_to(m_new[h], m_refs[h].shape)

    for h in hs:
        num = w_inter[h] * q_c[h] + sw_v[h]
        den = (w_inter[h] * jnp.sum(q[h] * n_vec[h], -1, keepdims=True)
               + jnp.sum(sw[h], -1, keepdims=True))
        hh = num / jnp.maximum(jnp.abs(den), jnp.exp(-m_t[h]))
        hn = hh * lax.rsqrt(jnp.mean(hh * hh, -1, keepdims=True) + NORM_EPS)
        sl = slice(h * dv, (h + 1) * dv)
        y_ref[:, sl] = (hn * g[:, sl] * _sigmoid(ob[:, sl])).astype(y_ref.dtype)


def _mlstm(y3, gcol3, grow3, conv_w, conv_b, gate_bias_cols, gate_bias_rows, mlstm_g, *, chunk):
    bsz, s, _ = y3.shape
    heads, dk, dv = B_HEADS, B_QK_DIM, B_V_DIM
    w = heads * dv
    assert 2 * heads * dk == w and s % chunk == 0 and chunk % LANES == 0
    nc = s // chunk
    return pl.pallas_call(
        functools.partial(_mlstm_kernel, chunk=chunk),
        grid=(bsz, nc),
        in_specs=[
            pl.BlockSpec((None, chunk, w), lambda b, c: (b, c, 3)),
            pl.BlockSpec((None, chunk, w), lambda b, c: (b, c, 4)),
            pl.BlockSpec((None, chunk, w), lambda b, c: (b, c, 5)),
            pl.BlockSpec((None, chunk, LANES), lambda b, c: (b, c, 0)),
            pl.BlockSpec((SUBLANES, chunk), lambda b, c: (0, b * nc + c)),
            pl.BlockSpec((CONV_WIDTH, w), lambda b, c: (0, 0)),
            pl.BlockSpec((1, w), lambda b, c: (0, 0)),
            pl.BlockSpec((1, LANES), lambda b, c: (0, 0)),
            pl.BlockSpec((SUBLANES, 1), lambda b, c: (0, 0)),
            pl.BlockSpec((1, w), lambda b, c: (0, 0)),
        ],
        out_specs=pl.BlockSpec((None, chunk, w), lambda b, c: (b, c, 0)),
        out_shape=jax.ShapeDtypeStruct((bsz, s, w), BF16),
        scratch_shapes=[
            pltpu.VMEM((BF16_SUBLANES + chunk, w), BF16),
        ] + heads * [pltpu.VMEM((dk, dv), F32), pltpu.VMEM((1, dk), F32), pltpu.VMEM((SUBLANES, LANES), F32)],
        compiler_params=_params("parallel", "arbitrary"),
        name="mlstm",
    )(y3, y3, y3, gcol3, grow3, conv_w, conv_b, gate_bias_cols, gate_bias_rows, mlstm_g)


def _gla_kernel(q_ref, k_ref, v_ref, r_ref, a1_ref, a1n_ref, wah_ref, wal_ref, ba_ref, g_ref, y_ref,
                b_ref, *st_refs, chunk):
    heads, dk, dv = C_HEADS, C_QK_DIM, C_V_DIM
    sub = GLA_SUB
    assert chunk == 2 * sub
    c = pl.program_id(1)
    row = lax.broadcasted_iota(jnp.int32, (chunk, chunk), 0)
    col = lax.broadcasted_iota(jnp.int32, (chunk, chunk), 1)
    causal = (row >= col) & ((row < sub) == (col < sub))
    first = lax.broadcasted_iota(jnp.int32, (chunk, 1), 0) < sub

    def cum_log_decay(a1):
        a_hi = a1.astype(BF16)
        a_lo = (a1 - a_hi.astype(F32)).astype(BF16)
        w_hi = wah_ref[...]
        z = (jnp.dot(a_hi, w_hi, preferred_element_type=F32)
             + (jnp.dot(a_lo, w_hi, preferred_element_type=F32)
                + jnp.dot(a_hi, wal_ref[...], preferred_element_type=F32))) + ba_ref[...]
        log_a = _log_sigmoid(z) * (1.0 / C_GATE_TEMP)
        return jnp.dot(causal.astype(F32), log_a, precision=HIGHEST, preferred_element_type=F32)

    @pl.when(c == 0)
    def _():
        for ref in st_refs:
            ref[...] = jnp.zeros_like(ref)
        b_ref[...] = cum_log_decay(a1_ref[...])

    b_all = b_ref[...]
    b_next = cum_log_decay(a1n_ref[...])

    r = r_ref[...].astype(F32)
    g = g_ref[...]
    for h in range(heads):
        ks = slice(h * dk, (h + 1) * dk)
        vs = slice(h * dv, (h + 1) * dv)
        b = b_all[:, ks]
        q = q_ref[:, ks].astype(F32) * (dk ** -0.5)
        k = k_ref[:, ks].astype(F32)
        v = v_ref[:, vs]
        tot0 = b[sub - 1:sub, :]
        tot1 = b[chunk - 1:chunk, :]
        q_dec = q * jnp.exp(b)
        k_dec = (k * jnp.exp(-b)).astype(BF16)
        k_end = k * jnp.exp(jnp.where(first, tot0, tot1) - b)
        k_new = (k_end * jnp.where(first, jnp.exp(tot1), 1.0)).astype(BF16)
        q_int = (q_dec * jnp.where(first, 1.0, jnp.exp(tot0))).astype(BF16)
        q_dec = q_dec.astype(BF16)
        st = st_refs[h][...]
        qk_t = lax.dot_general(q_dec, k_dec, NT_DIMS, preferred_element_type=F32)
        inter = lax.dot_general(q_int, st.astype(BF16), NT_DIMS, preferred_element_type=F32)
        st_refs[h][...] = (jnp.exp(tot0 + tot1) * st
                           + lax.dot_general(v, k_new, TN_DIMS, preferred_element_type=F32))
        cross = lax.dot_general(q_dec[sub:], k_end[:sub].astype(BF16), NT_DIMS, preferred_element_type=F32)
        att = jnp.where(causal, qk_t, 0.0)
        o = jnp.dot(att.astype(BF16), v, preferred_element_type=F32) + inter
        o_cross = jnp.dot(cross.astype(BF16), v[:sub], preferred_element_type=F32)
        o = jnp.concatenate([o[:sub], o[sub:] + o_cross], axis=0)

        on = o * lax.rsqrt(jnp.mean(o * o, -1, keepdims=True) + NORM_EPS)
        rr = r[:, vs]
        y_ref[:, vs] = (on * g[:, vs] * (rr * _sigmoid(rr))).astype(y_ref.dtype)

    b_ref[...] = b_next


def _gla(y3, a13, w_a2_hi, w_a2_lo, b_a, gla_g, *, chunk):
    bsz, s, _ = y3.shape
    heads, dk, dv = C_HEADS, C_QK_DIM, C_V_DIM
    wk, wv = heads * dk, heads * dv
    assert wv == 2 * wk and s % chunk == 0
    nc = s // chunk
    return pl.pallas_call(
        functools.partial(_gla_kernel, chunk=chunk),
        grid=(bsz, nc),
        in_specs=[
            pl.BlockSpec((None, chunk, wk), lambda b, c: (b, c, 0)),
            pl.BlockSpec((None, chunk, wk), lambda b, c: (b, c, 1)),
            pl.BlockSpec((None, chunk, wv), lambda b, c: (b, c, 1)),
            pl.BlockSpec((None, chunk, wv), lambda b, c: (b, c, 2)),
            pl.BlockSpec((None, chunk, LANES), lambda b, c: (b, c, 0)),
            pl.BlockSpec((None, chunk, LANES), lambda b, c: (b, jnp.minimum(c + 1, nc - 1), 0)),
            pl.BlockSpec((LANES, wk), lambda b, c: (0, 0)),
            pl.BlockSpec((LANES, wk), lambda b, c: (0, 0)),
            pl.BlockSpec((1, wk), lambda b, c: (0, 0)),
            pl.BlockSpec((1, wv), lambda b, c: (0, 0)),
        ],
        out_specs=pl.BlockSpec((None, chunk, wv), lambda b, c: (b, c, 0)),
        out_shape=jax.ShapeDtypeStruct((bsz, s, wv), BF16),
        scratch_shapes=[pltpu.VMEM((chunk, wk), F32)] + heads * [pltpu.VMEM((dv, dk), F32)],
        compiler_params=_params("parallel", "arbitrary"),
        name="gla",
    )(y3, y3, y3, y3, a13, a13, w_a2_hi, w_a2_lo, b_a, gla_g)


def _out_ln_kernel(*refs, n_in, alpha):
    y_refs, w_refs = refs[:n_in], refs[n_in:2 * n_in]
    x_ref, g_ref, b_ref, o_ref = refs[2 * n_in:]
    rb = o_ref.shape[0] // ROW_SPLIT
    for r in range(ROW_SPLIT):
        rows = slice(r * rb, (r + 1) * rb)
        acc = jnp.dot(y_refs[0][rows, :], w_refs[0][...], preferred_element_type=F32)
        for y_ref, w_ref in zip(y_refs[1:], w_refs[1:]):
            acc = acc + jnp.dot(y_ref[rows, :], w_ref[...], preferred_element_type=F32)
        o_ref[rows, :] = _layer_norm(alpha * x_ref[rows, :] + acc, g_ref[...], b_ref[...])


def _out_ln(ys, w, x, ln_g, ln_b, layer, w_layer, ln_slot, *, alpha, tm):
    t, d = x.shape
    n_in = len(ys)
    wd = ys[0].shape[1]
    assert all(y.shape[1] == wd for y in ys) and w.shape[1:] == (n_in * wd, d)
    ln_spec = pl.BlockSpec((None, None, 1, d), lambda i: (layer, ln_slot, 0, 0))
    in_specs = ([pl.BlockSpec((tm, wd), lambda i: (i, 0)) for _ in ys]
                + [pl.BlockSpec((None, wd, d), lambda i, k=k: (w_layer, k, 0)) for k in range(n_in)]
                + [pl.BlockSpec((tm, d), lambda i: (i, 0)), ln_spec, ln_spec])
    return pl.pallas_call(
        functools.partial(_out_ln_kernel, n_in=n_in, alpha=alpha),
        grid=(t // tm,),
        in_specs=in_specs,
        out_specs=pl.BlockSpec((tm, d), lambda i: (i, 0)),
        out_shape=jax.ShapeDtypeStruct((t, d), F32),
        compiler_params=_params("parallel"),
        name="out_ln",
    )(*ys, *([w] * n_in), x, ln_g, ln_b)


def _ple_kernel(x_ref, p_ref, wp_ref, wg_ref, o_ref):
    rb = o_ref.shape[0] // ROW_SPLIT
    for r in range(ROW_SPLIT):
        rows = slice(r * rb, (r + 1) * rb)
        x = x_ref[rows, :]
        e = jnp.dot(p_ref[rows, :].astype(BF16), wp_ref[...], preferred_element_type=F32)
        gate = jnp.dot(x.astype(BF16), wg_ref[...], preferred_element_type=F32)
        o_ref[rows, :] = x + e * _sigmoid(gate)


def _ple(x, p, w_proj, w_gate, layer, *, tm):
    t, d = x.shape
    s, pd = p.shape[2:]
    assert s % tm == 0
    per_seq = s // tm
    return pl.pallas_call(
        _ple_kernel,
        grid=(t // tm,),
        in_specs=[
            pl.BlockSpec((tm, d), lambda i: (i, 0)),
            pl.BlockSpec((None, None, tm, pd), lambda i: (layer, i // per_seq, i % per_seq, 0)),
            pl.BlockSpec((None, pd, d), lambda i: (layer, 0, 0)),
            pl.BlockSpec((None, d, d), lambda i: (layer, 0, 0)),
        ],
        out_specs=pl.BlockSpec((tm, d), lambda i: (i, 0)),
        out_shape=jax.ShapeDtypeStruct((t, d), F32),
        compiler_params=_params("parallel"),
        name="ple",
    )(x, p, w_proj, w_gate)


def _tiles(t, s):
    return dict(
        ffn_tm=min(1024, t), ffn_tf=256,
        proj_tm=min(1024, t), proj_tn=1024,
        row_tm=min(512, t),
        attn_t=min(512, s),
        mlstm_chunk=min(128, s),
        gla_chunk=min(2 * GLA_SUB, s),
    )


def _cast_job(w2d, unit, skip, count, steps):
    rows = next(r for r in range(BF16_SUBLANES, unit + 1, BF16_SUBLANES)
                if unit % r == 0 and count * (unit // r) <= steps)
    return w2d, rows, skip * (unit // rows), count * (unit // rows)


def _pad_cols(w, n):
    return jnp.pad(w, ((0, 0), (0, n - w.shape[1])))


def kernel(x, p, ln_g, ln_b, w_ffn_in, w_ffn_out, w_in_ab, w_out_ab, rel_bias, lambda_q1, lambda_k1,
           lambda_q2, lambda_k2, diff_norm, conv_w, conv_b, b_igate, b_fgate, mlstm_norm, w_in_c,
           w_alpha2, b_alpha, gla_norm, w_out_c, w_ple_proj, w_ple_gate):
    bsz, s, d = x.shape
    depth = p.shape[0]
    t = bsz * s
    tl = _tiles(t, s)
    alpha = (2 * depth) ** 0.25
    a_w = A_HEADS * A_V_DIM
    ab_main = 3 * a_w + 3 * B_HEADS * B_V_DIM
    c_main = 2 * C_HEADS * C_QK_DIM + 2 * C_HEADS * C_V_DIM
    row = lambda a: a.reshape(1, -1)

    n_ffn, f_ff = 2 * depth, w_ffn_out.shape[2]
    attn_steps = bsz * A_HEADS * (s // tl["attn_t"])
    wfi_first, wfo_first = w_ffn_in[0, :1].astype(BF16), w_ffn_out[0, :1].astype(BF16)
    ffn_w = {0: (wfi_first, wfo_first, 0)}
    cast_jobs = [
        _cast_job(w_ffn_in.reshape(n_ffn * d, -1), d, 1, n_ffn - 1, attn_steps),
        _cast_job(w_ffn_out.reshape(n_ffn * f_ff, d), f_ff, 1, n_ffn - 1, attn_steps),
        _cast_job(jnp.swapaxes(w_in_c, 1, 2).reshape(-1, d), w_in_c.shape[2], 0, w_in_c.shape[0], attn_steps),
    ]
    wo_ab, wo_c = w_out_ab.astype(BF16), w_out_c.astype(BF16)
    wpp, wpg = w_ple_proj.astype(BF16), w_ple_gate.astype(BF16)
    g4, b4 = ln_g.reshape(depth, 3, 1, d), ln_b.reshape(depth, 3, 1, d)

    xf = x.reshape(t, d)
    for i in range(depth):
        xf = _ffn_ln(xf, *ffn_w[2 * i], g4, b4, i, 0, alpha=alpha, tm=tl["ffn_tm"], tf=tl["ffn_tf"])
        if i % 2 == 0:
            e = i // 2
            lam_init = 0.8 - 0.6 * math.exp(-0.3 * i)
            y, gcol, grow = _in_proj(xf, w_in_ab, e, ab_main, tm=tl["proj_tm"], tn=tl["proj_tn"])
            y3 = y.reshape(bsz, s, ab_main)
            lam_vecs = jnp.stack([lambda_q1[e], lambda_k1[e], lambda_q2[e], lambda_k2[e]])
            ya, *cast = _diff_attention(y3, rel_bias, lam_vecs, row(diff_norm[e]), cast_jobs,
                                        lam_init=lam_init, t=tl["attn_t"])
            if i == 0:
                wfi_rest, wfo_rest = cast[0].reshape(n_ffn - 1, d, -1), cast[1].reshape(n_ffn - 1, f_ff, d)
                ffn_w.update({n: (wfi_rest, wfo_rest, n - 1) for n in range(1, n_ffn)})
                w_c_t = cast[2].reshape(w_in_c.shape[0], w_in_c.shape[2], d)
            gate_bias = jnp.concatenate([b_igate[e], b_fgate[e]])
            yb = _mlstm(y3, gcol.reshape(bsz, s, LANES), grow,
                        conv_w[e], row(conv_b[e]), _pad_cols(row(gate_bias), LANES), gate_bias.reshape(-1, 1),
                        row(mlstm_norm[e]), chunk=tl["mlstm_chunk"])
            ys, w_out, w_layer = [ya.reshape(t, a_w), yb.reshape(t, -1)], wo_ab, e
        else:
            o = i // 2
            y, a1, _ = _in_proj(xf, w_in_c, o, c_main, tm=tl["proj_tm"], tn=tl["proj_tn"], w_t=w_c_t)
            w_a2 = jnp.pad(w_alpha2[o], ((0, LANES - w_alpha2.shape[1]), (0, 0)))
            w_a2_hi = lax.reduce_precision(w_a2, exponent_bits=8, mantissa_bits=7)
            w_a2_lo = (w_a2 - w_a2_hi).astype(BF16)
            yc = _gla(y.reshape(bsz, s, c_main), a1.reshape(bsz, s, LANES), w_a2_hi.astype(BF16), w_a2_lo,
                      row(b_alpha[o]),
                      row(gla_norm[o]), chunk=tl["gla_chunk"])
            ys, w_out, w_layer = [yc.reshape(t, -1)], wo_c, o
        xf = _out_ln(ys, w_out, xf, g4, b4, i, w_layer, 1, alpha=alpha, tm=tl["row_tm"])
        xf = _ffn_ln(xf, *ffn_w[2 * i + 1], g4, b4, i, 2, alpha=alpha, tm=tl["ffn_tm"], tf=tl["ffn_tf"])
        xf = _ple(xf, p, wpp, wpg, i, tm=tl["row_tm"])
    return xf.reshape(bsz, s, d)
```

```python
import functools
import math

import jax
import jax.numpy as jnp
from jax import lax
from jax.experimental import pallas as pl
from jax.experimental.pallas import tpu as pltpu

F32 = jnp.float32
BF16 = jnp.bfloat16
HIGHEST = lax.Precision.HIGHEST

A_HEADS = 8
A_QK_DIM = 64
A_V_DIM = 128
B_HEADS = 4
B_QK_DIM = 128
B_V_DIM = 256
CONV_WIDTH = 4
C_HEADS = 4
C_QK_DIM = 256
C_V_DIM = 512
C_GATE_TEMP = 16.0
GLA_SUB = 64
N_BUCKETS = 32
MAX_DISTANCE = 128
LN_EPS = 1e-5
NORM_EPS = 1e-6

V7X_VMEM_LIMIT_BYTES = 56 * 1024 * 1024
LANES = 128
SUBLANES = 8
BF16_SUBLANES = 16
ATTN_ONES_ROWS = BF16_SUBLANES
MASKED_LOGIT = -1e30
ROW_SPLIT = 4

NT_DIMS = (((1,), (1,)), ((), ()))
TN_DIMS = (((0,), (0,)), ((), ()))


def _params(*sem):
    return pltpu.CompilerParams(dimension_semantics=sem, vmem_limit_bytes=V7X_VMEM_LIMIT_BYTES)


def _layer_norm(y, g, b, eps=LN_EPS):
    mu = jnp.mean(y, -1, keepdims=True)
    yc = y - mu
    var = jnp.mean(yc * yc, -1, keepdims=True)
    return yc * lax.rsqrt(var + eps) * g + b


def _sigmoid(x):
    return 1.0 / (1.0 + jnp.exp(-x))


def _log_sigmoid(x):
    return jnp.minimum(x, 0.0) - jnp.log(1.0 + jnp.exp(-jnp.abs(x)))


def _ffn_ln_kernel(x_ref, wg0_ref, wu0_ref, wo0_ref, wg1_ref, wu1_ref, wo1_ref, g_ref, b_ref, o_ref, xb_ref,
                   *, alpha, n_chunks):
    j = pl.program_id(1)
    last = pl.num_programs(1) - 1

    @pl.when(j == 0)
    def _():
        x = x_ref[...]
        xb_ref[...] = x.astype(BF16)
        o_ref[...] = (2.0 * alpha) * x

    def chunk(wg_ref, wu_ref, wo_ref):
        xb = xb_ref[...]
        gate = jnp.dot(xb, wg_ref[...], preferred_element_type=F32)
        up = jnp.dot(xb, wu_ref[...], preferred_element_type=F32)
        h = (gate * _sigmoid(gate) * up).astype(BF16)
        o_ref[...] += jnp.dot(h, wo_ref[...], preferred_element_type=F32)

    if n_chunks % 2 == 0:
        chunk(wg0_ref, wu0_ref, wo0_ref)
        chunk(wg1_ref, wu1_ref, wo1_ref)
    else:
        @pl.when(j < last)
        def _():
            chunk(wg0_ref, wu0_ref, wo0_ref)
            chunk(wg1_ref, wu1_ref, wo1_ref)

        @pl.when(j == last)
        def _():
            chunk(wg0_ref, wu0_ref, wo0_ref)

    @pl.when(j == last)
    def _():
        o_ref[...] = _layer_norm(o_ref[...], g_ref[...], b_ref[...], eps=4.0 * LN_EPS)


def _ffn_ln(x, w_in, w_out, w_idx, ln_g, ln_b, layer, ln_slot, *, alpha, tm, tf):
    t, d = x.shape
    f = w_out.shape[1]
    nf = f // tf
    assert t % tm == 0 and f % tf == 0 and w_in.shape[1:] == (d, 2 * f) and nf >= 2
    steps = (nf + 1) // 2
    first = lambda j: 2 * j
    second = lambda j: jnp.where(2 * j + 1 < nf, 2 * j + 1, nf - 2)
    ln_spec = pl.BlockSpec((None, None, 1, d), lambda i, j: (layer, ln_slot, 0, 0))

    def weight_specs(chunk_of):
        return [pl.BlockSpec((None, d, tf), lambda i, j: (w_idx, 0, chunk_of(j))),
                pl.BlockSpec((None, d, tf), lambda i, j: (w_idx, 0, nf + chunk_of(j))),
                pl.BlockSpec((None, tf, d), lambda i, j: (w_idx, chunk_of(j), 0))]

    return pl.pallas_call(
        functools.partial(_ffn_ln_kernel, alpha=alpha, n_chunks=nf),
        grid=(t // tm, steps),
        in_specs=[pl.BlockSpec((tm, d), lambda i, j: (i, 0))] + weight_specs(first) + weight_specs(second)
                 + [ln_spec, ln_spec],
        out_specs=pl.BlockSpec((tm, d), lambda i, j: (i, 0)),
        out_shape=jax.ShapeDtypeStruct((t, d), F32),
        scratch_shapes=[pltpu.VMEM((tm, d), BF16)],
        compiler_params=_params("parallel", "arbitrary"),
        name="ffn_ln",
    )(x, w_in, w_in, w_out, w_in, w_in, w_out, ln_g, ln_b)


def _in_proj_kernel(x_ref, w_ref, wgc_ref, wgr_ref, y_ref, gc_ref, gr_ref, xb_ref):
    j = pl.program_id(1)

    @pl.when(j == 0)
    def _():
        xb = x_ref[...].astype(BF16)
        xb_ref[...] = xb
        gc_ref[...] = jnp.dot(xb, wgc_ref[...], preferred_element_type=F32)
        gr_ref[...] = lax.dot_general(wgr_ref[...], xb, NT_DIMS, preferred_element_type=F32)

    y_ref[...] = lax.dot_general(xb_ref[...], w_ref[...].astype(BF16), NT_DIMS,
                                 preferred_element_type=F32).astype(y_ref.dtype)


def _in_proj(x, w_stack, layer, n_main, *, tm, tn, w_t=None):
    t, d = x.shape
    assert t % tm == 0 and n_main % tn == 0
    if w_t is None:
        w_t = jnp.swapaxes(w_stack, 1, 2)
    w_gates = w_stack[layer, :, n_main:].astype(BF16)
    w_gate_cols = _pad_cols(w_gates, LANES)
    w_gate_rows = jnp.transpose(w_gates[:, :SUBLANES])
    return pl.pallas_call(
        _in_proj_kernel,
        grid=(t // tm, n_main // tn),
        in_specs=[
            pl.BlockSpec((tm, d), lambda i, j: (i, 0)),
            pl.BlockSpec((None, tn, d), lambda i, j: (layer, j, 0)),
            pl.BlockSpec((d, LANES), lambda i, j: (0, 0)),
            pl.BlockSpec((SUBLANES, d), lambda i, j: (0, 0)),
        ],
        out_specs=[
            pl.BlockSpec((tm, tn), lambda i, j: (i, j)),
            pl.BlockSpec((tm, LANES), lambda i, j: (i, 0)),
            pl.BlockSpec((SUBLANES, tm), lambda i, j: (0, i)),
        ],
        out_shape=[
            jax.ShapeDtypeStruct((t, n_main), BF16),
            jax.ShapeDtypeStruct((t, LANES), F32),
            jax.ShapeDtypeStruct((SUBLANES, t), F32),
        ],
        scratch_shapes=[pltpu.VMEM((tm, d), BF16)],
        compiler_params=_params("parallel", "arbitrary"),
        name="in_proj",
    )(x, w_t, w_gate_cols, w_gate_rows)


def _diff_attn_kernel(q_ref, qn_ref, k_ref, v_ref, tab_ref, lam_ref, g_ref, *refs, t, lam_init, n_jobs):
    cast_in, (o_ref, *cast_out) = refs[:n_jobs], refs[n_jobs:2 * n_jobs + 1]
    (vt_ref, bdiag_ref, bnear_ref, m_ref, acc_ref, qst_ref,
     s0_ref, mx0_ref, sh0_ref, s1_ref, mx1_ref, sh1_ref, sd_ref, mxd_ref, shd_ref) = refs[2 * n_jobs + 1:]
    qi = pl.program_id(2)
    dk, dv = A_QK_DIM, A_V_DIM
    nk = k_ref.shape[0] // t
    DIAG = 2
    bufs = ((s0_ref, mx0_ref, sh0_ref), (s1_ref, mx1_ref, sh1_ref), (sd_ref, mxd_ref, shd_ref))

    @pl.when(qi == 0)
    def _():
        for c in range(nk):
            vt_ref[c, 0:dv, :] = jnp.transpose(v_ref[c * t:(c + 1) * t, :].astype(F32)).astype(BF16)
            vt_ref[c, dv:, :] = jnp.ones((vt_ref.shape[1] - dv, t), BF16)
        skew = pltpu.roll(jnp.broadcast_to(tab_ref[...], (t, 2 * t)), 0, 1, stride=1, stride_axis=0)
        key = lax.broadcasted_iota(jnp.int32, (t, t), 0)
        qry = lax.broadcasted_iota(jnp.int32, (t, t), 1)
        bdiag_ref[...] = jnp.where(qry >= key, skew[:, :t], MASKED_LOGIT)
        bnear_ref[...] = skew[:, t:]

    def stacked_queries(tile_ref):
        q_t = jnp.transpose(tile_ref[...].astype(F32)) * (dk ** -0.5)
        feat = lax.broadcasted_iota(jnp.int32, q_t.shape, 0)
        return jnp.concatenate([jnp.where(feat < dk, q_t, 0.0), jnp.where(feat >= dk, q_t, 0.0)],
                               axis=1).astype(BF16)

    m_ref[...] = jnp.full_like(m_ref, -jnp.inf)
    acc_ref[...] = jnp.zeros_like(acc_ref)

    def score(kj, buf, bias_ref, shift, qs_t):
        s_ref, mx_ref, sh_ref = bufs[buf]
        off = pl.multiple_of(kj * t, t)
        s = jnp.dot(k_ref[pl.ds(off, t), :], qs_t, preferred_element_type=F32)
        if bias_ref is not None:
            b = bias_ref[...]
            s = s + jnp.concatenate([b, b], axis=1)
        s_ref[...] = s
        mx_ref[...] = jnp.max(s, 0, keepdims=True) + shift
        sh_ref[...] = jnp.zeros_like(sh_ref) + shift

    def absorb(kj, buf):
        s_ref, mx_ref, sh_ref = bufs[buf]
        m_old = m_ref[...]
        m_new = jnp.maximum(m_old, mx_ref[...])
        p = jnp.exp(s_ref[...] - (m_new - sh_ref[...])).astype(BF16)
        acc_ref[...] = (jnp.exp(m_old - m_new) * acc_ref[...]
                        + jnp.dot(vt_ref[kj], p, preferred_element_type=F32))
        m_ref[...] = m_new

    c_far = tab_ref[:, 2 * t - 1:2 * t]
    for src_ref, dst_ref in zip(cast_in, cast_out):
        dst_ref[...] = src_ref[...].astype(BF16)

    @pl.when(qi == 0)
    def _():
        qst_ref[...] = stacked_queries(q_ref)
        score(0, DIAG, bdiag_ref, 0.0, qst_ref[...])

    qs_cur = qst_ref[...]
    qs_next = stacked_queries(qn_ref)
    kj_next = jnp.minimum(qi + 1, nk - 1)

    @pl.when(qi >= 1)
    def _():
        score(qi - 1, 1, bnear_ref, 0.0, qs_cur)
        absorb(qi, DIAG)

    def far_pair(p, carry):
        kj = qi - 2 - 2 * p
        score(kj, 0, None, c_far, qs_cur)
        absorb(kj + 1, 1)
        score(kj - 1, 1, None, c_far, qs_cur)
        absorb(kj, 0)
        return carry

    lax.fori_loop(0, jnp.maximum(qi - 1, 0) // 2, far_pair, 0)

    @pl.when((qi >= 2) & (qi % 2 == 0))
    def _():
        score(0, 0, None, c_far, qs_cur)
        absorb(1, 1)

    @pl.when(qi == 0)
    def _():
        absorb(0, DIAG)
        score(kj_next, DIAG, bdiag_ref, 0.0, qs_next)

    @pl.when((qi >= 2) & (qi % 2 == 0))
    def _():
        score(kj_next, DIAG, bdiag_ref, 0.0, qs_next)
        absorb(0, 0)

    @pl.when(qi % 2 == 1)
    def _():
        score(kj_next, DIAG, bdiag_ref, 0.0, qs_next)
        absorb(0, 1)

    qst_ref[...] = qs_next

    lv = lam_ref[...]
    lam = (jnp.exp(jnp.sum(lv[0:1] * lv[1:2], -1, keepdims=True))
           - jnp.exp(jnp.sum(lv[2:3] * lv[3:4], -1, keepdims=True)) + lam_init)
    acc = acc_ref[...]
    o_t = acc[0:dv] * (1.0 / acc[dv:dv + 1])
    out = jnp.transpose(o_t[:, :t] - lam * o_t[:, t:])
    hn = out * lax.rsqrt(jnp.mean(out * out, -1, keepdims=True) + NORM_EPS)
    o_ref[...] = (hn * g_ref[...] * (1.0 - lam_init)).astype(o_ref.dtype)


def _t5_bias_by_distance(rel_bias, n):
    r = jnp.arange(n, dtype=jnp.int32)
    max_exact = N_BUCKETS // 2
    rf = jnp.maximum(r, 1).astype(F32)
    large = max_exact + (jnp.log(rf / max_exact) / math.log(MAX_DISTANCE / max_exact)
                         * (N_BUCKETS - max_exact)).astype(jnp.int32)
    large = jnp.minimum(large, N_BUCKETS - 1)
    bucket = jnp.where(r < max_exact, r, large)
    return jnp.transpose(rel_bias[bucket]).astype(F32)


def _diff_attention(y3, rel_bias, lam_vecs, diff_g, jobs, *, lam_init, t):
    bsz, s, _ = y3.shape
    hd = 2 * A_QK_DIM
    dv = A_V_DIM
    assert hd == dv == LANES and s % t == 0 and t >= MAX_DISTANCE
    nq = s // t
    table = _t5_bias_by_distance(rel_bias, 2 * t).reshape(A_HEADS, 1, 2 * t)
    assert all(blocks <= bsz * A_HEADS * nq for _, _, _, blocks in jobs)

    def job_block(blocks):
        return lambda b, h, q: jnp.minimum((b * A_HEADS + h) * nq + q, blocks - 1)

    cast_in = [pl.BlockSpec((rows, w.shape[1]), lambda b, h, q, f=first, blk=job_block(blocks): (f + blk(b, h, q), 0))
               for w, rows, first, blocks in jobs]
    cast_out = [pl.BlockSpec((rows, w.shape[1]), lambda b, h, q, blk=job_block(blocks): (blk(b, h, q), 0))
                for w, rows, first, blocks in jobs]
    cast_shapes = [jax.ShapeDtypeStruct((rows * blocks, w.shape[1]), BF16) for w, rows, first, blocks in jobs]
    return pl.pallas_call(
        functools.partial(_diff_attn_kernel, t=t, lam_init=lam_init, n_jobs=len(jobs)),
        grid=(bsz, A_HEADS, nq),
        in_specs=[
            pl.BlockSpec((None, t, hd), lambda b, h, q: (b, q, h)),
            pl.BlockSpec((None, t, hd), lambda b, h, q: (b, jnp.minimum(q + 1, nq - 1), h)),
            pl.BlockSpec((None, s, hd), lambda b, h, q: (b, 0, A_HEADS + h)),
            pl.BlockSpec((None, s, dv), lambda b, h, q: (b, 0, 2 * A_HEADS + h)),
            pl.BlockSpec((None, 1, 2 * t), lambda b, h, q: (h, 0, 0)),
            pl.BlockSpec((4, A_QK_DIM), lambda b, h, q: (0, 0)),
            pl.BlockSpec((1, dv), lambda b, h, q: (0, h)),
        ] + cast_in,
        out_specs=[pl.BlockSpec((None, t, dv), lambda b, h, q: (b, q, h))] + cast_out,
        out_shape=[jax.ShapeDtypeStruct((bsz, s, A_HEADS * dv), BF16)] + cast_shapes,
        scratch_shapes=[pltpu.VMEM((nq, dv + ATTN_ONES_ROWS, t), BF16),
                        pltpu.VMEM((t, t), F32),
                        pltpu.VMEM((t, t), F32),
                        pltpu.VMEM((1, 2 * t), F32),
                        pltpu.VMEM((dv + ATTN_ONES_ROWS, 2 * t), F32),
                        pltpu.VMEM((hd, 2 * t), BF16)]
                       + 3 * [pltpu.VMEM((t, 2 * t), F32), pltpu.VMEM((1, 2 * t), F32),
                              pltpu.VMEM((1, 2 * t), F32)],
        compiler_params=_params("arbitrary", "arbitrary", "arbitrary"),
        name="diff_attn",
    )(y3, y3, y3, y3, table, lam_vecs, diff_g, *[w for w, _, _, _ in jobs])


def _mlstm_kernel(qk_ref, v_ref, ob_ref, gc_ref, gr_ref, cw_ref, cb_ref, bc_ref, br_ref, g_ref, y_ref,
                  xp_ref, *state_refs, chunk):
    heads, dk, dv = B_HEADS, B_QK_DIM, B_V_DIM
    c_refs, n_refs, m_refs = state_refs[0::3], state_refs[1::3], state_refs[2::3]
    pad = xp_ref.shape[0] - chunk
    c = pl.program_id(1)

    @pl.when(c == 0)
    def _():
        xp_ref[0:pad, :] = jnp.zeros((pad, xp_ref.shape[1]), xp_ref.dtype)
        for ref in state_refs:
            ref[...] = jnp.zeros_like(ref)

    x = qk_ref[...]
    xp_ref[pad:pad + chunk, :] = x
    xp = xp_ref[...]
    cw = cw_ref[...]
    dst = lax.broadcasted_iota(jnp.int32, (chunk, pad + chunk), 0)
    src = lax.broadcasted_iota(jnp.int32, (chunk, pad + chunk), 1)
    conv = cb_ref[...] + cw[CONV_WIDTH - 1:CONV_WIDTH, :] * x.astype(F32)
    for j in range(CONV_WIDTH - 1):
        shift = (src == dst + (pad - (CONV_WIDTH - 1) + j)).astype(BF16)
        conv = conv + cw[j:j + 1, :] * jnp.dot(shift, xp, preferred_element_type=F32)
    xp_ref[0:pad, :] = x[chunk - pad:chunk, :]
    qk = conv * _sigmoid(conv)

    gc = gc_ref[...] + bc_ref[...]
    gr = gr_ref[...] + br_ref[...]
    row = lax.broadcasted_iota(jnp.int32, (chunk, chunk), 0)
    col = lax.broadcasted_iota(jnp.int32, (chunk, chunk), 1)
    causal = row >= col
    tril = causal.astype(F32)
    triu = (row <= col).astype(F32)
    b_c = jnp.dot(tril, _log_sigmoid(gc), precision=HIGHEST, preferred_element_type=F32)
    b_r = jnp.dot(_log_sigmoid(gr), triu, precision=HIGHEST, preferred_element_type=F32)

    ob = ob_ref[...].astype(F32)
    g = g_ref[...]
    hs = range(heads)
    q = [qk[:, h * dk:(h + 1) * dk] * (dk ** -0.5) for h in hs]
    k = [qk[:, (heads + h) * dk:(heads + h + 1) * dk] for h in hs]
    v = [v_ref[:, h * dv:(h + 1) * dv] for h in hs]
    b_col = [b_c[:, heads + h:heads + h + 1] for h in hs]
    i_col = [gc[:, h:h + 1] for h in hs]
    b_row = [b_r[heads + h:heads + h + 1, :] for h in hs]
    i_row = [gr[h:h + 1, :] for h in hs]
    m_prev = [m_refs[h][0:1, 0:1] for h in hs]
    c_mat = [c_refs[h][...] for h in hs]
    n_vec = [n_refs[h][...] for h in hs]

    b_last = [b_row[h][:, chunk - 1:chunk] for h in hs]
    g_col = [b_last[h] - b_col[h] + i_col[h] for h in hs]
    m_new = [jnp.maximum(b_last[h] + m_prev[h], jnp.max(g_col[h], 0, keepdims=True)) for h in hs]
    decay = [jnp.exp(b_last[h] + m_prev[h] - m_new[h]) for h in hs]
    kw = [k[h] * jnp.exp(g_col[h] - m_new[h]) for h in hs]
    qb = [q[h].astype(BF16) for h in hs]
    qk_t = [lax.dot_general(qb[h], k[h].astype(BF16), NT_DIMS, preferred_element_type=F32) for h in hs]
    q_c = [jnp.dot(qb[h], c_mat[h].astype(BF16), preferred_element_type=F32) for h in hs]
    kw_v = [lax.dot_general(kw[h].astype(BF16), v[h], TN_DIMS, preferred_element_type=F32) for h in hs]

    d = [jnp.where(causal, b_col[h] - b_row[h] + i_row[h], -jnp.inf) for h in hs]
    a_col = [b_col[h] + m_prev[h] for h in hs]
    m_t = [jnp.maximum(a_col[h], jnp.max(d[h], -1, keepdims=True)) for h in hs]
    w_inter = [jnp.exp(a_col[h] - m_t[h]) for h in hs]
    sw = [qk_t[h] * jnp.exp(d[h] - m_t[h]) for h in hs]
    sw_v = [jnp.dot(sw[h].astype(BF16), v[h], preferred_element_type=F32) for h in hs]

    for h in hs:
        c_refs[h][...] = decay[h] * c_mat[h] + kw_v[h]
        n_refs[h][...] = decay[h] * n_vec[h] + jnp.sum(kw[h], 0, keepdims=True)
        m_refs[h][...] = jnp.broadcast_to(m_new[h], m_refs[h].shape)

    for h in hs:
        num = w_inter[h] * q_c[h] + sw_v[h]
        den = (w_inter[h] * jnp.sum(q[h] * n_vec[h], -1, keepdims=True)
               + jnp.sum(sw[h], -1, keepdims=True))
        hh = num / jnp.maximum(jnp.abs(den), jnp.exp(-m_t[h]))
        hn = hh * lax.rsqrt(jnp.mean(hh * hh, -1, keepdims=True) + NORM_EPS)
        sl = slice(h * dv, (h + 1) * dv)
        y_ref[:, sl] = (hn * g[:, sl] * _sigmoid(ob[:, sl])).astype(y_ref.dtype)


def _mlstm(y3, gcol3, grow3, conv_w, conv_b, gate_bias_cols, gate_bias_rows, mlstm_g, *, chunk):
    bsz, s, _ = y3.shape
    heads, dk, dv = B_HEADS, B_QK_DIM, B_V_DIM
    w = heads * dv
    assert 2 * heads * dk == w and s % chunk == 0 and chunk % LANES == 0
    nc = s // chunk
    return pl.pallas_call(
        functools.partial(_mlstm_kernel, chunk=chunk),
        grid=(bsz, nc),
        in_specs=[
            pl.BlockSpec((None, chunk, w), lambda b, c: (b, c, 3)),
            pl.BlockSpec((None, chunk, w), lambda b, c: (b, c, 4)),
            pl.BlockSpec((None, chunk, w), lambda b, c: (b, c, 5)),
            pl.BlockSpec((None, chunk, LANES), lambda b, c: (b, c, 0)),
            pl.BlockSpec((SUBLANES, chunk), lambda b, c: (0, b * nc + c)),
            pl.BlockSpec((CONV_WIDTH, w), lambda b, c: (0, 0)),
            pl.BlockSpec((1, w), lambda b, c: (0, 0)),
            pl.BlockSpec((1, LANES), lambda b, c: (0, 0)),
            pl.BlockSpec((SUBLANES, 1), lambda b, c: (0, 0)),
            pl.BlockSpec((1, w), lambda b, c: (0, 0)),
        ],
        out_specs=pl.BlockSpec((None, chunk, w), lambda b, c: (b, c, 0)),
        out_shape=jax.ShapeDtypeStruct((bsz, s, w), BF16),
        scratch_shapes=[
            pltpu.VMEM((BF16_SUBLANES + chunk, w), BF16),
        ] + heads * [pltpu.VMEM((dk, dv), F32), pltpu.VMEM((1, dk), F32), pltpu.VMEM((SUBLANES, LANES), F32)],
        compiler_params=_params("parallel", "arbitrary"),
        name="mlstm",
    )(y3, y3, y3, gcol3, grow3, conv_w, conv_b, gate_bias_cols, gate_bias_rows, mlstm_g)


def _gla_kernel(q_ref, k_ref, v_ref, r_ref, a1_ref, a1n_ref, wah_ref, wal_ref, ba_ref, g_ref, y_ref,
                b_ref, *st_refs, chunk):
    heads, dk, dv = C_HEADS, C_QK_DIM, C_V_DIM
    sub = GLA_SUB
    assert chunk == 2 * sub
    c = pl.program_id(1)
    row = lax.broadcasted_iota(jnp.int32, (chunk, chunk), 0)
    col = lax.broadcasted_iota(jnp.int32, (chunk, chunk), 1)
    causal = (row >= col) & ((row < sub) == (col < sub))
    first = lax.broadcasted_iota(jnp.int32, (chunk, 1), 0) < sub

    def cum_log_decay(a1):
        a_hi = a1.astype(BF16)
        a_lo = (a1 - a_hi.astype(F32)).astype(BF16)
        w_hi = wah_ref[...]
        z = (jnp.dot(a_hi, w_hi, preferred_element_type=F32)
             + (jnp.dot(a_lo, w_hi, preferred_element_type=F32)
                + jnp.dot(a_hi, wal_ref[...], preferred_element_type=F32))) + ba_ref[...]
        log_a = _log_sigmoid(z) * (1.0 / C_GATE_TEMP)
        return jnp.dot(causal.astype(F32), log_a, precision=HIGHEST, preferred_element_type=F32)

    @pl.when(c == 0)
    def _():
        for ref in st_refs:
            ref[...] = jnp.zeros_like(ref)
        b_ref[...] = cum_log_decay(a1_ref[...])

    b_all = b_ref[...]
    b_next = cum_log_decay(a1n_ref[...])

    r = r_ref[...].astype(F32)
    g = g_ref[...]
    for h in range(heads):
        ks = slice(h * dk, (h + 1) * dk)
        vs = slice(h * dv, (h + 1) * dv)
        b = b_all[:, ks]
        q = q_ref[:, ks].astype(F32) * (dk ** -0.5)
        k = k_ref[:, ks].astype(F32)
        v = v_ref[:, vs]
        tot0 = b[sub - 1:sub, :]
        tot1 = b[chunk - 1:chunk, :]
        q_dec = q * jnp.exp(b)
        k_dec = (k * jnp.exp(-b)).astype(BF16)
        k_end = k * jnp.exp(jnp.where(first, tot0, tot1) - b)
        k_new = (k_end * jnp.where(first, jnp.exp(tot1), 1.0)).astype(BF16)
        q_int = (q_dec * jnp.where(first, 1.0, jnp.exp(tot0))).astype(BF16)
        q_dec = q_dec.astype(BF16)
        st = st_refs[h][...]
        qk_t = lax.dot_general(q_dec, k_dec, NT_DIMS, preferred_element_type=F32)
        inter = lax.dot_general(q_int, st.astype(BF16), NT_DIMS, preferred_element_type=F32)
        st_refs[h][...] = (jnp.exp(tot0 + tot1) * st
                           + lax.dot_general(v, k_new, TN_DIMS, preferred_element_type=F32))
        cross = lax.dot_general(q_dec[sub:], k_end[:sub].astype(BF16), NT_DIMS, preferred_element_type=F32)
        att = jnp.where(causal, qk_t, 0.0)
        o = jnp.dot(att.astype(BF16), v, preferred_element_type=F32) + inter
        o_cross = jnp.dot(cross.astype(BF16), v[:sub], preferred_element_type=F32)
        o = jnp.concatenate([o[:sub], o[sub:] + o_cross], axis=0)

        on = o * lax.rsqrt(jnp.mean(o * o, -1, keepdims=True) + NORM_EPS)
        rr = r[:, vs]
        y_ref[:, vs] = (on * g[:, vs] * (rr * _sigmoid(rr))).astype(y_ref.dtype)

    b_ref[...] = b_next


def _gla(y3, a13, w_a2_hi, w_a2_lo, b_a, gla_g, *, chunk):
    bsz, s, _ = y3.shape
    heads, dk, dv = C_HEADS, C_QK_DIM, C_V_DIM
    wk, wv = heads * dk, heads * dv
    assert wv == 2 * wk and s % chunk == 0
    nc = s // chunk
    return pl.pallas_call(
        functools.partial(_gla_kernel, chunk=chunk),
        grid=(bsz, nc),
        in_specs=[
            pl.BlockSpec((None, chunk, wk), lambda b, c: (b, c, 0)),
            pl.BlockSpec((None, chunk, wk), lambda b, c: (b, c, 1)),
            pl.BlockSpec((None, chunk, wv), lambda b, c: (b, c, 1)),
            pl.BlockSpec((None, chunk, wv), lambda b, c: (b, c, 2)),
            pl.BlockSpec((None, chunk, LANES), lambda b, c: (b, c, 0)),
            pl.BlockSpec((None, chunk, LANES), lambda b, c: (b, jnp.minimum(c + 1, nc - 1), 0)),
            pl.BlockSpec((LANES, wk), lambda b, c: (0, 0)),
            pl.BlockSpec((LANES, wk), lambda b, c: (0, 0)),
            pl.BlockSpec((1, wk), lambda b, c: (0, 0)),
            pl.BlockSpec((1, wv), lambda b, c: (0, 0)),
        ],
        out_specs=pl.BlockSpec((None, chunk, wv), lambda b, c: (b, c, 0)),
        out_shape=jax.ShapeDtypeStruct((bsz, s, wv), BF16),
        scratch_shapes=[pltpu.VMEM((chunk, wk), F32)] + heads * [pltpu.VMEM((dv, dk), F32)],
        compiler_params=_params("parallel", "arbitrary"),
        name="gla",
    )(y3, y3, y3, y3, a13, a13, w_a2_hi, w_a2_lo, b_a, gla_g)


def _out_ln_kernel(*refs, n_in, alpha):
    y_refs, w_refs = refs[:n_in], refs[n_in:2 * n_in]
    x_ref, g_ref, b_ref, o_ref = refs[2 * n_in:]
    rb = o_ref.shape[0] // ROW_SPLIT
    for r in range(ROW_SPLIT):
        rows = slice(r * rb, (r + 1) * rb)
        acc = jnp.dot(y_refs[0][rows, :], w_refs[0][...], preferred_element_type=F32)
        for y_ref, w_ref in zip(y_refs[1:], w_refs[1:]):
            acc = acc + jnp.dot(y_ref[rows, :], w_ref[...], preferred_element_type=F32)
        o_ref[rows, :] = _layer_norm(alpha * x_ref[rows, :] + acc, g_ref[...], b_ref[...])


def _out_ln(ys, w, x, ln_g, ln_b, layer, w_layer, ln_slot, *, alpha, tm):
    t, d = x.shape
    n_in = len(ys)
    wd = ys[0].shape[1]
    assert all(y.shape[1] == wd for y in ys) and w.shape[1:] == (n_in * wd, d)
    ln_spec = pl.BlockSpec((None, None, 1, d), lambda i: (layer, ln_slot, 0, 0))
    in_specs = ([pl.BlockSpec((tm, wd), lambda i: (i, 0)) for _ in ys]
                + [pl.BlockSpec((None, wd, d), lambda i, k=k: (w_layer, k, 0)) for k in range(n_in)]
                + [pl.BlockSpec((tm, d), lambda i: (i, 0)), ln_spec, ln_spec])
    return pl.pallas_call(
        functools.partial(_out_ln_kernel, n_in=n_in, alpha=alpha),
        grid=(t // tm,),
        in_specs=in_specs,
        out_specs=pl.BlockSpec((tm, d), lambda i: (i, 0)),
        out_shape=jax.ShapeDtypeStruct((t, d), F32),
        compiler_params=_params("parallel"),
        name="out_ln",
    )(*ys, *([w] * n_in), x, ln_g, ln_b)


def _ple_kernel(x_ref, p_ref, wp_ref, wg_ref, o_ref):
    rb = o_ref.shape[0] // ROW_SPLIT
    for r in range(ROW_SPLIT):
        rows = slice(r * rb, (r + 1) * rb)
        x = x_ref[rows, :]
        e = jnp.dot(p_ref[rows, :].astype(BF16), wp_ref[...], preferred_element_type=F32)
        gate = jnp.dot(x.astype(BF16), wg_ref[...], preferred_element_type=F32)
        o_ref[rows, :] = x + e * _sigmoid(gate)


def _ple(x, p, w_proj, w_gate, layer, *, tm):
    t, d = x.shape
    s, pd = p.shape[2:]
    assert s % tm == 0
    per_seq = s // tm
    return pl.pallas_call(
        _ple_kernel,
        grid=(t // tm,),
        in_specs=[
            pl.BlockSpec((tm, d), lambda i: (i, 0)),
            pl.BlockSpec((None, None, tm, pd), lambda i: (layer, i // per_seq, i % per_seq, 0)),
            pl.BlockSpec((None, pd, d), lambda i: (layer, 0, 0)),
            pl.BlockSpec((None, d, d), lambda i: (layer, 0, 0)),
        ],
        out_specs=pl.BlockSpec((tm, d), lambda i: (i, 0)),
        out_shape=jax.ShapeDtypeStruct((t, d), F32),
        compiler_params=_params("parallel"),
        name="ple",
    )(x, p, w_proj, w_gate)


def _tiles(t, s):
    return dict(
        ffn_tm=min(1024, t), ffn_tf=256,
        proj_tm=min(1024, t), proj_tn=1024,
        row_tm=min(512, t),
        attn_t=min(512, s),
        mlstm_chunk=min(128, s),
        gla_chunk=min(2 * GLA_SUB, s),
    )


def _cast_job(w2d, unit, skip, count, steps):
    rows = next(r for r in range(BF16_SUBLANES, unit + 1, BF16_SUBLANES)
                if unit % r == 0 and count * (unit // r) <= steps)
    return w2d, rows, skip * (unit // rows), count * (unit // rows)


def _pad_cols(w, n):
    return jnp.pad(w, ((0, 0), (0, n - w.shape[1])))


def kernel(x, p, ln_g, ln_b, w_ffn_in, w_ffn_out, w_in_ab, w_out_ab, rel_bias, lambda_q1, lambda_k1,
           lambda_q2, lambda_k2, diff_norm, conv_w, conv_b, b_igate, b_fgate, mlstm_norm, w_in_c,
           w_alpha2, b_alpha, gla_norm, w_out_c, w_ple_proj, w_ple_gate):
    bsz, s, d = x.shape
    depth = p.shape[0]
    t = bsz * s
    tl = _tiles(t, s)
    alpha = (2 * depth) ** 0.25
    a_w = A_HEADS * A_V_DIM
    ab_main = 3 * a_w + 3 * B_HEADS * B_V_DIM
    c_main = 2 * C_HEADS * C_QK_DIM + 2 * C_HEADS * C_V_DIM
    row = lambda a: a.reshape(1, -1)

    n_ffn, f_ff = 2 * depth, w_ffn_out.shape[2]
    attn_steps = bsz * A_HEADS * (s // tl["attn_t"])
    wfi_first, wfo_first = w_ffn_in[0, :1].astype(BF16), w_ffn_out[0, :1].astype(BF16)
    ffn_w = {0: (wfi_first, wfo_first, 0)}
    cast_jobs = [
        _cast_job(w_ffn_in.reshape(n_ffn * d, -1), d, 1, n_ffn - 1, attn_steps),
        _cast_job(w_ffn_out.reshape(n_ffn * f_ff, d), f_ff, 1, n_ffn - 1, attn_steps),
        _cast_job(jnp.swapaxes(w_in_c, 1, 2).reshape(-1, d), w_in_c.shape[2], 0, w_in_c.shape[0], attn_steps),
    ]
    wo_ab, wo_c = w_out_ab.astype(BF16), w_out_c.astype(BF16)
    wpp, wpg = w_ple_proj.astype(BF16), w_ple_gate.astype(BF16)
    g4, b4 = ln_g.reshape(depth, 3, 1, d), ln_b.reshape(depth, 3, 1, d)

    xf = x.reshape(t, d)
    for i in range(depth):
        xf = _ffn_ln(xf, *ffn_w[2 * i], g4, b4, i, 0, alpha=alpha, tm=tl["ffn_tm"], tf=tl["ffn_tf"])
        if i % 2 == 0:
            e = i // 2
            lam_init = 0.8 - 0.6 * math.exp(-0.3 * i)
            y, gcol, grow = _in_proj(xf, w_in_ab, e, ab_main, tm=tl["proj_tm"], tn=tl["proj_tn"])
            y3 = y.reshape(bsz, s, ab_main)
            lam_vecs = jnp.stack([lambda_q1[e], lambda_k1[e], lambda_q2[e], lambda_k2[e]])
            ya, *cast = _diff_attention(y3, rel_bias, lam_vecs, row(diff_norm[e]), cast_jobs,
                                        lam_init=lam_init, t=tl["attn_t"])
            if i == 0:
                wfi_rest, wfo_rest = cast[0].reshape(n_ffn - 1, d, -1), cast[1].reshape(n_ffn - 1, f_ff, d)
                ffn_w.update({n: (wfi_rest, wfo_rest, n - 1) for n in range(1, n_ffn)})
                w_c_t = cast[2].reshape(w_in_c.shape[0], w_in_c.shape[2], d)
            gate_bias = jnp.concatenate([b_igate[e], b_fgate[e]])
            yb = _mlstm(y3, gcol.reshape(bsz, s, LANES), grow,
                        conv_w[e], row(conv_b[e]), _pad_cols(row(gate_bias), LANES), gate_bias.reshape(-1, 1),
                        row(mlstm_norm[e]), chunk=tl["mlstm_chunk"])
            ys, w_out, w_layer = [ya.reshape(t, a_w), yb.reshape(t, -1)], wo_ab, e
        else:
            o = i // 2
            y, a1, _ = _in_proj(xf, w_in_c, o, c_main, tm=tl["proj_tm"], tn=tl["proj_tn"], w_t=w_c_t)
            w_a2 = jnp.pad(w_alpha2[o], ((0, LANES - w_alpha2.shape[1]), (0, 0)))
            w_a2_hi = lax.reduce_precision(w_a2, exponent_bits=8, mantissa_bits=7)
            w_a2_lo = (w_a2 - w_a2_hi).astype(BF16)
            yc = _gla(y.reshape(bsz, s, c_main), a1.reshape(bsz, s, LANES), w_a2_hi.astype(BF16), w_a2_lo,
                      row(b_alpha[o]),
                      row(gla_norm[o]), chunk=tl["gla_chunk"])
            ys, w_out, w_layer = [yc.reshape(t, -1)], wo_c, o
        xf = _out_ln(ys, w_out, xf, g4, b4, i, w_layer, 1, alpha=alpha, tm=tl["row_tm"])
        xf = _ffn_ln(xf, *ffn_w[2 * i + 1], g4, b4, i, 2, alpha=alpha, tm=tl["ffn_tm"], tf=tl["ffn_tf"])
        xf = _ple(xf, p, wpp, wpg, i, tm=tl["row_tm"])
    return xf.reshape(bsz, s, d)
```

```python
import functools
import math

import jax
import jax.numpy as jnp
from jax import lax
from jax.experimental import pallas as pl
from jax.experimental.pallas import tpu as pltpu

F32 = jnp.float32
BF16 = jnp.bfloat16
HIGHEST = lax.Precision.HIGHEST

A_HEADS = 8
A_QK_DIM = 64
A_V_DIM = 128
B_HEADS = 4
B_QK_DIM = 128
B_V_DIM = 256
CONV_WIDTH = 4
C_HEADS = 4
C_QK_DIM = 256
C_V_DIM = 512
C_GATE_TEMP = 16.0
GLA_SUB = 64
N_BUCKETS = 32
MAX_DISTANCE = 128
LN_EPS = 1e-5
NORM_EPS = 1e-6

V7X_VMEM_LIMIT_BYTES = 56 * 1024 * 1024
LANES = 128
SUBLANES = 8
BF16_SUBLANES = 16
ATTN_ONES_ROWS = BF16_SUBLANES
MASKED_LOGIT = -1e30
ROW_SPLIT = 4

NT_DIMS = (((1,), (1,)), ((), ()))
TN_DIMS = (((0,), (0,)), ((), ()))


def _params(*sem):
    return pltpu.CompilerParams(dimension_semantics=sem, vmem_limit_bytes=V7X_VMEM_LIMIT_BYTES)


def _layer_norm(y, g, b, eps=LN_EPS):
    mu = jnp.mean(y, -1, keepdims=True)
    yc = y - mu
    var = jnp.mean(yc * yc, -1, keepdims=True)
    return yc * lax.rsqrt(var + eps) * g + b


def _sigmoid(x):
    return 1.0 / (1.0 + jnp.exp(-x))


def _log_sigmoid(x):
    return jnp.minimum(x, 0.0) - jnp.log(1.0 + jnp.exp(-jnp.abs(x)))


def _ffn_ln_kernel(x_ref, wg0_ref, wu0_ref, wo0_ref, wg1_ref, wu1_ref, wo1_ref, g_ref, b_ref, o_ref, xb_ref,
                   *, alpha, n_chunks):
    j = pl.program_id(1)
    last = pl.num_programs(1) - 1

    @pl.when(j == 0)
    def _():
        x = x_ref[...]
        xb_ref[...] = x.astype(BF16)
        o_ref[...] = (2.0 * alpha) * x

    def chunk(wg_ref, wu_ref, wo_ref, finish=False):
        xb = xb_ref[...]
        gate = jnp.dot(xb, wg_ref[...], preferred_element_type=F32)
        up = jnp.dot(xb, wu_ref[...], preferred_element_type=F32)
        h = (gate * _sigmoid(gate) * up).astype(BF16)
        if not finish:
            o_ref[...] += jnp.dot(h, wo_ref[...], preferred_element_type=F32)
            return
        rb = o_ref.shape[0] // ROW_SPLIT
        for r in range(ROW_SPLIT):
            rows = slice(r * rb, (r + 1) * rb)
            y = o_ref[rows, :] + jnp.dot(h[rows, :], wo_ref[...], preferred_element_type=F32)
            o_ref[rows, :] = _layer_norm(y, g_ref[...], b_ref[...], eps=4.0 * LN_EPS)

    @pl.when(j < last)
    def _():
        chunk(wg0_ref, wu0_ref, wo0_ref)
        chunk(wg1_ref, wu1_ref, wo1_ref)

    @pl.when(j == last)
    def _():
        if n_chunks % 2 == 0:
            chunk(wg0_ref, wu0_ref, wo0_ref)
            chunk(wg1_ref, wu1_ref, wo1_ref, finish=True)
        else:
            chunk(wg0_ref, wu0_ref, wo0_ref, finish=True)


def _ffn_ln(x, w_in, w_out, w_idx, ln_g, ln_b, layer, ln_slot, *, alpha, tm, tf):
    t, d = x.shape
    f = w_out.shape[1]
    nf = f // tf
    assert t % tm == 0 and f % tf == 0 and w_in.shape[1:] == (d, 2 * f) and nf >= 2
    steps = (nf + 1) // 2
    first = lambda j: 2 * j
    second = lambda j: jnp.where(2 * j + 1 < nf, 2 * j + 1, nf - 2)
    ln_spec = pl.BlockSpec((None, None, 1, d), lambda i, j: (layer, ln_slot, 0, 0))

    def weight_specs(chunk_of):
        return [pl.BlockSpec((None, d, tf), lambda i, j: (w_idx, 0, chunk_of(j))),
                pl.BlockSpec((None, d, tf), lambda i, j: (w_idx, 0, nf + chunk_of(j))),
                pl.BlockSpec((None, tf, d), lambda i, j: (w_idx, chunk_of(j), 0))]

    return pl.pallas_call(
        functools.partial(_ffn_ln_kernel, alpha=alpha, n_chunks=nf),
        grid=(t // tm, steps),
        in_specs=[pl.BlockSpec((tm, d), lambda i, j: (i, 0))] + weight_specs(first) + weight_specs(second)
                 + [ln_spec, ln_spec],
        out_specs=pl.BlockSpec((tm, d), lambda i, j: (i, 0)),
        out_shape=jax.ShapeDtypeStruct((t, d), F32),
        scratch_shapes=[pltpu.VMEM((tm, d), BF16)],
        compiler_params=_params("parallel", "arbitrary"),
        name="ffn_ln",
    )(x, w_in, w_in, w_out, w_in, w_in, w_out, ln_g, ln_b)


def _in_proj_kernel(x_ref, w_ref, wgc_ref, wgr_ref, y_ref, gc_ref, gr_ref, xb_ref):
    j = pl.program_id(1)

    @pl.when(j == 0)
    def _():
        xb = x_ref[...].astype(BF16)
        xb_ref[...] = xb
        gc_ref[...] = jnp.dot(xb, wgc_ref[...], preferred_element_type=F32)
        gr_ref[...] = lax.dot_general(wgr_ref[...], xb, NT_DIMS, preferred_element_type=F32)

    y_ref[...] = lax.dot_general(xb_ref[...], w_ref[...].astype(BF16), NT_DIMS,
                                 preferred_element_type=F32).astype(y_ref.dtype)


def _in_proj(x, w_stack, layer, n_main, *, tm, tn, w_t=None):
    t, d = x.shape
    assert t % tm == 0 and n_main % tn == 0
    if w_t is None:
        w_t = jnp.swapaxes(w_stack, 1, 2)
    w_gates = w_stack[layer, :, n_main:].astype(BF16)
    w_gate_cols = _pad_cols(w_gates, LANES)
    w_gate_rows = jnp.transpose(w_gates[:, :SUBLANES])
    return pl.pallas_call(
        _in_proj_kernel,
        grid=(t // tm, n_main // tn),
        in_specs=[
            pl.BlockSpec((tm, d), lambda i, j: (i, 0)),
            pl.BlockSpec((None, tn, d), lambda i, j: (layer, j, 0)),
            pl.BlockSpec((d, LANES), lambda i, j: (0, 0)),
            pl.BlockSpec((SUBLANES, d), lambda i, j: (0, 0)),
        ],
        out_specs=[
            pl.BlockSpec((tm, tn), lambda i, j: (i, j)),
            pl.BlockSpec((tm, LANES), lambda i, j: (i, 0)),
            pl.BlockSpec((SUBLANES, tm), lambda i, j: (0, i)),
        ],
        out_shape=[
            jax.ShapeDtypeStruct((t, n_main), BF16),
            jax.ShapeDtypeStruct((t, LANES), F32),
            jax.ShapeDtypeStruct((SUBLANES, t), F32),
        ],
        scratch_shapes=[pltpu.VMEM((tm, d), BF16)],
        compiler_params=_params("parallel", "arbitrary"),
        name="in_proj",
    )(x, w_t, w_gate_cols, w_gate_rows)


def _diff_attn_kernel(q_ref, k_ref, v_ref, tab_ref, lam_ref, g_ref, *refs, t, lam_init, n_jobs):
    cast_in, (o_ref, *cast_out) = refs[:n_jobs], refs[n_jobs:2 * n_jobs + 1]
    (vt_ref, bdiag_ref, bnear_ref, m_ref, acc_ref,
     s0_ref, mx0_ref, sh0_ref, s1_ref, mx1_ref, sh1_ref) = refs[2 * n_jobs + 1:]
    qi = pl.program_id(2)
    dk, dv = A_QK_DIM, A_V_DIM
    nk = k_ref.shape[0] // t
    bufs = ((s0_ref, mx0_ref, sh0_ref), (s1_ref, mx1_ref, sh1_ref))

    @pl.when(qi == 0)
    def _():
        for c in range(nk):
            vt_ref[c, 0:dv, :] = jnp.transpose(v_ref[c * t:(c + 1) * t, :].astype(F32)).astype(BF16)
            vt_ref[c, dv:, :] = jnp.ones((vt_ref.shape[1] - dv, t), BF16)
        skew = pltpu.roll(jnp.broadcast_to(tab_ref[...], (t, 2 * t)), 0, 1, stride=1, stride_axis=0)
        key = lax.broadcasted_iota(jnp.int32, (t, t), 0)
        qry = lax.broadcasted_iota(jnp.int32, (t, t), 1)
        bdiag_ref[...] = jnp.where(qry >= key, skew[:, :t], MASKED_LOGIT)
        bnear_ref[...] = skew[:, t:]

    q_t = jnp.transpose(q_ref[...].astype(F32)) * (dk ** -0.5)
    feat = lax.broadcasted_iota(jnp.int32, q_t.shape, 0)
    qs_t = jnp.concatenate([jnp.where(feat < dk, q_t, 0.0), jnp.where(feat >= dk, q_t, 0.0)],
                           axis=1).astype(BF16)

    m_ref[...] = jnp.full_like(m_ref, -jnp.inf)
    acc_ref[...] = jnp.zeros_like(acc_ref)

    def score(kj, buf, bias_ref, shift):
        s_ref, mx_ref, sh_ref = bufs[buf]
        off = pl.multiple_of(kj * t, t)
        s = jnp.dot(k_ref[pl.ds(off, t), :], qs_t, preferred_element_type=F32)
        if bias_ref is not None:
            b = bias_ref[...]
            s = s + jnp.concatenate([b, b], axis=1)
        s_ref[...] = s
        mx_ref[...] = jnp.max(s, 0, keepdims=True) + shift
        sh_ref[...] = jnp.zeros_like(sh_ref) + shift

    def absorb(kj, buf):
        s_ref, mx_ref, sh_ref = bufs[buf]
        m_old = m_ref[...]
        m_new = jnp.maximum(m_old, mx_ref[...])
        p = jnp.exp(s_ref[...] - (m_new - sh_ref[...])).astype(BF16)
        acc_ref[...] = (jnp.exp(m_old - m_new) * acc_ref[...]
                        + jnp.dot(vt_ref[kj], p, preferred_element_type=F32))
        m_ref[...] = m_new

    c_far = tab_ref[:, 2 * t - 1:2 * t]
    for src_ref, dst_ref in zip(cast_in, cast_out):
        dst_ref[...] = src_ref[...].astype(BF16)
    score(qi, 0, bdiag_ref, 0.0)

    @pl.when(qi >= 1)
    def _():
        score(qi - 1, 1, bnear_ref, 0.0)
        absorb(qi, 0)

    def far_pair(p, carry):
        kj = qi - 2 - 2 * p
        score(kj, 0, None, c_far)
        absorb(kj + 1, 1)
        score(kj - 1, 1, None, c_far)
        absorb(kj, 0)
        return carry

    lax.fori_loop(0, jnp.maximum(qi - 1, 0) // 2, far_pair, 0)

    @pl.when((qi >= 2) & (qi % 2 == 0))
    def _():
        score(0, 0, None, c_far)
        absorb(1, 1)

    @pl.when(qi % 2 == 0)
    def _():
        absorb(0, 0)

    @pl.when(qi % 2 == 1)
    def _():
        absorb(0, 1)

    lv = lam_ref[...]
    lam = (jnp.exp(jnp.sum(lv[0:1] * lv[1:2], -1, keepdims=True))
           - jnp.exp(jnp.sum(lv[2:3] * lv[3:4], -1, keepdims=True)) + lam_init)
    acc = acc_ref[...]
    o_t = acc[0:dv] * (1.0 / acc[dv:dv + 1])
    out = jnp.transpose(o_t[:, :t] - lam * o_t[:, t:])
    hn = out * lax.rsqrt(jnp.mean(out * out, -1, keepdims=True) + NORM_EPS)
    o_ref[...] = (hn * g_ref[...] * (1.0 - lam_init)).astype(o_ref.dtype)


def _t5_bias_by_distance(rel_bias, n):
    r = jnp.arange(n, dtype=jnp.int32)
    max_exact = N_BUCKETS // 2
    rf = jnp.maximum(r, 1).astype(F32)
    large = max_exact + (jnp.log(rf / max_exact) / math.log(MAX_DISTANCE / max_exact)
                         * (N_BUCKETS - max_exact)).astype(jnp.int32)
    large = jnp.minimum(large, N_BUCKETS - 1)
    bucket = jnp.where(r < max_exact, r, large)
    return jnp.transpose(rel_bias[bucket]).astype(F32)


def _diff_attention(y3, rel_bias, lam_vecs, diff_g, jobs, *, lam_init, t):
    bsz, s, _ = y3.shape
    hd = 2 * A_QK_DIM
    dv = A_V_DIM
    assert hd == dv == LANES and s % t == 0 and t >= MAX_DISTANCE
    nq = s // t
    table = _t5_bias_by_distance(rel_bias, 2 * t).reshape(A_HEADS, 1, 2 * t)
    assert all(blocks <= bsz * A_HEADS * nq for _, _, _, blocks in jobs)

    def job_block(blocks):
        return lambda b, h, q: jnp.minimum((b * A_HEADS + h) * nq + q, blocks - 1)

    cast_in = [pl.BlockSpec((rows, w.shape[1]), lambda b, h, q, f=first, blk=job_block(blocks): (f + blk(b, h, q), 0))
               for w, rows, first, blocks in jobs]
    cast_out = [pl.BlockSpec((rows, w.shape[1]), lambda b, h, q, blk=job_block(blocks): (blk(b, h, q), 0))
                for w, rows, first, blocks in jobs]
    cast_shapes = [jax.ShapeDtypeStruct((rows * blocks, w.shape[1]), BF16) for w, rows, first, blocks in jobs]
    return pl.pallas_call(
        functools.partial(_diff_attn_kernel, t=t, lam_init=lam_init, n_jobs=len(jobs)),
        grid=(bsz, A_HEADS, nq),
        in_specs=[
            pl.BlockSpec((None, t, hd), lambda b, h, q: (b, q, h)),
            pl.BlockSpec((None, s, hd), lambda b, h, q: (b, 0, A_HEADS + h)),
            pl.BlockSpec((None, s, dv), lambda b, h, q: (b, 0, 2 * A_HEADS + h)),
            pl.BlockSpec((None, 1, 2 * t), lambda b, h, q: (h, 0, 0)),
            pl.BlockSpec((4, A_QK_DIM), lambda b, h, q: (0, 0)),
            pl.BlockSpec((1, dv), lambda b, h, q: (0, h)),
        ] + cast_in,
        out_specs=[pl.BlockSpec((None, t, dv), lambda b, h, q: (b, q, h))] + cast_out,
        out_shape=[jax.ShapeDtypeStruct((bsz, s, A_HEADS * dv), BF16)] + cast_shapes,
        scratch_shapes=[pltpu.VMEM((nq, dv + ATTN_ONES_ROWS, t), BF16),
                        pltpu.VMEM((t, t), F32),
                        pltpu.VMEM((t, t), F32),
                        pltpu.VMEM((1, 2 * t), F32),
                        pltpu.VMEM((dv + ATTN_ONES_ROWS, 2 * t), F32)]
                       + 2 * [pltpu.VMEM((t, 2 * t), F32), pltpu.VMEM((1, 2 * t), F32),
                              pltpu.VMEM((1, 2 * t), F32)],
        compiler_params=_params("arbitrary", "arbitrary", "arbitrary"),
        name="diff_attn",
    )(y3, y3, y3, table, lam_vecs, diff_g, *[w for w, _, _, _ in jobs])


def _mlstm_kernel(qk_ref, v_ref, ob_ref, gc_ref, gr_ref, cw_ref, cb_ref, bc_ref, br_ref, g_ref, y_ref,
                  xp_ref, *state_refs, chunk):
    heads, dk, dv = B_HEADS, B_QK_DIM, B_V_DIM
    c_refs, n_refs, m_refs = state_refs[0::3], state_refs[1::3], state_refs[2::3]
    pad = xp_ref.shape[0] - chunk
    c = pl.program_id(1)

    @pl.when(c == 0)
    def _():
        xp_ref[0:pad, :] = jnp.zeros((pad, xp_ref.shape[1]), xp_ref.dtype)
        for ref in state_refs:
            ref[...] = jnp.zeros_like(ref)

    x = qk_ref[...]
    xp_ref[pad:pad + chunk, :] = x
    xp = xp_ref[...]
    cw = cw_ref[...]
    dst = lax.broadcasted_iota(jnp.int32, (chunk, pad + chunk), 0)
    src = lax.broadcasted_iota(jnp.int32, (chunk, pad + chunk), 1)
    conv = cb_ref[...] + cw[CONV_WIDTH - 1:CONV_WIDTH, :] * x.astype(F32)
    for j in range(CONV_WIDTH - 1):
        shift = (src == dst + (pad - (CONV_WIDTH - 1) + j)).astype(BF16)
        conv = conv + cw[j:j + 1, :] * jnp.dot(shift, xp, preferred_element_type=F32)
    xp_ref[0:pad, :] = x[chunk - pad:chunk, :]
    qk = conv * _sigmoid(conv)

    gc = gc_ref[...] + bc_ref[...]
    gr = gr_ref[...] + br_ref[...]
    row = lax.broadcasted_iota(jnp.int32, (chunk, chunk), 0)
    col = lax.broadcasted_iota(jnp.int32, (chunk, chunk), 1)
    causal = row >= col
    tril = causal.astype(F32)
    triu = (row <= col).astype(F32)
    b_c = jnp.dot(tril, _log_sigmoid(gc), precision=HIGHEST, preferred_element_type=F32)
    b_r = jnp.dot(_log_sigmoid(gr), triu, precision=HIGHEST, preferred_element_type=F32)

    ob = ob_ref[...].astype(F32)
    g = g_ref[...]
    hs = range(heads)
    q = [qk[:, h * dk:(h + 1) * dk] * (dk ** -0.5) for h in hs]
    k = [qk[:, (heads + h) * dk:(heads + h + 1) * dk] for h in hs]
    v = [v_ref[:, h * dv:(h + 1) * dv] for h in hs]
    b_col = [b_c[:, heads + h:heads + h + 1] for h in hs]
    i_col = [gc[:, h:h + 1] for h in hs]
    b_row = [b_r[heads + h:heads + h + 1, :] for h in hs]
    i_row = [gr[h:h + 1, :] for h in hs]
    m_prev = [m_refs[h][0:1, 0:1] for h in hs]
    c_mat = [c_refs[h][...] for h in hs]
    n_vec = [n_refs[h][...] for h in hs]

    b_last = [b_row[h][:, chunk - 1:chunk] for h in hs]
    g_col = [b_last[h] - b_col[h] + i_col[h] for h in hs]
    m_new = [jnp.maximum(b_last[h] + m_prev[h], jnp.max(g_col[h], 0, keepdims=True)) for h in hs]
    decay = [jnp.exp(b_last[h] + m_prev[h] - m_new[h]) for h in hs]
    kw = [k[h] * jnp.exp(g_col[h] - m_new[h]) for h in hs]
    qb = [q[h].astype(BF16) for h in hs]
    qk_t = [lax.dot_general(qb[h], k[h].astype(BF16), NT_DIMS, preferred_element_type=F32) for h in hs]
    q_c = [jnp.dot(qb[h], c_mat[h].astype(BF16), preferred_element_type=F32) for h in hs]
    kw_v = [lax.dot_general(kw[h].astype(BF16), v[h], TN_DIMS, preferred_element_type=F32) for h in hs]

    d = [jnp.where(causal, b_col[h] - b_row[h] + i_row[h], -jnp.inf) for h in hs]
    a_col = [b_col[h] + m_prev[h] for h in hs]
    m_t = [jnp.maximum(a_col[h], jnp.max(d[h], -1, keepdims=True)) for h in hs]
    w_inter = [jnp.exp(a_col[h] - m_t[h]) for h in hs]
    sw = [qk_t[h] * jnp.exp(d[h] - m_t[h]) for h in hs]
    sw_v = [jnp.dot(sw[h].astype(BF16), v[h], preferred_element_type=F32) for h in hs]

    for h in hs:
        c_refs[h][...] = decay[h] * c_mat[h] + kw_v[h]
        n_refs[h][...] = decay[h] * n_vec[h] + jnp.sum(kw[h], 0, keepdims=True)
        m_refs[h][...] = jnp.broadcast_to(m_new[h], m_refs[h].shape)

    for h in hs:
        num = w_inter[h] * q_c[h] + sw_v[h]
        den = (w_inter[h] * jnp.sum(q[h] * n_vec[h], -1, keepdims=True)
               + jnp.sum(sw[h], -1, keepdims=True))
        hh = num / jnp.maximum(jnp.abs(den), jnp.exp(-m_t[h]))
        hn = hh * lax.rsqrt(jnp.mean(hh * hh, -1, keepdims=True) + NORM_EPS)
        sl = slice(h * dv, (h + 1) * dv)
        y_ref[:, sl] = (hn * g[:, sl] * _sigmoid(ob[:, sl])).astype(y_ref.dtype)


def _mlstm(y3, gcol3, grow3, conv_w, conv_b, gate_bias_cols, gate_bias_rows, mlstm_g, *, chunk):
    bsz, s, _ = y3.shape
    heads, dk, dv = B_HEADS, B_QK_DIM, B_V_DIM
    w = heads * dv
    assert 2 * heads * dk == w and s % chunk == 0 and chunk % LANES == 0
    nc = s // chunk
    return pl.pallas_call(
        functools.partial(_mlstm_kernel, chunk=chunk),
        grid=(bsz, nc),
        in_specs=[
            pl.BlockSpec((None, chunk, w), lambda b, c: (b, c, 3)),
            pl.BlockSpec((None, chunk, w), lambda b, c: (b, c, 4)),
            pl.BlockSpec((None, chunk, w), lambda b, c: (b, c, 5)),
            pl.BlockSpec((None, chunk, LANES), lambda b, c: (b, c, 0)),
            pl.BlockSpec((SUBLANES, chunk), lambda b, c: (0, b * nc + c)),
            pl.BlockSpec((CONV_WIDTH, w), lambda b, c: (0, 0)),
            pl.BlockSpec((1, w), lambda b, c: (0, 0)),
            pl.BlockSpec((1, LANES), lambda b, c: (0, 0)),
            pl.BlockSpec((SUBLANES, 1), lambda b, c: (0, 0)),
            pl.BlockSpec((1, w), lambda b, c: (0, 0)),
        ],
        out_specs=pl.BlockSpec((None, chunk, w), lambda b, c: (b, c, 0)),
        out_shape=jax.ShapeDtypeStruct((bsz, s, w), BF16),
        scratch_shapes=[
            pltpu.VMEM((BF16_SUBLANES + chunk, w), BF16),
        ] + heads * [pltpu.VMEM((dk, dv), F32), pltpu.VMEM((1, dk), F32), pltpu.VMEM((SUBLANES, LANES), F32)],
        compiler_params=_params("parallel", "arbitrary"),
        name="mlstm",
    )(y3, y3, y3, gcol3, grow3, conv_w, conv_b, gate_bias_cols, gate_bias_rows, mlstm_g)


def _gla_kernel(q_ref, k_ref, v_ref, r_ref, a1_ref, a1n_ref, wah_ref, wal_ref, ba_ref, g_ref, y_ref,
                b_ref, *st_refs, chunk):
    heads, dk, dv = C_HEADS, C_QK_DIM, C_V_DIM
    sub = GLA_SUB
    assert chunk == 2 * sub
    c = pl.program_id(1)
    row = lax.broadcasted_iota(jnp.int32, (chunk, chunk), 0)
    col = lax.broadcasted_iota(jnp.int32, (chunk, chunk), 1)
    causal = (row >= col) & ((row < sub) == (col < sub))
    first = lax.broadcasted_iota(jnp.int32, (chunk, 1), 0) < sub

    def cum_log_decay(a1):
        a_hi = a1.astype(BF16)
        a_lo = (a1 - a_hi.astype(F32)).astype(BF16)
        w_hi = wah_ref[...]
        z = (jnp.dot(a_hi, w_hi, preferred_element_type=F32)
             + (jnp.dot(a_lo, w_hi, preferred_element_type=F32)
                + jnp.dot(a_hi, wal_ref[...], preferred_element_type=F32))) + ba_ref[...]
        log_a = _log_sigmoid(z) * (1.0 / C_GATE_TEMP)
        return jnp.dot(causal.astype(F32), log_a, precision=HIGHEST, preferred_element_type=F32)

    @pl.when(c == 0)
    def _():
        for ref in st_refs:
            ref[...] = jnp.zeros_like(ref)
        b_ref[...] = cum_log_decay(a1_ref[...])

    b_all = b_ref[...]
    b_next = cum_log_decay(a1n_ref[...])

    r = r_ref[...].astype(F32)
    g = g_ref[...]
    for h in range(heads):
        ks = slice(h * dk, (h + 1) * dk)
        vs = slice(h * dv, (h + 1) * dv)
        b = b_all[:, ks]
        q = q_ref[:, ks].astype(F32) * (dk ** -0.5)
        k = k_ref[:, ks].astype(F32)
        v = v_ref[:, vs]
        tot0 = b[sub - 1:sub, :]
        tot1 = b[chunk - 1:chunk, :]
        q_dec = q * jnp.exp(b)
        k_dec = (k * jnp.exp(-b)).astype(BF16)
        k_end = k * jnp.exp(jnp.where(first, tot0, tot1) - b)
        k_new = (k_end * jnp.where(first, jnp.exp(tot1), 1.0)).astype(BF16)
        q_int = (q_dec * jnp.where(first, 1.0, jnp.exp(tot0))).astype(BF16)
        q_dec = q_dec.astype(BF16)
        st = st_refs[h][...]
        qk_t = lax.dot_general(q_dec, k_dec, NT_DIMS, preferred_element_type=F32)
        inter = lax.dot_general(q_int, st.astype(BF16), NT_DIMS, preferred_element_type=F32)
        st_refs[h][...] = (jnp.exp(tot0 + tot1) * st
                           + lax.dot_general(v, k_new, TN_DIMS, preferred_element_type=F32))
        cross = lax.dot_general(q_dec[sub:], k_end[:sub].astype(BF16), NT_DIMS, preferred_element_type=F32)
        att = jnp.where(causal, qk_t, 0.0)
        o = jnp.dot(att.astype(BF16), v, preferred_element_type=F32) + inter
        o_cross = jnp.dot(cross.astype(BF16), v[:sub], preferred_element_type=F32)
        o = jnp.concatenate([o[:sub], o[sub:] + o_cross], axis=0)

        on = o * lax.rsqrt(jnp.mean(o * o, -1, keepdims=True) + NORM_EPS)
        rr = r[:, vs]
        y_ref[:, vs] = (on * g[:, vs] * (rr * _sigmoid(rr))).astype(y_ref.dtype)

    b_ref[...] = b_next


def _gla(y3, a13, w_a2_hi, w_a2_lo, b_a, gla_g, *, chunk):
    bsz, s, _ = y3.shape
    heads, dk, dv = C_HEADS, C_QK_DIM, C_V_DIM
    wk, wv = heads * dk, heads * dv
    assert wv == 2 * wk and s % chunk == 0
    nc = s // chunk
    return pl.pallas_call(
        functools.partial(_gla_kernel, chunk=chunk),
        grid=(bsz, nc),
        in_specs=[
            pl.BlockSpec((None, chunk, wk), lambda b, c: (b, c, 0)),
            pl.BlockSpec((None, chunk, wk), lambda b, c: (b, c, 1)),
            pl.BlockSpec((None, chunk, wv), lambda b, c: (b, c, 1)),
            pl.BlockSpec((None, chunk, wv), lambda b, c: (b, c, 2)),
            pl.BlockSpec((None, chunk, LANES), lambda b, c: (b, c, 0)),
            pl.BlockSpec((None, chunk, LANES), lambda b, c: (b, jnp.minimum(c + 1, nc - 1), 0)),
            pl.BlockSpec((LANES, wk), lambda b, c: (0, 0)),
            pl.BlockSpec((LANES, wk), lambda b, c: (0, 0)),
            pl.BlockSpec((1, wk), lambda b, c: (0, 0)),
            pl.BlockSpec((1, wv), lambda b, c: (0, 0)),
        ],
        out_specs=pl.BlockSpec((None, chunk, wv), lambda b, c: (b, c, 0)),
        out_shape=jax.ShapeDtypeStruct((bsz, s, wv), BF16),
        scratch_shapes=[pltpu.VMEM((chunk, wk), F32)] + heads * [pltpu.VMEM((dv, dk), F32)],
        compiler_params=_params("parallel", "arbitrary"),
        name="gla",
    )(y3, y3, y3, y3, a13, a13, w_a2_hi, w_a2_lo, b_a, gla_g)


def _out_ln_kernel(*refs, n_in, alpha):
    y_refs, w_refs = refs[:n_in], refs[n_in:2 * n_in]
    x_ref, g_ref, b_ref, o_ref = refs[2 * n_in:]
    rb = o_ref.shape[0] // ROW_SPLIT
    for r in range(ROW_SPLIT):
        rows = slice(r * rb, (r + 1) * rb)
        acc = jnp.dot(y_refs[0][rows, :], w_refs[0][...], preferred_element_type=F32)
        for y_ref, w_ref in zip(y_refs[1:], w_refs[1:]):
            acc = acc + jnp.dot(y_ref[rows, :], w_ref[...], preferred_element_type=F32)
        o_ref[rows, :] = _layer_norm(alpha * x_ref[rows, :] + acc, g_ref[...], b_ref[...])


def _out_ln(ys, w, x, ln_g, ln_b, layer, w_layer, ln_slot, *, alpha, tm):
    t, d = x.shape
    n_in = len(ys)
    wd = ys[0].shape[1]
    assert all(y.shape[1] == wd for y in ys) and w.shape[1:] == (n_in * wd, d)
    ln_spec = pl.BlockSpec((None, None, 1, d), lambda i: (layer, ln_slot, 0, 0))
    in_specs = ([pl.BlockSpec((tm, wd), lambda i: (i, 0)) for _ in ys]
                + [pl.BlockSpec((None, wd, d), lambda i, k=k: (w_layer, k, 0)) for k in range(n_in)]
                + [pl.BlockSpec((tm, d), lambda i: (i, 0)), ln_spec, ln_spec])
    return pl.pallas_call(
        functools.partial(_out_ln_kernel, n_in=n_in, alpha=alpha),
        grid=(t // tm,),
        in_specs=in_specs,
        out_specs=pl.BlockSpec((tm, d), lambda i: (i, 0)),
        out_shape=jax.ShapeDtypeStruct((t, d), F32),
        compiler_params=_params("parallel"),
        name="out_ln",
    )(*ys, *([w] * n_in), x, ln_g, ln_b)


def _ple_kernel(x_ref, p_ref, wp_ref, wg_ref, o_ref):
    rb = o_ref.shape[0] // ROW_SPLIT
    for r in range(ROW_SPLIT):
        rows = slice(r * rb, (r + 1) * rb)
        x = x_ref[rows, :]
        e = jnp.dot(p_ref[rows, :].astype(BF16), wp_ref[...], preferred_element_type=F32)
        gate = jnp.dot(x.astype(BF16), wg_ref[...], preferred_element_type=F32)
        o_ref[rows, :] = x + e * _sigmoid(gate)


def _ple(x, p, w_proj, w_gate, layer, *, tm):
    t, d = x.shape
    s, pd = p.shape[2:]
    assert s % tm == 0
    per_seq = s // tm
    return pl.pallas_call(
        _ple_kernel,
        grid=(t // tm,),
        in_specs=[
            pl.BlockSpec((tm, d), lambda i: (i, 0)),
            pl.BlockSpec((None, None, tm, pd), lambda i: (layer, i // per_seq, i % per_seq, 0)),
            pl.BlockSpec((None, pd, d), lambda i: (layer, 0, 0)),
            pl.BlockSpec((None, d, d), lambda i: (layer, 0, 0)),
        ],
        out_specs=pl.BlockSpec((tm, d), lambda i: (i, 0)),
        out_shape=jax.ShapeDtypeStruct((t, d), F32),
        compiler_params=_params("parallel"),
        name="ple",
    )(x, p, w_proj, w_gate)


def _tiles(t, s):
    return dict(
        ffn_tm=min(1024, t), ffn_tf=256,
        proj_tm=min(1024, t), proj_tn=1024,
        row_tm=min(512, t),
        attn_t=min(512, s),
        mlstm_chunk=min(128, s),
        gla_chunk=min(2 * GLA_SUB, s),
    )


def _cast_job(w2d, unit, skip, count, steps):
    rows = next(r for r in range(BF16_SUBLANES, unit + 1, BF16_SUBLANES)
                if unit % r == 0 and count * (unit // r) <= steps)
    return w2d, rows, skip * (unit // rows), count * (unit // rows)


def _pad_cols(w, n):
    return jnp.pad(w, ((0, 0), (0, n - w.shape[1])))


def kernel(x, p, ln_g, ln_b, w_ffn_in, w_ffn_out, w_in_ab, w_out_ab, rel_bias, lambda_q1, lambda_k1,
           lambda_q2, lambda_k2, diff_norm, conv_w, conv_b, b_igate, b_fgate, mlstm_norm, w_in_c,
           w_alpha2, b_alpha, gla_norm, w_out_c, w_ple_proj, w_ple_gate):
    bsz, s, d = x.shape
    depth = p.shape[0]
    t = bsz * s
    tl = _tiles(t, s)
    alpha = (2 * depth) ** 0.25
    a_w = A_HEADS * A_V_DIM
    ab_main = 3 * a_w + 3 * B_HEADS * B_V_DIM
    c_main = 2 * C_HEADS * C_QK_DIM + 2 * C_HEADS * C_V_DIM
    row = lambda a: a.reshape(1, -1)

    n_ffn, f_ff = 2 * depth, w_ffn_out.shape[2]
    attn_steps = bsz * A_HEADS * (s // tl["attn_t"])
    wfi_first, wfo_first = w_ffn_in[0, :1].astype(BF16), w_ffn_out[0, :1].astype(BF16)
    ffn_w = {0: (wfi_first, wfo_first, 0)}
    cast_jobs = [
        _cast_job(w_ffn_in.reshape(n_ffn * d, -1), d, 1, n_ffn - 1, attn_steps),
        _cast_job(w_ffn_out.reshape(n_ffn * f_ff, d), f_ff, 1, n_ffn - 1, attn_steps),
        _cast_job(jnp.swapaxes(w_in_c, 1, 2).reshape(-1, d), w_in_c.shape[2], 0, w_in_c.shape[0], attn_steps),
    ]
    wo_ab, wo_c = w_out_ab.astype(BF16), w_out_c.astype(BF16)
    wpp, wpg = w_ple_proj.astype(BF16), w_ple_gate.astype(BF16)
    g4, b4 = ln_g.reshape(depth, 3, 1, d), ln_b.reshape(depth, 3, 1, d)

    xf = x.reshape(t, d)
    for i in range(depth):
        xf = _ffn_ln(xf, *ffn_w[2 * i], g4, b4, i, 0, alpha=alpha, tm=tl["ffn_tm"], tf=tl["ffn_tf"])
        if i % 2 == 0:
            e = i // 2
            lam_init = 0.8 - 0.6 * math.exp(-0.3 * i)
            y, gcol, grow = _in_proj(xf, w_in_ab, e, ab_main, tm=tl["proj_tm"], tn=tl["proj_tn"])
            y3 = y.reshape(bsz, s, ab_main)
            lam_vecs = jnp.stack([lambda_q1[e], lambda_k1[e], lambda_q2[e], lambda_k2[e]])
            ya, *cast = _diff_attention(y3, rel_bias, lam_vecs, row(diff_norm[e]), cast_jobs,
                                        lam_init=lam_init, t=tl["attn_t"])
            if i == 0:
                wfi_rest, wfo_rest = cast[0].reshape(n_ffn - 1, d, -1), cast[1].reshape(n_ffn - 1, f_ff, d)
                ffn_w.update({n: (wfi_rest, wfo_rest, n - 1) for n in range(1, n_ffn)})
                w_c_t = cast[2].reshape(w_in_c.shape[0], w_in_c.shape[2], d)
            gate_bias = jnp.concatenate([b_igate[e], b_fgate[e]])
            yb = _mlstm(y3, gcol.reshape(bsz, s, LANES), grow,
                        conv_w[e], row(conv_b[e]), _pad_cols(row(gate_bias), LANES), gate_bias.reshape(-1, 1),
                        row(mlstm_norm[e]), chunk=tl["mlstm_chunk"])
            ys, w_out, w_layer = [ya.reshape(t, a_w), yb.reshape(t, -1)], wo_ab, e
        else:
            o = i // 2
            y, a1, _ = _in_proj(xf, w_in_c, o, c_main, tm=tl["proj_tm"], tn=tl["proj_tn"], w_t=w_c_t)
            w_a2 = jnp.pad(w_alpha2[o], ((0, LANES - w_alpha2.shape[1]), (0, 0)))
            w_a2_hi = lax.reduce_precision(w_a2, exponent_bits=8, mantissa_bits=7)
            w_a2_lo = (w_a2 - w_a2_hi).astype(BF16)
            yc = _gla(y.reshape(bsz, s, c_main), a1.reshape(bsz, s, LANES), w_a2_hi.astype(BF16), w_a2_lo,
                      row(b_alpha[o]),
                      row(gla_norm[o]), chunk=tl["gla_chunk"])
            ys, w_out, w_layer = [yc.reshape(t, -1)], wo_c, o
        xf = _out_ln(ys, w_out, xf, g4, b4, i, w_layer, 1, alpha=alpha, tm=tl["row_tm"])
        xf = _ffn_ln(xf, *ffn_w[2 * i + 1], g4, b4, i, 2, alpha=alpha, tm=tl["ffn_tm"], tf=tl["ffn_tf"])
        xf = _ple(xf, p, wpp, wpg, i, tm=tl["row_tm"])
    return xf.reshape(bsz, s, d)
```

```python
import functools
import math

import jax
import jax.numpy as jnp
from jax import lax
from jax.experimental import pallas as pl
from jax.experimental.pallas import tpu as pltpu

F32 = jnp.float32
BF16 = jnp.bfloat16
HIGHEST = lax.Precision.HIGHEST

A_HEADS = 8
A_QK_DIM = 64
A_V_DIM = 128
B_HEADS = 4
B_QK_DIM = 128
B_V_DIM = 256
CONV_WIDTH = 4
C_HEADS = 4
C_QK_DIM = 256
C_V_DIM = 512
C_GATE_TEMP = 16.0
GLA_SUB = 64
N_BUCKETS = 32
MAX_DISTANCE = 128
LN_EPS = 1e-5
NORM_EPS = 1e-6

V7X_VMEM_LIMIT_BYTES = 56 * 1024 * 1024
LANES = 128
SUBLANES = 8
BF16_SUBLANES = 16
ATTN_ONES_ROWS = BF16_SUBLANES
MASKED_LOGIT = -1e30
ROW_SPLIT = 4

NT_DIMS = (((1,), (1,)), ((), ()))
TN_DIMS = (((0,), (0,)), ((), ()))


def _params(*sem):
    return pltpu.CompilerParams(dimension_semantics=sem, vmem_limit_bytes=V7X_VMEM_LIMIT_BYTES)


def _layer_norm(y, g, b, eps=LN_EPS):
    mu = jnp.mean(y, -1, keepdims=True)
    yc = y - mu
    var = jnp.mean(yc * yc, -1, keepdims=True)
    return yc * lax.rsqrt(var + eps) * g + b


def _sigmoid(x):
    return 1.0 / (1.0 + jnp.exp(-x))


def _log_sigmoid(x):
    return jnp.minimum(x, 0.0) - jnp.log(1.0 + jnp.exp(-jnp.abs(x)))


def _ffn_ln_kernel(x_ref, wg0_ref, wu0_ref, wo0_ref, wg1_ref, wu1_ref, wo1_ref, g_ref, b_ref, o_ref, xb_ref,
                   *, alpha, n_chunks):
    j = pl.program_id(1)
    last = pl.num_programs(1) - 1

    @pl.when(j == 0)
    def _():
        x = x_ref[...]
        xb_ref[...] = x.astype(BF16)
        o_ref[...] = (2.0 * alpha) * x

    def chunk(wg_ref, wu_ref, wo_ref):
        xb = xb_ref[...]
        gate = jnp.dot(xb, wg_ref[...], preferred_element_type=F32)
        up = jnp.dot(xb, wu_ref[...], preferred_element_type=F32)
        h = (gate * _sigmoid(gate) * up).astype(BF16)
        o_ref[...] += jnp.dot(h, wo_ref[...], preferred_element_type=F32)

    if n_chunks % 2 == 0:
        chunk(wg0_ref, wu0_ref, wo0_ref)
        chunk(wg1_ref, wu1_ref, wo1_ref)
    else:
        @pl.when(j < last)
        def _():
            chunk(wg0_ref, wu0_ref, wo0_ref)
            chunk(wg1_ref, wu1_ref, wo1_ref)

        @pl.when(j == last)
        def _():
            chunk(wg0_ref, wu0_ref, wo0_ref)

    @pl.when(j == last)
    def _():
        o_ref[...] = _layer_norm(o_ref[...], g_ref[...], b_ref[...], eps=4.0 * LN_EPS)


def _ffn_ln(x, w_in, w_out, w_idx, ln_g, ln_b, layer, ln_slot, *, alpha, tm, tf):
    t, d = x.shape
    f = w_out.shape[1]
    nf = f // tf
    assert t % tm == 0 and f % tf == 0 and w_in.shape[1:] == (d, 2 * f) and nf >= 2
    steps = (nf + 1) // 2
    first = lambda j: 2 * j
    second = lambda j: jnp.where(2 * j + 1 < nf, 2 * j + 1, nf - 2)
    ln_spec = pl.BlockSpec((None, None, 1, d), lambda i, j: (layer, ln_slot, 0, 0))

    def weight_specs(chunk_of):
        return [pl.BlockSpec((None, d, tf), lambda i, j: (w_idx, 0, chunk_of(j))),
                pl.BlockSpec((None, d, tf), lambda i, j: (w_idx, 0, nf + chunk_of(j))),
                pl.BlockSpec((None, tf, d), lambda i, j: (w_idx, chunk_of(j), 0))]

    return pl.pallas_call(
        functools.partial(_ffn_ln_kernel, alpha=alpha, n_chunks=nf),
        grid=(t // tm, steps),
        in_specs=[pl.BlockSpec((tm, d), lambda i, j: (i, 0))] + weight_specs(first) + weight_specs(second)
                 + [ln_spec, ln_spec],
        out_specs=pl.BlockSpec((tm, d), lambda i, j: (i, 0)),
        out_shape=jax.ShapeDtypeStruct((t, d), F32),
        scratch_shapes=[pltpu.VMEM((tm, d), BF16)],
        compiler_params=_params("parallel", "arbitrary"),
        name="ffn_ln",
    )(x, w_in, w_in, w_out, w_in, w_in, w_out, ln_g, ln_b)


def _in_proj_kernel(x_ref, w_ref, wgc_ref, wgr_ref, y_ref, gc_ref, gr_ref, xb_ref):
    j = pl.program_id(1)

    @pl.when(j == 0)
    def _():
        xb = x_ref[...].astype(BF16)
        xb_ref[...] = xb
        gc_ref[...] = jnp.dot(xb, wgc_ref[...], preferred_element_type=F32)
        gr_ref[...] = lax.dot_general(wgr_ref[...], xb, NT_DIMS, preferred_element_type=F32)

    y_ref[...] = lax.dot_general(xb_ref[...], w_ref[...].astype(BF16), NT_DIMS,
                                 preferred_element_type=F32).astype(y_ref.dtype)


def _in_proj(x, w_stack, layer, n_main, *, tm, tn, w_t=None):
    t, d = x.shape
    assert t % tm == 0 and n_main % tn == 0
    if w_t is None:
        w_t = jnp.swapaxes(w_stack, 1, 2)
    w_gates = w_stack[layer, :, n_main:].astype(BF16)
    w_gate_cols = _pad_cols(w_gates, LANES)
    w_gate_rows = jnp.transpose(w_gates[:, :SUBLANES])
    return pl.pallas_call(
        _in_proj_kernel,
        grid=(t // tm, n_main // tn),
        in_specs=[
            pl.BlockSpec((tm, d), lambda i, j: (i, 0)),
            pl.BlockSpec((None, tn, d), lambda i, j: (layer, j, 0)),
            pl.BlockSpec((d, LANES), lambda i, j: (0, 0)),
            pl.BlockSpec((SUBLANES, d), lambda i, j: (0, 0)),
        ],
        out_specs=[
            pl.BlockSpec((tm, tn), lambda i, j: (i, j)),
            pl.BlockSpec((tm, LANES), lambda i, j: (i, 0)),
            pl.BlockSpec((SUBLANES, tm), lambda i, j: (0, i)),
        ],
        out_shape=[
            jax.ShapeDtypeStruct((t, n_main), BF16),
            jax.ShapeDtypeStruct((t, LANES), F32),
            jax.ShapeDtypeStruct((SUBLANES, t), F32),
        ],
        scratch_shapes=[pltpu.VMEM((tm, d), BF16)],
        compiler_params=_params("parallel", "arbitrary"),
        name="in_proj",
    )(x, w_t, w_gate_cols, w_gate_rows)


def _diff_attn_kernel(q_ref, k_ref, v_ref, tab_ref, lam_ref, g_ref, *refs, t, lam_init, n_jobs):
    cast_in, (o_ref, *cast_out) = refs[:n_jobs], refs[n_jobs:2 * n_jobs + 1]
    (vt_ref, bdiag_ref, bnear_ref, m_ref, acc_ref,
     s0_ref, mx0_ref, sh0_ref, s1_ref, mx1_ref, sh1_ref) = refs[2 * n_jobs + 1:]
    qi = pl.program_id(2)
    dk, dv = A_QK_DIM, A_V_DIM
    nk = k_ref.shape[0] // t
    bufs = ((s0_ref, mx0_ref, sh0_ref), (s1_ref, mx1_ref, sh1_ref))

    @pl.when(qi == 0)
    def _():
        for c in range(nk):
            vt_ref[c, 0:dv, :] = jnp.transpose(v_ref[c * t:(c + 1) * t, :].astype(F32)).astype(BF16)
            vt_ref[c, dv:, :] = jnp.ones((vt_ref.shape[1] - dv, t), BF16)
        skew = pltpu.roll(jnp.broadcast_to(tab_ref[...], (t, 2 * t)), 0, 1, stride=1, stride_axis=0)
        key = lax.broadcasted_iota(jnp.int32, (t, t), 0)
        qry = lax.broadcasted_iota(jnp.int32, (t, t), 1)
        bdiag_ref[...] = jnp.where(qry >= key, skew[:, :t], MASKED_LOGIT)
        bnear_ref[...] = skew[:, t:]

    q_t = jnp.transpose(q_ref[...].astype(F32)) * (dk ** -0.5)
    feat = lax.broadcasted_iota(jnp.int32, q_t.shape, 0)
    qs_t = jnp.concatenate([jnp.where(feat < dk, q_t, 0.0), jnp.where(feat >= dk, q_t, 0.0)],
                           axis=1).astype(BF16)

    m_ref[...] = jnp.full_like(m_ref, -jnp.inf)
    acc_ref[...] = jnp.zeros_like(acc_ref)

    def score(kj, buf, bias_ref, shift):
        s_ref, mx_ref, sh_ref = bufs[buf]
        off = pl.multiple_of(kj * t, t)
        s = jnp.dot(k_ref[pl.ds(off, t), :], qs_t, preferred_element_type=F32)
        if bias_ref is not None:
            b = bias_ref[...]
            s = s + jnp.concatenate([b, b], axis=1)
        s_ref[...] = s
        mx_ref[...] = jnp.max(s, 0, keepdims=True) + shift
        sh_ref[...] = jnp.zeros_like(sh_ref) + shift

    def absorb(kj, buf):
        s_ref, mx_ref, sh_ref = bufs[buf]
        m_old = m_ref[...]
        m_new = jnp.maximum(m_old, mx_ref[...])
        p = jnp.exp(s_ref[...] - (m_new - sh_ref[...])).astype(BF16)
        acc_ref[...] = (jnp.exp(m_old - m_new) * acc_ref[...]
                        + jnp.dot(vt_ref[kj], p, preferred_element_type=F32))
        m_ref[...] = m_new

    c_far = tab_ref[:, 2 * t - 1:2 * t]
    for src_ref, dst_ref in zip(cast_in, cast_out):
        dst_ref[...] = src_ref[...].astype(BF16)
    score(qi, 0, bdiag_ref, 0.0)

    @pl.when(qi >= 1)
    def _():
        score(qi - 1, 1, bnear_ref, 0.0)
        absorb(qi, 0)

    def far_pair(p, carry):
        kj = qi - 2 - 2 * p
        score(kj, 0, None, c_far)
        absorb(kj + 1, 1)
        score(kj - 1, 1, None, c_far)
        absorb(kj, 0)
        return carry

    lax.fori_loop(0, jnp.maximum(qi - 1, 0) // 2, far_pair, 0)

    @pl.when((qi >= 2) & (qi % 2 == 0))
    def _():
        score(0, 0, None, c_far)
        absorb(1, 1)

    @pl.when(qi % 2 == 0)
    def _():
        absorb(0, 0)

    @pl.when(qi % 2 == 1)
    def _():
        absorb(0, 1)

    lv = lam_ref[...]
    lam = (jnp.exp(jnp.sum(lv[0:1] * lv[1:2], -1, keepdims=True))
           - jnp.exp(jnp.sum(lv[2:3] * lv[3:4], -1, keepdims=True)) + lam_init)
    acc = acc_ref[...]
    o_t = acc[0:dv] * (1.0 / acc[dv:dv + 1])
    out = jnp.transpose(o_t[:, :t] - lam * o_t[:, t:])
    hn = out * lax.rsqrt(jnp.mean(out * out, -1, keepdims=True) + NORM_EPS)
    o_ref[...] = (hn * g_ref[...] * (1.0 - lam_init)).astype(o_ref.dtype)


def _t5_bias_by_distance(rel_bias, n):
    r = jnp.arange(n, dtype=jnp.int32)
    max_exact = N_BUCKETS // 2
    rf = jnp.maximum(r, 1).astype(F32)
    large = max_exact + (jnp.log(rf / max_exact) / math.log(MAX_DISTANCE / max_exact)
                         * (N_BUCKETS - max_exact)).astype(jnp.int32)
    large = jnp.minimum(large, N_BUCKETS - 1)
    bucket = jnp.where(r < max_exact, r, large)
    return jnp.transpose(rel_bias[bucket]).astype(F32)


def _diff_attention(y3, rel_bias, lam_vecs, diff_g, jobs, *, lam_init, t):
    bsz, s, _ = y3.shape
    hd = 2 * A_QK_DIM
    dv = A_V_DIM
    assert hd == dv == LANES and s % t == 0 and t >= MAX_DISTANCE
    nq = s // t
    table = _t5_bias_by_distance(rel_bias, 2 * t).reshape(A_HEADS, 1, 2 * t)
    assert all(blocks <= bsz * A_HEADS * nq for _, _, _, blocks in jobs)

    def job_block(blocks):
        return lambda b, h, q: jnp.minimum((b * A_HEADS + h) * nq + q, blocks - 1)

    cast_in = [pl.BlockSpec((rows, w.shape[1]), lambda b, h, q, f=first, blk=job_block(blocks): (f + blk(b, h, q), 0))
               for w, rows, first, blocks in jobs]
    cast_out = [pl.BlockSpec((rows, w.shape[1]), lambda b, h, q, blk=job_block(blocks): (blk(b, h, q), 0))
                for w, rows, first, blocks in jobs]
    cast_shapes = [jax.ShapeDtypeStruct((rows * blocks, w.shape[1]), BF16) for w, rows, first, blocks in jobs]
    return pl.pallas_call(
        functools.partial(_diff_attn_kernel, t=t, lam_init=lam_init, n_jobs=len(jobs)),
        grid=(bsz, A_HEADS, nq),
        in_specs=[
            pl.BlockSpec((None, t, hd), lambda b, h, q: (b, q, h)),
            pl.BlockSpec((None, s, hd), lambda b, h, q: (b, 0, A_HEADS + h)),
            pl.BlockSpec((None, s, dv), lambda b, h, q: (b, 0, 2 * A_HEADS + h)),
            pl.BlockSpec((None, 1, 2 * t), lambda b, h, q: (h, 0, 0)),
            pl.BlockSpec((4, A_QK_DIM), lambda b, h, q: (0, 0)),
            pl.BlockSpec((1, dv), lambda b, h, q: (0, h)),
        ] + cast_in,
        out_specs=[pl.BlockSpec((None, t, dv), lambda b, h, q: (b, q, h))] + cast_out,
        out_shape=[jax.ShapeDtypeStruct((bsz, s, A_HEADS * dv), BF16)] + cast_shapes,
        scratch_shapes=[pltpu.VMEM((nq, dv + ATTN_ONES_ROWS, t), BF16),
                        pltpu.VMEM((t, t), F32),
                        pltpu.VMEM((t, t), F32),
                        pltpu.VMEM((1, 2 * t), F32),
                        pltpu.VMEM((dv + ATTN_ONES_ROWS, 2 * t), F32)]
                       + 2 * [pltpu.VMEM((t, 2 * t), F32), pltpu.VMEM((1, 2 * t), F32),
                              pltpu.VMEM((1, 2 * t), F32)],
        compiler_params=_params("arbitrary", "arbitrary", "arbitrary"),
        name="diff_attn",
    )(y3, y3, y3, table, lam_vecs, diff_g, *[w for w, _, _, _ in jobs])


def _mlstm_kernel(qk_ref, v_ref, ob_ref, gc_ref, gr_ref, cw_ref, cb_ref, bc_ref, br_ref, g_ref, y_ref,
                  xp_ref, *state_refs, chunk):
    heads, dk, dv = B_HEADS, B_QK_DIM, B_V_DIM
    c_refs, n_refs, m_refs = state_refs[0::3], state_refs[1::3], state_refs[2::3]
    pad = xp_ref.shape[0] - chunk
    c = pl.program_id(1)

    @pl.when(c == 0)
    def _():
        xp_ref[0:pad, :] = jnp.zeros((pad, xp_ref.shape[1]), xp_ref.dtype)
        for ref in state_refs:
            ref[...] = jnp.zeros_like(ref)

    x = qk_ref[...]
    xp_ref[pad:pad + chunk, :] = x
    xp = xp_ref[...]
    cw = cw_ref[...]
    dst = lax.broadcasted_iota(jnp.int32, (chunk, pad + chunk), 0)
    src = lax.broadcasted_iota(jnp.int32, (chunk, pad + chunk), 1)
    conv = cb_ref[...] + cw[CONV_WIDTH - 1:CONV_WIDTH, :] * x.astype(F32)
    for j in range(CONV_WIDTH - 1):
        shift = (src == dst + (pad - (CONV_WIDTH - 1) + j)).astype(BF16)
        conv = conv + cw[j:j + 1, :] * jnp.dot(shift, xp, preferred_element_type=F32)
    xp_ref[0:pad, :] = x[chunk - pad:chunk, :]
    qk = conv * _sigmoid(conv)

    gc = gc_ref[...] + bc_ref[...]
    gr = gr_ref[...] + br_ref[...]
    row = lax.broadcasted_iota(jnp.int32, (chunk, chunk), 0)
    col = lax.broadcasted_iota(jnp.int32, (chunk, chunk), 1)
    causal = row >= col
    tril = causal.astype(F32)
    triu = (row <= col).astype(F32)
    b_c = jnp.dot(tril, _log_sigmoid(gc), precision=HIGHEST, preferred_element_type=F32)
    b_r = jnp.dot(_log_sigmoid(gr), triu, precision=HIGHEST, preferred_element_type=F32)

    ob = ob_ref[...].astype(F32)
    g = g_ref[...]
    hs = range(heads)
    q = [qk[:, h * dk:(h + 1) * dk] * (dk ** -0.5) for h in hs]
    k = [qk[:, (heads + h) * dk:(heads + h + 1) * dk] for h in hs]
    v = [v_ref[:, h * dv:(h + 1) * dv] for h in hs]
    b_col = [b_c[:, heads + h:heads + h + 1] for h in hs]
    i_col = [gc[:, h:h + 1] for h in hs]
    b_row = [b_r[heads + h:heads + h + 1, :] for h in hs]
    i_row = [gr[h:h + 1, :] for h in hs]
    m_prev = [m_refs[h][0:1, 0:1] for h in hs]
    c_mat = [c_refs[h][...] for h in hs]
    n_vec = [n_refs[h][...] for h in hs]

    b_last = [b_row[h][:, chunk - 1:chunk] for h in hs]
    g_col = [b_last[h] - b_col[h] + i_col[h] for h in hs]
    m_new = [jnp.maximum(b_last[h] + m_prev[h], jnp.max(g_col[h], 0, keepdims=True)) for h in hs]
    decay = [jnp.exp(b_last[h] + m_prev[h] - m_new[h]) for h in hs]
    kw = [k[h] * jnp.exp(g_col[h] - m_new[h]) for h in hs]
    qb = [q[h].astype(BF16) for h in hs]
    qk_t = [lax.dot_general(qb[h], k[h].astype(BF16), NT_DIMS, preferred_element_type=F32) for h in hs]
    q_c = [jnp.dot(qb[h], c_mat[h].astype(BF16), preferred_element_type=F32) for h in hs]
    kw_v = [lax.dot_general(kw[h].astype(BF16), v[h], TN_DIMS, preferred_element_type=F32) for h in hs]

    d = [jnp.where(causal, b_col[h] - b_row[h] + i_row[h], -jnp.inf) for h in hs]
    a_col = [b_col[h] + m_prev[h] for h in hs]
    m_t = [jnp.maximum(a_col[h], jnp.max(d[h], -1, keepdims=True)) for h in hs]
    w_inter = [jnp.exp(a_col[h] - m_t[h]) for h in hs]
    sw = [qk_t[h] * jnp.exp(d[h] - m_t[h]) for h in hs]
    sw_v = [jnp.dot(sw[h].astype(BF16), v[h], preferred_element_type=F32) for h in hs]

    for h in hs:
        c_refs[h][...] = decay[h] * c_mat[h] + kw_v[h]
        n_refs[h][...] = decay[h] * n_vec[h] + jnp.sum(kw[h], 0, keepdims=True)
        m_refs[h][...] = jnp.broadcast_to(m_new[h], m_refs[h].shape)

    for h in hs:
        num = w_inter[h] * q_c[h] + sw_v[h]
        den = (w_inter[h] * jnp.sum(q[h] * n_vec[h], -1, keepdims=True)
               + jnp.sum(sw[h], -1, keepdims=True))
        hh = num / jnp.maximum(jnp.abs(den), jnp.exp(-m_t[h]))
        hn = hh * lax.rsqrt(jnp.mean(hh * hh, -1, keepdims=True) + NORM_EPS)
        sl = slice(h * dv, (h + 1) * dv)
        y_ref[:, sl] = (hn * g[:, sl] * _sigmoid(ob[:, sl])).astype(y_ref.dtype)


def _mlstm(y3, gcol3, grow3, conv_w, conv_b, gate_bias_cols, gate_bias_rows, mlstm_g, *, chunk):
    bsz, s, _ = y3.shape
    heads, dk, dv = B_HEADS, B_QK_DIM, B_V_DIM
    w = heads * dv
    assert 2 * heads * dk == w and s % chunk == 0 and chunk % LANES == 0
    nc = s // chunk
    return pl.pallas_call(
        functools.partial(_mlstm_kernel, chunk=chunk),
        grid=(bsz, nc),
        in_specs=[
            pl.BlockSpec((None, chunk, w), lambda b, c: (b, c, 3)),
            pl.BlockSpec((None, chunk, w), lambda b, c: (b, c, 4)),
            pl.BlockSpec((None, chunk, w), lambda b, c: (b, c, 5)),
            pl.BlockSpec((None, chunk, LANES), lambda b, c: (b, c, 0)),
            pl.BlockSpec((SUBLANES, chunk), lambda b, c: (0, b * nc + c)),
            pl.BlockSpec((CONV_WIDTH, w), lambda b, c: (0, 0)),
            pl.BlockSpec((1, w), lambda b, c: (0, 0)),
            pl.BlockSpec((1, LANES), lambda b, c: (0, 0)),
            pl.BlockSpec((SUBLANES, 1), lambda b, c: (0, 0)),
            pl.BlockSpec((1, w), lambda b, c: (0, 0)),
        ],
        out_specs=pl.BlockSpec((None, chunk, w), lambda b, c: (b, c, 0)),
        out_shape=jax.ShapeDtypeStruct((bsz, s, w), BF16),
        scratch_shapes=[
            pltpu.VMEM((BF16_SUBLANES + chunk, w), BF16),
        ] + heads * [pltpu.VMEM((dk, dv), F32), pltpu.VMEM((1, dk), F32), pltpu.VMEM((SUBLANES, LANES), F32)],
        compiler_params=_params("parallel", "arbitrary"),
        name="mlstm",
    )(y3, y3, y3, gcol3, grow3, conv_w, conv_b, gate_bias_cols, gate_bias_rows, mlstm_g)


def _gla_kernel(q_ref, k_ref, v_ref, r_ref, a1_ref, a1n_ref, wah_ref, wal_ref, ba_ref, g_ref, y_ref,
                b_ref, *st_refs, chunk):
    heads, dk, dv = C_HEADS, C_QK_DIM, C_V_DIM
    sub = GLA_SUB
    assert chunk == 2 * sub
    c = pl.program_id(1)
    row = lax.broadcasted_iota(jnp.int32, (chunk, chunk), 0)
    col = lax.broadcasted_iota(jnp.int32, (chunk, chunk), 1)
    causal = (row >= col) & ((row < sub) == (col < sub))
    first = lax.broadcasted_iota(jnp.int32, (chunk, 1), 0) < sub

    def cum_log_decay(a1):
        a_hi = a1.astype(BF16)
        a_lo = (a1 - a_hi.astype(F32)).astype(BF16)
        w_hi = wah_ref[...]
        z = (jnp.dot(a_hi, w_hi, preferred_element_type=F32)
             + (jnp.dot(a_lo, w_hi, preferred_element_type=F32)
                + jnp.dot(a_hi, wal_ref[...], preferred_element_type=F32))) + ba_ref[...]
        log_a = _log_sigmoid(z) * (1.0 / C_GATE_TEMP)
        return jnp.dot(causal.astype(F32), log_a, precision=HIGHEST, preferred_element_type=F32)

    @pl.when(c == 0)
    def _():
        for ref in st_refs:
            ref[...] = jnp.zeros_like(ref)
        b_ref[...] = cum_log_decay(a1_ref[...])

    b_all = b_ref[...]
    b_next = cum_log_decay(a1n_ref[...])

    r = r_ref[...].astype(F32)
    g = g_ref[...]
    for h in range(heads):
        ks = slice(h * dk, (h + 1) * dk)
        vs = slice(h * dv, (h + 1) * dv)
        b = b_all[:, ks]
        q = q_ref[:, ks].astype(F32) * (dk ** -0.5)
        k = k_ref[:, ks].astype(F32)
        v = v_ref[:, vs]
        tot0 = b[sub - 1:sub, :]
        tot1 = b[chunk - 1:chunk, :]
        q_dec = q * jnp.exp(b)
        k_dec = (k * jnp.exp(-b)).astype(BF16)
        k_end = k * jnp.exp(jnp.where(first, tot0, tot1) - b)
        k_new = (k_end * jnp.where(first, jnp.exp(tot1), 1.0)).astype(BF16)
        q_int = (q_dec * jnp.where(first, 1.0, jnp.exp(tot0))).astype(BF16)
        q_dec = q_dec.astype(BF16)
        st = st_refs[h][...]
        qk_t = lax.dot_general(q_dec, k_dec, NT_DIMS, preferred_element_type=F32)
        inter = lax.dot_general(q_int, st.astype(BF16), NT_DIMS, preferred_element_type=F32)
        st_refs[h][...] = (jnp.exp(tot0 + tot1) * st
                           + lax.dot_general(v, k_new, TN_DIMS, preferred_element_type=F32))
        cross = lax.dot_general(q_dec[sub:], k_end[:sub].astype(BF16), NT_DIMS, preferred_element_type=F32)
        att = jnp.where(causal, qk_t, 0.0)
        o = jnp.dot(att.astype(BF16), v, preferred_element_type=F32) + inter
        o_cross = jnp.dot(cross.astype(BF16), v[:sub], preferred_element_type=F32)
        o = jnp.concatenate([o[:sub], o[sub:] + o_cross], axis=0)

        on = o * lax.rsqrt(jnp.mean(o * o, -1, keepdims=True) + NORM_EPS)
        rr = r[:, vs]
        y_ref[:, vs] = (on * g[:, vs] * (rr * _sigmoid(rr))).astype(y_ref.dtype)

    b_ref[...] = b_next


def _gla(y3, a13, w_a2_hi, w_a2_lo, b_a, gla_g, *, chunk):
    bsz, s, _ = y3.shape
    heads, dk, dv = C_HEADS, C_QK_DIM, C_V_DIM
    wk, wv = heads * dk, heads * dv
    assert wv == 2 * wk and s % chunk == 0
    nc = s // chunk
    return pl.pallas_call(
        functools.partial(_gla_kernel, chunk=chunk),
        grid=(bsz, nc),
        in_specs=[
            pl.BlockSpec((None, chunk, wk), lambda b, c: (b, c, 0)),
            pl.BlockSpec((None, chunk, wk), lambda b, c: (b, c, 1)),
            pl.BlockSpec((None, chunk, wv), lambda b, c: (b, c, 1)),
            pl.BlockSpec((None, chunk, wv), lambda b, c: (b, c, 2)),
            pl.BlockSpec((None, chunk, LANES), lambda b, c: (b, c, 0)),
            pl.BlockSpec((None, chunk, LANES), lambda b, c: (b, jnp.minimum(c + 1, nc - 1), 0)),
            pl.BlockSpec((LANES, wk), lambda b, c: (0, 0)),
            pl.BlockSpec((LANES, wk), lambda b, c: (0, 0)),
            pl.BlockSpec((1, wk), lambda b, c: (0, 0)),
            pl.BlockSpec((1, wv), lambda b, c: (0, 0)),
        ],
        out_specs=pl.BlockSpec((None, chunk, wv), lambda b, c: (b, c, 0)),
        out_shape=jax.ShapeDtypeStruct((bsz, s, wv), BF16),
        scratch_shapes=[pltpu.VMEM((chunk, wk), F32)] + heads * [pltpu.VMEM((dv, dk), F32)],
        compiler_params=_params("parallel", "arbitrary"),
        name="gla",
    )(y3, y3, y3, y3, a13, a13, w_a2_hi, w_a2_lo, b_a, gla_g)


def _out_ln_kernel(*refs, n_in, alpha):
    y_refs, w_refs = refs[:n_in], refs[n_in:2 * n_in]
    x_ref, g_ref, b_ref, o_ref = refs[2 * n_in:]
    rb = o_ref.shape[0] // ROW_SPLIT
    for r in range(ROW_SPLIT):
        rows = slice(r * rb, (r + 1) * rb)
        acc = jnp.dot(y_refs[0][rows, :], w_refs[0][...], preferred_element_type=F32)
        for y_ref, w_ref in zip(y_refs[1:], w_refs[1:]):
            acc = acc + jnp.dot(y_ref[rows, :], w_ref[...], preferred_element_type=F32)
        o_ref[rows, :] = _layer_norm(alpha * x_ref[rows, :] + acc, g_ref[...], b_ref[...])


def _out_ln(ys, w, x, ln_g, ln_b, layer, w_layer, ln_slot, *, alpha, tm):
    t, d = x.shape
    n_in = len(ys)
    wd = ys[0].shape[1]
    assert all(y.shape[1] == wd for y in ys) and w.shape[1:] == (n_in * wd, d)
    ln_spec = pl.BlockSpec((None, None, 1, d), lambda i: (layer, ln_slot, 0, 0))
    in_specs = ([pl.BlockSpec((tm, wd), lambda i: (i, 0)) for _ in ys]
                + [pl.BlockSpec((None, wd, d), lambda i, k=k: (w_layer, k, 0)) for k in range(n_in)]
                + [pl.BlockSpec((tm, d), lambda i: (i, 0)), ln_spec, ln_spec])
    return pl.pallas_call(
        functools.partial(_out_ln_kernel, n_in=n_in, alpha=alpha),
        grid=(t // tm,),
        in_specs=in_specs,
        out_specs=pl.BlockSpec((tm, d), lambda i: (i, 0)),
        out_shape=jax.ShapeDtypeStruct((t, d), F32),
        compiler_params=_params("parallel"),
        name="out_ln",
    )(*ys, *([w] * n_in), x, ln_g, ln_b)


def _ple_kernel(x_ref, p_ref, wp_ref, wg_ref, o_ref):
    rb = o_ref.shape[0] // ROW_SPLIT
    for r in range(ROW_SPLIT):
        rows = slice(r * rb, (r + 1) * rb)
        x = x_ref[rows, :]
        e = jnp.dot(p_ref[rows, :].astype(BF16), wp_ref[...], preferred_element_type=F32)
        gate = jnp.dot(x.astype(BF16), wg_ref[...], preferred_element_type=F32)
        o_ref[rows, :] = x + e * _sigmoid(gate)


def _ple(x, p, w_proj, w_gate, layer, *, tm):
    t, d = x.shape
    s, pd = p.shape[2:]
    assert s % tm == 0
    per_seq = s // tm
    return pl.pallas_call(
        _ple_kernel,
        grid=(t // tm,),
        in_specs=[
            pl.BlockSpec((tm, d), lambda i: (i, 0)),
            pl.BlockSpec((None, None, tm, pd), lambda i: (layer, i // per_seq, i % per_seq, 0)),
            pl.BlockSpec((None, pd, d), lambda i: (layer, 0, 0)),
            pl.BlockSpec((None, d, d), lambda i: (layer, 0, 0)),
        ],
        out_specs=pl.BlockSpec((tm, d), lambda i: (i, 0)),
        out_shape=jax.ShapeDtypeStruct((t, d), F32),
        compiler_params=_params("parallel"),
        name="ple",
    )(x, p, w_proj, w_gate)


def _tiles(t, s):
    return dict(
        ffn_tm=min(1024, t), ffn_tf=256,
        proj_tm=min(1024, t), proj_tn=1024,
        row_tm=min(512, t),
        attn_t=min(512, s),
        mlstm_chunk=min(256, s),
        gla_chunk=min(2 * GLA_SUB, s),
    )


def _cast_job(w2d, unit, skip, count, steps):
    rows = next(r for r in range(BF16_SUBLANES, unit + 1, BF16_SUBLANES)
                if unit % r == 0 and count * (unit // r) <= steps)
    return w2d, rows, skip * (unit // rows), count * (unit // rows)


def _pad_cols(w, n):
    return jnp.pad(w, ((0, 0), (0, n - w.shape[1])))


def kernel(x, p, ln_g, ln_b, w_ffn_in, w_ffn_out, w_in_ab, w_out_ab, rel_bias, lambda_q1, lambda_k1,
           lambda_q2, lambda_k2, diff_norm, conv_w, conv_b, b_igate, b_fgate, mlstm_norm, w_in_c,
           w_alpha2, b_alpha, gla_norm, w_out_c, w_ple_proj, w_ple_gate):
    bsz, s, d = x.shape
    depth = p.shape[0]
    t = bsz * s
    tl = _tiles(t, s)
    alpha = (2 * depth) ** 0.25
    a_w = A_HEADS * A_V_DIM
    ab_main = 3 * a_w + 3 * B_HEADS * B_V_DIM
    c_main = 2 * C_HEADS * C_QK_DIM + 2 * C_HEADS * C_V_DIM
    row = lambda a: a.reshape(1, -1)

    n_ffn, f_ff = 2 * depth, w_ffn_out.shape[2]
    attn_steps = bsz * A_HEADS * (s // tl["attn_t"])
    wfi_first, wfo_first = w_ffn_in[0, :1].astype(BF16), w_ffn_out[0, :1].astype(BF16)
    ffn_w = {0: (wfi_first, wfo_first, 0)}
    cast_jobs = [
        _cast_job(w_ffn_in.reshape(n_ffn * d, -1), d, 1, n_ffn - 1, attn_steps),
        _cast_job(w_ffn_out.reshape(n_ffn * f_ff, d), f_ff, 1, n_ffn - 1, attn_steps),
        _cast_job(jnp.swapaxes(w_in_c, 1, 2).reshape(-1, d), w_in_c.shape[2], 0, w_in_c.shape[0], attn_steps),
    ]
    wo_ab, wo_c = w_out_ab.astype(BF16), w_out_c.astype(BF16)
    wpp, wpg = w_ple_proj.astype(BF16), w_ple_gate.astype(BF16)
    g4, b4 = ln_g.reshape(depth, 3, 1, d), ln_b.reshape(depth, 3, 1, d)

    xf = x.reshape(t, d)
    for i in range(depth):
        xf = _ffn_ln(xf, *ffn_w[2 * i], g4, b4, i, 0, alpha=alpha, tm=tl["ffn_tm"], tf=tl["ffn_tf"])
        if i % 2 == 0:
            e = i // 2
            lam_init = 0.8 - 0.6 * math.exp(-0.3 * i)
            y, gcol, grow = _in_proj(xf, w_in_ab, e, ab_main, tm=tl["proj_tm"], tn=tl["proj_tn"])
            y3 = y.reshape(bsz, s, ab_main)
            lam_vecs = jnp.stack([lambda_q1[e], lambda_k1[e], lambda_q2[e], lambda_k2[e]])
            ya, *cast = _diff_attention(y3, rel_bias, lam_vecs, row(diff_norm[e]), cast_jobs,
                                        lam_init=lam_init, t=tl["attn_t"])
            if i == 0:
                wfi_rest, wfo_rest = cast[0].reshape(n_ffn - 1, d, -1), cast[1].reshape(n_ffn - 1, f_ff, d)
                ffn_w.update({n: (wfi_rest, wfo_rest, n - 1) for n in range(1, n_ffn)})
                w_c_t = cast[2].reshape(w_in_c.shape[0], w_in_c.shape[2], d)
            gate_bias = jnp.concatenate([b_igate[e], b_fgate[e]])
            yb = _mlstm(y3, gcol.reshape(bsz, s, LANES), grow,
                        conv_w[e], row(conv_b[e]), _pad_cols(row(gate_bias), LANES), gate_bias.reshape(-1, 1),
                        row(mlstm_norm[e]), chunk=tl["mlstm_chunk"])
            ys, w_out, w_layer = [ya.reshape(t, a_w), yb.reshape(t, -1)], wo_ab, e
        else:
            o = i // 2
            y, a1, _ = _in_proj(xf, w_in_c, o, c_main, tm=tl["proj_tm"], tn=tl["proj_tn"], w_t=w_c_t)
            w_a2 = jnp.pad(w_alpha2[o], ((0, LANES - w_alpha2.shape[1]), (0, 0)))
            w_a2_hi = lax.reduce_precision(w_a2, exponent_bits=8, mantissa_bits=7)
            w_a2_lo = (w_a2 - w_a2_hi).astype(BF16)
            yc = _gla(y.reshape(bsz, s, c_main), a1.reshape(bsz, s, LANES), w_a2_hi.astype(BF16), w_a2_lo,
                      row(b_alpha[o]),
                      row(gla_norm[o]), chunk=tl["gla_chunk"])
            ys, w_out, w_layer = [yc.reshape(t, -1)], wo_c, o
        xf = _out_ln(ys, w_out, xf, g4, b4, i, w_layer, 1, alpha=alpha, tm=tl["row_tm"])
        xf = _ffn_ln(xf, *ffn_w[2 * i + 1], g4, b4, i, 2, alpha=alpha, tm=tl["ffn_tm"], tf=tl["ffn_tf"])
        xf = _ple(xf, p, wpp, wpg, i, tm=tl["row_tm"])
    return xf.reshape(bsz, s, d)
```

```python
import functools
import math

import jax
import jax.numpy as jnp
from jax import lax
from jax.experimental import pallas as pl
from jax.experimental.pallas import tpu as pltpu

F32 = jnp.float32
BF16 = jnp.bfloat16
HIGHEST = lax.Precision.HIGHEST

A_HEADS = 8
A_QK_DIM = 64
A_V_DIM = 128
B_HEADS = 4
B_QK_DIM = 128
B_V_DIM = 256
CONV_WIDTH = 4
C_HEADS = 4
C_QK_DIM = 256
C_V_DIM = 512
C_GATE_TEMP = 16.0
GLA_SUB = 64
N_BUCKETS = 32
MAX_DISTANCE = 128
LN_EPS = 1e-5
NORM_EPS = 1e-6

V7X_VMEM_LIMIT_BYTES = 56 * 1024 * 1024
LANES = 128
SUBLANES = 8
BF16_SUBLANES = 16
ATTN_ONES_ROWS = BF16_SUBLANES
MASKED_LOGIT = -1e30
ROW_SPLIT = 4

NT_DIMS = (((1,), (1,)), ((), ()))
TN_DIMS = (((0,), (0,)), ((), ()))


def _params(*sem):
    return pltpu.CompilerParams(dimension_semantics=sem, vmem_limit_bytes=V7X_VMEM_LIMIT_BYTES)


def _layer_norm(y, g, b, eps=LN_EPS):
    mu = jnp.mean(y, -1, keepdims=True)
    yc = y - mu
    var = jnp.mean(yc * yc, -1, keepdims=True)
    return yc * lax.rsqrt(var + eps) * g + b


def _sigmoid(x):
    return 1.0 / (1.0 + jnp.exp(-x))


def _log_sigmoid(x):
    return jnp.minimum(x, 0.0) - jnp.log(1.0 + jnp.exp(-jnp.abs(x)))


def _ffn_ln_kernel(x_ref, wg0_ref, wu0_ref, wo0_ref, wg1_ref, wu1_ref, wo1_ref, g_ref, b_ref, o_ref, xb_ref,
                   *, alpha, n_chunks):
    j = pl.program_id(1)
    last = pl.num_programs(1) - 1

    @pl.when(j == 0)
    def _():
        x = x_ref[...]
        xb_ref[...] = x.astype(BF16)
        o_ref[...] = (2.0 * alpha) * x

    def chunk(wg_ref, wu_ref, wo_ref):
        xb = xb_ref[...]
        gate = jnp.dot(xb, wg_ref[...], preferred_element_type=F32)
        up = jnp.dot(xb, wu_ref[...], preferred_element_type=F32)
        h = (gate * _sigmoid(gate) * up).astype(BF16)
        o_ref[...] += jnp.dot(h, wo_ref[...], preferred_element_type=F32)

    if n_chunks % 2 == 0:
        chunk(wg0_ref, wu0_ref, wo0_ref)
        chunk(wg1_ref, wu1_ref, wo1_ref)
    else:
        @pl.when(j < last)
        def _():
            chunk(wg0_ref, wu0_ref, wo0_ref)
            chunk(wg1_ref, wu1_ref, wo1_ref)

        @pl.when(j == last)
        def _():
            chunk(wg0_ref, wu0_ref, wo0_ref)

    @pl.when(j == last)
    def _():
        o_ref[...] = _layer_norm(o_ref[...], g_ref[...], b_ref[...], eps=4.0 * LN_EPS)


def _ffn_ln(x, w_in, w_out, w_idx, ln_g, ln_b, layer, ln_slot, *, alpha, tm, tf):
    t, d = x.shape
    f = w_out.shape[1]
    nf = f // tf
    assert t % tm == 0 and f % tf == 0 and w_in.shape[1:] == (d, 2 * f) and nf >= 2
    steps = (nf + 1) // 2
    first = lambda j: 2 * j
    second = lambda j: jnp.where(2 * j + 1 < nf, 2 * j + 1, nf - 2)
    ln_spec = pl.BlockSpec((None, None, 1, d), lambda i, j: (layer, ln_slot, 0, 0))

    def weight_specs(chunk_of):
        return [pl.BlockSpec((None, d, tf), lambda i, j: (w_idx, 0, chunk_of(j))),
                pl.BlockSpec((None, d, tf), lambda i, j: (w_idx, 0, nf + chunk_of(j))),
                pl.BlockSpec((None, tf, d), lambda i, j: (w_idx, chunk_of(j), 0))]

    return pl.pallas_call(
        functools.partial(_ffn_ln_kernel, alpha=alpha, n_chunks=nf),
        grid=(t // tm, steps),
        in_specs=[pl.BlockSpec((tm, d), lambda i, j: (i, 0))] + weight_specs(first) + weight_specs(second)
                 + [ln_spec, ln_spec],
        out_specs=pl.BlockSpec((tm, d), lambda i, j: (i, 0)),
        out_shape=jax.ShapeDtypeStruct((t, d), F32),
        scratch_shapes=[pltpu.VMEM((tm, d), BF16)],
        compiler_params=_params("parallel", "arbitrary"),
        name="ffn_ln",
    )(x, w_in, w_in, w_out, w_in, w_in, w_out, ln_g, ln_b)


def _in_proj_kernel(x_ref, w_ref, wgc_ref, wgr_ref, y_ref, gc_ref, gr_ref, xb_ref):
    j = pl.program_id(1)

    @pl.when(j == 0)
    def _():
        xb = x_ref[...].astype(BF16)
        xb_ref[...] = xb
        gc_ref[...] = jnp.dot(xb, wgc_ref[...], preferred_element_type=F32)
        gr_ref[...] = lax.dot_general(wgr_ref[...], xb, NT_DIMS, preferred_element_type=F32)

    y_ref[...] = lax.dot_general(xb_ref[...], w_ref[...].astype(BF16), NT_DIMS,
                                 preferred_element_type=F32).astype(y_ref.dtype)


def _in_proj(x, w_stack, layer, n_main, *, tm, tn, w_t=None):
    t, d = x.shape
    assert t % tm == 0 and n_main % tn == 0
    if w_t is None:
        w_t = jnp.swapaxes(w_stack, 1, 2)
    w_gates = w_stack[layer, :, n_main:].astype(BF16)
    w_gate_cols = _pad_cols(w_gates, LANES)
    w_gate_rows = jnp.transpose(w_gates[:, :SUBLANES])
    return pl.pallas_call(
        _in_proj_kernel,
        grid=(t // tm, n_main // tn),
        in_specs=[
            pl.BlockSpec((tm, d), lambda i, j: (i, 0)),
            pl.BlockSpec((None, tn, d), lambda i, j: (layer, j, 0)),
            pl.BlockSpec((d, LANES), lambda i, j: (0, 0)),
            pl.BlockSpec((SUBLANES, d), lambda i, j: (0, 0)),
        ],
        out_specs=[
            pl.BlockSpec((tm, tn), lambda i, j: (i, j)),
            pl.BlockSpec((tm, LANES), lambda i, j: (i, 0)),
            pl.BlockSpec((SUBLANES, tm), lambda i, j: (0, i)),
        ],
        out_shape=[
            jax.ShapeDtypeStruct((t, n_main), BF16),
            jax.ShapeDtypeStruct((t, LANES), F32),
            jax.ShapeDtypeStruct((SUBLANES, t), F32),
        ],
        scratch_shapes=[pltpu.VMEM((tm, d), BF16)],
        compiler_params=_params("parallel", "arbitrary"),
        name="in_proj",
    )(x, w_t, w_gate_cols, w_gate_rows)


def _diff_attn_kernel(q_ref, k_ref, v_ref, tab_ref, lam_ref, g_ref, *refs, t, lam_init, n_jobs):
    cast_in, (o_ref, *cast_out) = refs[:n_jobs], refs[n_jobs:2 * n_jobs + 1]
    (vt_ref, bdiag_ref, bnear_ref, m_ref, acc_ref,
     s0_ref, mx0_ref, sh0_ref, s1_ref, mx1_ref, sh1_ref) = refs[2 * n_jobs + 1:]
    qi = pl.program_id(2)
    dk, dv = A_QK_DIM, A_V_DIM
    nk = k_ref.shape[0] // t
    bufs = ((s0_ref, mx0_ref, sh0_ref), (s1_ref, mx1_ref, sh1_ref))

    @pl.when(qi == 0)
    def _():
        for c in range(nk):
            vt_ref[c, 0:dv, :] = jnp.transpose(v_ref[c * t:(c + 1) * t, :].astype(F32)).astype(BF16)
            vt_ref[c, dv:, :] = jnp.ones((vt_ref.shape[1] - dv, t), BF16)
        skew = pltpu.roll(jnp.broadcast_to(tab_ref[...], (t, 2 * t)), 0, 1, stride=1, stride_axis=0)
        key = lax.broadcasted_iota(jnp.int32, (t, t), 0)
        qry = lax.broadcasted_iota(jnp.int32, (t, t), 1)
        bdiag_ref[...] = jnp.where(qry >= key, skew[:, :t], MASKED_LOGIT)
        bnear_ref[...] = skew[:, t:]

    q_t = jnp.transpose(q_ref[...].astype(F32)) * (dk ** -0.5)
    feat = lax.broadcasted_iota(jnp.int32, q_t.shape, 0)
    qs_t = jnp.concatenate([jnp.where(feat < dk, q_t, 0.0), jnp.where(feat >= dk, q_t, 0.0)],
                           axis=1).astype(BF16)

    m_ref[...] = jnp.full_like(m_ref, -jnp.inf)
    acc_ref[...] = jnp.zeros_like(acc_ref)

    def score(kj, buf, bias_ref, shift):
        s_ref, mx_ref, sh_ref = bufs[buf]
        off = pl.multiple_of(kj * t, t)
        s = jnp.dot(k_ref[pl.ds(off, t), :], qs_t, preferred_element_type=F32)
        if bias_ref is not None:
            b = bias_ref[...]
            s = s + jnp.concatenate([b, b], axis=1)
        s_ref[...] = s
        mx_ref[...] = jnp.max(s, 0, keepdims=True) + shift
        sh_ref[...] = jnp.zeros_like(sh_ref) + shift

    def absorb(kj, buf):
        s_ref, mx_ref, sh_ref = bufs[buf]
        m_old = m_ref[...]
        m_new = jnp.maximum(m_old, mx_ref[...])
        p = jnp.exp(s_ref[...] - (m_new - sh_ref[...])).astype(BF16)
        acc_ref[...] = (jnp.exp(m_old - m_new) * acc_ref[...]
                        + jnp.dot(vt_ref[kj], p, preferred_element_type=F32))
        m_ref[...] = m_new

    c_far = tab_ref[:, 2 * t - 1:2 * t]
    for src_ref, dst_ref in zip(cast_in, cast_out):
        dst_ref[...] = src_ref[...].astype(BF16)
    score(qi, 0, bdiag_ref, 0.0)

    @pl.when(qi >= 1)
    def _():
        score(qi - 1, 1, bnear_ref, 0.0)
        absorb(qi, 0)

    def far_pair(p, carry):
        kj = qi - 2 - 2 * p
        score(kj, 0, None, c_far)
        absorb(kj + 1, 1)
        score(kj - 1, 1, None, c_far)
        absorb(kj, 0)
        return carry

    lax.fori_loop(0, jnp.maximum(qi - 1, 0) // 2, far_pair, 0)

    @pl.when((qi >= 2) & (qi % 2 == 0))
    def _():
        score(0, 0, None, c_far)
        absorb(1, 1)

    @pl.when(qi % 2 == 0)
    def _():
        absorb(0, 0)

    @pl.when(qi % 2 == 1)
    def _():
        absorb(0, 1)

    lv = lam_ref[...]
    lam = (jnp.exp(jnp.sum(lv[0:1] * lv[1:2], -1, keepdims=True))
           - jnp.exp(jnp.sum(lv[2:3] * lv[3:4], -1, keepdims=True)) + lam_init)
    acc = acc_ref[...]
    o_t = acc[0:dv] * (1.0 / acc[dv:dv + 1])
    out = jnp.transpose(o_t[:, :t] - lam * o_t[:, t:])
    hn = out * lax.rsqrt(jnp.mean(out * out, -1, keepdims=True) + NORM_EPS)
    o_ref[...] = (hn * g_ref[...] * (1.0 - lam_init)).astype(o_ref.dtype)


def _t5_bias_by_distance(rel_bias, n):
    r = jnp.arange(n, dtype=jnp.int32)
    max_exact = N_BUCKETS // 2
    rf = jnp.maximum(r, 1).astype(F32)
    large = max_exact + (jnp.log(rf / max_exact) / math.log(MAX_DISTANCE / max_exact)
                         * (N_BUCKETS - max_exact)).astype(jnp.int32)
    large = jnp.minimum(large, N_BUCKETS - 1)
    bucket = jnp.where(r < max_exact, r, large)
    return jnp.transpose(rel_bias[bucket]).astype(F32)


def _diff_attention(y3, rel_bias, lam_vecs, diff_g, jobs, *, lam_init, t):
    bsz, s, _ = y3.shape
    hd = 2 * A_QK_DIM
    dv = A_V_DIM
    assert hd == dv == LANES and s % t == 0 and t >= MAX_DISTANCE
    nq = s // t
    table = _t5_bias_by_distance(rel_bias, 2 * t).reshape(A_HEADS, 1, 2 * t)
    assert all(blocks <= bsz * A_HEADS * nq for _, _, _, blocks in jobs)

    def job_block(blocks):
        return lambda b, h, q: jnp.minimum((b * A_HEADS + h) * nq + q, blocks - 1)

    cast_in = [pl.BlockSpec((rows, w.shape[1]), lambda b, h, q, f=first, blk=job_block(blocks): (f + blk(b, h, q), 0))
               for w, rows, first, blocks in jobs]
    cast_out = [pl.BlockSpec((rows, w.shape[1]), lambda b, h, q, blk=job_block(blocks): (blk(b, h, q), 0))
                for w, rows, first, blocks in jobs]
    cast_shapes = [jax.ShapeDtypeStruct((rows * blocks, w.shape[1]), BF16) for w, rows, first, blocks in jobs]
    return pl.pallas_call(
        functools.partial(_diff_attn_kernel, t=t, lam_init=lam_init, n_jobs=len(jobs)),
        grid=(bsz, A_HEADS, nq),
        in_specs=[
            pl.BlockSpec((None, t, hd), lambda b, h, q: (b, q, h)),
            pl.BlockSpec((None, s, hd), lambda b, h, q: (b, 0, A_HEADS + h)),
            pl.BlockSpec((None, s, dv), lambda b, h, q: (b, 0, 2 * A_HEADS + h)),
            pl.BlockSpec((None, 1, 2 * t), lambda b, h, q: (h, 0, 0)),
            pl.BlockSpec((4, A_QK_DIM), lambda b, h, q: (0, 0)),
            pl.BlockSpec((1, dv), lambda b, h, q: (0, h)),
        ] + cast_in,
        out_specs=[pl.BlockSpec((None, t, dv), lambda b, h, q: (b, q, h))] + cast_out,
        out_shape=[jax.ShapeDtypeStruct((bsz, s, A_HEADS * dv), BF16)] + cast_shapes,
        scratch_shapes=[pltpu.VMEM((nq, dv + ATTN_ONES_ROWS, t), BF16),
                        pltpu.VMEM((t, t), F32),
                        pltpu.VMEM((t, t), F32),
                        pltpu.VMEM((1, 2 * t), F32),
                        pltpu.VMEM((dv + ATTN_ONES_ROWS, 2 * t), F32)]
                       + 2 * [pltpu.VMEM((t, 2 * t), F32), pltpu.VMEM((1, 2 * t), F32),
                              pltpu.VMEM((1, 2 * t), F32)],
        compiler_params=_params("arbitrary", "arbitrary", "arbitrary"),
        name="diff_attn",
    )(y3, y3, y3, table, lam_vecs, diff_g, *[w for w, _, _, _ in jobs])


def _mlstm_kernel(qk_ref, v_ref, ob_ref, gc_ref, gr_ref, cw_ref, cb_ref, bc_ref, br_ref, g_ref, y_ref,
                  xp_ref, *state_refs, chunk):
    heads, dk, dv = B_HEADS, B_QK_DIM, B_V_DIM
    c_refs, n_refs, m_refs = state_refs[0::3], state_refs[1::3], state_refs[2::3]
    pad = xp_ref.shape[0] - chunk
    c = pl.program_id(1)

    @pl.when(c == 0)
    def _():
        xp_ref[0:pad, :] = jnp.zeros((pad, xp_ref.shape[1]), xp_ref.dtype)
        for ref in state_refs:
            ref[...] = jnp.zeros_like(ref)

    x = qk_ref[...]
    xp_ref[pad:pad + chunk, :] = x
    xp = xp_ref[...]
    cw = cw_ref[...]
    dst = lax.broadcasted_iota(jnp.int32, (chunk, pad + chunk), 0)
    src = lax.broadcasted_iota(jnp.int32, (chunk, pad + chunk), 1)
    conv = cb_ref[...] + cw[CONV_WIDTH - 1:CONV_WIDTH, :] * x.astype(F32)
    for j in range(CONV_WIDTH - 1):
        shift = (src == dst + (pad - (CONV_WIDTH - 1) + j)).astype(BF16)
        conv = conv + cw[j:j + 1, :] * jnp.dot(shift, xp, preferred_element_type=F32)
    xp_ref[0:pad, :] = x[chunk - pad:chunk, :]
    qk = conv * _sigmoid(conv)

    gc = gc_ref[...] + bc_ref[...]
    gr = gr_ref[...] + br_ref[...]
    row = lax.broadcasted_iota(jnp.int32, (chunk, chunk), 0)
    col = lax.broadcasted_iota(jnp.int32, (chunk, chunk), 1)
    causal = row >= col
    tril = causal.astype(F32)
    triu = (row <= col).astype(F32)
    b_c = jnp.dot(tril, _log_sigmoid(gc), precision=HIGHEST, preferred_element_type=F32)
    b_r = jnp.dot(_log_sigmoid(gr), triu, precision=HIGHEST, preferred_element_type=F32)

    ob = ob_ref[...].astype(F32)
    g = g_ref[...]
    hs = range(heads)
    q = [qk[:, h * dk:(h + 1) * dk] * (dk ** -0.5) for h in hs]
    k = [qk[:, (heads + h) * dk:(heads + h + 1) * dk] for h in hs]
    v = [v_ref[:, h * dv:(h + 1) * dv] for h in hs]
    b_col = [b_c[:, heads + h:heads + h + 1] for h in hs]
    i_col = [gc[:, h:h + 1] for h in hs]
    b_row = [b_r[heads + h:heads + h + 1, :] for h in hs]
    i_row = [gr[h:h + 1, :] for h in hs]
    m_prev = [m_refs[h][0:1, 0:1] for h in hs]
    c_mat = [c_refs[h][...] for h in hs]
    n_vec = [n_refs[h][...] for h in hs]

    b_last = [b_row[h][:, chunk - 1:chunk] for h in hs]
    g_col = [b_last[h] - b_col[h] + i_col[h] for h in hs]
    m_new = [jnp.maximum(b_last[h] + m_prev[h], jnp.max(g_col[h], 0, keepdims=True)) for h in hs]
    decay = [jnp.exp(b_last[h] + m_prev[h] - m_new[h]) for h in hs]
    kw = [k[h] * jnp.exp(g_col[h] - m_new[h]) for h in hs]
    qb = [q[h].astype(BF16) for h in hs]
    qk_t = [lax.dot_general(qb[h], k[h].astype(BF16), NT_DIMS, preferred_element_type=F32) for h in hs]
    q_c = [jnp.dot(qb[h], c_mat[h].astype(BF16), preferred_element_type=F32) for h in hs]
    kw_v = [lax.dot_general(kw[h].astype(BF16), v[h], TN_DIMS, preferred_element_type=F32) for h in hs]

    d = [jnp.where(causal, b_col[h] - b_row[h] + i_row[h], -jnp.inf) for h in hs]
    a_col = [b_col[h] + m_prev[h] for h in hs]
    m_t = [jnp.maximum(a_col[h], jnp.max(d[h], -1, keepdims=True)) for h in hs]
    w_inter = [jnp.exp(a_col[h] - m_t[h]) for h in hs]
    sw = [qk_t[h] * jnp.exp(d[h] - m_t[h]) for h in hs]
    sw_v = [jnp.dot(sw[h].astype(BF16), v[h], preferred_element_type=F32) for h in hs]

    for h in hs:
        c_refs[h][...] = decay[h] * c_mat[h] + kw_v[h]
        n_refs[h][...] = decay[h] * n_vec[h] + jnp.sum(kw[h], 0, keepdims=True)
        m_refs[h][...] = jnp.broadcast_to(m_new[h], m_refs[h].shape)

    for h in hs:
        num = w_inter[h] * q_c[h] + sw_v[h]
        den = (w_inter[h] * jnp.sum(q[h] * n_vec[h], -1, keepdims=True)
               + jnp.sum(sw[h], -1, keepdims=True))
        hh = num / jnp.maximum(jnp.abs(den), jnp.exp(-m_t[h]))
        hn = hh * lax.rsqrt(jnp.mean(hh * hh, -1, keepdims=True) + NORM_EPS)
        sl = slice(h * dv, (h + 1) * dv)
        y_ref[:, sl] = (hn * g[:, sl] * _sigmoid(ob[:, sl])).astype(y_ref.dtype)


def _mlstm(y3, gcol3, grow3, conv_w, conv_b, gate_bias_cols, gate_bias_rows, mlstm_g, *, chunk):
    bsz, s, _ = y3.shape
    heads, dk, dv = B_HEADS, B_QK_DIM, B_V_DIM
    w = heads * dv
    assert 2 * heads * dk == w and s % chunk == 0 and chunk % LANES == 0
    nc = s // chunk
    return pl.pallas_call(
        functools.partial(_mlstm_kernel, chunk=chunk),
        grid=(bsz, nc),
        in_specs=[
            pl.BlockSpec((None, chunk, w), lambda b, c: (b, c, 3)),
            pl.BlockSpec((None, chunk, w), lambda b, c: (b, c, 4)),
            pl.BlockSpec((None, chunk, w), lambda b, c: (b, c, 5)),
            pl.BlockSpec((None, chunk, LANES), lambda b, c: (b, c, 0)),
            pl.BlockSpec((SUBLANES, chunk), lambda b, c: (0, b * nc + c)),
            pl.BlockSpec((CONV_WIDTH, w), lambda b, c: (0, 0)),
            pl.BlockSpec((1, w), lambda b, c: (0, 0)),
            pl.BlockSpec((1, LANES), lambda b, c: (0, 0)),
            pl.BlockSpec((SUBLANES, 1), lambda b, c: (0, 0)),
            pl.BlockSpec((1, w), lambda b, c: (0, 0)),
        ],
        out_specs=pl.BlockSpec((None, chunk, w), lambda b, c: (b, c, 0)),
        out_shape=jax.ShapeDtypeStruct((bsz, s, w), BF16),
        scratch_shapes=[
            pltpu.VMEM((BF16_SUBLANES + chunk, w), BF16),
        ] + heads * [pltpu.VMEM((dk, dv), F32), pltpu.VMEM((1, dk), F32), pltpu.VMEM((SUBLANES, LANES), F32)],
        compiler_params=_params("parallel", "arbitrary"),
        name="mlstm",
    )(y3, y3, y3, gcol3, grow3, conv_w, conv_b, gate_bias_cols, gate_bias_rows, mlstm_g)


def _gla_kernel(q_ref, k_ref, v_ref, r_ref, a1_ref, a1n_ref, wah_ref, wal_ref, ba_ref, g_ref, y_ref,
                b_ref, *st_refs, chunk):
    heads, dk, dv = C_HEADS, C_QK_DIM, C_V_DIM
    sub = GLA_SUB
    assert chunk == 2 * sub
    c = pl.program_id(1)
    row = lax.broadcasted_iota(jnp.int32, (chunk, chunk), 0)
    col = lax.broadcasted_iota(jnp.int32, (chunk, chunk), 1)
    causal = (row >= col) & ((row < sub) == (col < sub))
    first = lax.broadcasted_iota(jnp.int32, (chunk, 1), 0) < sub

    def cum_log_decay(a1):
        a_hi = a1.astype(BF16)
        a_lo = (a1 - a_hi.astype(F32)).astype(BF16)
        w_hi = wah_ref[...]
        z = (jnp.dot(a_hi, w_hi, preferred_element_type=F32)
             + (jnp.dot(a_lo, w_hi, preferred_element_type=F32)
                + jnp.dot(a_hi, wal_ref[...], preferred_element_type=F32))) + ba_ref[...]
        log_a = _log_sigmoid(z) * (1.0 / C_GATE_TEMP)
        return jnp.dot(causal.astype(F32), log_a, precision=HIGHEST, preferred_element_type=F32)

    @pl.when(c == 0)
    def _():
        for ref in st_refs:
            ref[...] = jnp.zeros_like(ref)
        b_ref[...] = cum_log_decay(a1_ref[0:chunk, :])

    g = g_ref[...]
    n_inner = q_ref.shape[0] // chunk
    for i in range(n_inner):
        rows = slice(i * chunk, (i + 1) * chunk)
        b_all = b_ref[...]
        b_next = cum_log_decay(a1_ref[(i + 1) * chunk:(i + 2) * chunk, :] if i + 1 < n_inner else a1n_ref[...])

        r = r_ref[rows, :].astype(F32)
        for h in range(heads):
            ks = slice(h * dk, (h + 1) * dk)
            vs = slice(h * dv, (h + 1) * dv)
            b = b_all[:, ks]
            q = q_ref[rows, ks].astype(F32) * (dk ** -0.5)
            k = k_ref[rows, ks].astype(F32)
            v = v_ref[rows, vs]
            tot0 = b[sub - 1:sub, :]
            tot1 = b[chunk - 1:chunk, :]
            q_dec = q * jnp.exp(b)
            k_dec = (k * jnp.exp(-b)).astype(BF16)
            k_end = k * jnp.exp(jnp.where(first, tot0, tot1) - b)
            k_new = (k_end * jnp.where(first, jnp.exp(tot1), 1.0)).astype(BF16)
            q_int = (q_dec * jnp.where(first, 1.0, jnp.exp(tot0))).astype(BF16)
            q_dec = q_dec.astype(BF16)
            st = st_refs[h][...]
            qk_t = lax.dot_general(q_dec, k_dec, NT_DIMS, preferred_element_type=F32)
            inter = lax.dot_general(q_int, st.astype(BF16), NT_DIMS, preferred_element_type=F32)
            st_refs[h][...] = (jnp.exp(tot0 + tot1) * st
                               + lax.dot_general(v, k_new, TN_DIMS, preferred_element_type=F32))
            cross = lax.dot_general(q_dec[sub:], k_end[:sub].astype(BF16), NT_DIMS, preferred_element_type=F32)
            att = jnp.where(causal, qk_t, 0.0)
            o = jnp.dot(att.astype(BF16), v, preferred_element_type=F32) + inter
            o_cross = jnp.dot(cross.astype(BF16), v[:sub], preferred_element_type=F32)
            o = jnp.concatenate([o[:sub], o[sub:] + o_cross], axis=0)

            on = o * lax.rsqrt(jnp.mean(o * o, -1, keepdims=True) + NORM_EPS)
            rr = r[:, vs]
            y_ref[rows, vs] = (on * g[:, vs] * (rr * _sigmoid(rr))).astype(y_ref.dtype)

        b_ref[...] = b_next


def _gla(y3, a13, w_a2_hi, w_a2_lo, b_a, gla_g, *, chunk, block):
    bsz, s, _ = y3.shape
    heads, dk, dv = C_HEADS, C_QK_DIM, C_V_DIM
    wk, wv = heads * dk, heads * dv
    assert wv == 2 * wk and s % block == 0 and block % chunk == 0
    nb, per_block = s // block, block // chunk
    return pl.pallas_call(
        functools.partial(_gla_kernel, chunk=chunk),
        grid=(bsz, nb),
        in_specs=[
            pl.BlockSpec((None, block, wk), lambda b, c: (b, c, 0)),
            pl.BlockSpec((None, block, wk), lambda b, c: (b, c, 1)),
            pl.BlockSpec((None, block, wv), lambda b, c: (b, c, 1)),
            pl.BlockSpec((None, block, wv), lambda b, c: (b, c, 2)),
            pl.BlockSpec((None, block, LANES), lambda b, c: (b, c, 0)),
            pl.BlockSpec((None, chunk, LANES), lambda b, c: (b, jnp.minimum(c + 1, nb - 1) * per_block, 0)),
            pl.BlockSpec((LANES, wk), lambda b, c: (0, 0)),
            pl.BlockSpec((LANES, wk), lambda b, c: (0, 0)),
            pl.BlockSpec((1, wk), lambda b, c: (0, 0)),
            pl.BlockSpec((1, wv), lambda b, c: (0, 0)),
        ],
        out_specs=pl.BlockSpec((None, block, wv), lambda b, c: (b, c, 0)),
        out_shape=jax.ShapeDtypeStruct((bsz, s, wv), BF16),
        scratch_shapes=[pltpu.VMEM((chunk, wk), F32)] + heads * [pltpu.VMEM((dv, dk), F32)],
        compiler_params=_params("parallel", "arbitrary"),
        name="gla",
    )(y3, y3, y3, y3, a13, a13, w_a2_hi, w_a2_lo, b_a, gla_g)


def _out_ln_kernel(*refs, n_in, alpha):
    y_refs, w_refs = refs[:n_in], refs[n_in:2 * n_in]
    x_ref, g_ref, b_ref, o_ref = refs[2 * n_in:]
    rb = o_ref.shape[0] // ROW_SPLIT
    for r in range(ROW_SPLIT):
        rows = slice(r * rb, (r + 1) * rb)
        acc = jnp.dot(y_refs[0][rows, :], w_refs[0][...], preferred_element_type=F32)
        for y_ref, w_ref in zip(y_refs[1:], w_refs[1:]):
            acc = acc + jnp.dot(y_ref[rows, :], w_ref[...], preferred_element_type=F32)
        o_ref[rows, :] = _layer_norm(alpha * x_ref[rows, :] + acc, g_ref[...], b_ref[...])


def _out_ln(ys, w, x, ln_g, ln_b, layer, w_layer, ln_slot, *, alpha, tm):
    t, d = x.shape
    n_in = len(ys)
    wd = ys[0].shape[1]
    assert all(y.shape[1] == wd for y in ys) and w.shape[1:] == (n_in * wd, d)
    ln_spec = pl.BlockSpec((None, None, 1, d), lambda i: (layer, ln_slot, 0, 0))
    in_specs = ([pl.BlockSpec((tm, wd), lambda i: (i, 0)) for _ in ys]
                + [pl.BlockSpec((None, wd, d), lambda i, k=k: (w_layer, k, 0)) for k in range(n_in)]
                + [pl.BlockSpec((tm, d), lambda i: (i, 0)), ln_spec, ln_spec])
    return pl.pallas_call(
        functools.partial(_out_ln_kernel, n_in=n_in, alpha=alpha),
        grid=(t // tm,),
        in_specs=in_specs,
        out_specs=pl.BlockSpec((tm, d), lambda i: (i, 0)),
        out_shape=jax.ShapeDtypeStruct((t, d), F32),
        compiler_params=_params("parallel"),
        name="out_ln",
    )(*ys, *([w] * n_in), x, ln_g, ln_b)


def _ple_kernel(x_ref, p_ref, wp_ref, wg_ref, o_ref):
    rb = o_ref.shape[0] // ROW_SPLIT
    for r in range(ROW_SPLIT):
        rows = slice(r * rb, (r + 1) * rb)
        x = x_ref[rows, :]
        e = jnp.dot(p_ref[rows, :].astype(BF16), wp_ref[...], preferred_element_type=F32)
        gate = jnp.dot(x.astype(BF16), wg_ref[...], preferred_element_type=F32)
        o_ref[rows, :] = x + e * _sigmoid(gate)


def _ple(x, p, w_proj, w_gate, layer, *, tm):
    t, d = x.shape
    s, pd = p.shape[2:]
    assert s % tm == 0
    per_seq = s // tm
    return pl.pallas_call(
        _ple_kernel,
        grid=(t // tm,),
        in_specs=[
            pl.BlockSpec((tm, d), lambda i: (i, 0)),
            pl.BlockSpec((None, None, tm, pd), lambda i: (layer, i // per_seq, i % per_seq, 0)),
            pl.BlockSpec((None, pd, d), lambda i: (layer, 0, 0)),
            pl.BlockSpec((None, d, d), lambda i: (layer, 0, 0)),
        ],
        out_specs=pl.BlockSpec((tm, d), lambda i: (i, 0)),
        out_shape=jax.ShapeDtypeStruct((t, d), F32),
        compiler_params=_params("parallel"),
        name="ple",
    )(x, p, w_proj, w_gate)


def _tiles(t, s):
    return dict(
        ffn_tm=min(1024, t), ffn_tf=256,
        proj_tm=min(1024, t), proj_tn=1024,
        row_tm=min(512, t),
        attn_t=min(512, s),
        mlstm_chunk=min(256, s),
        gla_chunk=min(2 * GLA_SUB, s), gla_block=min(4 * GLA_SUB, s),
    )


def _cast_job(w2d, unit, skip, count, steps):
    rows = next(r for r in range(BF16_SUBLANES, unit + 1, BF16_SUBLANES)
                if unit % r == 0 and count * (unit // r) <= steps)
    return w2d, rows, skip * (unit // rows), count * (unit // rows)


def _pad_cols(w, n):
    return jnp.pad(w, ((0, 0), (0, n - w.shape[1])))


def kernel(x, p, ln_g, ln_b, w_ffn_in, w_ffn_out, w_in_ab, w_out_ab, rel_bias, lambda_q1, lambda_k1,
           lambda_q2, lambda_k2, diff_norm, conv_w, conv_b, b_igate, b_fgate, mlstm_norm, w_in_c,
           w_alpha2, b_alpha, gla_norm, w_out_c, w_ple_proj, w_ple_gate):
    bsz, s, d = x.shape
    depth = p.shape[0]
    t = bsz * s
    tl = _tiles(t, s)
    alpha = (2 * depth) ** 0.25
    a_w = A_HEADS * A_V_DIM
    ab_main = 3 * a_w + 3 * B_HEADS * B_V_DIM
    c_main = 2 * C_HEADS * C_QK_DIM + 2 * C_HEADS * C_V_DIM
    row = lambda a: a.reshape(1, -1)

    n_ffn, f_ff = 2 * depth, w_ffn_out.shape[2]
    attn_steps = bsz * A_HEADS * (s // tl["attn_t"])
    wfi_first, wfo_first = w_ffn_in[0, :1].astype(BF16), w_ffn_out[0, :1].astype(BF16)
    ffn_w = {0: (wfi_first, wfo_first, 0)}
    cast_jobs = [
        _cast_job(w_ffn_in.reshape(n_ffn * d, -1), d, 1, n_ffn - 1, attn_steps),
        _cast_job(w_ffn_out.reshape(n_ffn * f_ff, d), f_ff, 1, n_ffn - 1, attn_steps),
        _cast_job(jnp.swapaxes(w_in_c, 1, 2).reshape(-1, d), w_in_c.shape[2], 0, w_in_c.shape[0], attn_steps),
    ]
    wo_ab, wo_c = w_out_ab.astype(BF16), w_out_c.astype(BF16)
    wpp, wpg = w_ple_proj.astype(BF16), w_ple_gate.astype(BF16)
    g4, b4 = ln_g.reshape(depth, 3, 1, d), ln_b.reshape(depth, 3, 1, d)

    xf = x.reshape(t, d)
    for i in range(depth):
        xf = _ffn_ln(xf, *ffn_w[2 * i], g4, b4, i, 0, alpha=alpha, tm=tl["ffn_tm"], tf=tl["ffn_tf"])
        if i % 2 == 0:
            e = i // 2
            lam_init = 0.8 - 0.6 * math.exp(-0.3 * i)
            y, gcol, grow = _in_proj(xf, w_in_ab, e, ab_main, tm=tl["proj_tm"], tn=tl["proj_tn"])
            y3 = y.reshape(bsz, s, ab_main)
            lam_vecs = jnp.stack([lambda_q1[e], lambda_k1[e], lambda_q2[e], lambda_k2[e]])
            ya, *cast = _diff_attention(y3, rel_bias, lam_vecs, row(diff_norm[e]), cast_jobs,
                                        lam_init=lam_init, t=tl["attn_t"])
            if i == 0:
                wfi_rest, wfo_rest = cast[0].reshape(n_ffn - 1, d, -1), cast[1].reshape(n_ffn - 1, f_ff, d)
                ffn_w.update({n: (wfi_rest, wfo_rest, n - 1) for n in range(1, n_ffn)})
                w_c_t = cast[2].reshape(w_in_c.shape[0], w_in_c.shape[2], d)
            gate_bias = jnp.concatenate([b_igate[e], b_fgate[e]])
            yb = _mlstm(y3, gcol.reshape(bsz, s, LANES), grow,
                        conv_w[e], row(conv_b[e]), _pad_cols(row(gate_bias), LANES), gate_bias.reshape(-1, 1),
                        row(mlstm_norm[e]), chunk=tl["mlstm_chunk"])
            ys, w_out, w_layer = [ya.reshape(t, a_w), yb.reshape(t, -1)], wo_ab, e
        else:
            o = i // 2
            y, a1, _ = _in_proj(xf, w_in_c, o, c_main, tm=tl["proj_tm"], tn=tl["proj_tn"], w_t=w_c_t)
            w_a2 = jnp.pad(w_alpha2[o], ((0, LANES - w_alpha2.shape[1]), (0, 0)))
            w_a2_hi = lax.reduce_precision(w_a2, exponent_bits=8, mantissa_bits=7)
            w_a2_lo = (w_a2 - w_a2_hi).astype(BF16)
            yc = _gla(y.reshape(bsz, s, c_main), a1.reshape(bsz, s, LANES), w_a2_hi.astype(BF16), w_a2_lo,
                      row(b_alpha[o]),
                      row(gla_norm[o]), chunk=tl["gla_chunk"], block=tl["gla_block"])
            ys, w_out, w_layer = [yc.reshape(t, -1)], wo_c, o
        xf = _out_ln(ys, w_out, xf, g4, b4, i, w_layer, 1, alpha=alpha, tm=tl["row_tm"])
        xf = _ffn_ln(xf, *ffn_w[2 * i + 1], g4, b4, i, 2, alpha=alpha, tm=tl["ffn_tm"], tf=tl["ffn_tf"])
        xf = _ple(xf, p, wpp, wpg, i, tm=tl["row_tm"])
    return xf.reshape(bsz, s, d)
```

```python
import functools
import math

import jax
import jax.numpy as jnp
from jax import lax
from jax.experimental import pallas as pl
from jax.experimental.pallas import tpu as pltpu

F32 = jnp.float32
BF16 = jnp.bfloat16
HIGHEST = lax.Precision.HIGHEST

A_HEADS = 8
A_QK_DIM = 64
A_V_DIM = 128
B_HEADS = 4
B_QK_DIM = 128
B_V_DIM = 256
CONV_WIDTH = 4
C_HEADS = 4
C_QK_DIM = 256
C_V_DIM = 512
C_GATE_TEMP = 16.0
GLA_SUB = 64
N_BUCKETS = 32
MAX_DISTANCE = 128
LN_EPS = 1e-5
NORM_EPS = 1e-6

V7X_VMEM_LIMIT_BYTES = 56 * 1024 * 1024
LANES = 128
SUBLANES = 8
BF16_SUBLANES = 16
ATTN_ONES_ROWS = BF16_SUBLANES
MASKED_LOGIT = -1e30
ROW_SPLIT = 4

NT_DIMS = (((1,), (1,)), ((), ()))
TN_DIMS = (((0,), (0,)), ((), ()))


def _params(*sem):
    return pltpu.CompilerParams(dimension_semantics=sem, vmem_limit_bytes=V7X_VMEM_LIMIT_BYTES)


def _layer_norm(y, g, b, eps=LN_EPS):
    mu = jnp.mean(y, -1, keepdims=True)
    yc = y - mu
    var = jnp.mean(yc * yc, -1, keepdims=True)
    return yc * lax.rsqrt(var + eps) * g + b


def _sigmoid(x):
    return 1.0 / (1.0 + jnp.exp(-x))


def _log_sigmoid(x):
    return jnp.minimum(x, 0.0) - jnp.log(1.0 + jnp.exp(-jnp.abs(x)))


def _ffn_ln_kernel(x_ref, wg0_ref, wu0_ref, wo0_ref, wg1_ref, wu1_ref, wo1_ref, g_ref, b_ref, o_ref, xb_ref,
                   *, alpha, n_chunks):
    j = pl.program_id(1)
    last = pl.num_programs(1) - 1

    @pl.when(j == 0)
    def _():
        x = x_ref[...]
        xb_ref[...] = x.astype(BF16)
        o_ref[...] = (2.0 * alpha) * x

    def chunk(wg_ref, wu_ref, wo_ref):
        xb = xb_ref[...]
        gate = jnp.dot(xb, wg_ref[...], preferred_element_type=F32)
        up = jnp.dot(xb, wu_ref[...], preferred_element_type=F32)
        h = (gate * _sigmoid(gate) * up).astype(BF16)
        o_ref[...] += jnp.dot(h, wo_ref[...], preferred_element_type=F32)

    if n_chunks % 2 == 0:
        chunk(wg0_ref, wu0_ref, wo0_ref)
        chunk(wg1_ref, wu1_ref, wo1_ref)
    else:
        @pl.when(j < last)
        def _():
            chunk(wg0_ref, wu0_ref, wo0_ref)
            chunk(wg1_ref, wu1_ref, wo1_ref)

        @pl.when(j == last)
        def _():
            chunk(wg0_ref, wu0_ref, wo0_ref)

    @pl.when(j == last)
    def _():
        o_ref[...] = _layer_norm(o_ref[...], g_ref[...], b_ref[...], eps=4.0 * LN_EPS)


def _ffn_ln(x, w_in, w_out, w_idx, ln_g, ln_b, layer, ln_slot, *, alpha, tm, tf):
    t, d = x.shape
    f = w_out.shape[1]
    nf = f // tf
    assert t % tm == 0 and f % tf == 0 and w_in.shape[1:] == (d, 2 * f) and nf >= 2
    steps = (nf + 1) // 2
    first = lambda j: 2 * j
    second = lambda j: jnp.where(2 * j + 1 < nf, 2 * j + 1, nf - 2)
    ln_spec = pl.BlockSpec((None, None, 1, d), lambda i, j: (layer, ln_slot, 0, 0))

    def weight_specs(chunk_of):
        return [pl.BlockSpec((None, d, tf), lambda i, j: (w_idx, 0, chunk_of(j))),
                pl.BlockSpec((None, d, tf), lambda i, j: (w_idx, 0, nf + chunk_of(j))),
                pl.BlockSpec((None, tf, d), lambda i, j: (w_idx, chunk_of(j), 0))]

    return pl.pallas_call(
        functools.partial(_ffn_ln_kernel, alpha=alpha, n_chunks=nf),
        grid=(t // tm, steps),
        in_specs=[pl.BlockSpec((tm, d), lambda i, j: (i, 0))] + weight_specs(first) + weight_specs(second)
                 + [ln_spec, ln_spec],
        out_specs=pl.BlockSpec((tm, d), lambda i, j: (i, 0)),
        out_shape=jax.ShapeDtypeStruct((t, d), F32),
        scratch_shapes=[pltpu.VMEM((tm, d), BF16)],
        compiler_params=_params("parallel", "arbitrary"),
        name="ffn_ln",
    )(x, w_in, w_in, w_out, w_in, w_in, w_out, ln_g, ln_b)


def _in_proj_kernel(x_ref, w_ref, wgc_ref, wgr_ref, y_ref, gc_ref, gr_ref, xb_ref):
    j = pl.program_id(1)

    @pl.when(j == 0)
    def _():
        xb = x_ref[...].astype(BF16)
        xb_ref[...] = xb
        gc_ref[...] = jnp.dot(xb, wgc_ref[...], preferred_element_type=F32)
        gr_ref[...] = lax.dot_general(wgr_ref[...], xb, NT_DIMS, preferred_element_type=F32)

    y_ref[...] = lax.dot_general(xb_ref[...], w_ref[...].astype(BF16), NT_DIMS,
                                 preferred_element_type=F32).astype(y_ref.dtype)


def _in_proj(x, w_stack, layer, n_main, *, tm, tn, w_t=None):
    t, d = x.shape
    assert t % tm == 0 and n_main % tn == 0
    if w_t is None:
        w_t = jnp.swapaxes(w_stack, 1, 2)
    w_gates = w_stack[layer, :, n_main:].astype(BF16)
    w_gate_cols = _pad_cols(w_gates, LANES)
    w_gate_rows = jnp.transpose(w_gates[:, :SUBLANES])
    return pl.pallas_call(
        _in_proj_kernel,
        grid=(t // tm, n_main // tn),
        in_specs=[
            pl.BlockSpec((tm, d), lambda i, j: (i, 0)),
            pl.BlockSpec((None, tn, d), lambda i, j: (layer, j, 0)),
            pl.BlockSpec((d, LANES), lambda i, j: (0, 0)),
            pl.BlockSpec((SUBLANES, d), lambda i, j: (0, 0)),
        ],
        out_specs=[
            pl.BlockSpec((tm, tn), lambda i, j: (i, j)),
            pl.BlockSpec((tm, LANES), lambda i, j: (i, 0)),
            pl.BlockSpec((SUBLANES, tm), lambda i, j: (0, i)),
        ],
        out_shape=[
            jax.ShapeDtypeStruct((t, n_main), BF16),
            jax.ShapeDtypeStruct((t, LANES), F32),
            jax.ShapeDtypeStruct((SUBLANES, t), F32),
        ],
        scratch_shapes=[pltpu.VMEM((tm, d), BF16)],
        compiler_params=_params("parallel", "arbitrary"),
        name="in_proj",
    )(x, w_t, w_gate_cols, w_gate_rows)


def _diff_attn_kernel(q_ref, k_ref, v_ref, tab_ref, lam_ref, g_ref, *refs, t, lam_init, n_jobs):
    cast_in, (o_ref, *cast_out) = refs[:n_jobs], refs[n_jobs:2 * n_jobs + 1]
    (vt_ref, bdiag_ref, bnear_ref, m_ref, acc_ref,
     s0_ref, mx0_ref, sh0_ref, s1_ref, mx1_ref, sh1_ref) = refs[2 * n_jobs + 1:]
    qi = pl.program_id(2)
    dk, dv = A_QK_DIM, A_V_DIM
    nk = k_ref.shape[0] // t
    bufs = ((s0_ref, mx0_ref, sh0_ref), (s1_ref, mx1_ref, sh1_ref))

    @pl.when(qi == 0)
    def _():
        for c in range(nk):
            vt_ref[c, 0:dv, :] = jnp.transpose(v_ref[c * t:(c + 1) * t, :].astype(F32)).astype(BF16)
            vt_ref[c, dv:, :] = jnp.ones((vt_ref.shape[1] - dv, t), BF16)
        skew = pltpu.roll(jnp.broadcast_to(tab_ref[...], (t, 2 * t)), 0, 1, stride=1, stride_axis=0)
        key = lax.broadcasted_iota(jnp.int32, (t, t), 0)
        qry = lax.broadcasted_iota(jnp.int32, (t, t), 1)
        bdiag_ref[...] = jnp.where(qry >= key, skew[:, :t], MASKED_LOGIT)
        bnear_ref[...] = skew[:, t:]

    q_t = jnp.transpose(q_ref[...].astype(F32)) * (dk ** -0.5)
    feat = lax.broadcasted_iota(jnp.int32, q_t.shape, 0)
    qs_t = jnp.concatenate([jnp.where(feat < dk, q_t, 0.0), jnp.where(feat >= dk, q_t, 0.0)],
                           axis=1).astype(BF16)

    m_ref[...] = jnp.full_like(m_ref, -jnp.inf)
    acc_ref[...] = jnp.zeros_like(acc_ref)

    def score(kj, buf, bias_ref, shift):
        s_ref, mx_ref, sh_ref = bufs[buf]
        off = pl.multiple_of(kj * t, t)
        s = jnp.dot(k_ref[pl.ds(off, t), :], qs_t, preferred_element_type=F32)
        if bias_ref is not None:
            b = bias_ref[...]
            s = s + jnp.concatenate([b, b], axis=1)
        s_ref[...] = s
        mx_ref[...] = jnp.max(s, 0, keepdims=True) + shift
        sh_ref[...] = jnp.zeros_like(sh_ref) + shift

    def absorb(kj, buf):
        s_ref, mx_ref, sh_ref = bufs[buf]
        m_old = m_ref[...]
        m_new = jnp.maximum(m_old, mx_ref[...])
        p = jnp.exp(s_ref[...] - (m_new - sh_ref[...])).astype(BF16)
        acc_ref[...] = (jnp.exp(m_old - m_new) * acc_ref[...]
                        + jnp.dot(vt_ref[kj], p, preferred_element_type=F32))
        m_ref[...] = m_new

    c_far = tab_ref[:, 2 * t - 1:2 * t]
    for src_ref, dst_ref in zip(cast_in, cast_out):
        dst_ref[...] = src_ref[...].astype(BF16)
    score(qi, 0, bdiag_ref, 0.0)

    @pl.when(qi >= 1)
    def _():
        score(qi - 1, 1, bnear_ref, 0.0)
        absorb(qi, 0)

    def far_pair(p, carry):
        kj = qi - 2 - 2 * p
        score(kj, 0, None, c_far)
        absorb(kj + 1, 1)
        score(kj - 1, 1, None, c_far)
        absorb(kj, 0)
        return carry

    lax.fori_loop(0, jnp.maximum(qi - 1, 0) // 2, far_pair, 0)

    @pl.when((qi >= 2) & (qi % 2 == 0))
    def _():
        score(0, 0, None, c_far)
        absorb(1, 1)

    @pl.when(qi % 2 == 0)
    def _():
        absorb(0, 0)

    @pl.when(qi % 2 == 1)
    def _():
        absorb(0, 1)

    lv = lam_ref[...]
    lam = (jnp.exp(jnp.sum(lv[0:1] * lv[1:2], -1, keepdims=True))
           - jnp.exp(jnp.sum(lv[2:3] * lv[3:4], -1, keepdims=True)) + lam_init)
    acc = acc_ref[...]
    o_t = acc[0:dv] * (1.0 / acc[dv:dv + 1])
    out = jnp.transpose(o_t[:, :t] - lam * o_t[:, t:])
    hn = out * lax.rsqrt(jnp.mean(out * out, -1, keepdims=True) + NORM_EPS)
    o_ref[...] = (hn * g_ref[...] * (1.0 - lam_init)).astype(o_ref.dtype)


def _t5_bias_by_distance(rel_bias, n):
    r = jnp.arange(n, dtype=jnp.int32)
    max_exact = N_BUCKETS // 2
    rf = jnp.maximum(r, 1).astype(F32)
    large = max_exact + (jnp.log(rf / max_exact) / math.log(MAX_DISTANCE / max_exact)
                         * (N_BUCKETS - max_exact)).astype(jnp.int32)
    large = jnp.minimum(large, N_BUCKETS - 1)
    bucket = jnp.where(r < max_exact, r, large)
    return jnp.transpose(rel_bias[bucket]).astype(F32)


def _diff_attention(y3, rel_bias, lam_vecs, diff_g, jobs, *, lam_init, t):
    bsz, s, _ = y3.shape
    hd = 2 * A_QK_DIM
    dv = A_V_DIM
    assert hd == dv == LANES and s % t == 0 and t >= MAX_DISTANCE
    nq = s // t
    table = _t5_bias_by_distance(rel_bias, 2 * t).reshape(A_HEADS, 1, 2 * t)
    assert all(blocks <= bsz * A_HEADS * nq for _, _, _, blocks in jobs)

    def job_block(blocks):
        return lambda b, h, q: jnp.minimum((b * A_HEADS + h) * nq + q, blocks - 1)

    cast_in = [pl.BlockSpec((rows, w.shape[1]), lambda b, h, q, f=first, blk=job_block(blocks): (f + blk(b, h, q), 0))
               for w, rows, first, blocks in jobs]
    cast_out = [pl.BlockSpec((rows, w.shape[1]), lambda b, h, q, blk=job_block(blocks): (blk(b, h, q), 0))
                for w, rows, first, blocks in jobs]
    cast_shapes = [jax.ShapeDtypeStruct((rows * blocks, w.shape[1]), BF16) for w, rows, first, blocks in jobs]
    return pl.pallas_call(
        functools.partial(_diff_attn_kernel, t=t, lam_init=lam_init, n_jobs=len(jobs)),
        grid=(bsz, A_HEADS, nq),
        in_specs=[
            pl.BlockSpec((None, t, hd), lambda b, h, q: (b, q, h)),
            pl.BlockSpec((None, s, hd), lambda b, h, q: (b, 0, A_HEADS + h)),
            pl.BlockSpec((None, s, dv), lambda b, h, q: (b, 0, 2 * A_HEADS + h)),
            pl.BlockSpec((None, 1, 2 * t), lambda b, h, q: (h, 0, 0)),
            pl.BlockSpec((4, A_QK_DIM), lambda b, h, q: (0, 0)),
            pl.BlockSpec((1, dv), lambda b, h, q: (0, h)),
        ] + cast_in,
        out_specs=[pl.BlockSpec((None, t, dv), lambda b, h, q: (b, q, h))] + cast_out,
        out_shape=[jax.ShapeDtypeStruct((bsz, s, A_HEADS * dv), BF16)] + cast_shapes,
        scratch_shapes=[pltpu.VMEM((nq, dv + ATTN_ONES_ROWS, t), BF16),
                        pltpu.VMEM((t, t), F32),
                        pltpu.VMEM((t, t), F32),
                        pltpu.VMEM((1, 2 * t), F32),
                        pltpu.VMEM((dv + ATTN_ONES_ROWS, 2 * t), F32)]
                       + 2 * [pltpu.VMEM((t, 2 * t), F32), pltpu.VMEM((1, 2 * t), F32),
                              pltpu.VMEM((1, 2 * t), F32)],
        compiler_params=_params("arbitrary", "arbitrary", "arbitrary"),
        name="diff_attn",
    )(y3, y3, y3, table, lam_vecs, diff_g, *[w for w, _, _, _ in jobs])


def _mlstm_kernel(qk_ref, v_ref, ob_ref, gc_ref, gr_ref, cw_ref, cb_ref, bc_ref, br_ref, g_ref, y_ref,
                  xp_ref, *state_refs, chunk):
    heads, dk, dv = B_HEADS, B_QK_DIM, B_V_DIM
    c_refs, n_refs, m_refs = state_refs[0::3], state_refs[1::3], state_refs[2::3]
    pad = xp_ref.shape[0] - chunk
    c = pl.program_id(1)

    @pl.when(c == 0)
    def _():
        xp_ref[0:pad, :] = jnp.zeros((pad, xp_ref.shape[1]), xp_ref.dtype)
        for ref in state_refs:
            ref[...] = jnp.zeros_like(ref)

    x = qk_ref[...]
    xp_ref[pad:pad + chunk, :] = x
    xp = xp_ref[...]
    cw = cw_ref[...]
    dst = lax.broadcasted_iota(jnp.int32, (chunk, pad + chunk), 0)
    src = lax.broadcasted_iota(jnp.int32, (chunk, pad + chunk), 1)
    conv = cb_ref[...] + cw[CONV_WIDTH - 1:CONV_WIDTH, :] * x.astype(F32)
    for j in range(CONV_WIDTH - 1):
        shift = (src == dst + (pad - (CONV_WIDTH - 1) + j)).astype(BF16)
        conv = conv + cw[j:j + 1, :] * jnp.dot(shift, xp, preferred_element_type=F32)
    xp_ref[0:pad, :] = x[chunk - pad:chunk, :]
    qk = conv * _sigmoid(conv)

    gc = gc_ref[...] + bc_ref[...]
    gr = gr_ref[...] + br_ref[...]
    row = lax.broadcasted_iota(jnp.int32, (chunk, chunk), 0)
    col = lax.broadcasted_iota(jnp.int32, (chunk, chunk), 1)
    causal = row >= col
    tril = causal.astype(F32)
    triu = (row <= col).astype(F32)
    b_c = jnp.dot(tril, _log_sigmoid(gc), precision=HIGHEST, preferred_element_type=F32)
    b_r = jnp.dot(_log_sigmoid(gr), triu, precision=HIGHEST, preferred_element_type=F32)

    ob = ob_ref[...].astype(F32)
    g = g_ref[...]
    hs = range(heads)
    q = [qk[:, h * dk:(h + 1) * dk] * (dk ** -0.5) for h in hs]
    k = [qk[:, (heads + h) * dk:(heads + h + 1) * dk] for h in hs]
    v = [v_ref[:, h * dv:(h + 1) * dv] for h in hs]
    b_col = [b_c[:, heads + h:heads + h + 1] for h in hs]
    i_col = [gc[:, h:h + 1] for h in hs]
    b_row = [b_r[heads + h:heads + h + 1, :] for h in hs]
    i_row = [gr[h:h + 1, :] for h in hs]
    m_prev = [m_refs[h][0:1, 0:1] for h in hs]
    c_mat = [c_refs[h][...] for h in hs]
    n_vec = [n_refs[h][...] for h in hs]

    b_last = [b_row[h][:, chunk - 1:chunk] for h in hs]
    g_col = [b_last[h] - b_col[h] + i_col[h] for h in hs]
    m_new = [jnp.maximum(b_last[h] + m_prev[h], jnp.max(g_col[h], 0, keepdims=True)) for h in hs]
    decay = [jnp.exp(b_last[h] + m_prev[h] - m_new[h]) for h in hs]
    kw = [k[h] * jnp.exp(g_col[h] - m_new[h]) for h in hs]
    qb = [q[h].astype(BF16) for h in hs]
    qk_t = [lax.dot_general(qb[h], k[h].astype(BF16), NT_DIMS, preferred_element_type=F32) for h in hs]
    q_c = [jnp.dot(qb[h], c_mat[h].astype(BF16), preferred_element_type=F32) for h in hs]
    kw_v = [lax.dot_general(kw[h].astype(BF16), v[h], TN_DIMS, preferred_element_type=F32) for h in hs]

    d = [jnp.where(causal, b_col[h] - b_row[h] + i_row[h], -jnp.inf) for h in hs]
    a_col = [b_col[h] + m_prev[h] for h in hs]
    m_t = [jnp.maximum(a_col[h], jnp.max(d[h], -1, keepdims=True)) for h in hs]
    w_inter = [jnp.exp(a_col[h] - m_t[h]) for h in hs]
    sw = [qk_t[h] * jnp.exp(d[h] - m_t[h]) for h in hs]
    sw_v = [jnp.dot(sw[h].astype(BF16), v[h], preferred_element_type=F32) for h in hs]

    for h in hs:
        c_refs[h][...] = decay[h] * c_mat[h] + kw_v[h]
        n_refs[h][...] = decay[h] * n_vec[h] + jnp.sum(kw[h], 0, keepdims=True)
        m_refs[h][...] = jnp.broadcast_to(m_new[h], m_refs[h].shape)

    for h in hs:
        num = w_inter[h] * q_c[h] + sw_v[h]
        den = (w_inter[h] * jnp.sum(q[h] * n_vec[h], -1, keepdims=True)
               + jnp.sum(sw[h], -1, keepdims=True))
        hh = num / jnp.maximum(jnp.abs(den), jnp.exp(-m_t[h]))
        hn = hh * lax.rsqrt(jnp.mean(hh * hh, -1, keepdims=True) + NORM_EPS)
        sl = slice(h * dv, (h + 1) * dv)
        y_ref[:, sl] = (hn * g[:, sl] * _sigmoid(ob[:, sl])).astype(y_ref.dtype)


def _mlstm(y3, gcol3, grow3, conv_w, conv_b, gate_bias_cols, gate_bias_rows, mlstm_g, *, chunk):
    bsz, s, _ = y3.shape
    heads, dk, dv = B_HEADS, B_QK_DIM, B_V_DIM
    w = heads * dv
    assert 2 * heads * dk == w and s % chunk == 0 and chunk % LANES == 0
    nc = s // chunk
    return pl.pallas_call(
        functools.partial(_mlstm_kernel, chunk=chunk),
        grid=(bsz, nc),
        in_specs=[
            pl.BlockSpec((None, chunk, w), lambda b, c: (b, c, 3)),
            pl.BlockSpec((None, chunk, w), lambda b, c: (b, c, 4)),
            pl.BlockSpec((None, chunk, w), lambda b, c: (b, c, 5)),
            pl.BlockSpec((None, chunk, LANES), lambda b, c: (b, c, 0)),
            pl.BlockSpec((SUBLANES, chunk), lambda b, c: (0, b * nc + c)),
            pl.BlockSpec((CONV_WIDTH, w), lambda b, c: (0, 0)),
            pl.BlockSpec((1, w), lambda b, c: (0, 0)),
            pl.BlockSpec((1, LANES), lambda b, c: (0, 0)),
            pl.BlockSpec((SUBLANES, 1), lambda b, c: (0, 0)),
            pl.BlockSpec((1, w), lambda b, c: (0, 0)),
        ],
        out_specs=pl.BlockSpec((None, chunk, w), lambda b, c: (b, c, 0)),
        out_shape=jax.ShapeDtypeStruct((bsz, s, w), BF16),
        scratch_shapes=[
            pltpu.VMEM((BF16_SUBLANES + chunk, w), BF16),
        ] + heads * [pltpu.VMEM((dk, dv), F32), pltpu.VMEM((1, dk), F32), pltpu.VMEM((SUBLANES, LANES), F32)],
        compiler_params=_params("parallel", "arbitrary"),
        name="mlstm",
    )(y3, y3, y3, gcol3, grow3, conv_w, conv_b, gate_bias_cols, gate_bias_rows, mlstm_g)


def _gla_kernel(q_ref, k_ref, v_ref, r_ref, a1_ref, a1n_ref, wah_ref, wal_ref, ba_ref, g_ref, y_ref,
                b_ref, *st_refs, chunk):
    heads, dk, dv = C_HEADS, C_QK_DIM, C_V_DIM
    sub = GLA_SUB
    assert chunk == 2 * sub
    c = pl.program_id(1)
    row = lax.broadcasted_iota(jnp.int32, (chunk, chunk), 0)
    col = lax.broadcasted_iota(jnp.int32, (chunk, chunk), 1)
    causal = (row >= col) & ((row < sub) == (col < sub))
    first = lax.broadcasted_iota(jnp.int32, (chunk, 1), 0) < sub

    def cum_log_decay(a1):
        a_hi = a1.astype(BF16)
        a_lo = (a1 - a_hi.astype(F32)).astype(BF16)
        w_hi = wah_ref[...]
        z = (jnp.dot(a_hi, w_hi, preferred_element_type=F32)
             + (jnp.dot(a_lo, w_hi, preferred_element_type=F32)
                + jnp.dot(a_hi, wal_ref[...], preferred_element_type=F32))) + ba_ref[...]
        log_a = _log_sigmoid(z) * (1.0 / C_GATE_TEMP)
        return jnp.dot(causal.astype(F32), log_a, precision=HIGHEST, preferred_element_type=F32)

    @pl.when(c == 0)
    def _():
        for ref in st_refs:
            ref[...] = jnp.zeros_like(ref)
        b_ref[...] = cum_log_decay(a1_ref[0:chunk, :])

    g = g_ref[...]
    n_inner = q_ref.shape[0] // chunk
    for i in range(n_inner):
        rows = slice(i * chunk, (i + 1) * chunk)
        b_all = b_ref[...]
        b_next = cum_log_decay(a1_ref[(i + 1) * chunk:(i + 2) * chunk, :] if i + 1 < n_inner else a1n_ref[...])

        r = r_ref[rows, :].astype(F32)
        for h in range(heads):
            ks = slice(h * dk, (h + 1) * dk)
            vs = slice(h * dv, (h + 1) * dv)
            b = b_all[:, ks]
            q = q_ref[rows, ks].astype(F32) * (dk ** -0.5)
            k = k_ref[rows, ks].astype(F32)
            v = v_ref[rows, vs]
            tot0 = b[sub - 1:sub, :]
            tot1 = b[chunk - 1:chunk, :]
            q_dec = q * jnp.exp(b)
            k_dec = (k * jnp.exp(-b)).astype(BF16)
            k_end = k * jnp.exp(jnp.where(first, tot0, tot1) - b)
            k_new = (k_end * jnp.where(first, jnp.exp(tot1), 1.0)).astype(BF16)
            q_int = (q_dec * jnp.where(first, 1.0, jnp.exp(tot0))).astype(BF16)
            q_dec = q_dec.astype(BF16)
            st = st_refs[h][...]
            qk_t = lax.dot_general(q_dec, k_dec, NT_DIMS, preferred_element_type=F32)
            inter = lax.dot_general(q_int, st.astype(BF16), NT_DIMS, preferred_element_type=F32)
            st_refs[h][...] = (jnp.exp(tot0 + tot1) * st
                               + lax.dot_general(v, k_new, TN_DIMS, preferred_element_type=F32))
            cross = lax.dot_general(q_dec[sub:], k_end[:sub].astype(BF16), NT_DIMS, preferred_element_type=F32)
            att = jnp.where(causal, qk_t, 0.0)
            o = jnp.dot(att.astype(BF16), v, preferred_element_type=F32) + inter
            o_cross = jnp.dot(cross.astype(BF16), v[:sub], preferred_element_type=F32)
            o = jnp.concatenate([o[:sub], o[sub:] + o_cross], axis=0)

            on = o * lax.rsqrt(jnp.mean(o * o, -1, keepdims=True) + NORM_EPS)
            rr = r[:, vs]
            y_ref[rows, vs] = (on * g[:, vs] * (rr * _sigmoid(rr))).astype(y_ref.dtype)

        b_ref[...] = b_next


def _gla(y3, a13, w_a2_hi, w_a2_lo, b_a, gla_g, *, chunk, block):
    bsz, s, _ = y3.shape
    heads, dk, dv = C_HEADS, C_QK_DIM, C_V_DIM
    wk, wv = heads * dk, heads * dv
    assert wv == 2 * wk and s % block == 0 and block % chunk == 0
    nb, per_block = s // block, block // chunk
    return pl.pallas_call(
        functools.partial(_gla_kernel, chunk=chunk),
        grid=(bsz, nb),
        in_specs=[
            pl.BlockSpec((None, block, wk), lambda b, c: (b, c, 0)),
            pl.BlockSpec((None, block, wk), lambda b, c: (b, c, 1)),
            pl.BlockSpec((None, block, wv), lambda b, c: (b, c, 1)),
            pl.BlockSpec((None, block, wv), lambda b, c: (b, c, 2)),
            pl.BlockSpec((None, block, LANES), lambda b, c: (b, c, 0)),
            pl.BlockSpec((None, chunk, LANES), lambda b, c: (b, jnp.minimum(c + 1, nb - 1) * per_block, 0)),
            pl.BlockSpec((LANES, wk), lambda b, c: (0, 0)),
            pl.BlockSpec((LANES, wk), lambda b, c: (0, 0)),
            pl.BlockSpec((1, wk), lambda b, c: (0, 0)),
            pl.BlockSpec((1, wv), lambda b, c: (0, 0)),
        ],
        out_specs=pl.BlockSpec((None, block, wv), lambda b, c: (b, c, 0)),
        out_shape=jax.ShapeDtypeStruct((bsz, s, wv), BF16),
        scratch_shapes=[pltpu.VMEM((chunk, wk), F32)] + heads * [pltpu.VMEM((dv, dk), F32)],
        compiler_params=_params("parallel", "arbitrary"),
        name="gla",
    )(y3, y3, y3, y3, a13, a13, w_a2_hi, w_a2_lo, b_a, gla_g)


def _out_ln_kernel(*refs, n_in, alpha):
    y_refs, w_refs = refs[:n_in], refs[n_in:2 * n_in]
    x_ref, g_ref, b_ref, o_ref = refs[2 * n_in:]
    rb = o_ref.shape[0] // ROW_SPLIT
    for r in range(ROW_SPLIT):
        rows = slice(r * rb, (r + 1) * rb)
        acc = jnp.dot(y_refs[0][rows, :], w_refs[0][...], preferred_element_type=F32)
        for y_ref, w_ref in zip(y_refs[1:], w_refs[1:]):
            acc = acc + jnp.dot(y_ref[rows, :], w_ref[...], preferred_element_type=F32)
        o_ref[rows, :] = _layer_norm(alpha * x_ref[rows, :] + acc, g_ref[...], b_ref[...])


def _out_ln(ys, w, x, ln_g, ln_b, layer, w_layer, ln_slot, *, alpha, tm):
    t, d = x.shape
    n_in = len(ys)
    wd = ys[0].shape[1]
    assert all(y.shape[1] == wd for y in ys) and w.shape[1:] == (n_in * wd, d)
    ln_spec = pl.BlockSpec((None, None, 1, d), lambda i: (layer, ln_slot, 0, 0))
    in_specs = ([pl.BlockSpec((tm, wd), lambda i: (i, 0)) for _ in ys]
                + [pl.BlockSpec((None, wd, d), lambda i, k=k: (w_layer, k, 0)) for k in range(n_in)]
                + [pl.BlockSpec((tm, d), lambda i: (i, 0)), ln_spec, ln_spec])
    return pl.pallas_call(
        functools.partial(_out_ln_kernel, n_in=n_in, alpha=alpha),
        grid=(t // tm,),
        in_specs=in_specs,
        out_specs=pl.BlockSpec((tm, d), lambda i: (i, 0)),
        out_shape=jax.ShapeDtypeStruct((t, d), F32),
        compiler_params=_params("parallel"),
        name="out_ln",
    )(*ys, *([w] * n_in), x, ln_g, ln_b)


def _ple_kernel(x_ref, p_ref, wp_ref, wg_ref, o_ref):
    rb = o_ref.shape[0] // ROW_SPLIT
    for r in range(ROW_SPLIT):
        rows = slice(r * rb, (r + 1) * rb)
        x = x_ref[rows, :]
        e = jnp.dot(p_ref[rows, :].astype(BF16), wp_ref[...], preferred_element_type=F32)
        gate = jnp.dot(x.astype(BF16), wg_ref[...], preferred_element_type=F32)
        o_ref[rows, :] = x + e * _sigmoid(gate)


def _ple(x, p, w_proj, w_gate, layer, *, tm):
    t, d = x.shape
    s, pd = p.shape[2:]
    assert s % tm == 0
    per_seq = s // tm
    return pl.pallas_call(
        _ple_kernel,
        grid=(t // tm,),
        in_specs=[
            pl.BlockSpec((tm, d), lambda i: (i, 0)),
            pl.BlockSpec((None, None, tm, pd), lambda i: (layer, i // per_seq, i % per_seq, 0)),
            pl.BlockSpec((None, pd, d), lambda i: (layer, 0, 0)),
            pl.BlockSpec((None, d, d), lambda i: (layer, 0, 0)),
        ],
        out_specs=pl.BlockSpec((tm, d), lambda i: (i, 0)),
        out_shape=jax.ShapeDtypeStruct((t, d), F32),
        compiler_params=_params("parallel"),
        name="ple",
    )(x, p, w_proj, w_gate)


def _tiles(t, s):
    return dict(
        ffn_tm=min(1024, t), ffn_tf=256,
        proj_tm=min(1024, t), proj_tn=1024, proj_tn_bf16=1536,
        row_tm=min(512, t),
        attn_t=min(512, s),
        mlstm_chunk=min(256, s),
        gla_chunk=min(2 * GLA_SUB, s), gla_block=min(8 * GLA_SUB, s),
    )


def _cast_job(w2d, unit, skip, count, steps):
    rows = next(r for r in range(BF16_SUBLANES, unit + 1, BF16_SUBLANES)
                if unit % r == 0 and count * (unit // r) <= steps)
    return w2d, rows, skip * (unit // rows), count * (unit // rows)


def _pad_cols(w, n):
    return jnp.pad(w, ((0, 0), (0, n - w.shape[1])))


def kernel(x, p, ln_g, ln_b, w_ffn_in, w_ffn_out, w_in_ab, w_out_ab, rel_bias, lambda_q1, lambda_k1,
           lambda_q2, lambda_k2, diff_norm, conv_w, conv_b, b_igate, b_fgate, mlstm_norm, w_in_c,
           w_alpha2, b_alpha, gla_norm, w_out_c, w_ple_proj, w_ple_gate):
    bsz, s, d = x.shape
    depth = p.shape[0]
    t = bsz * s
    tl = _tiles(t, s)
    alpha = (2 * depth) ** 0.25
    a_w = A_HEADS * A_V_DIM
    ab_main = 3 * a_w + 3 * B_HEADS * B_V_DIM
    c_main = 2 * C_HEADS * C_QK_DIM + 2 * C_HEADS * C_V_DIM
    row = lambda a: a.reshape(1, -1)

    n_ffn, f_ff = 2 * depth, w_ffn_out.shape[2]
    attn_steps = bsz * A_HEADS * (s // tl["attn_t"])
    wfi_first, wfo_first = w_ffn_in[0, :1].astype(BF16), w_ffn_out[0, :1].astype(BF16)
    ffn_w = {0: (wfi_first, wfo_first, 0)}
    cast_jobs = [
        _cast_job(w_ffn_in.reshape(n_ffn * d, -1), d, 1, n_ffn - 1, attn_steps),
        _cast_job(w_ffn_out.reshape(n_ffn * f_ff, d), f_ff, 1, n_ffn - 1, attn_steps),
        _cast_job(jnp.swapaxes(w_in_c, 1, 2).reshape(-1, d), w_in_c.shape[2], 0, w_in_c.shape[0], attn_steps),
    ]
    wo_ab, wo_c = w_out_ab.astype(BF16), w_out_c.astype(BF16)
    wpp, wpg = w_ple_proj.astype(BF16), w_ple_gate.astype(BF16)
    g4, b4 = ln_g.reshape(depth, 3, 1, d), ln_b.reshape(depth, 3, 1, d)

    xf = x.reshape(t, d)
    for i in range(depth):
        xf = _ffn_ln(xf, *ffn_w[2 * i], g4, b4, i, 0, alpha=alpha, tm=tl["ffn_tm"], tf=tl["ffn_tf"])
        if i % 2 == 0:
            e = i // 2
            lam_init = 0.8 - 0.6 * math.exp(-0.3 * i)
            y, gcol, grow = _in_proj(xf, w_in_ab, e, ab_main, tm=tl["proj_tm"], tn=tl["proj_tn"])
            y3 = y.reshape(bsz, s, ab_main)
            lam_vecs = jnp.stack([lambda_q1[e], lambda_k1[e], lambda_q2[e], lambda_k2[e]])
            ya, *cast = _diff_attention(y3, rel_bias, lam_vecs, row(diff_norm[e]), cast_jobs,
                                        lam_init=lam_init, t=tl["attn_t"])
            if i == 0:
                wfi_rest, wfo_rest = cast[0].reshape(n_ffn - 1, d, -1), cast[1].reshape(n_ffn - 1, f_ff, d)
                ffn_w.update({n: (wfi_rest, wfo_rest, n - 1) for n in range(1, n_ffn)})
                w_c_t = cast[2].reshape(w_in_c.shape[0], w_in_c.shape[2], d)
            gate_bias = jnp.concatenate([b_igate[e], b_fgate[e]])
            yb = _mlstm(y3, gcol.reshape(bsz, s, LANES), grow,
                        conv_w[e], row(conv_b[e]), _pad_cols(row(gate_bias), LANES), gate_bias.reshape(-1, 1),
                        row(mlstm_norm[e]), chunk=tl["mlstm_chunk"])
            ys, w_out, w_layer = [ya.reshape(t, a_w), yb.reshape(t, -1)], wo_ab, e
        else:
            o = i // 2
            y, a1, _ = _in_proj(xf, w_in_c, o, c_main, tm=tl["proj_tm"], tn=tl["proj_tn_bf16"], w_t=w_c_t)
            w_a2 = jnp.pad(w_alpha2[o], ((0, LANES - w_alpha2.shape[1]), (0, 0)))
            w_a2_hi = lax.reduce_precision(w_a2, exponent_bits=8, mantissa_bits=7)
            w_a2_lo = (w_a2 - w_a2_hi).astype(BF16)
            yc = _gla(y.reshape(bsz, s, c_main), a1.reshape(bsz, s, LANES), w_a2_hi.astype(BF16), w_a2_lo,
                      row(b_alpha[o]),
                      row(gla_norm[o]), chunk=tl["gla_chunk"], block=tl["gla_block"])
            ys, w_out, w_layer = [yc.reshape(t, -1)], wo_c, o
        xf = _out_ln(ys, w_out, xf, g4, b4, i, w_layer, 1, alpha=alpha, tm=tl["row_tm"])
        xf = _ffn_ln(xf, *ffn_w[2 * i + 1], g4, b4, i, 2, alpha=alpha, tm=tl["ffn_tm"], tf=tl["ffn_tf"])
        xf = _ple(xf, p, wpp, wpg, i, tm=tl["row_tm"])
    return xf.reshape(bsz, s, d)
```

```python
import functools
import math

import jax
import jax.numpy as jnp
from jax import lax
from jax.experimental import pallas as pl
from jax.experimental.pallas import tpu as pltpu

F32 = jnp.float32
BF16 = jnp.bfloat16
HIGHEST = lax.Precision.HIGHEST

A_HEADS = 8
A_QK_DIM = 64
A_V_DIM = 128
B_HEADS = 4
B_QK_DIM = 128
B_V_DIM = 256
CONV_WIDTH = 4
C_HEADS = 4
C_QK_DIM = 256
C_V_DIM = 512
C_GATE_TEMP = 16.0
GLA_SUB = 64
N_BUCKETS = 32
MAX_DISTANCE = 128
LN_EPS = 1e-5
NORM_EPS = 1e-6

V7X_VMEM_LIMIT_BYTES = 56 * 1024 * 1024
LANES = 128
SUBLANES = 8
BF16_SUBLANES = 16
ATTN_ONES_ROWS = BF16_SUBLANES
MASKED_LOGIT = -1e30
ROW_SPLIT = 4

NT_DIMS = (((1,), (1,)), ((), ()))
TN_DIMS = (((0,), (0,)), ((), ()))


def _params(*sem):
    return pltpu.CompilerParams(dimension_semantics=sem, vmem_limit_bytes=V7X_VMEM_LIMIT_BYTES)


def _layer_norm(y, g, b, eps=LN_EPS):
    mu = jnp.mean(y, -1, keepdims=True)
    yc = y - mu
    var = jnp.mean(yc * yc, -1, keepdims=True)
    return yc * lax.rsqrt(var + eps) * g + b


def _sigmoid(x):
    return 1.0 / (1.0 + jnp.exp(-x))


def _log_sigmoid(x):
    return jnp.minimum(x, 0.0) - jnp.log(1.0 + jnp.exp(-jnp.abs(x)))


def _ffn_ln_kernel(x_ref, wg0_ref, wu0_ref, wo0_ref, wg1_ref, wu1_ref, wo1_ref, g_ref, b_ref, o_ref, xb_ref,
                   *, alpha, n_chunks):
    j = pl.program_id(1)
    last = pl.num_programs(1) - 1

    @pl.when(j == 0)
    def _():
        x = x_ref[...]
        xb_ref[...] = x.astype(BF16)
        o_ref[...] = (2.0 * alpha) * x

    def chunk(wg_ref, wu_ref, wo_ref):
        xb = xb_ref[...]
        gate = jnp.dot(xb, wg_ref[...], preferred_element_type=F32)
        up = jnp.dot(xb, wu_ref[...], preferred_element_type=F32)
        h = (gate * _sigmoid(gate) * up).astype(BF16)
        o_ref[...] += jnp.dot(h, wo_ref[...], preferred_element_type=F32)

    if n_chunks % 2 == 0:
        chunk(wg0_ref, wu0_ref, wo0_ref)
        chunk(wg1_ref, wu1_ref, wo1_ref)
    else:
        @pl.when(j < last)
        def _():
            chunk(wg0_ref, wu0_ref, wo0_ref)
            chunk(wg1_ref, wu1_ref, wo1_ref)

        @pl.when(j == last)
        def _():
            chunk(wg0_ref, wu0_ref, wo0_ref)

    @pl.when(j == last)
    def _():
        o_ref[...] = _layer_norm(o_ref[...], g_ref[...], b_ref[...], eps=4.0 * LN_EPS)


def _ffn_ln(x, w_in, w_out, w_idx, ln_g, ln_b, layer, ln_slot, *, alpha, tm, tf):
    t, d = x.shape
    f = w_out.shape[1]
    nf = f // tf
    assert t % tm == 0 and f % tf == 0 and w_in.shape[1:] == (d, 2 * f) and nf >= 2
    steps = (nf + 1) // 2
    first = lambda j: 2 * j
    second = lambda j: jnp.where(2 * j + 1 < nf, 2 * j + 1, nf - 2)
    ln_spec = pl.BlockSpec((None, None, 1, d), lambda i, j: (layer, ln_slot, 0, 0))

    def weight_specs(chunk_of):
        return [pl.BlockSpec((None, d, tf), lambda i, j: (w_idx, 0, chunk_of(j))),
                pl.BlockSpec((None, d, tf), lambda i, j: (w_idx, 0, nf + chunk_of(j))),
                pl.BlockSpec((None, tf, d), lambda i, j: (w_idx, chunk_of(j), 0))]

    return pl.pallas_call(
        functools.partial(_ffn_ln_kernel, alpha=alpha, n_chunks=nf),
        grid=(t // tm, steps),
        in_specs=[pl.BlockSpec((tm, d), lambda i, j: (i, 0))] + weight_specs(first) + weight_specs(second)
                 + [ln_spec, ln_spec],
        out_specs=pl.BlockSpec((tm, d), lambda i, j: (i, 0)),
        out_shape=jax.ShapeDtypeStruct((t, d), F32),
        scratch_shapes=[pltpu.VMEM((tm, d), BF16)],
        compiler_params=_params("parallel", "arbitrary"),
        name="ffn_ln",
    )(x, w_in, w_in, w_out, w_in, w_in, w_out, ln_g, ln_b)


def _in_proj_kernel(x_ref, w_ref, wgc_ref, wgr_ref, y_ref, gc_ref, gr_ref, xb_ref):
    j = pl.program_id(1)

    @pl.when(j == 0)
    def _():
        xb = x_ref[...].astype(BF16)
        xb_ref[...] = xb
        gc_ref[...] = jnp.dot(xb, wgc_ref[...], preferred_element_type=F32)
        gr_ref[...] = lax.dot_general(wgr_ref[...], xb, NT_DIMS, preferred_element_type=F32)

    y_ref[...] = lax.dot_general(xb_ref[...], w_ref[...], NT_DIMS,
                                 preferred_element_type=F32).astype(y_ref.dtype)


def _in_proj(x, w_stack, layer, n_main, *, tm, tn, w_t=None):
    t, d = x.shape
    assert t % tm == 0 and n_main % tn == 0
    if w_t is None:
        w_t = jnp.swapaxes(w_stack, 1, 2).astype(BF16)
    w_gates = w_stack[layer, :, n_main:].astype(BF16)
    w_gate_cols = _pad_cols(w_gates, LANES)
    w_gate_rows = jnp.transpose(w_gates[:, :SUBLANES])
    return pl.pallas_call(
        _in_proj_kernel,
        grid=(t // tm, n_main // tn),
        in_specs=[
            pl.BlockSpec((tm, d), lambda i, j: (i, 0)),
            pl.BlockSpec((None, tn, d), lambda i, j: (layer, j, 0)),
            pl.BlockSpec((d, LANES), lambda i, j: (0, 0)),
            pl.BlockSpec((SUBLANES, d), lambda i, j: (0, 0)),
        ],
        out_specs=[
            pl.BlockSpec((tm, tn), lambda i, j: (i, j)),
            pl.BlockSpec((tm, LANES), lambda i, j: (i, 0)),
            pl.BlockSpec((SUBLANES, tm), lambda i, j: (0, i)),
        ],
        out_shape=[
            jax.ShapeDtypeStruct((t, n_main), BF16),
            jax.ShapeDtypeStruct((t, LANES), F32),
            jax.ShapeDtypeStruct((SUBLANES, t), F32),
        ],
        scratch_shapes=[pltpu.VMEM((tm, d), BF16)],
        compiler_params=_params("parallel", "arbitrary"),
        name="in_proj",
    )(x, w_t, w_gate_cols, w_gate_rows)


def _diff_attn_kernel(q_ref, k_ref, v_ref, tab_ref, lam_ref, g_ref, *refs, t, lam_init, n_jobs):
    cast_in, (o_ref, *cast_out) = refs[:n_jobs], refs[n_jobs:2 * n_jobs + 1]
    (vt_ref, bdiag_ref, bnear_ref, m_ref, acc_ref,
     s0_ref, mx0_ref, sh0_ref, s1_ref, mx1_ref, sh1_ref) = refs[2 * n_jobs + 1:]
    qi = pl.program_id(2)
    dk, dv = A_QK_DIM, A_V_DIM
    nk = k_ref.shape[0] // t
    bufs = ((s0_ref, mx0_ref, sh0_ref), (s1_ref, mx1_ref, sh1_ref))

    @pl.when(qi == 0)
    def _():
        for c in range(nk):
            vt_ref[c, 0:dv, :] = jnp.transpose(v_ref[c * t:(c + 1) * t, :].astype(F32)).astype(BF16)
            vt_ref[c, dv:, :] = jnp.ones((vt_ref.shape[1] - dv, t), BF16)
        skew = pltpu.roll(jnp.broadcast_to(tab_ref[...], (t, 2 * t)), 0, 1, stride=1, stride_axis=0)
        key = lax.broadcasted_iota(jnp.int32, (t, t), 0)
        qry = lax.broadcasted_iota(jnp.int32, (t, t), 1)
        bdiag_ref[...] = jnp.where(qry >= key, skew[:, :t], MASKED_LOGIT)
        bnear_ref[...] = skew[:, t:]

    q_t = jnp.transpose(q_ref[...].astype(F32)) * (dk ** -0.5)
    feat = lax.broadcasted_iota(jnp.int32, q_t.shape, 0)
    qs_t = jnp.concatenate([jnp.where(feat < dk, q_t, 0.0), jnp.where(feat >= dk, q_t, 0.0)],
                           axis=1).astype(BF16)

    m_ref[...] = jnp.full_like(m_ref, -jnp.inf)
    acc_ref[...] = jnp.zeros_like(acc_ref)

    def score(kj, buf, bias_ref, shift):
        s_ref, mx_ref, sh_ref = bufs[buf]
        off = pl.multiple_of(kj * t, t)
        s = jnp.dot(k_ref[pl.ds(off, t), :], qs_t, preferred_element_type=F32)
        if bias_ref is not None:
            b = bias_ref[...]
            s = s + jnp.concatenate([b, b], axis=1)
        s_ref[...] = s
        mx_ref[...] = jnp.max(s, 0, keepdims=True) + shift
        sh_ref[...] = jnp.zeros_like(sh_ref) + shift

    def absorb(kj, buf):
        s_ref, mx_ref, sh_ref = bufs[buf]
        m_old = m_ref[...]
        m_new = jnp.maximum(m_old, mx_ref[...])
        p = jnp.exp(s_ref[...] - (m_new - sh_ref[...])).astype(BF16)
        acc_ref[...] = (jnp.exp(m_old - m_new) * acc_ref[...]
                        + jnp.dot(vt_ref[kj], p, preferred_element_type=F32))
        m_ref[...] = m_new

    c_far = tab_ref[:, 2 * t - 1:2 * t]
    for src_ref, dst_ref in zip(cast_in, cast_out):
        dst_ref[...] = src_ref[...].astype(BF16)
    score(qi, 0, bdiag_ref, 0.0)

    @pl.when(qi >= 1)
    def _():
        score(qi - 1, 1, bnear_ref, 0.0)
        absorb(qi, 0)

    def far_pair(p, carry):
        kj = qi - 2 - 2 * p
        score(kj, 0, None, c_far)
        absorb(kj + 1, 1)
        score(kj - 1, 1, None, c_far)
        absorb(kj, 0)
        return carry

    lax.fori_loop(0, jnp.maximum(qi - 1, 0) // 2, far_pair, 0)

    @pl.when((qi >= 2) & (qi % 2 == 0))
    def _():
        score(0, 0, None, c_far)
        absorb(1, 1)

    @pl.when(qi % 2 == 0)
    def _():
        absorb(0, 0)

    @pl.when(qi % 2 == 1)
    def _():
        absorb(0, 1)

    lv = lam_ref[...]
    lam = (jnp.exp(jnp.sum(lv[0:1] * lv[1:2], -1, keepdims=True))
           - jnp.exp(jnp.sum(lv[2:3] * lv[3:4], -1, keepdims=True)) + lam_init)
    acc = acc_ref[...]
    o_t = acc[0:dv] * (1.0 / acc[dv:dv + 1])
    out = jnp.transpose(o_t[:, :t] - lam * o_t[:, t:])
    hn = out * lax.rsqrt(jnp.mean(out * out, -1, keepdims=True) + NORM_EPS)
    o_ref[...] = (hn * g_ref[...] * (1.0 - lam_init)).astype(o_ref.dtype)


def _t5_bias_by_distance(rel_bias, n):
    r = jnp.arange(n, dtype=jnp.int32)
    max_exact = N_BUCKETS // 2
    rf = jnp.maximum(r, 1).astype(F32)
    large = max_exact + (jnp.log(rf / max_exact) / math.log(MAX_DISTANCE / max_exact)
                         * (N_BUCKETS - max_exact)).astype(jnp.int32)
    large = jnp.minimum(large, N_BUCKETS - 1)
    bucket = jnp.where(r < max_exact, r, large)
    return jnp.transpose(rel_bias[bucket]).astype(F32)


def _diff_attention(y3, rel_bias, lam_vecs, diff_g, jobs, *, lam_init, t):
    bsz, s, _ = y3.shape
    hd = 2 * A_QK_DIM
    dv = A_V_DIM
    assert hd == dv == LANES and s % t == 0 and t >= MAX_DISTANCE
    nq = s // t
    table = _t5_bias_by_distance(rel_bias, 2 * t).reshape(A_HEADS, 1, 2 * t)
    assert all(blocks <= bsz * A_HEADS * nq for _, _, _, blocks in jobs)

    def job_block(blocks):
        return lambda b, h, q: jnp.minimum((b * A_HEADS + h) * nq + q, blocks - 1)

    cast_in = [pl.BlockSpec((rows, w.shape[1]), lambda b, h, q, f=first, blk=job_block(blocks): (f + blk(b, h, q), 0))
               for w, rows, first, blocks in jobs]
    cast_out = [pl.BlockSpec((rows, w.shape[1]), lambda b, h, q, blk=job_block(blocks): (blk(b, h, q), 0))
                for w, rows, first, blocks in jobs]
    cast_shapes = [jax.ShapeDtypeStruct((rows * blocks, w.shape[1]), BF16) for w, rows, first, blocks in jobs]
    return pl.pallas_call(
        functools.partial(_diff_attn_kernel, t=t, lam_init=lam_init, n_jobs=len(jobs)),
        grid=(bsz, A_HEADS, nq),
        in_specs=[
            pl.BlockSpec((None, t, hd), lambda b, h, q: (b, q, h)),
            pl.BlockSpec((None, s, hd), lambda b, h, q: (b, 0, A_HEADS + h)),
            pl.BlockSpec((None, s, dv), lambda b, h, q: (b, 0, 2 * A_HEADS + h)),
            pl.BlockSpec((None, 1, 2 * t), lambda b, h, q: (h, 0, 0)),
            pl.BlockSpec((4, A_QK_DIM), lambda b, h, q: (0, 0)),
            pl.BlockSpec((1, dv), lambda b, h, q: (0, h)),
        ] + cast_in,
        out_specs=[pl.BlockSpec((None, t, dv), lambda b, h, q: (b, q, h))] + cast_out,
        out_shape=[jax.ShapeDtypeStruct((bsz, s, A_HEADS * dv), BF16)] + cast_shapes,
        scratch_shapes=[pltpu.VMEM((nq, dv + ATTN_ONES_ROWS, t), BF16),
                        pltpu.VMEM((t, t), F32),
                        pltpu.VMEM((t, t), F32),
                        pltpu.VMEM((1, 2 * t), F32),
                        pltpu.VMEM((dv + ATTN_ONES_ROWS, 2 * t), F32)]
                       + 2 * [pltpu.VMEM((t, 2 * t), F32), pltpu.VMEM((1, 2 * t), F32),
                              pltpu.VMEM((1, 2 * t), F32)],
        compiler_params=_params("arbitrary", "arbitrary", "arbitrary"),
        name="diff_attn",
    )(y3, y3, y3, table, lam_vecs, diff_g, *[w for w, _, _, _ in jobs])


def _mlstm_kernel(qk_ref, v_ref, ob_ref, gc_ref, gr_ref, cw_ref, cb_ref, bc_ref, br_ref, g_ref, y_ref,
                  xp_ref, *state_refs, chunk):
    heads, dk, dv = B_HEADS, B_QK_DIM, B_V_DIM
    c_refs, n_refs, m_refs = state_refs[0::3], state_refs[1::3], state_refs[2::3]
    pad = xp_ref.shape[0] - chunk
    c = pl.program_id(1)

    @pl.when(c == 0)
    def _():
        xp_ref[0:pad, :] = jnp.zeros((pad, xp_ref.shape[1]), xp_ref.dtype)
        for ref in state_refs:
            ref[...] = jnp.zeros_like(ref)

    x = qk_ref[...]
    xp_ref[pad:pad + chunk, :] = x
    xp = xp_ref[...]
    cw = cw_ref[...]
    dst = lax.broadcasted_iota(jnp.int32, (chunk, pad + chunk), 0)
    src = lax.broadcasted_iota(jnp.int32, (chunk, pad + chunk), 1)
    conv = cb_ref[...] + cw[CONV_WIDTH - 1:CONV_WIDTH, :] * x.astype(F32)
    for j in range(CONV_WIDTH - 1):
        shift = (src == dst + (pad - (CONV_WIDTH - 1) + j)).astype(BF16)
        conv = conv + cw[j:j + 1, :] * jnp.dot(shift, xp, preferred_element_type=F32)
    xp_ref[0:pad, :] = x[chunk - pad:chunk, :]
    qk = conv * _sigmoid(conv)

    gc = gc_ref[...] + bc_ref[...]
    gr = gr_ref[...] + br_ref[...]
    row = lax.broadcasted_iota(jnp.int32, (chunk, chunk), 0)
    col = lax.broadcasted_iota(jnp.int32, (chunk, chunk), 1)
    causal = row >= col
    tril = causal.astype(F32)
    triu = (row <= col).astype(F32)
    b_c = jnp.dot(tril, _log_sigmoid(gc), precision=HIGHEST, preferred_element_type=F32)
    b_r = jnp.dot(_log_sigmoid(gr), triu, precision=HIGHEST, preferred_element_type=F32)

    ob = ob_ref[...].astype(F32)
    g = g_ref[...]
    hs = range(heads)
    q = [qk[:, h * dk:(h + 1) * dk] * (dk ** -0.5) for h in hs]
    k = [qk[:, (heads + h) * dk:(heads + h + 1) * dk] for h in hs]
    v = [v_ref[:, h * dv:(h + 1) * dv] for h in hs]
    b_col = [b_c[:, heads + h:heads + h + 1] for h in hs]
    i_col = [gc[:, h:h + 1] for h in hs]
    b_row = [b_r[heads + h:heads + h + 1, :] for h in hs]
    i_row = [gr[h:h + 1, :] for h in hs]
    m_prev = [m_refs[h][0:1, 0:1] for h in hs]
    c_mat = [c_refs[h][...] for h in hs]
    n_vec = [n_refs[h][...] for h in hs]

    b_last = [b_row[h][:, chunk - 1:chunk] for h in hs]
    g_col = [b_last[h] - b_col[h] + i_col[h] for h in hs]
    m_new = [jnp.maximum(b_last[h] + m_prev[h], jnp.max(g_col[h], 0, keepdims=True)) for h in hs]
    decay = [jnp.exp(b_last[h] + m_prev[h] - m_new[h]) for h in hs]
    kw = [k[h] * jnp.exp(g_col[h] - m_new[h]) for h in hs]
    qb = [q[h].astype(BF16) for h in hs]
    qk_t = [lax.dot_general(qb[h], k[h].astype(BF16), NT_DIMS, preferred_element_type=F32) for h in hs]
    q_c = [jnp.dot(qb[h], c_mat[h].astype(BF16), preferred_element_type=F32) for h in hs]
    kw_v = [lax.dot_general(kw[h].astype(BF16), v[h], TN_DIMS, preferred_element_type=F32) for h in hs]

    d = [jnp.where(causal, b_col[h] - b_row[h] + i_row[h], -jnp.inf) for h in hs]
    a_col = [b_col[h] + m_prev[h] for h in hs]
    m_t = [jnp.maximum(a_col[h], jnp.max(d[h], -1, keepdims=True)) for h in hs]
    w_inter = [jnp.exp(a_col[h] - m_t[h]) for h in hs]
    sw = [qk_t[h] * jnp.exp(d[h] - m_t[h]) for h in hs]
    sw_v = [jnp.dot(sw[h].astype(BF16), v[h], preferred_element_type=F32) for h in hs]

    for h in hs:
        c_refs[h][...] = decay[h] * c_mat[h] + kw_v[h]
        n_refs[h][...] = decay[h] * n_vec[h] + jnp.sum(kw[h], 0, keepdims=True)
        m_refs[h][...] = jnp.broadcast_to(m_new[h], m_refs[h].shape)

    for h in hs:
        num = w_inter[h] * q_c[h] + sw_v[h]
        den = (w_inter[h] * jnp.sum(q[h] * n_vec[h], -1, keepdims=True)
               + jnp.sum(sw[h], -1, keepdims=True))
        hh = num / jnp.maximum(jnp.abs(den), jnp.exp(-m_t[h]))
        hn = hh * lax.rsqrt(jnp.mean(hh * hh, -1, keepdims=True) + NORM_EPS)
        sl = slice(h * dv, (h + 1) * dv)
        y_ref[:, sl] = (hn * g[:, sl] * _sigmoid(ob[:, sl])).astype(y_ref.dtype)


def _mlstm(y3, gcol3, grow3, conv_w, conv_b, gate_bias_cols, gate_bias_rows, mlstm_g, *, chunk):
    bsz, s, _ = y3.shape
    heads, dk, dv = B_HEADS, B_QK_DIM, B_V_DIM
    w = heads * dv
    assert 2 * heads * dk == w and s % chunk == 0 and chunk % LANES == 0
    nc = s // chunk
    return pl.pallas_call(
        functools.partial(_mlstm_kernel, chunk=chunk),
        grid=(bsz, nc),
        in_specs=[
            pl.BlockSpec((None, chunk, w), lambda b, c: (b, c, 3)),
            pl.BlockSpec((None, chunk, w), lambda b, c: (b, c, 4)),
            pl.BlockSpec((None, chunk, w), lambda b, c: (b, c, 5)),
            pl.BlockSpec((None, chunk, LANES), lambda b, c: (b, c, 0)),
            pl.BlockSpec((SUBLANES, chunk), lambda b, c: (0, b * nc + c)),
            pl.BlockSpec((CONV_WIDTH, w), lambda b, c: (0, 0)),
            pl.BlockSpec((1, w), lambda b, c: (0, 0)),
            pl.BlockSpec((1, LANES), lambda b, c: (0, 0)),
            pl.BlockSpec((SUBLANES, 1), lambda b, c: (0, 0)),
            pl.BlockSpec((1, w), lambda b, c: (0, 0)),
        ],
        out_specs=pl.BlockSpec((None, chunk, w), lambda b, c: (b, c, 0)),
        out_shape=jax.ShapeDtypeStruct((bsz, s, w), BF16),
        scratch_shapes=[
            pltpu.VMEM((BF16_SUBLANES + chunk, w), BF16),
        ] + heads * [pltpu.VMEM((dk, dv), F32), pltpu.VMEM((1, dk), F32), pltpu.VMEM((SUBLANES, LANES), F32)],
        compiler_params=_params("parallel", "arbitrary"),
        name="mlstm",
    )(y3, y3, y3, gcol3, grow3, conv_w, conv_b, gate_bias_cols, gate_bias_rows, mlstm_g)


def _gla_kernel(q_ref, k_ref, v_ref, r_ref, a1_ref, a1n_ref, wah_ref, wal_ref, ba_ref, g_ref, y_ref,
                b_ref, *st_refs, chunk):
    heads, dk, dv = C_HEADS, C_QK_DIM, C_V_DIM
    sub = GLA_SUB
    assert chunk == 2 * sub
    c = pl.program_id(1)
    row = lax.broadcasted_iota(jnp.int32, (chunk, chunk), 0)
    col = lax.broadcasted_iota(jnp.int32, (chunk, chunk), 1)
    causal = (row >= col) & ((row < sub) == (col < sub))
    first = lax.broadcasted_iota(jnp.int32, (chunk, 1), 0) < sub

    def cum_log_decay(a1):
        a_hi = a1.astype(BF16)
        a_lo = (a1 - a_hi.astype(F32)).astype(BF16)
        w_hi = wah_ref[...]
        z = (jnp.dot(a_hi, w_hi, preferred_element_type=F32)
             + (jnp.dot(a_lo, w_hi, preferred_element_type=F32)
                + jnp.dot(a_hi, wal_ref[...], preferred_element_type=F32))) + ba_ref[...]
        log_a = _log_sigmoid(z) * (1.0 / C_GATE_TEMP)
        return jnp.dot(causal.astype(F32), log_a, precision=HIGHEST, preferred_element_type=F32)

    @pl.when(c == 0)
    def _():
        for ref in st_refs:
            ref[...] = jnp.zeros_like(ref)
        b_ref[...] = cum_log_decay(a1_ref[0:chunk, :])

    g = g_ref[...]
    n_inner = q_ref.shape[0] // chunk
    for i in range(n_inner):
        rows = slice(i * chunk, (i + 1) * chunk)
        b_all = b_ref[...]
        b_next = cum_log_decay(a1_ref[(i + 1) * chunk:(i + 2) * chunk, :] if i + 1 < n_inner else a1n_ref[...])

        r = r_ref[rows, :].astype(F32)
        for h in range(heads):
            ks = slice(h * dk, (h + 1) * dk)
            vs = slice(h * dv, (h + 1) * dv)
            b = b_all[:, ks]
            q = q_ref[rows, ks].astype(F32) * (dk ** -0.5)
            k = k_ref[rows, ks].astype(F32)
            v = v_ref[rows, vs]
            tot0 = b[sub - 1:sub, :]
            tot1 = b[chunk - 1:chunk, :]
            q_dec = q * jnp.exp(b)
            k_dec = (k * jnp.exp(-b)).astype(BF16)
            k_end = k * jnp.exp(jnp.where(first, tot0, tot1) - b)
            k_new = (k_end * jnp.where(first, jnp.exp(tot1), 1.0)).astype(BF16)
            q_int = (q_dec * jnp.where(first, 1.0, jnp.exp(tot0))).astype(BF16)
            q_dec = q_dec.astype(BF16)
            st = st_refs[h][...]
            qk_t = lax.dot_general(q_dec, k_dec, NT_DIMS, preferred_element_type=F32)
            inter = lax.dot_general(q_int, st.astype(BF16), NT_DIMS, preferred_element_type=F32)
            st_refs[h][...] = (jnp.exp(tot0 + tot1) * st
                               + lax.dot_general(v, k_new, TN_DIMS, preferred_element_type=F32))
            cross = lax.dot_general(q_dec[sub:], k_end[:sub].astype(BF16), NT_DIMS, preferred_element_type=F32)
            att = jnp.where(causal, qk_t, 0.0)
            o = jnp.dot(att.astype(BF16), v, preferred_element_type=F32) + inter
            o_cross = jnp.dot(cross.astype(BF16), v[:sub], preferred_element_type=F32)
            o = jnp.concatenate([o[:sub], o[sub:] + o_cross], axis=0)

            on = o * lax.rsqrt(jnp.mean(o * o, -1, keepdims=True) + NORM_EPS)
            rr = r[:, vs]
            y_ref[rows, vs] = (on * g[:, vs] * (rr * _sigmoid(rr))).astype(y_ref.dtype)

        b_ref[...] = b_next


def _gla(y3, a13, w_a2_hi, w_a2_lo, b_a, gla_g, *, chunk, block):
    bsz, s, _ = y3.shape
    heads, dk, dv = C_HEADS, C_QK_DIM, C_V_DIM
    wk, wv = heads * dk, heads * dv
    assert wv == 2 * wk and s % block == 0 and block % chunk == 0
    nb, per_block = s // block, block // chunk
    return pl.pallas_call(
        functools.partial(_gla_kernel, chunk=chunk),
        grid=(bsz, nb),
        in_specs=[
            pl.BlockSpec((None, block, wk), lambda b, c: (b, c, 0)),
            pl.BlockSpec((None, block, wk), lambda b, c: (b, c, 1)),
            pl.BlockSpec((None, block, wv), lambda b, c: (b, c, 1)),
            pl.BlockSpec((None, block, wv), lambda b, c: (b, c, 2)),
            pl.BlockSpec((None, block, LANES), lambda b, c: (b, c, 0)),
            pl.BlockSpec((None, chunk, LANES), lambda b, c: (b, jnp.minimum(c + 1, nb - 1) * per_block, 0)),
            pl.BlockSpec((LANES, wk), lambda b, c: (0, 0)),
            pl.BlockSpec((LANES, wk), lambda b, c: (0, 0)),
            pl.BlockSpec((1, wk), lambda b, c: (0, 0)),
            pl.BlockSpec((1, wv), lambda b, c: (0, 0)),
        ],
        out_specs=pl.BlockSpec((None, block, wv), lambda b, c: (b, c, 0)),
        out_shape=jax.ShapeDtypeStruct((bsz, s, wv), BF16),
        scratch_shapes=[pltpu.VMEM((chunk, wk), F32)] + heads * [pltpu.VMEM((dv, dk), F32)],
        compiler_params=_params("parallel", "arbitrary"),
        name="gla",
    )(y3, y3, y3, y3, a13, a13, w_a2_hi, w_a2_lo, b_a, gla_g)


def _out_ln_kernel(*refs, n_in, alpha):
    y_refs, w_refs = refs[:n_in], refs[n_in:2 * n_in]
    x_ref, g_ref, b_ref, o_ref = refs[2 * n_in:]
    rb = o_ref.shape[0] // ROW_SPLIT
    for r in range(ROW_SPLIT):
        rows = slice(r * rb, (r + 1) * rb)
        acc = jnp.dot(y_refs[0][rows, :], w_refs[0][...], preferred_element_type=F32)
        for y_ref, w_ref in zip(y_refs[1:], w_refs[1:]):
            acc = acc + jnp.dot(y_ref[rows, :], w_ref[...], preferred_element_type=F32)
        o_ref[rows, :] = _layer_norm(alpha * x_ref[rows, :] + acc, g_ref[...], b_ref[...])


def _out_ln(ys, w, x, ln_g, ln_b, layer, w_layer, ln_slot, *, alpha, tm):
    t, d = x.shape
    n_in = len(ys)
    wd = ys[0].shape[1]
    assert all(y.shape[1] == wd for y in ys) and w.shape[1:] == (n_in * wd, d)
    ln_spec = pl.BlockSpec((None, None, 1, d), lambda i: (layer, ln_slot, 0, 0))
    in_specs = ([pl.BlockSpec((tm, wd), lambda i: (i, 0)) for _ in ys]
                + [pl.BlockSpec((None, wd, d), lambda i, k=k: (w_layer, k, 0)) for k in range(n_in)]
                + [pl.BlockSpec((tm, d), lambda i: (i, 0)), ln_spec, ln_spec])
    return pl.pallas_call(
        functools.partial(_out_ln_kernel, n_in=n_in, alpha=alpha),
        grid=(t // tm,),
        in_specs=in_specs,
        out_specs=pl.BlockSpec((tm, d), lambda i: (i, 0)),
        out_shape=jax.ShapeDtypeStruct((t, d), F32),
        compiler_params=_params("parallel"),
        name="out_ln",
    )(*ys, *([w] * n_in), x, ln_g, ln_b)


def _ple_kernel(x_ref, p_ref, wp_ref, wg_ref, o_ref):
    rb = o_ref.shape[0] // ROW_SPLIT
    for r in range(ROW_SPLIT):
        rows = slice(r * rb, (r + 1) * rb)
        x = x_ref[rows, :]
        e = jnp.dot(p_ref[rows, :].astype(BF16), wp_ref[...], preferred_element_type=F32)
        gate = jnp.dot(x.astype(BF16), wg_ref[...], preferred_element_type=F32)
        o_ref[rows, :] = x + e * _sigmoid(gate)


def _ple(x, p, w_proj, w_gate, layer, *, tm):
    t, d = x.shape
    s, pd = p.shape[2:]
    assert s % tm == 0
    per_seq = s // tm
    return pl.pallas_call(
        _ple_kernel,
        grid=(t // tm,),
        in_specs=[
            pl.BlockSpec((tm, d), lambda i: (i, 0)),
            pl.BlockSpec((None, None, tm, pd), lambda i: (layer, i // per_seq, i % per_seq, 0)),
            pl.BlockSpec((None, pd, d), lambda i: (layer, 0, 0)),
            pl.BlockSpec((None, d, d), lambda i: (layer, 0, 0)),
        ],
        out_specs=pl.BlockSpec((tm, d), lambda i: (i, 0)),
        out_shape=jax.ShapeDtypeStruct((t, d), F32),
        compiler_params=_params("parallel"),
        name="ple",
    )(x, p, w_proj, w_gate)


def _tiles(t, s):
    return dict(
        ffn_tm=min(1024, t), ffn_tf=256,
        proj_tm=min(1024, t), proj_tn=2048,
        row_tm=min(512, t),
        attn_t=min(512, s),
        mlstm_chunk=min(256, s),
        gla_chunk=min(2 * GLA_SUB, s), gla_block=min(8 * GLA_SUB, s),
    )


def _cast_job(w2d, unit, skip, count, steps):
    rows = next(r for r in range(BF16_SUBLANES, unit + 1, BF16_SUBLANES)
                if unit % r == 0 and count * (unit // r) <= steps)
    return w2d, rows, skip * (unit // rows), count * (unit // rows)


def _pad_cols(w, n):
    return jnp.pad(w, ((0, 0), (0, n - w.shape[1])))


def kernel(x, p, ln_g, ln_b, w_ffn_in, w_ffn_out, w_in_ab, w_out_ab, rel_bias, lambda_q1, lambda_k1,
           lambda_q2, lambda_k2, diff_norm, conv_w, conv_b, b_igate, b_fgate, mlstm_norm, w_in_c,
           w_alpha2, b_alpha, gla_norm, w_out_c, w_ple_proj, w_ple_gate):
    bsz, s, d = x.shape
    depth = p.shape[0]
    t = bsz * s
    tl = _tiles(t, s)
    alpha = (2 * depth) ** 0.25
    a_w = A_HEADS * A_V_DIM
    ab_main = 3 * a_w + 3 * B_HEADS * B_V_DIM
    c_main = 2 * C_HEADS * C_QK_DIM + 2 * C_HEADS * C_V_DIM
    row = lambda a: a.reshape(1, -1)

    n_ffn, f_ff = 2 * depth, w_ffn_out.shape[2]
    attn_steps = bsz * A_HEADS * (s // tl["attn_t"])
    wfi_first, wfo_first = w_ffn_in[0, :1].astype(BF16), w_ffn_out[0, :1].astype(BF16)
    ffn_w = {0: (wfi_first, wfo_first, 0)}
    cast_jobs = [
        _cast_job(w_ffn_in.reshape(n_ffn * d, -1), d, 1, n_ffn - 1, attn_steps),
        _cast_job(w_ffn_out.reshape(n_ffn * f_ff, d), f_ff, 1, n_ffn - 1, attn_steps),
        _cast_job(jnp.swapaxes(w_in_c, 1, 2).reshape(-1, d), w_in_c.shape[2], 0, w_in_c.shape[0], attn_steps),
    ]
    wo_ab, wo_c = w_out_ab.astype(BF16), w_out_c.astype(BF16)
    wpp, wpg = w_ple_proj.astype(BF16), w_ple_gate.astype(BF16)
    g4, b4 = ln_g.reshape(depth, 3, 1, d), ln_b.reshape(depth, 3, 1, d)

    xf = x.reshape(t, d)
    for i in range(depth):
        xf = _ffn_ln(xf, *ffn_w[2 * i], g4, b4, i, 0, alpha=alpha, tm=tl["ffn_tm"], tf=tl["ffn_tf"])
        if i % 2 == 0:
            e = i // 2
            lam_init = 0.8 - 0.6 * math.exp(-0.3 * i)
            y, gcol, grow = _in_proj(xf, w_in_ab, e, ab_main, tm=tl["proj_tm"], tn=tl["proj_tn"])
            y3 = y.reshape(bsz, s, ab_main)
            lam_vecs = jnp.stack([lambda_q1[e], lambda_k1[e], lambda_q2[e], lambda_k2[e]])
            ya, *cast = _diff_attention(y3, rel_bias, lam_vecs, row(diff_norm[e]), cast_jobs,
                                        lam_init=lam_init, t=tl["attn_t"])
            if i == 0:
                wfi_rest, wfo_rest = cast[0].reshape(n_ffn - 1, d, -1), cast[1].reshape(n_ffn - 1, f_ff, d)
                ffn_w.update({n: (wfi_rest, wfo_rest, n - 1) for n in range(1, n_ffn)})
                w_c_t = cast[2].reshape(w_in_c.shape[0], w_in_c.shape[2], d)
            gate_bias = jnp.concatenate([b_igate[e], b_fgate[e]])
            yb = _mlstm(y3, gcol.reshape(bsz, s, LANES), grow,
                        conv_w[e], row(conv_b[e]), _pad_cols(row(gate_bias), LANES), gate_bias.reshape(-1, 1),
                        row(mlstm_norm[e]), chunk=tl["mlstm_chunk"])
            ys, w_out, w_layer = [ya.reshape(t, a_w), yb.reshape(t, -1)], wo_ab, e
        else:
            o = i // 2
            y, a1, _ = _in_proj(xf, w_in_c, o, c_main, tm=tl["proj_tm"], tn=tl["proj_tn"], w_t=w_c_t)
            w_a2 = jnp.pad(w_alpha2[o], ((0, LANES - w_alpha2.shape[1]), (0, 0)))
            w_a2_hi = lax.reduce_precision(w_a2, exponent_bits=8, mantissa_bits=7)
            w_a2_lo = (w_a2 - w_a2_hi).astype(BF16)
            yc = _gla(y.reshape(bsz, s, c_main), a1.reshape(bsz, s, LANES), w_a2_hi.astype(BF16), w_a2_lo,
                      row(b_alpha[o]),
                      row(gla_norm[o]), chunk=tl["gla_chunk"], block=tl["gla_block"])
            ys, w_out, w_layer = [yc.reshape(t, -1)], wo_c, o
        xf = _out_ln(ys, w_out, xf, g4, b4, i, w_layer, 1, alpha=alpha, tm=tl["row_tm"])
        xf = _ffn_ln(xf, *ffn_w[2 * i + 1], g4, b4, i, 2, alpha=alpha, tm=tl["ffn_tm"], tf=tl["ffn_tf"])
        xf = _ple(xf, p, wpp, wpg, i, tm=tl["row_tm"])
    return xf.reshape(bsz, s, d)
```

```python
import functools
import math

import jax
import jax.numpy as jnp
from jax import lax
from jax.experimental import pallas as pl
from jax.experimental.pallas import tpu as pltpu

F32 = jnp.float32
BF16 = jnp.bfloat16
HIGHEST = lax.Precision.HIGHEST

A_HEADS = 8
A_QK_DIM = 64
A_V_DIM = 128
B_HEADS = 4
B_QK_DIM = 128
B_V_DIM = 256
CONV_WIDTH = 4
C_HEADS = 4
C_QK_DIM = 256
C_V_DIM = 512
C_GATE_TEMP = 16.0
GLA_SUB = 64
N_BUCKETS = 32
MAX_DISTANCE = 128
LN_EPS = 1e-5
NORM_EPS = 1e-6

V7X_VMEM_LIMIT_BYTES = 56 * 1024 * 1024
LANES = 128
SUBLANES = 8
BF16_SUBLANES = 16
ATTN_ONES_ROWS = BF16_SUBLANES
MASKED_LOGIT = -1e30
ROW_SPLIT = 4

NT_DIMS = (((1,), (1,)), ((), ()))
TN_DIMS = (((0,), (0,)), ((), ()))


def _params(*sem):
    return pltpu.CompilerParams(dimension_semantics=sem, vmem_limit_bytes=V7X_VMEM_LIMIT_BYTES)


def _layer_norm(y, g, b, eps=LN_EPS):
    mu = jnp.mean(y, -1, keepdims=True)
    yc = y - mu
    var = jnp.mean(yc * yc, -1, keepdims=True)
    return yc * lax.rsqrt(var + eps) * g + b


def _sigmoid(x):
    return 1.0 / (1.0 + jnp.exp(-x))


def _log_sigmoid(x):
    return jnp.minimum(x, 0.0) - jnp.log(1.0 + jnp.exp(-jnp.abs(x)))


def _ffn_ln_kernel(x_ref, wg0_ref, wu0_ref, wo0_ref, wg1_ref, wu1_ref, wo1_ref, g_ref, b_ref, o_ref, xb_ref,
                   *, alpha, n_chunks):
    j = pl.program_id(1)
    last = pl.num_programs(1) - 1

    @pl.when(j == 0)
    def _():
        x = x_ref[...]
        xb_ref[...] = x.astype(BF16)
        o_ref[...] = (2.0 * alpha) * x

    def chunk(wg_ref, wu_ref, wo_ref):
        xb = xb_ref[...]
        gate = jnp.dot(xb, wg_ref[...], preferred_element_type=F32)
        up = jnp.dot(xb, wu_ref[...], preferred_element_type=F32)
        h = (gate * _sigmoid(gate) * up).astype(BF16)
        o_ref[...] += jnp.dot(h, wo_ref[...], preferred_element_type=F32)

    if n_chunks % 2 == 0:
        chunk(wg0_ref, wu0_ref, wo0_ref)
        chunk(wg1_ref, wu1_ref, wo1_ref)
    else:
        @pl.when(j < last)
        def _():
            chunk(wg0_ref, wu0_ref, wo0_ref)
            chunk(wg1_ref, wu1_ref, wo1_ref)

        @pl.when(j == last)
        def _():
            chunk(wg0_ref, wu0_ref, wo0_ref)

    @pl.when(j == last)
    def _():
        o_ref[...] = _layer_norm(o_ref[...], g_ref[...], b_ref[...], eps=4.0 * LN_EPS)


def _ffn_ln(x, w_in, w_out, w_idx, ln_g, ln_b, layer, ln_slot, *, alpha, tm, tf):
    t, d = x.shape
    f = w_out.shape[1]
    nf = f // tf
    assert t % tm == 0 and f % tf == 0 and w_in.shape[1:] == (d, 2 * f) and nf >= 2
    steps = (nf + 1) // 2
    first = lambda j: 2 * j
    second = lambda j: jnp.where(2 * j + 1 < nf, 2 * j + 1, nf - 2)
    ln_spec = pl.BlockSpec((None, None, 1, d), lambda i, j: (layer, ln_slot, 0, 0))

    def weight_specs(chunk_of):
        return [pl.BlockSpec((None, d, tf), lambda i, j: (w_idx, 0, chunk_of(j))),
                pl.BlockSpec((None, d, tf), lambda i, j: (w_idx, 0, nf + chunk_of(j))),
                pl.BlockSpec((None, tf, d), lambda i, j: (w_idx, chunk_of(j), 0))]

    return pl.pallas_call(
        functools.partial(_ffn_ln_kernel, alpha=alpha, n_chunks=nf),
        grid=(t // tm, steps),
        in_specs=[pl.BlockSpec((tm, d), lambda i, j: (i, 0))] + weight_specs(first) + weight_specs(second)
                 + [ln_spec, ln_spec],
        out_specs=pl.BlockSpec((tm, d), lambda i, j: (i, 0)),
        out_shape=jax.ShapeDtypeStruct((t, d), F32),
        scratch_shapes=[pltpu.VMEM((tm, d), BF16)],
        compiler_params=_params("parallel", "arbitrary"),
        name="ffn_ln",
    )(x, w_in, w_in, w_out, w_in, w_in, w_out, ln_g, ln_b)


def _in_proj_kernel(x_ref, w_ref, wgc_ref, wgr_ref, y_ref, gc_ref, gr_ref, xb_ref):
    j = pl.program_id(1)

    @pl.when(j == 0)
    def _():
        xb = x_ref[...].astype(BF16)
        xb_ref[...] = xb
        gc_ref[...] = jnp.dot(xb, wgc_ref[...], preferred_element_type=F32)
        gr_ref[...] = lax.dot_general(wgr_ref[...], xb, NT_DIMS, preferred_element_type=F32)

    y_ref[...] = lax.dot_general(xb_ref[...], w_ref[...], NT_DIMS,
                                 preferred_element_type=F32).astype(y_ref.dtype)


def _in_proj(x, w_stack, layer, n_main, *, tm, tn, w_t=None):
    t, d = x.shape
    assert t % tm == 0 and n_main % tn == 0
    if w_t is None:
        w_t = jnp.swapaxes(w_stack, 1, 2).astype(BF16)
    w_gates_t = w_t[layer, n_main:, :]
    w_gate_cols = _pad_cols(jnp.transpose(w_gates_t), LANES)
    w_gate_rows = w_gates_t[:SUBLANES]
    return pl.pallas_call(
        _in_proj_kernel,
        grid=(t // tm, n_main // tn),
        in_specs=[
            pl.BlockSpec((tm, d), lambda i, j: (i, 0)),
            pl.BlockSpec((None, tn, d), lambda i, j: (layer, j, 0)),
            pl.BlockSpec((d, LANES), lambda i, j: (0, 0)),
            pl.BlockSpec((SUBLANES, d), lambda i, j: (0, 0)),
        ],
        out_specs=[
            pl.BlockSpec((tm, tn), lambda i, j: (i, j)),
            pl.BlockSpec((tm, LANES), lambda i, j: (i, 0)),
            pl.BlockSpec((SUBLANES, tm), lambda i, j: (0, i)),
        ],
        out_shape=[
            jax.ShapeDtypeStruct((t, n_main), BF16),
            jax.ShapeDtypeStruct((t, LANES), F32),
            jax.ShapeDtypeStruct((SUBLANES, t), F32),
        ],
        scratch_shapes=[pltpu.VMEM((tm, d), BF16)],
        compiler_params=_params("parallel", "arbitrary"),
        name="in_proj",
    )(x, w_t, w_gate_cols, w_gate_rows)


def _diff_attn_kernel(q_ref, k_ref, v_ref, tab_ref, lam_ref, g_ref, *refs, t, lam_init, n_jobs):
    cast_in, (o_ref, *cast_out) = refs[:n_jobs], refs[n_jobs:2 * n_jobs + 1]
    (vt_ref, bdiag_ref, bnear_ref, m_ref, acc_ref,
     s0_ref, mx0_ref, sh0_ref, s1_ref, mx1_ref, sh1_ref) = refs[2 * n_jobs + 1:]
    qi = pl.program_id(2)
    dk, dv = A_QK_DIM, A_V_DIM
    nk = k_ref.shape[0] // t
    bufs = ((s0_ref, mx0_ref, sh0_ref), (s1_ref, mx1_ref, sh1_ref))

    @pl.when(qi == 0)
    def _():
        for c in range(nk):
            vt_ref[c, 0:dv, :] = jnp.transpose(v_ref[c * t:(c + 1) * t, :].astype(F32)).astype(BF16)
            vt_ref[c, dv:, :] = jnp.ones((vt_ref.shape[1] - dv, t), BF16)
        skew = pltpu.roll(jnp.broadcast_to(tab_ref[...], (t, 2 * t)), 0, 1, stride=1, stride_axis=0)
        key = lax.broadcasted_iota(jnp.int32, (t, t), 0)
        qry = lax.broadcasted_iota(jnp.int32, (t, t), 1)
        bdiag_ref[...] = jnp.where(qry >= key, skew[:, :t], MASKED_LOGIT)
        bnear_ref[...] = skew[:, t:]

    q_t = jnp.transpose(q_ref[...].astype(F32)) * (dk ** -0.5)
    feat = lax.broadcasted_iota(jnp.int32, q_t.shape, 0)
    qs_t = jnp.concatenate([jnp.where(feat < dk, q_t, 0.0), jnp.where(feat >= dk, q_t, 0.0)],
                           axis=1).astype(BF16)

    m_ref[...] = jnp.full_like(m_ref, -jnp.inf)
    acc_ref[...] = jnp.zeros_like(acc_ref)

    def score(kj, buf, bias_ref, shift):
        s_ref, mx_ref, sh_ref = bufs[buf]
        off = pl.multiple_of(kj * t, t)
        s = jnp.dot(k_ref[pl.ds(off, t), :], qs_t, preferred_element_type=F32)
        if bias_ref is not None:
            b = bias_ref[...]
            s = s + jnp.concatenate([b, b], axis=1)
        s_ref[...] = s
        mx_ref[...] = jnp.max(s, 0, keepdims=True) + shift
        sh_ref[...] = jnp.zeros_like(sh_ref) + shift

    def absorb(kj, buf):
        s_ref, mx_ref, sh_ref = bufs[buf]
        m_old = m_ref[...]
        m_new = jnp.maximum(m_old, mx_ref[...])
        p = jnp.exp(s_ref[...] - (m_new - sh_ref[...])).astype(BF16)
        acc_ref[...] = (jnp.exp(m_old - m_new) * acc_ref[...]
                        + jnp.dot(vt_ref[kj], p, preferred_element_type=F32))
        m_ref[...] = m_new

    c_far = tab_ref[:, 2 * t - 1:2 * t]
    for src_ref, dst_ref in zip(cast_in, cast_out):
        dst_ref[...] = src_ref[...].astype(BF16)
    score(qi, 0, bdiag_ref, 0.0)

    @pl.when(qi >= 1)
    def _():
        score(qi - 1, 1, bnear_ref, 0.0)
        absorb(qi, 0)

    def far_pair(p, carry):
        kj = qi - 2 - 2 * p
        score(kj, 0, None, c_far)
        absorb(kj + 1, 1)
        score(kj - 1, 1, None, c_far)
        absorb(kj, 0)
        return carry

    lax.fori_loop(0, jnp.maximum(qi - 1, 0) // 2, far_pair, 0)

    @pl.when((qi >= 2) & (qi % 2 == 0))
    def _():
        score(0, 0, None, c_far)
        absorb(1, 1)

    @pl.when(qi % 2 == 0)
    def _():
        absorb(0, 0)

    @pl.when(qi % 2 == 1)
    def _():
        absorb(0, 1)

    lv = lam_ref[...]
    lam = (jnp.exp(jnp.sum(lv[0:1] * lv[1:2], -1, keepdims=True))
           - jnp.exp(jnp.sum(lv[2:3] * lv[3:4], -1, keepdims=True)) + lam_init)
    acc = acc_ref[...]
    o_t = acc[0:dv] * (1.0 / acc[dv:dv + 1])
    out = jnp.transpose(o_t[:, :t] - lam * o_t[:, t:])
    hn = out * lax.rsqrt(jnp.mean(out * out, -1, keepdims=True) + NORM_EPS)
    o_ref[...] = (hn * g_ref[...] * (1.0 - lam_init)).astype(o_ref.dtype)


def _t5_bias_by_distance(rel_bias, n):
    r = jnp.arange(n, dtype=jnp.int32)
    max_exact = N_BUCKETS // 2
    rf = jnp.maximum(r, 1).astype(F32)
    large = max_exact + (jnp.log(rf / max_exact) / math.log(MAX_DISTANCE / max_exact)
                         * (N_BUCKETS - max_exact)).astype(jnp.int32)
    large = jnp.minimum(large, N_BUCKETS - 1)
    bucket = jnp.where(r < max_exact, r, large)
    return jnp.transpose(rel_bias[bucket]).astype(F32)


def _diff_attention(y3, rel_bias, lam_vecs, diff_g, jobs, *, lam_init, t):
    bsz, s, _ = y3.shape
    hd = 2 * A_QK_DIM
    dv = A_V_DIM
    assert hd == dv == LANES and s % t == 0 and t >= MAX_DISTANCE
    nq = s // t
    table = _t5_bias_by_distance(rel_bias, 2 * t).reshape(A_HEADS, 1, 2 * t)
    assert all(blocks <= bsz * A_HEADS * nq for _, _, _, blocks in jobs)

    def job_block(blocks):
        return lambda b, h, q: jnp.minimum((b * A_HEADS + h) * nq + q, blocks - 1)

    cast_in = [pl.BlockSpec((rows, w.shape[1]), lambda b, h, q, f=first, blk=job_block(blocks): (f + blk(b, h, q), 0))
               for w, rows, first, blocks in jobs]
    cast_out = [pl.BlockSpec((rows, w.shape[1]), lambda b, h, q, blk=job_block(blocks): (blk(b, h, q), 0))
                for w, rows, first, blocks in jobs]
    cast_shapes = [jax.ShapeDtypeStruct((rows * blocks, w.shape[1]), BF16) for w, rows, first, blocks in jobs]
    return pl.pallas_call(
        functools.partial(_diff_attn_kernel, t=t, lam_init=lam_init, n_jobs=len(jobs)),
        grid=(bsz, A_HEADS, nq),
        in_specs=[
            pl.BlockSpec((None, t, hd), lambda b, h, q: (b, q, h)),
            pl.BlockSpec((None, s, hd), lambda b, h, q: (b, 0, A_HEADS + h)),
            pl.BlockSpec((None, s, dv), lambda b, h, q: (b, 0, 2 * A_HEADS + h)),
            pl.BlockSpec((None, 1, 2 * t), lambda b, h, q: (h, 0, 0)),
            pl.BlockSpec((4, A_QK_DIM), lambda b, h, q: (0, 0)),
            pl.BlockSpec((1, dv), lambda b, h, q: (0, h)),
        ] + cast_in,
        out_specs=[pl.BlockSpec((None, t, dv), lambda b, h, q: (b, q, h))] + cast_out,
        out_shape=[jax.ShapeDtypeStruct((bsz, s, A_HEADS * dv), BF16)] + cast_shapes,
        scratch_shapes=[pltpu.VMEM((nq, dv + ATTN_ONES_ROWS, t), BF16),
                        pltpu.VMEM((t, t), F32),
                        pltpu.VMEM((t, t), F32),
                        pltpu.VMEM((1, 2 * t), F32),
                        pltpu.VMEM((dv + ATTN_ONES_ROWS, 2 * t), F32)]
                       + 2 * [pltpu.VMEM((t, 2 * t), F32), pltpu.VMEM((1, 2 * t), F32),
                              pltpu.VMEM((1, 2 * t), F32)],
        compiler_params=_params("arbitrary", "arbitrary", "arbitrary"),
        name="diff_attn",
    )(y3, y3, y3, table, lam_vecs, diff_g, *[w for w, _, _, _ in jobs])


def _mlstm_kernel(qk_ref, v_ref, ob_ref, gc_ref, gr_ref, cw_ref, cb_ref, bc_ref, br_ref, g_ref, y_ref,
                  xp_ref, *state_refs, chunk):
    heads, dk, dv = B_HEADS, B_QK_DIM, B_V_DIM
    c_refs, n_refs, m_refs = state_refs[0::3], state_refs[1::3], state_refs[2::3]
    pad = xp_ref.shape[0] - chunk
    c = pl.program_id(1)

    @pl.when(c == 0)
    def _():
        xp_ref[0:pad, :] = jnp.zeros((pad, xp_ref.shape[1]), xp_ref.dtype)
        for ref in state_refs:
            ref[...] = jnp.zeros_like(ref)

    x = qk_ref[...]
    xp_ref[pad:pad + chunk, :] = x
    xp = xp_ref[...]
    cw = cw_ref[...]
    dst = lax.broadcasted_iota(jnp.int32, (chunk, pad + chunk), 0)
    src = lax.broadcasted_iota(jnp.int32, (chunk, pad + chunk), 1)
    conv = cb_ref[...] + cw[CONV_WIDTH - 1:CONV_WIDTH, :] * x.astype(F32)
    for j in range(CONV_WIDTH - 1):
        shift = (src == dst + (pad - (CONV_WIDTH - 1) + j)).astype(BF16)
        conv = conv + cw[j:j + 1, :] * jnp.dot(shift, xp, preferred_element_type=F32)
    xp_ref[0:pad, :] = x[chunk - pad:chunk, :]
    qk = conv * _sigmoid(conv)

    gc = gc_ref[...] + bc_ref[...]
    gr = gr_ref[...] + br_ref[...]
    row = lax.broadcasted_iota(jnp.int32, (chunk, chunk), 0)
    col = lax.broadcasted_iota(jnp.int32, (chunk, chunk), 1)
    causal = row >= col
    tril = causal.astype(F32)
    triu = (row <= col).astype(F32)
    b_c = jnp.dot(tril, _log_sigmoid(gc), precision=HIGHEST, preferred_element_type=F32)
    b_r = jnp.dot(_log_sigmoid(gr), triu, precision=HIGHEST, preferred_element_type=F32)

    ob = ob_ref[...].astype(F32)
    g = g_ref[...]
    hs = range(heads)
    q = [qk[:, h * dk:(h + 1) * dk] * (dk ** -0.5) for h in hs]
    k = [qk[:, (heads + h) * dk:(heads + h + 1) * dk] for h in hs]
    v = [v_ref[:, h * dv:(h + 1) * dv] for h in hs]
    b_col = [b_c[:, heads + h:heads + h + 1] for h in hs]
    i_col = [gc[:, h:h + 1] for h in hs]
    b_row = [b_r[heads + h:heads + h + 1, :] for h in hs]
    i_row = [gr[h:h + 1, :] for h in hs]
    m_prev = [m_refs[h][0:1, 0:1] for h in hs]
    c_mat = [c_refs[h][...] for h in hs]
    n_vec = [n_refs[h][...] for h in hs]

    b_last = [b_row[h][:, chunk - 1:chunk] for h in hs]
    g_col = [b_last[h] - b_col[h] + i_col[h] for h in hs]
    m_new = [jnp.maximum(b_last[h] + m_prev[h], jnp.max(g_col[h], 0, keepdims=True)) for h in hs]
    decay = [jnp.exp(b_last[h] + m_prev[h] - m_new[h]) for h in hs]
    kw = [k[h] * jnp.exp(g_col[h] - m_new[h]) for h in hs]
    qb = [q[h].astype(BF16) for h in hs]
    qk_t = [lax.dot_general(qb[h], k[h].astype(BF16), NT_DIMS, preferred_element_type=F32) for h in hs]
    q_c = [jnp.dot(qb[h], c_mat[h].astype(BF16), preferred_element_type=F32) for h in hs]
    kw_v = [lax.dot_general(kw[h].astype(BF16), v[h], TN_DIMS, preferred_element_type=F32) for h in hs]

    d = [jnp.where(causal, b_col[h] - b_row[h] + i_row[h], -jnp.inf) for h in hs]
    a_col = [b_col[h] + m_prev[h] for h in hs]
    m_t = [jnp.maximum(a_col[h], jnp.max(d[h], -1, keepdims=True)) for h in hs]
    w_inter = [jnp.exp(a_col[h] - m_t[h]) for h in hs]
    sw = [qk_t[h] * jnp.exp(d[h] - m_t[h]) for h in hs]
    sw_v = [jnp.dot(sw[h].astype(BF16), v[h], preferred_element_type=F32) for h in hs]

    for h in hs:
        c_refs[h][...] = decay[h] * c_mat[h] + kw_v[h]
        n_refs[h][...] = decay[h] * n_vec[h] + jnp.sum(kw[h], 0, keepdims=True)
        m_refs[h][...] = jnp.broadcast_to(m_new[h], m_refs[h].shape)

    for h in hs:
        num = w_inter[h] * q_c[h] + sw_v[h]
        den = (w_inter[h] * jnp.sum(q[h] * n_vec[h], -1, keepdims=True)
               + jnp.sum(sw[h], -1, keepdims=True))
        hh = num / jnp.maximum(jnp.abs(den), jnp.exp(-m_t[h]))
        hn = hh * lax.rsqrt(jnp.mean(hh * hh, -1, keepdims=True) + NORM_EPS)
        sl = slice(h * dv, (h + 1) * dv)
        y_ref[:, sl] = (hn * g[:, sl] * _sigmoid(ob[:, sl])).astype(y_ref.dtype)


def _mlstm(y3, gcol3, grow3, conv_w, conv_b, gate_bias_cols, gate_bias_rows, mlstm_g, *, chunk):
    bsz, s, _ = y3.shape
    heads, dk, dv = B_HEADS, B_QK_DIM, B_V_DIM
    w = heads * dv
    assert 2 * heads * dk == w and s % chunk == 0 and chunk % LANES == 0
    nc = s // chunk
    return pl.pallas_call(
        functools.partial(_mlstm_kernel, chunk=chunk),
        grid=(bsz, nc),
        in_specs=[
            pl.BlockSpec((None, chunk, w), lambda b, c: (b, c, 3)),
            pl.BlockSpec((None, chunk, w), lambda b, c: (b, c, 4)),
            pl.BlockSpec((None, chunk, w), lambda b, c: (b, c, 5)),
            pl.BlockSpec((None, chunk, LANES), lambda b, c: (b, c, 0)),
            pl.BlockSpec((SUBLANES, chunk), lambda b, c: (0, b * nc + c)),
            pl.BlockSpec((CONV_WIDTH, w), lambda b, c: (0, 0)),
            pl.BlockSpec((1, w), lambda b, c: (0, 0)),
            pl.BlockSpec((1, LANES), lambda b, c: (0, 0)),
            pl.BlockSpec((SUBLANES, 1), lambda b, c: (0, 0)),
            pl.BlockSpec((1, w), lambda b, c: (0, 0)),
        ],
        out_specs=pl.BlockSpec((None, chunk, w), lambda b, c: (b, c, 0)),
        out_shape=jax.ShapeDtypeStruct((bsz, s, w), BF16),
        scratch_shapes=[
            pltpu.VMEM((BF16_SUBLANES + chunk, w), BF16),
        ] + heads * [pltpu.VMEM((dk, dv), F32), pltpu.VMEM((1, dk), F32), pltpu.VMEM((SUBLANES, LANES), F32)],
        compiler_params=_params("parallel", "arbitrary"),
        name="mlstm",
    )(y3, y3, y3, gcol3, grow3, conv_w, conv_b, gate_bias_cols, gate_bias_rows, mlstm_g)


def _gla_kernel(q_ref, k_ref, v_ref, r_ref, a1_ref, a1n_ref, wah_ref, wal_ref, ba_ref, g_ref, y_ref,
                b_ref, *st_refs, chunk):
    heads, dk, dv = C_HEADS, C_QK_DIM, C_V_DIM
    sub = GLA_SUB
    assert chunk == 2 * sub
    c = pl.program_id(1)
    row = lax.broadcasted_iota(jnp.int32, (chunk, chunk), 0)
    col = lax.broadcasted_iota(jnp.int32, (chunk, chunk), 1)
    causal = (row >= col) & ((row < sub) == (col < sub))
    first = lax.broadcasted_iota(jnp.int32, (chunk, 1), 0) < sub

    def cum_log_decay(a1):
        a_hi = a1.astype(BF16)
        a_lo = (a1 - a_hi.astype(F32)).astype(BF16)
        w_hi = wah_ref[...]
        z = (jnp.dot(a_hi, w_hi, preferred_element_type=F32)
             + (jnp.dot(a_lo, w_hi, preferred_element_type=F32)
                + jnp.dot(a_hi, wal_ref[...], preferred_element_type=F32))) + ba_ref[...]
        log_a = _log_sigmoid(z) * (1.0 / C_GATE_TEMP)
        return jnp.dot(causal.astype(F32), log_a, precision=HIGHEST, preferred_element_type=F32)

    @pl.when(c == 0)
    def _():
        for ref in st_refs:
            ref[...] = jnp.zeros_like(ref)
        b_ref[...] = cum_log_decay(a1_ref[0:chunk, :])

    g = g_ref[...]
    n_inner = q_ref.shape[0] // chunk
    for i in range(n_inner):
        rows = slice(i * chunk, (i + 1) * chunk)
        b_all = b_ref[...]
        b_next = cum_log_decay(a1_ref[(i + 1) * chunk:(i + 2) * chunk, :] if i + 1 < n_inner else a1n_ref[...])

        r = r_ref[rows, :].astype(F32)
        for h in range(heads):
            ks = slice(h * dk, (h + 1) * dk)
            vs = slice(h * dv, (h + 1) * dv)
            b = b_all[:, ks]
            q = q_ref[rows, ks].astype(F32) * (dk ** -0.5)
            k = k_ref[rows, ks].astype(F32)
            v = v_ref[rows, vs]
            tot0 = b[sub - 1:sub, :]
            tot1 = b[chunk - 1:chunk, :]
            q_dec = q * jnp.exp(b)
            k_dec = (k * jnp.exp(-b)).astype(BF16)
            k_end = k * jnp.exp(jnp.where(first, tot0, tot1) - b)
            k_new = (k_end * jnp.where(first, jnp.exp(tot1), 1.0)).astype(BF16)
            q_int = (q_dec * jnp.where(first, 1.0, jnp.exp(tot0))).astype(BF16)
            q_dec = q_dec.astype(BF16)
            st = st_refs[h][...]
            qk_t = lax.dot_general(q_dec, k_dec, NT_DIMS, preferred_element_type=F32)
            inter = lax.dot_general(q_int, st.astype(BF16), NT_DIMS, preferred_element_type=F32)
            st_refs[h][...] = (jnp.exp(tot0 + tot1) * st
                               + lax.dot_general(v, k_new, TN_DIMS, preferred_element_type=F32))
            cross = lax.dot_general(q_dec[sub:], k_end[:sub].astype(BF16), NT_DIMS, preferred_element_type=F32)
            att = jnp.where(causal, qk_t, 0.0)
            o = jnp.dot(att.astype(BF16), v, preferred_element_type=F32) + inter
            o_cross = jnp.dot(cross.astype(BF16), v[:sub], preferred_element_type=F32)
            o = jnp.concatenate([o[:sub], o[sub:] + o_cross], axis=0)

            on = o * lax.rsqrt(jnp.mean(o * o, -1, keepdims=True) + NORM_EPS)
            rr = r[:, vs]
            y_ref[rows, vs] = (on * g[:, vs] * (rr * _sigmoid(rr))).astype(y_ref.dtype)

        b_ref[...] = b_next


def _gla(y3, a13, w_a2_hi, w_a2_lo, b_a, gla_g, *, chunk, block):
    bsz, s, _ = y3.shape
    heads, dk, dv = C_HEADS, C_QK_DIM, C_V_DIM
    wk, wv = heads * dk, heads * dv
    assert wv == 2 * wk and s % block == 0 and block % chunk == 0
    nb, per_block = s // block, block // chunk
    return pl.pallas_call(
        functools.partial(_gla_kernel, chunk=chunk),
        grid=(bsz, nb),
        in_specs=[
            pl.BlockSpec((None, block, wk), lambda b, c: (b, c, 0)),
            pl.BlockSpec((None, block, wk), lambda b, c: (b, c, 1)),
            pl.BlockSpec((None, block, wv), lambda b, c: (b, c, 1)),
            pl.BlockSpec((None, block, wv), lambda b, c: (b, c, 2)),
            pl.BlockSpec((None, block, LANES), lambda b, c: (b, c, 0)),
            pl.BlockSpec((None, chunk, LANES), lambda b, c: (b, jnp.minimum(c + 1, nb - 1) * per_block, 0)),
            pl.BlockSpec((LANES, wk), lambda b, c: (0, 0)),
            pl.BlockSpec((LANES, wk), lambda b, c: (0, 0)),
            pl.BlockSpec((1, wk), lambda b, c: (0, 0)),
            pl.BlockSpec((1, wv), lambda b, c: (0, 0)),
        ],
        out_specs=pl.BlockSpec((None, block, wv), lambda b, c: (b, c, 0)),
        out_shape=jax.ShapeDtypeStruct((bsz, s, wv), BF16),
        scratch_shapes=[pltpu.VMEM((chunk, wk), F32)] + heads * [pltpu.VMEM((dv, dk), F32)],
        compiler_params=_params("parallel", "arbitrary"),
        name="gla",
    )(y3, y3, y3, y3, a13, a13, w_a2_hi, w_a2_lo, b_a, gla_g)


def _out_ln_kernel(*refs, n_in, alpha):
    y_refs, w_refs = refs[:n_in], refs[n_in:2 * n_in]
    x_ref, g_ref, b_ref, o_ref = refs[2 * n_in:]
    rb = o_ref.shape[0] // ROW_SPLIT
    for r in range(ROW_SPLIT):
        rows = slice(r * rb, (r + 1) * rb)
        acc = jnp.dot(y_refs[0][rows, :], w_refs[0][...], preferred_element_type=F32)
        for y_ref, w_ref in zip(y_refs[1:], w_refs[1:]):
            acc = acc + jnp.dot(y_ref[rows, :], w_ref[...], preferred_element_type=F32)
        o_ref[rows, :] = _layer_norm(alpha * x_ref[rows, :] + acc, g_ref[...], b_ref[...])


def _out_ln(ys, w, x, ln_g, ln_b, layer, w_layer, ln_slot, *, alpha, tm):
    t, d = x.shape
    n_in = len(ys)
    wd = ys[0].shape[1]
    assert all(y.shape[1] == wd for y in ys) and w.shape[1:] == (n_in * wd, d)
    ln_spec = pl.BlockSpec((None, None, 1, d), lambda i: (layer, ln_slot, 0, 0))
    in_specs = ([pl.BlockSpec((tm, wd), lambda i: (i, 0)) for _ in ys]
                + [pl.BlockSpec((None, wd, d), lambda i, k=k: (w_layer, k, 0)) for k in range(n_in)]
                + [pl.BlockSpec((tm, d), lambda i: (i, 0)), ln_spec, ln_spec])
    return pl.pallas_call(
        functools.partial(_out_ln_kernel, n_in=n_in, alpha=alpha),
        grid=(t // tm,),
        in_specs=in_specs,
        out_specs=pl.BlockSpec((tm, d), lambda i: (i, 0)),
        out_shape=jax.ShapeDtypeStruct((t, d), F32),
        compiler_params=_params("parallel"),
        name="out_ln",
    )(*ys, *([w] * n_in), x, ln_g, ln_b)


def _ple_kernel(x_ref, p_ref, wp_ref, wg_ref, o_ref):
    rb = o_ref.shape[0] // ROW_SPLIT
    for r in range(ROW_SPLIT):
        rows = slice(r * rb, (r + 1) * rb)
        x = x_ref[rows, :]
        e = jnp.dot(p_ref[rows, :].astype(BF16), wp_ref[...], preferred_element_type=F32)
        gate = jnp.dot(x.astype(BF16), wg_ref[...], preferred_element_type=F32)
        o_ref[rows, :] = x + e * _sigmoid(gate)


def _ple(x, p, w_proj, w_gate, layer, *, tm):
    t, d = x.shape
    s, pd = p.shape[2:]
    assert s % tm == 0
    per_seq = s // tm
    return pl.pallas_call(
        _ple_kernel,
        grid=(t // tm,),
        in_specs=[
            pl.BlockSpec((tm, d), lambda i: (i, 0)),
            pl.BlockSpec((None, None, tm, pd), lambda i: (layer, i // per_seq, i % per_seq, 0)),
            pl.BlockSpec((None, pd, d), lambda i: (layer, 0, 0)),
            pl.BlockSpec((None, d, d), lambda i: (layer, 0, 0)),
        ],
        out_specs=pl.BlockSpec((tm, d), lambda i: (i, 0)),
        out_shape=jax.ShapeDtypeStruct((t, d), F32),
        compiler_params=_params("parallel"),
        name="ple",
    )(x, p, w_proj, w_gate)


def _tiles(t, s):
    return dict(
        ffn_tm=min(1024, t), ffn_tf=256,
        proj_tm=min(1024, t), proj_tn=2048,
        row_tm=min(512, t),
        attn_t=min(512, s),
        mlstm_chunk=min(256, s),
        gla_chunk=min(2 * GLA_SUB, s), gla_block=min(8 * GLA_SUB, s),
    )


def _cast_job(w2d, unit, skip, count, steps):
    rows = next(r for r in range(BF16_SUBLANES, unit + 1, BF16_SUBLANES)
                if unit % r == 0 and count * (unit // r) <= steps)
    return w2d, rows, skip * (unit // rows), count * (unit // rows)


def _pad_cols(w, n):
    return jnp.pad(w, ((0, 0), (0, n - w.shape[1])))


def kernel(x, p, ln_g, ln_b, w_ffn_in, w_ffn_out, w_in_ab, w_out_ab, rel_bias, lambda_q1, lambda_k1,
           lambda_q2, lambda_k2, diff_norm, conv_w, conv_b, b_igate, b_fgate, mlstm_norm, w_in_c,
           w_alpha2, b_alpha, gla_norm, w_out_c, w_ple_proj, w_ple_gate):
    bsz, s, d = x.shape
    depth = p.shape[0]
    t = bsz * s
    tl = _tiles(t, s)
    alpha = (2 * depth) ** 0.25
    a_w = A_HEADS * A_V_DIM
    ab_main = 3 * a_w + 3 * B_HEADS * B_V_DIM
    c_main = 2 * C_HEADS * C_QK_DIM + 2 * C_HEADS * C_V_DIM
    row = lambda a: a.reshape(1, -1)

    n_ffn, f_ff = 2 * depth, w_ffn_out.shape[2]
    attn_steps = bsz * A_HEADS * (s // tl["attn_t"])
    wfi_first, wfo_first = w_ffn_in[0, :1].astype(BF16), w_ffn_out[0, :1].astype(BF16)
    ffn_w = {0: (wfi_first, wfo_first, 0)}
    cast_jobs = [
        _cast_job(w_ffn_in.reshape(n_ffn * d, -1), d, 1, n_ffn - 1, attn_steps),
        _cast_job(w_ffn_out.reshape(n_ffn * f_ff, d), f_ff, 1, n_ffn - 1, attn_steps),
        _cast_job(jnp.swapaxes(w_in_c, 1, 2).reshape(-1, d), w_in_c.shape[2], 0, w_in_c.shape[0], attn_steps),
    ]
    wo_ab, wo_c = w_out_ab.astype(BF16), w_out_c.astype(BF16)
    wpp, wpg = w_ple_proj.astype(BF16), w_ple_gate.astype(BF16)
    g4, b4 = ln_g.reshape(depth, 3, 1, d), ln_b.reshape(depth, 3, 1, d)

    xf = x.reshape(t, d)
    for i in range(depth):
        xf = _ffn_ln(xf, *ffn_w[2 * i], g4, b4, i, 0, alpha=alpha, tm=tl["ffn_tm"], tf=tl["ffn_tf"])
        if i % 2 == 0:
            e = i // 2
            lam_init = 0.8 - 0.6 * math.exp(-0.3 * i)
            y, gcol, grow = _in_proj(xf, w_in_ab, e, ab_main, tm=tl["proj_tm"], tn=tl["proj_tn"])
            y3 = y.reshape(bsz, s, ab_main)
            lam_vecs = jnp.stack([lambda_q1[e], lambda_k1[e], lambda_q2[e], lambda_k2[e]])
            ya, *cast = _diff_attention(y3, rel_bias, lam_vecs, row(diff_norm[e]), cast_jobs,
                                        lam_init=lam_init, t=tl["attn_t"])
            if i == 0:
                wfi_rest, wfo_rest = cast[0].reshape(n_ffn - 1, d, -1), cast[1].reshape(n_ffn - 1, f_ff, d)
                ffn_w.update({n: (wfi_rest, wfo_rest, n - 1) for n in range(1, n_ffn)})
                w_c_t = cast[2].reshape(w_in_c.shape[0], w_in_c.shape[2], d)
            gate_bias = jnp.concatenate([b_igate[e], b_fgate[e]])
            yb = _mlstm(y3, gcol.reshape(bsz, s, LANES), grow,
                        conv_w[e], row(conv_b[e]), _pad_cols(row(gate_bias), LANES), gate_bias.reshape(-1, 1),
                        row(mlstm_norm[e]), chunk=tl["mlstm_chunk"])
            ys, w_out, w_layer = [ya.reshape(t, a_w), yb.reshape(t, -1)], wo_ab, e
        else:
            o = i // 2
            y, a1, _ = _in_proj(xf, w_in_c, o, c_main, tm=tl["proj_tm"], tn=tl["proj_tn"], w_t=w_c_t)
            w_a2 = jnp.pad(w_alpha2[o], ((0, LANES - w_alpha2.shape[1]), (0, 0)))
            w_a2_hi = lax.reduce_precision(w_a2, exponent_bits=8, mantissa_bits=7)
            w_a2_lo = (w_a2 - w_a2_hi).astype(BF16)
            yc = _gla(y.reshape(bsz, s, c_main), a1.reshape(bsz, s, LANES), w_a2_hi.astype(BF16), w_a2_lo,
                      row(b_alpha[o]),
                      row(gla_norm[o]), chunk=tl["gla_chunk"], block=tl["gla_block"])
            ys, w_out, w_layer = [yc.reshape(t, -1)], wo_c, o
        xf = _out_ln(ys, w_out, xf, g4, b4, i, w_layer, 1, alpha=alpha, tm=tl["row_tm"])
        xf = _ffn_ln(xf, *ffn_w[2 * i + 1], g4, b4, i, 2, alpha=alpha, tm=tl["ffn_tm"], tf=tl["ffn_tf"])
        xf = _ple(xf, p, wpp, wpg, i, tm=tl["row_tm"])
    return xf.reshape(bsz, s, d)
```
